```python
import jax, jax.numpy as jnp
from jax import lax
import numpy as np

D_MODEL = 2048
BATCH = 8
SEQ = 8192
DEPTH = 1

SB_HEADS = 8
SB_HEAD_DIM = 128
SB_WIDTH = SB_HEADS * SB_HEAD_DIM
SB_BLOCK = 128
HG_HEADS = 8
HG_KEY_DIM = 128
HG_VAL_DIM = 128
HG_K_WIDTH = HG_HEADS * HG_KEY_DIM
HG_V_WIDTH = HG_HEADS * HG_VAL_DIM
HG_CHUNK = 64
IN_COLS = 3 * SB_WIDTH + 2 * HG_K_WIDTH + 2 * HG_V_WIDTH + 2 * D_MODEL
D_FF = 5632
CONV_WIDTH = 3
EPS = 1e-6

kernel_name = "hybrid_stickbreaking_hgrn2_convffn"


def rmsnorm(x, g):
    xf = x.astype(jnp.float32)
    y = xf * lax.rsqrt(jnp.mean(xf * xf, axis=-1, keepdims=True) + EPS)
    return (y * g.astype(jnp.float32)).astype(x.dtype)


def split_heads(a, n_heads):
    b, s, w = a.shape
    return a.reshape(b, s, n_heads, w // n_heads).transpose(0, 2, 1, 3)


def merge_heads(a):
    b, h, s, d = a.shape
    return a.transpose(0, 2, 1, 3).reshape(b, s, h * d)


def stick_breaking_attention(q, k, v):
    seq = q.shape[2]
    scale = SB_HEAD_DIM ** -0.5
    outs = []
    for blk in range(seq // SB_BLOCK):
        t0, t1 = blk * SB_BLOCK, (blk + 1) * SB_BLOCK
        qb = q[:, :, t0:t1].astype(jnp.float32)
        kb = k[:, :, :t1].astype(jnp.float32)
        z = jnp.einsum('bhtd,bhsd->bhts', qb, kb) * scale
        t_idx = t0 + jnp.arange(SB_BLOCK)[:, None]
        s_idx = jnp.arange(t1)[None, :]
        strict = s_idx < t_idx
        log_keep = jnp.where(strict, jax.nn.log_sigmoid(-z), 0.0)
        later = lax.cumsum(log_keep, axis=3, reverse=True) - log_keep
        log_w = jnp.where(strict, jax.nn.log_sigmoid(z) + later, -jnp.inf)
        w = jnp.exp(log_w)
        outs.append(jnp.einsum('bhts,bhsd->bhtd', w, v[:, :, :t1].astype(jnp.float32)))
    return jnp.concatenate(outs, axis=2).astype(v.dtype)


def hgrn2_chunkwise(q, k, v, log_f):
    b_, h_, seq, dk = q.shape
    dv = v.shape[-1]
    n_chunks = seq // HG_CHUNK

    def to_chunks(a):
        return jnp.moveaxis(a.astype(jnp.float32).reshape(b_, h_, n_chunks, HG_CHUNK, a.shape[-1]), 2, 0)

    qc, kc, vc, gc = to_chunks(q), to_chunks(k), to_chunks(v), to_chunks(log_f)
    causal = jnp.tril(jnp.ones((HG_CHUNK, HG_CHUNK), dtype=bool))[:, :, None]

    def step(state, inp):
        qi, ki, vi, gi = inp
        b = jnp.cumsum(gi, axis=2)
        o_inter = jnp.einsum('bhtd,bhde->bhte', qi * jnp.exp(b), state)
        diff = b[:, :, :, None, :] - b[:, :, None, :, :]
        decay = jnp.exp(jnp.where(causal, diff, -jnp.inf))
        scores = jnp.einsum('bhtd,bhsd,bhtsd->bhts', qi, ki, decay)
        o_intra = jnp.einsum('bhts,bhse->bhte', scores, vi)
        b_last = b[:, :, -1:, :]
        k_dec = ki * jnp.exp(b_last - b)
        new_state = jnp.exp(b_last[:, :, 0, :])[..., None] * state + jnp.einsum('bhsd,bhse->bhde', k_dec, vi)
        return new_state, o_inter + o_intra

    state0 = jnp.zeros((b_, h_, dk, dv), jnp.float32)
    _, o = lax.scan(step, state0, (qc, kc, vc, gc))
    return jnp.moveaxis(o, 0, 2).reshape(b_, h_, seq, dv)


def causal_depthwise_conv(a, w, bias):
    seq = a.shape[1]
    a_pad = jnp.pad(a, ((0, 0), (CONV_WIDTH - 1, 0), (0, 0)))
    out = bias
    for j in range(CONV_WIDTH):
        out = out + w[j] * a_pad[:, j:j + seq]
    return out


def _fwd_setup_inputs(seed: int = 0) -> dict:
    key = jax.random.key(seed)
    ks = jax.random.split(key, 16)
    f32 = jnp.float32
    nrm = lambda k, shape, s: jax.random.normal(k, shape, f32) * s
    return {
        "x": nrm(ks[0], (BATCH, SEQ, D_MODEL), 1.0),
        "g_mix": 1.0 + nrm(ks[1], (DEPTH, D_MODEL), 0.02),
        "w_in": nrm(ks[2], (DEPTH, D_MODEL, IN_COLS), D_MODEL ** -0.5),
        "g_q": 1.0 + nrm(ks[3], (DEPTH, SB_HEAD_DIM), 0.02),
        "g_k": 1.0 + nrm(ks[4], (DEPTH, SB_HEAD_DIM), 0.02),
        "lb_logits": nrm(ks[5], (DEPTH + 1, HG_K_WIDTH), 0.5),
        "g_hg_out": 1.0 + nrm(ks[6], (DEPTH, HG_VAL_DIM), 0.02),
        "p_a": nrm(ks[7], (DEPTH, SB_WIDTH, D_MODEL), SB_WIDTH ** -0.5),
        "p_b": nrm(ks[8], (DEPTH, HG_V_WIDTH, D_MODEL), HG_V_WIDTH ** -0.5),
        "w_o": nrm(ks[9], (DEPTH, D_MODEL, D_MODEL), D_MODEL ** -0.5),
        "g_ffn": 1.0 + nrm(ks[10], (DEPTH, D_MODEL), 0.02),
        "w_up": nrm(ks[11], (DEPTH, D_MODEL, 2 * D_FF), D_MODEL ** -0.5),
        "conv_w": nrm(ks[12], (DEPTH, CONV_WIDTH, 2 * D_FF), CONV_WIDTH ** -0.5),
        "conv_b": nrm(ks[13], (DEPTH, 2 * D_FF), 0.02),
        "w_down": nrm(ks[14], (DEPTH, D_FF, D_MODEL), D_FF ** -0.5),
    }


def _fwd_reference(x, g_mix, w_in, g_q, g_k, lb_logits, g_hg_out, p_a, p_b, w_o,
              g_ffn, w_up, conv_w, conv_b, w_down):
    splits = [SB_WIDTH, 2 * SB_WIDTH, 3 * SB_WIDTH,
              3 * SB_WIDTH + HG_K_WIDTH, 3 * SB_WIDTH + 2 * HG_K_WIDTH,
              3 * SB_WIDTH + 2 * HG_K_WIDTH + HG_V_WIDTH,
              3 * SB_WIDTH + 2 * HG_K_WIDTH + 2 * HG_V_WIDTH,
              3 * SB_WIDTH + 2 * HG_K_WIDTH + 2 * HG_V_WIDTH + D_MODEL]
    lower_bounds = jnp.cumsum(jax.nn.softmax(lb_logits.astype(jnp.float32), axis=0), axis=0)
    h = x
    for layer in range(DEPTH):
        u = rmsnorm(h, g_mix[layer])
        proj = u @ w_in[layer]
        sb_q, sb_k, sb_v, hg_q, hg_f, hg_i, hg_og, gate_a, gate_b = jnp.split(proj, splits, axis=-1)

        q = rmsnorm(split_heads(sb_q, SB_HEADS), g_q[layer])
        k = rmsnorm(split_heads(sb_k, SB_HEADS), g_k[layer])
        v = split_heads(sb_v, SB_HEADS)
        y_a = merge_heads(stick_breaking_attention(q, k, v))

        lb = lower_bounds[layer].reshape(1, HG_HEADS, 1, HG_KEY_DIM)
        f = lb + (1.0 - lb) * jax.nn.sigmoid(split_heads(hg_f, HG_HEADS).astype(jnp.float32))
        o_b = hgrn2_chunkwise(jax.nn.silu(split_heads(hg_q, HG_HEADS)), 1.0 - f,
                              split_heads(hg_i, HG_HEADS), jnp.log(f))
        o_b = rmsnorm(o_b, g_hg_out[layer])
        y_b = (merge_heads(o_b) * jax.nn.silu(hg_og.astype(jnp.float32))).astype(h.dtype)

        m = jax.nn.sigmoid(gate_a) * (y_a @ p_a[layer]) + jax.nn.sigmoid(gate_b) * (y_b @ p_b[layer])
        h = h + m @ w_o[layer]

        up = rmsnorm(h, g_ffn[layer]) @ w_up[layer]
        up = causal_depthwise_conv(up, conv_w[layer], conv_b[layer])
        gate, val = jnp.split(up, 2, axis=-1)
        h = h + (jax.nn.silu(gate) * val) @ w_down[layer]
    return h


import jax as _jax
import jax.numpy as _jnp

TWIN_FORMAT = 'train_step'
FWD_PARAMS = ['x', 'g_mix', 'w_in', 'g_q', 'g_k', 'lb_logits', 'g_hg_out', 'p_a', 'p_b', 'w_o', 'g_ffn', 'w_up', 'conv_w', 'conv_b', 'w_down']
TWIN_WEIGHTS = ['g_mix', 'w_in', 'g_q', 'g_k', 'lb_logits', 'g_hg_out', 'p_a', 'p_b', 'w_o', 'g_ffn', 'w_up', 'conv_w', 'conv_b', 'w_down']
TWIN_DIFF_INPUT = 'x'
TWIN_INPUTS = ['x', 'g_mix', 'w_in', 'g_q', 'g_k', 'lb_logits', 'g_hg_out', 'p_a', 'p_b', 'w_o', 'g_ffn', 'w_up', 'conv_w', 'conv_b', 'w_down', 'loss_target', 'm_g_mix', 'm_w_in', 'm_g_q', 'm_g_k', 'm_lb_logits', 'm_g_hg_out', 'm_p_a', 'm_p_b', 'm_w_o', 'm_g_ffn', 'm_w_up', 'm_conv_w', 'm_conv_b', 'm_w_down', 'v_g_mix', 'v_w_in', 'v_g_q', 'v_g_k', 'v_lb_logits', 'v_g_hg_out', 'v_p_a', 'v_p_b', 'v_w_o', 'v_g_ffn', 'v_w_up', 'v_conv_w', 'v_conv_b', 'v_w_down']
TWIN_OUTPUTS = ['loss', 'grad_x', 'grad_g_mix', 'grad_w_in', 'grad_g_q', 'grad_g_k', 'grad_lb_logits', 'grad_g_hg_out', 'grad_p_a', 'grad_p_b', 'grad_w_o', 'grad_g_ffn', 'grad_w_up', 'grad_conv_w', 'grad_conv_b', 'grad_w_down', 'delta_g_mix', 'delta_w_in', 'delta_g_q', 'delta_g_k', 'delta_lb_logits', 'delta_g_hg_out', 'delta_p_a', 'delta_p_b', 'delta_w_o', 'delta_g_ffn', 'delta_w_up', 'delta_conv_w', 'delta_conv_b', 'delta_w_down', 'new_m_g_mix', 'new_m_w_in', 'new_m_g_q', 'new_m_g_k', 'new_m_lb_logits', 'new_m_g_hg_out', 'new_m_p_a', 'new_m_p_b', 'new_m_w_o', 'new_m_g_ffn', 'new_m_w_up', 'new_m_conv_w', 'new_m_conv_b', 'new_m_w_down', 'new_v_g_mix', 'new_v_w_in', 'new_v_g_q', 'new_v_g_k', 'new_v_lb_logits', 'new_v_g_hg_out', 'new_v_p_a', 'new_v_p_b', 'new_v_w_o', 'new_v_g_ffn', 'new_v_w_up', 'new_v_conv_w', 'new_v_conv_b', 'new_v_w_down']
TWIN_LEAF_KINDS = {'loss': 'loss', 'grad_x': 'grad_x', 'grad_g_mix': 'grad_w', 'grad_w_in': 'grad_w', 'grad_g_q': 'grad_w', 'grad_g_k': 'grad_w', 'grad_lb_logits': 'grad_w', 'grad_g_hg_out': 'grad_w', 'grad_p_a': 'grad_w', 'grad_p_b': 'grad_w', 'grad_w_o': 'grad_w', 'grad_g_ffn': 'grad_w', 'grad_w_up': 'grad_w', 'grad_conv_w': 'grad_w', 'grad_conv_b': 'grad_w', 'grad_w_down': 'grad_w', 'delta_g_mix': 'delta_w', 'delta_w_in': 'delta_w', 'delta_g_q': 'delta_w', 'delta_g_k': 'delta_w', 'delta_lb_logits': 'delta_w', 'delta_g_hg_out': 'delta_w', 'delta_p_a': 'delta_w', 'delta_p_b': 'delta_w', 'delta_w_o': 'delta_w', 'delta_g_ffn': 'delta_w', 'delta_w_up': 'delta_w', 'delta_conv_w': 'delta_w', 'delta_conv_b': 'delta_w', 'delta_w_down': 'delta_w', 'new_m_g_mix': 'new_m', 'new_m_w_in': 'new_m', 'new_m_g_q': 'new_m', 'new_m_g_k': 'new_m', 'new_m_lb_logits': 'new_m', 'new_m_g_hg_out': 'new_m', 'new_m_p_a': 'new_m', 'new_m_p_b': 'new_m', 'new_m_w_o': 'new_m', 'new_m_g_ffn': 'new_m', 'new_m_w_up': 'new_m', 'new_m_conv_w': 'new_m', 'new_m_conv_b': 'new_m', 'new_m_w_down': 'new_m', 'new_v_g_mix': 'new_v', 'new_v_w_in': 'new_v', 'new_v_g_q': 'new_v', 'new_v_g_k': 'new_v', 'new_v_lb_logits': 'new_v', 'new_v_g_hg_out': 'new_v', 'new_v_p_a': 'new_v', 'new_v_p_b': 'new_v', 'new_v_w_o': 'new_v', 'new_v_g_ffn': 'new_v', 'new_v_w_up': 'new_v', 'new_v_conv_w': 'new_v', 'new_v_conv_b': 'new_v', 'new_v_w_down': 'new_v'}


def _forward(args):
    return _fwd_reference(*[args[k] for k in FWD_PARAMS])


def _output_shape():
    def fwd():
        inp = _fwd_setup_inputs(0)
        return _fwd_reference(*[inp[k] for k in FWD_PARAMS])
    out = _jax.eval_shape(fwd)
    return out.shape, out.dtype

N_MICROBATCH = 1
ADAM_LR = 0.001
ADAM_B1 = 0.9
ADAM_B2 = 0.999
ADAM_EPS = 1e-08
ADAM_WD = 0.01
ADAM_STEP = 10
PER_EXAMPLE_BATCH_AXIS = {'x': 0, 'loss_target': 0}
SHARED_INPUTS = []
_WEIGHT_DTYPES = {'g_mix': _jnp.float32, 'w_in': _jnp.float32, 'g_q': _jnp.float32, 'g_k': _jnp.float32, 'lb_logits': _jnp.float32, 'g_hg_out': _jnp.float32, 'p_a': _jnp.float32, 'p_b': _jnp.float32, 'w_o': _jnp.float32, 'g_ffn': _jnp.float32, 'w_up': _jnp.float32, 'conv_w': _jnp.float32, 'conv_b': _jnp.float32, 'w_down': _jnp.float32}
MOMENT_SCALE = {'g_mix': 8.523549e+00, 'w_in': 1.145563e-01, 'g_q': 9.212157e+00, 'g_k': 9.219330e+00, 'lb_logits': 1.408052e-02, 'g_hg_out': 5.356912e+01, 'p_a': 1.617909e-01, 'p_b': 1.388404e-01, 'w_o': 2.131281e-01, 'g_ffn': 2.632107e+01, 'w_up': 1.633346e-01, 'conv_w': 3.665096e+00, 'conv_b': 3.230194e+00, 'w_down': 2.258222e-01}


def _to_microbatches(a, axis):
    t = _jnp.moveaxis(a, axis, 0)
    t = t.reshape((N_MICROBATCH, t.shape[0] // N_MICROBATCH) + t.shape[1:])
    return _jnp.moveaxis(t, 1, axis + 1)


def setup_inputs(seed: int = 0) -> dict:
    inp = _fwd_setup_inputs(seed)
    key = _jax.random.fold_in(_jax.random.key(seed), 7919)
    shape, _ = _output_shape()
    out = dict(inp)
    out["loss_target"] = _jax.random.normal(_jax.random.fold_in(key, 0), shape, _jnp.float32)
    for i, name in enumerate(TWIN_WEIGHTS):
        w = inp[name].astype(_jnp.float32)
        if MOMENT_SCALE is None:
            s = _jnp.sqrt(_jnp.mean(_jnp.square(w)) + 1e-30)
        else:
            s = MOMENT_SCALE[name]
        km, kv = _jax.random.split(_jax.random.fold_in(key, i + 1))
        out[name] = w
        out["m_" + name] = s * _jax.random.normal(km, w.shape, _jnp.float32)
        out["v_" + name] = (s * s) * _jax.random.uniform(kv, w.shape, _jnp.float32, 0.5, 1.5)
    if N_MICROBATCH > 1:
        for name, axis in PER_EXAMPLE_BATCH_AXIS.items():
            out[name] = _to_microbatches(out[name], axis)
    return {'x': out['x'], 'g_mix': out['g_mix'], 'w_in': out['w_in'], 'g_q': out['g_q'], 'g_k': out['g_k'], 'lb_logits': out['lb_logits'], 'g_hg_out': out['g_hg_out'], 'p_a': out['p_a'], 'p_b': out['p_b'], 'w_o': out['w_o'], 'g_ffn': out['g_ffn'], 'w_up': out['w_up'], 'conv_w': out['conv_w'], 'conv_b': out['conv_b'], 'w_down': out['w_down'], 'loss_target': out['loss_target'], 'm_g_mix': out['m_g_mix'], 'm_w_in': out['m_w_in'], 'm_g_q': out['m_g_q'], 'm_g_k': out['m_g_k'], 'm_lb_logits': out['m_lb_logits'], 'm_g_hg_out': out['m_g_hg_out'], 'm_p_a': out['m_p_a'], 'm_p_b': out['m_p_b'], 'm_w_o': out['m_w_o'], 'm_g_ffn': out['m_g_ffn'], 'm_w_up': out['m_w_up'], 'm_conv_w': out['m_conv_w'], 'm_conv_b': out['m_conv_b'], 'm_w_down': out['m_w_down'], 'v_g_mix': out['v_g_mix'], 'v_w_in': out['v_w_in'], 'v_g_q': out['v_g_q'], 'v_g_k': out['v_g_k'], 'v_lb_logits': out['v_lb_logits'], 'v_g_hg_out': out['v_g_hg_out'], 'v_p_a': out['v_p_a'], 'v_p_b': out['v_p_b'], 'v_w_o': out['v_w_o'], 'v_g_ffn': out['v_g_ffn'], 'v_w_up': out['v_w_up'], 'v_conv_w': out['v_conv_w'], 'v_conv_b': out['v_conv_b'], 'v_w_down': out['v_w_down']}


def _loss(weights, diff, rest, loss_target):
    with _jax.named_scope("forward"):
        args = {**rest, TWIN_DIFF_INPUT: diff, **{k: w.astype(_WEIGHT_DTYPES[k]) for k, w in weights.items()}}
        y = _forward(args)
    with _jax.named_scope("loss_head"):
        err = _jnp.square(y.astype(_jnp.float32) - loss_target)
        return 0.5 * _jnp.sum(_jnp.mean(err, axis=-1)) if err.ndim else 0.5 * err


def _adamw(w, g, m, v):
    m = ADAM_B1 * m + (1.0 - ADAM_B1) * g
    v = ADAM_B2 * v + (1.0 - ADAM_B2) * _jnp.square(g)
    m_hat = m / (1.0 - ADAM_B1 ** ADAM_STEP)
    v_hat = v / (1.0 - ADAM_B2 ** ADAM_STEP)
    delta = -ADAM_LR * (m_hat / (_jnp.sqrt(v_hat) + ADAM_EPS) + ADAM_WD * w)
    return delta, m, v


def reference(x, g_mix, w_in, g_q, g_k, lb_logits, g_hg_out, p_a, p_b, w_o, g_ffn, w_up, conv_w, conv_b, w_down, loss_target, m_g_mix, m_w_in, m_g_q, m_g_k, m_lb_logits, m_g_hg_out, m_p_a, m_p_b, m_w_o, m_g_ffn, m_w_up, m_conv_w, m_conv_b, m_w_down, v_g_mix, v_w_in, v_g_q, v_g_k, v_lb_logits, v_g_hg_out, v_p_a, v_p_b, v_w_o, v_g_ffn, v_w_up, v_conv_w, v_conv_b, v_w_down):
    given = dict(x=x, g_mix=g_mix, w_in=w_in, g_q=g_q, g_k=g_k, lb_logits=lb_logits, g_hg_out=g_hg_out, p_a=p_a, p_b=p_b, w_o=w_o, g_ffn=g_ffn, w_up=w_up, conv_w=conv_w, conv_b=conv_b, w_down=w_down, loss_target=loss_target, m_g_mix=m_g_mix, m_w_in=m_w_in, m_g_q=m_g_q, m_g_k=m_g_k, m_lb_logits=m_lb_logits, m_g_hg_out=m_g_hg_out, m_p_a=m_p_a, m_p_b=m_p_b, m_w_o=m_w_o, m_g_ffn=m_g_ffn, m_w_up=m_w_up, m_conv_w=m_conv_w, m_conv_b=m_conv_b, m_w_down=m_w_down, v_g_mix=v_g_mix, v_w_in=v_w_in, v_g_q=v_g_q, v_g_k=v_g_k, v_lb_logits=v_lb_logits, v_g_hg_out=v_g_hg_out, v_p_a=v_p_a, v_p_b=v_p_b, v_w_o=v_w_o, v_g_ffn=v_g_ffn, v_w_up=v_w_up, v_conv_w=v_conv_w, v_conv_b=v_conv_b, v_w_down=v_w_down)
    weights = {n: given[n] for n in TWIN_WEIGHTS}
    shared = {n: given[n] for n in SHARED_INPUTS}
    per_example = {n: given[n] for n in ['x']}
    grad_fn = _jax.value_and_grad(_loss, argnums=(0, 1))

    def one_microbatch(ex, loss_target):
        ex = dict(ex)
        diff = ex.pop(TWIN_DIFF_INPUT)
        return grad_fn(weights, diff, {**shared, **ex}, loss_target)

    if N_MICROBATCH == 1:
        loss, (grad_w, grad_x) = one_microbatch(per_example, given["loss_target"])
    else:
        def body(carry, xs):
            loss_sum, grad_sum = carry
            l_k, (gw_k, gx_k) = one_microbatch(xs[0], xs[1])
            with _jax.named_scope("update"):
                return (loss_sum + l_k, _jax.tree.map(_jnp.add, grad_sum, gw_k)), gx_k

        init = (_jnp.zeros((), _jnp.float32), _jax.tree.map(_jnp.zeros_like, weights))
        (loss, grad_w), grad_x = _jax.lax.scan(body, init, (per_example, given["loss_target"]))
    with _jax.named_scope("update"):
        delta_w, new_m, new_v = {}, {}, {}
        for n in TWIN_WEIGHTS:
            delta_w[n], new_m[n], new_v[n] = _adamw(weights[n], grad_w[n], given["m_" + n], given["v_" + n])
    return (loss, grad_x, *[grad_w[n] for n in TWIN_WEIGHTS], *[delta_w[n] for n in TWIN_WEIGHTS],
            *[new_m[n] for n in TWIN_WEIGHTS], *[new_v[n] for n in TWIN_WEIGHTS])
```

```python
import functools

import numpy as np
import jax
import jax.numpy as jnp
from jax import lax
from jax.experimental import pallas as pl
from jax.experimental.pallas import tpu as pltpu

F32 = jnp.float32
BF16 = jnp.bfloat16

N_DEV = 8
HEAD_DIM = 128
HG_CHUNK = 64
HG_LEVELS = (1, 2, 4, 8, 16, 32)
EPS = 1e-6
ADAM_LR = 0.001
ADAM_B1 = 0.9
ADAM_B2 = 0.999
ADAM_EPS = 1e-08
ADAM_WD = 0.01
ADAM_STEP = 10
LANE = 128
SUBLANE = 8
VMEM_CAP = 56 << 20
MESH_AXES = ("x", "y", "c")
MESH_ID = pl.DeviceIdType.MESH


def _tile(dim, pref, align):
    if dim <= pref:
        return dim
    t = (pref // align) * align
    while t >= align:
        if dim % t == 0:
            return t
        t -= align
    return dim


def _vmem(est_bytes):
    return int(min(max(2 * est_bytes + (8 << 20), 32 << 20), VMEM_CAP))


def _params(sem, est_bytes):
    return pltpu.CompilerParams(dimension_semantics=sem, vmem_limit_bytes=_vmem(est_bytes))


def _sigmoid(x):
    return 1.0 / (1.0 + jnp.exp(-x))


_DIMS = {"nn": (((1,), (0,)), ((), ())), "nt": (((1,), (1,)), ((), ())), "tn": (((0,), (0,)), ((), ()))}


def _matmul(a, b, *, mode, name, out_dtype=F32, add=None, b_shards=False, out_shards=False, lead_a=None,
            lead_b=None, tm_pref=512, tn_pref=1408, tk_pref=1408):
    if mode == "tn":
        K, M = a.shape[-2:]
    else:
        M, K = a.shape[-2:]
    if lead_b is not None:
        assert not b_shards and mode == "tn"
    if b_shards:
        S = b.shape[0]
        if mode == "nn":
            assert b.shape[1] == K
            N, tn, tk = S * b.shape[2], b.shape[2], _tile(K, tk_pref, LANE)
        else:
            assert mode == "nt" and S * b.shape[2] == K
            N, tk, tn = b.shape[1], b.shape[2], _tile(b.shape[1], tn_pref, LANE)
    else:
        if mode == "nt":
            N = b.shape[0]
            assert b.shape[1] == K
        else:
            N = b.shape[-1]
            assert b.shape[-2] == K
        tn = _tile(N, tn_pref, LANE)
        tk = _tile(K, tk_pref, LANE)
    if out_shards:
        assert N % N_DEV == 0
        tn = N // N_DEV
    tm = _tile(M, tm_pref, LANE)
    nm, nn, nk = M // tm, N // tn, K // tk
    assert nm * tm == M and nn * tn == N and nk * tk == K

    if mode == "tn":
        a_spec = pl.BlockSpec((tk, tm), lambda i, j, k: (k, i))
    elif lead_a is not None:
        a_spec = pl.BlockSpec((None, tm, tk), lambda i, j, k: (lead_a, i, k))
    else:
        a_spec = pl.BlockSpec((tm, tk), lambda i, j, k: (i, k))
    if lead_b is not None:
        b_spec = pl.BlockSpec((None, tk, tn), lambda i, j, k: (lead_b, k, j))
    elif b_shards and mode == "nn":
        b_spec = pl.BlockSpec((None, tk, tn), lambda i, j, k: (j, k, 0))
    elif b_shards:
        b_spec = pl.BlockSpec((None, tn, tk), lambda i, j, k: (k, j, 0))
    elif mode == "nt":
        b_spec = pl.BlockSpec((tn, tk), lambda i, j, k: (j, k))
    else:
        b_spec = pl.BlockSpec((tk, tn), lambda i, j, k: (k, j))
    in_specs = [a_spec, b_spec]
    operands = [a, b]
    if add is not None:
        assert not out_shards and add.shape == (M, N)
        in_specs.append(pl.BlockSpec((tm, tn), lambda i, j, k: (i, j)))
        operands.append(add)
    if out_shards:
        out_shape = jax.ShapeDtypeStruct((N_DEV, M, tn), out_dtype)
        out_spec = pl.BlockSpec((None, tm, tn), lambda i, j, k: (j, i, 0))
    else:
        out_shape = jax.ShapeDtypeStruct((M, N), out_dtype)
        out_spec = pl.BlockSpec((tm, tn), lambda i, j, k: (i, j))
    dims = _DIMS[mode]
    has_add = add is not None

    def body(*refs):
        a_ref, b_ref = refs[0], refs[1]
        add_ref = refs[2] if has_add else None
        o_ref = refs[3] if has_add else refs[2]
        acc_ref = refs[-1]
        k = pl.program_id(2)
        part = lax.dot_general(a_ref[...], b_ref[...], dims, preferred_element_type=F32)

        def finish(total):
            if has_add:
                total = add_ref[...] + total
            o_ref[...] = total.astype(out_dtype)

        if nk == 1:
            finish(part)
        else:
            @pl.when(k == 0)
            def _():
                acc_ref[...] = part

            @pl.when(jnp.logical_and(k > 0, k < nk - 1))
            def _():
                acc_ref[...] += part

            @pl.when(k == nk - 1)
            def _():
                finish(acc_ref[...] + part)

    est = 2 * (tm * tk * 2 + tk * tn * 2 + tm * tn * 4 * (2 if has_add else 1)) + tm * tn * 4 * 2
    return pl.pallas_call(
        body, name=name, grid=(nm, nn, nk), in_specs=in_specs, out_specs=out_spec, out_shape=out_shape,
        scratch_shapes=[pltpu.VMEM((tm, tn) if nk > 1 else (SUBLANE, LANE), F32)],
        compiler_params=_params(("parallel", "parallel", "arbitrary"), est),
    )(*operands)


def _rmsnorm_fwd(x, g, name):
    T, D = x.shape
    tr = _tile(T, 256, SUBLANE)

    def body(x_ref, g_ref, o_ref):
        xf = x_ref[...]
        r = lax.rsqrt(jnp.mean(xf * xf, axis=-1, keepdims=True) + EPS)
        o_ref[...] = ((xf * r) * g_ref[...]).astype(BF16)

    return pl.pallas_call(
        body, name=name, grid=(T // tr,),
        in_specs=[pl.BlockSpec((tr, D), lambda i: (i, 0)), pl.BlockSpec((1, D), lambda i: (0, 0))],
        out_specs=pl.BlockSpec((tr, D), lambda i: (i, 0)), out_shape=jax.ShapeDtypeStruct((T, D), BF16),
        compiler_params=_params(("parallel",), tr * D * 6 * 2),
    )(x, g)


def _rmsnorm_bwd(x, g, dy, res, name):
    T, D = x.shape
    tr = _tile(T, 256, SUBLANE)
    nsteps = T // tr

    def body(x_ref, g_ref, dy_ref, res_ref, dx_ref, dxb_ref, dg_ref, acc_ref):
        i = pl.program_id(0)
        xf = x_ref[...]
        dyf = dy_ref[...].astype(F32)
        r = lax.rsqrt(jnp.mean(xf * xf, axis=-1, keepdims=True) + EPS)
        w = dyf * g_ref[...]
        s = jnp.mean(xf * w, axis=-1, keepdims=True)
        dx = res_ref[...] + (r * w - xf * (r * r * r * s))
        dx_ref[...] = dx
        dxb_ref[...] = dx.astype(BF16)
        part = jnp.sum((dyf * (xf * r)).reshape(tr // SUBLANE, SUBLANE, D), axis=0)

        @pl.when(i == 0)
        def _():
            acc_ref[...] = part

        @pl.when(i > 0)
        def _():
            acc_ref[...] += part

        @pl.when(i == nsteps - 1)
        def _():
            dg_ref[...] = jnp.sum(acc_ref[...], axis=0, keepdims=True)

    row = pl.BlockSpec((tr, D), lambda i: (i, 0))
    vec = pl.BlockSpec((1, D), lambda i: (0, 0))
    return pl.pallas_call(
        body, name=name, grid=(nsteps,), in_specs=[row, vec, row, row], out_specs=[row, row, vec],
        out_shape=[jax.ShapeDtypeStruct((T, D), F32), jax.ShapeDtypeStruct((T, D), BF16),
                   jax.ShapeDtypeStruct((1, D), F32)],
        scratch_shapes=[pltpu.VMEM((SUBLANE, D), F32)],
        compiler_params=_params(("arbitrary",), tr * D * 18 * 2),
    )(x, g, dy, res)


def _qkv_prep(proj, gains, width, name):
    T = proj.shape[0]
    tr = _tile(T, 256, SUBLANE)
    heads = width // HEAD_DIM

    def body(p_ref, g_ref, o_ref):
        j = pl.program_id(1)
        g = g_ref[...]
        for h in range(heads):
            xh = p_ref[:, h * HEAD_DIM:(h + 1) * HEAD_DIM]
            r = lax.rsqrt(jnp.mean(xh * xh, axis=-1, keepdims=True) + EPS)
            r = jnp.where(j < 2, r, 1.0)
            o_ref[:, h * HEAD_DIM:(h + 1) * HEAD_DIM] = ((xh * r) * g).astype(BF16)

    return pl.pallas_call(
        body, name=name, grid=(T // tr, 3),
        in_specs=[pl.BlockSpec((tr, width), lambda i, j: (i, j)),
                  pl.BlockSpec((None, 1, HEAD_DIM), lambda i, j: (j, 0, 0))],
        out_specs=pl.BlockSpec((tr, width), lambda i, j: (i, j)),
        out_shape=jax.ShapeDtypeStruct((T, 3 * width), BF16),
        compiler_params=_params(("parallel", "parallel"), tr * width * 6 * 2),
    )(proj, gains)


def _qkv_bwd(proj, gains, dq, dk, dv, dproj, name):
    T = proj.shape[0]
    width = dq.shape[1]
    tr = _tile(T, 256, SUBLANE)
    heads = width // HEAD_DIM
    nrow = T // tr

    def body(p_ref, g_ref, dq_ref, dk_ref, dv_ref, dp_in, o_ref, dg_ref, acc_ref):
        del dp_in
        i = pl.program_id(0)
        j = pl.program_id(1)
        g = g_ref[...]

        @pl.when(jnp.logical_and(i == 0, j == 0))
        def _():
            acc_ref[...] = jnp.zeros_like(acc_ref)

        part = jnp.zeros((SUBLANE, HEAD_DIM), F32)
        for h in range(heads):
            sl = slice(h * HEAD_DIM, (h + 1) * HEAD_DIM)
            xh = p_ref[:, sl]
            dyh = jnp.where(j == 0, dq_ref[:, sl], jnp.where(j == 1, dk_ref[:, sl], dv_ref[:, sl]))
            r = lax.rsqrt(jnp.mean(xh * xh, axis=-1, keepdims=True) + EPS)
            w = dyh * g
            s = jnp.mean(xh * w, axis=-1, keepdims=True)
            dx = r * w - xh * (r * r * r * s)
            o_ref[:, sl] = jnp.where(j < 2, dx, dyh).astype(BF16)
            part = part + jnp.sum((dyh * (xh * r)).reshape(tr // SUBLANE, SUBLANE, HEAD_DIM), axis=0)
        acc_ref[j] += part

        @pl.when(jnp.logical_and(i == nrow - 1, j == 2))
        def _():
            dg_ref[...] = jnp.sum(acc_ref[...], axis=1, keepdims=True)

    blk = pl.BlockSpec((tr, width), lambda i, j: (i, j))
    dblk = pl.BlockSpec((tr, width), lambda i, j: (i, 0))
    return pl.pallas_call(
        body, name=name, grid=(nrow, 3),
        in_specs=[blk, pl.BlockSpec((None, 1, HEAD_DIM), lambda i, j: (j, 0, 0)), dblk, dblk, dblk,
                  pl.BlockSpec(memory_space=pl.ANY)],
        out_specs=[blk, pl.BlockSpec((3, 1, HEAD_DIM), lambda i, j: (0, 0, 0))],
        out_shape=[jax.ShapeDtypeStruct(dproj.shape, BF16), jax.ShapeDtypeStruct((3, 1, HEAD_DIM), F32)],
        input_output_aliases={5: 0},
        scratch_shapes=[pltpu.VMEM((3, SUBLANE, HEAD_DIM), F32)],
        compiler_params=_params(("arbitrary", "arbitrary"), tr * width * 18 * 2),
    )(proj, gains, dq, dk, dv, dproj)


def _split2(x):
    hi = x.astype(BF16)
    lo = (x - hi.astype(F32)).astype(BF16)
    return hi, lo


def _sb_scores(q, kb, mask):
    z = lax.dot_general(q, kb, _DIMS["nt"], preferred_element_type=F32) * (HEAD_DIM ** -0.5)
    lk = -(jnp.maximum(z, 0.0) + jnp.log(1.0 + jnp.exp(-jnp.abs(z))))
    if mask is not None:
        lk = jnp.where(mask, lk, 0.0)
    return z, lk, lk + z


def _sb_block_size(T):
    return _tile(T, 256, LANE)


def _sb_fwd(qkv, width, name):
    T = qkv.shape[0]
    heads = width // HEAD_DIM
    bq = _sb_block_size(T)

    def body(q_ref, k_ref, v_ref, o_ref, o32_ref):
        i = pl.program_id(1)
        q = q_ref[...]
        row = lax.broadcasted_iota(jnp.int32, (bq, bq), 0)
        col = lax.broadcasted_iota(jnp.int32, (bq, bq), 1)
        upper = (row > col).astype(BF16)
        causal = col < row

        def block(j, carry, acc, mask):
            start = pl.multiple_of(j * bq, bq)
            kb = k_ref[pl.ds(start, bq), :]
            vb = v_ref[pl.ds(start, bq), :]
            _, lk, lb = _sb_scores(q, kb, mask)
            hi, lo = _split2(lk)
            later = (jnp.dot(hi, upper, preferred_element_type=F32)
                     + jnp.dot(lo, upper, preferred_element_type=F32)) + carry
            w = jnp.exp(lb + later)
            if mask is not None:
                w = jnp.where(mask, w, 0.0)
            whi, wlo = _split2(w)
            acc = acc + (jnp.dot(whi, vb, preferred_element_type=F32) + jnp.dot(wlo, vb, preferred_element_type=F32))
            return carry + jnp.sum(lk, axis=1, keepdims=True), acc

        carry, acc = block(i, jnp.zeros((bq, 1), F32), jnp.zeros((bq, HEAD_DIM), F32), causal)

        def step(n, state):
            return block(i - 1 - n, state[0], state[1], None)

        carry, acc = lax.fori_loop(0, i, step, (carry, acc))
        o_ref[...] = acc.astype(BF16)
        o32_ref[...] = acc

    oblk = pl.BlockSpec((bq, HEAD_DIM), lambda h, i: (i, h))
    return pl.pallas_call(
        body, name=name, grid=(heads, T // bq),
        in_specs=[pl.BlockSpec((bq, HEAD_DIM), lambda h, i: (i, h)),
                  pl.BlockSpec((T, HEAD_DIM), lambda h, i: (0, heads + h)),
                  pl.BlockSpec((T, HEAD_DIM), lambda h, i: (0, 2 * heads + h))],
        out_specs=[oblk, oblk],
        out_shape=[jax.ShapeDtypeStruct((T, width), BF16), jax.ShapeDtypeStruct((T, width), F32)],
        compiler_params=_params(("parallel", "arbitrary"), 2 * T * HEAD_DIM * 2 * 2 + 16 * bq * bq * 4),
    )(qkv, qkv, qkv)


def _sb_bwd(qkv, y, dy, width, name):
    T = qkv.shape[0]
    heads = width // HEAD_DIM
    bq = _sb_block_size(T)
    scale = HEAD_DIM ** -0.5

    def body(q_ref, k_ref, v_ref, y_ref, dy_ref, dq_ref, dk_ref, dv_ref):
        i = pl.program_id(1)
        q = q_ref[...]
        do = dy_ref[...]
        delta = jnp.sum(do.astype(F32) * y_ref[...].astype(F32), axis=1, keepdims=True)
        row = lax.broadcasted_iota(jnp.int32, (bq, bq), 0)
        col = lax.broadcasted_iota(jnp.int32, (bq, bq), 1)
        upper = (row > col).astype(BF16)
        upper_incl = (row >= col).astype(BF16)
        causal = col < row

        @pl.when(i == 0)
        def _():
            dk_ref[...] = jnp.zeros_like(dk_ref)
            dv_ref[...] = jnp.zeros_like(dv_ref)

        def block(j, carry, gcarry, dq, mask):
            start = pl.multiple_of(j * bq, bq)
            kb = k_ref[pl.ds(start, bq), :]
            vb = v_ref[pl.ds(start, bq), :]
            _, lk, lb = _sb_scores(q, kb, mask)
            hi, lo = _split2(lk)
            later = (jnp.dot(hi, upper, preferred_element_type=F32)
                     + jnp.dot(lo, upper, preferred_element_type=F32)) + carry
            w = jnp.exp(lb + later)
            if mask is not None:
                w = jnp.where(mask, w, 0.0)
            dw = lax.dot_general(do, vb, _DIMS["nt"], preferred_element_type=F32)
            g = dw * w
            ghi, glo = _split2(g)
            gsuf = (jnp.dot(ghi, upper_incl, preferred_element_type=F32)
                    + jnp.dot(glo, upper_incl, preferred_element_type=F32)) + gcarry
            beta = jnp.exp(lb)
            dz = g - beta * (g + (delta - gsuf))
            if mask is not None:
                dz = jnp.where(mask, dz, 0.0)
            dzs = (dz * scale).astype(BF16)
            dq = dq + jnp.dot(dzs, kb, preferred_element_type=F32)
            dk_ref[pl.ds(start, bq), :] += lax.dot_general(dzs, q, _DIMS["tn"], preferred_element_type=F32)
            dv_ref[pl.ds(start, bq), :] += lax.dot_general(w.astype(BF16), do, _DIMS["tn"],
                                                           preferred_element_type=F32)
            return (carry + jnp.sum(lk, axis=1, keepdims=True), gcarry + jnp.sum(g, axis=1, keepdims=True), dq)

        zero = jnp.zeros((bq, 1), F32)
        state = block(i, zero, zero, jnp.zeros((bq, HEAD_DIM), F32), causal)

        def step(n, state):
            return block(i - 1 - n, state[0], state[1], state[2], None)

        state = lax.fori_loop(0, i, step, state)
        dq_ref[...] = state[2]

    qblk = pl.BlockSpec((bq, HEAD_DIM), lambda h, i: (i, h))
    full = pl.BlockSpec((T, HEAD_DIM), lambda h, i: (0, h))
    out = jax.ShapeDtypeStruct((T, width), F32)
    return pl.pallas_call(
        body, name=name, grid=(heads, T // bq),
        in_specs=[qblk, pl.BlockSpec((T, HEAD_DIM), lambda h, i: (0, heads + h)),
                  pl.BlockSpec((T, HEAD_DIM), lambda h, i: (0, 2 * heads + h)), qblk, qblk],
        out_specs=[qblk, full, full], out_shape=[out, out, out],
        compiler_params=_params(("parallel", "arbitrary"), 2 * T * HEAD_DIM * 12 + 24 * bq * bq * 4),
    )(qkv, qkv, qkv, y, dy)


def _hg_constants():
    C = HG_CHUNK
    t = np.arange(C)[:, None]
    j = np.arange(C)[None, :]
    blocks = [(j <= t), (j > t)]
    masks = []
    for n in HG_LEVELS:
        right = (t % (2 * n)) >= n
        mid = (t // (2 * n)) * (2 * n) + n - 1
        blocks.append(right & (j > mid) & (j <= t))
        blocks.append((~right) & (j > t) & (j <= mid))
        tt, ss = np.arange(C)[:, None], np.arange(C)[None, :]
        same = (tt // (2 * n)) == (ss // (2 * n))
        masks.append(same & ((tt % (2 * n)) >= n) & ((ss % (2 * n)) < n))
    sums = np.concatenate(blocks, axis=0).astype(np.float32)
    return jnp.asarray(sums, BF16), jnp.asarray(np.stack(masks).astype(np.float32))


def _split3(x):
    hi = x.astype(BF16)
    r1 = x - hi.astype(F32)
    mid = r1.astype(BF16)
    lo = (r1 - mid.astype(F32)).astype(BF16)
    return hi, mid, lo


def _hg_gates(hq, hf, lb):
    sq = _sigmoid(hq)
    sf = _sigmoid(hf)
    f = lb + (1.0 - lb) * sf
    return hq * sq, sq, f, sf, 1.0 - f, jnp.log(f)


def _hg_exponents(sums, g):
    hi, mid, lo = _split3(g)
    return (jnp.dot(sums, hi, preferred_element_type=F32) + jnp.dot(sums, mid, preferred_element_type=F32)
            + jnp.dot(sums, lo, preferred_element_type=F32))


def _hg_intra(q, k, e_all, masks_ref):
    C = HG_CHUNK
    a = jnp.zeros((C, C), F32)
    parts = []
    for li in range(len(HG_LEVELS)):
        eq = jnp.exp(e_all[(2 + 2 * li) * C:(3 + 2 * li) * C])
        ek = jnp.exp(e_all[(3 + 2 * li) * C:(4 + 2 * li) * C])
        qt = q * eq
        kt = k * ek
        a = a + masks_ref[li] * lax.dot_general(qt.astype(BF16), kt.astype(BF16), _DIMS["nt"],
                                                preferred_element_type=F32)
        parts.append((eq, ek, qt, kt))
    return a, parts


def _lower_bound(lbl_ref):
    return _sigmoid(lbl_ref[0:1, :] - lbl_ref[1:2, :])


def _hg_fwd(proj, lb_logits, g_out, col0, width, name):
    T = proj.shape[0]
    heads = width // HEAD_DIM
    C = HG_CHUNK
    tb = _tile(T, 512, C)
    cpb = tb // C
    nb = T // tb
    sums, masks = _hg_constants()
    cb = col0 // HEAD_DIM
    assert col0 % HEAD_DIM == 0

    def body(hq_ref, hf_ref, hi_ref, og_ref, lbl_ref, go_ref, sums_ref, masks_ref, y_ref, o_ref, st_ref, s_ref):
        i = pl.program_id(1)

        @pl.when(i == 0)
        def _():
            s_ref[...] = jnp.zeros_like(s_ref)

        lb = _lower_bound(lbl_ref)
        go = go_ref[...]

        def chunk(c, _):
            rows = pl.ds(pl.multiple_of(c * C, C), C)
            q, _, _, _, k, g = _hg_gates(hq_ref[rows, :], hf_ref[rows, :], lb)
            v = hi_ref[rows, :]
            vb = v.astype(BF16)
            st = s_ref[...]
            st_ref[c] = st
            e_all = _hg_exponents(sums_ref[...], g)
            b = e_all[0:C]
            suf = e_all[C:2 * C]
            ebl = jnp.exp(b[C - 1:C, :])
            qe = q * jnp.exp(b)
            o = lax.dot_general(qe.astype(BF16), st.astype(BF16), _DIMS["nt"], preferred_element_type=F32)
            a, _ = _hg_intra(q, k, e_all, masks_ref)
            o = o + jnp.dot(a.astype(BF16), vb, preferred_element_type=F32)
            o = o + jnp.sum(q * k, axis=1, keepdims=True) * v
            kd = k * jnp.exp(suf)
            s_ref[...] = st * ebl + lax.dot_general(vb, kd.astype(BF16), _DIMS["tn"], preferred_element_type=F32)
            o_ref[rows, :] = o
            r = lax.rsqrt(jnp.mean(o * o, axis=-1, keepdims=True) + EPS)
            og = og_ref[rows, :]
            y_ref[rows, :] = (((o * r) * go) * (og * _sigmoid(og))).astype(BF16)
            return 0

        lax.fori_loop(0, cpb, chunk, 0)

    def col(k):
        return pl.BlockSpec((tb, HEAD_DIM), lambda h, i: (i, cb + k * heads + h))

    nsum = sums.shape[0]
    out_blk = pl.BlockSpec((tb, HEAD_DIM), lambda h, i: (i, h))
    return pl.pallas_call(
        body, name=name, grid=(heads, nb),
        in_specs=[col(0), col(1), col(2), col(3),
                  pl.BlockSpec((2, HEAD_DIM), lambda h, i: (0, h)),
                  pl.BlockSpec((1, HEAD_DIM), lambda h, i: (0, 0)),
                  pl.BlockSpec((nsum, C), lambda h, i: (0, 0)),
                  pl.BlockSpec((len(HG_LEVELS), C, C), lambda h, i: (0, 0, 0))],
        out_specs=[out_blk, out_blk,
                   pl.BlockSpec((None, cpb, HEAD_DIM, HEAD_DIM), lambda h, i: (h, i, 0, 0))],
        out_shape=[jax.ShapeDtypeStruct((T, width), BF16), jax.ShapeDtypeStruct((T, width), F32),
                   jax.ShapeDtypeStruct((heads, T // C, HEAD_DIM, HEAD_DIM), F32)],
        scratch_shapes=[pltpu.VMEM((HEAD_DIM, HEAD_DIM), F32)],
        compiler_params=_params(("parallel", "arbitrary"), tb * HEAD_DIM * 4 * 7 + cpb * HEAD_DIM * HEAD_DIM * 4),
    )(proj, proj, proj, proj, lb_logits, g_out, sums, masks)


def _hg_bwd(proj, lb_logits, g_out, o_saved, states, dy, dproj, col0, width, name):
    T = proj.shape[0]
    heads = width // HEAD_DIM
    C = HG_CHUNK
    tb = _tile(T, 512, C)
    cpb = tb // C
    nb = T // tb
    sums, masks = _hg_constants()
    nsum = sums.shape[0]
    cb = col0 // HEAD_DIM
    nlev = len(HG_LEVELS)

    def body(hq_ref, hf_ref, hi_ref, og_ref, lbl_ref, go_ref, sums_ref, masks_ref, o_ref, st_ref, dy_ref, dp_in,
             dp_ref, dlb_ref, dgo_ref, ds_ref, de_ref, dlb_acc, dgo_acc, dout_ref):
        del dp_in
        h = pl.program_id(0)
        i = pl.program_id(1)
        kk = pl.program_id(2)

        @pl.when(jnp.logical_and(i == 0, kk == 0))
        def _():
            ds_ref[...] = jnp.zeros_like(ds_ref)
            dlb_acc[...] = jnp.zeros_like(dlb_acc)

        @pl.when(jnp.logical_and(jnp.logical_and(h == 0, i == 0), kk == 0))
        def _():
            dgo_acc[...] = jnp.zeros_like(dgo_acc)

        lb = _lower_bound(lbl_ref)
        go = go_ref[...]
        last_row = lax.broadcasted_iota(jnp.int32, (C, HEAD_DIM), 0) == C - 1

        def chunk(n, _):
            c = cpb - 1 - n
            rows = pl.ds(pl.multiple_of(c * C, C), C)
            hq = hq_ref[rows, :]
            og = og_ref[rows, :]
            q, sq, f, sf, k, g = _hg_gates(hq, hf_ref[rows, :], lb)
            v = hi_ref[rows, :]
            vb = v.astype(BF16)
            st = st_ref[c]
            stb = st.astype(BF16)
            dst = ds_ref[...]
            dstb = dst.astype(BF16)
            o = o_ref[rows, :]
            dyc = dy_ref[rows, :].astype(F32)
            sg = _sigmoid(og)
            r = lax.rsqrt(jnp.mean(o * o, axis=-1, keepdims=True) + EPS)
            on = (o * r) * go
            don = dyc * (og * sg)
            dout_ref[3, rows, :] = (dyc * on * (sg * (1.0 + og * (1.0 - sg)))).astype(BF16)
            dgo_acc[...] += jnp.sum((don * (o * r)).reshape(C // SUBLANE, SUBLANE, HEAD_DIM), axis=0)
            wn = don * go
            do = r * wn - o * (r * r * r * jnp.mean(o * wn, axis=-1, keepdims=True))
            dob = do.astype(BF16)
            e_all = _hg_exponents(sums_ref[...], g)
            eb = jnp.exp(e_all[0:C])
            esuf = jnp.exp(e_all[C:2 * C])
            ebl = eb[C - 1:C, :]
            qe = q * eb
            kd = k * esuf
            a, parts = _hg_intra(q, k, e_all, masks_ref)
            dv = lax.dot_general(a.astype(BF16), dob, _DIMS["tn"], preferred_element_type=F32)
            qk = jnp.sum(q * k, axis=1, keepdims=True)
            dv = dv + qk * do + lax.dot_general(kd.astype(BF16), dstb, _DIMS["nt"], preferred_element_type=F32)
            dqk = jnp.sum(do * v, axis=1, keepdims=True)
            dq = dqk * k
            dk = dqk * q
            da = lax.dot_general(dob, vb, _DIMS["nt"], preferred_element_type=F32)
            for li in range(nlev):
                eq, ek, qt, kt = parts[li]
                dan = (masks_ref[li] * da).astype(BF16)
                dqt = jnp.dot(dan, kt.astype(BF16), preferred_element_type=F32)
                dkt = lax.dot_general(dan, qt.astype(BF16), _DIMS["tn"], preferred_element_type=F32)
                dq = dq + dqt * eq
                dk = dk + dkt * ek
                de_ref[(2 + 2 * li) * C:(3 + 2 * li) * C, :] = dqt * qt
                de_ref[(3 + 2 * li) * C:(4 + 2 * li) * C, :] = dkt * kt
            dqe = jnp.dot(dob, stb, preferred_element_type=F32)
            dq = dq + dqe * eb
            dkd = jnp.dot(vb, dstb, preferred_element_type=F32)
            dk = dk + dkd * esuf
            debl = jnp.sum(dst * st, axis=0, keepdims=True)
            de_ref[0:C, :] = dqe * qe + jnp.where(last_row, debl * ebl, 0.0)
            de_ref[C:2 * C, :] = dkd * kd
            ds_ref[...] = dst * ebl + lax.dot_general(dob, qe.astype(BF16), _DIMS["tn"], preferred_element_type=F32)
            dehi, delo = _split2(de_ref[...])
            dg = (lax.dot_general(sums_ref[...], dehi, _DIMS["tn"], preferred_element_type=F32)
                  + lax.dot_general(sums_ref[...], delo, _DIMS["tn"], preferred_element_type=F32))
            df = dg / f - dk
            dout_ref[0, rows, :] = (dq * (sq * (1.0 + hq * (1.0 - sq)))).astype(BF16)
            dout_ref[1, rows, :] = (df * (1.0 - lb) * (sf * (1.0 - sf))).astype(BF16)
            dout_ref[2, rows, :] = dv.astype(BF16)
            dlb_acc[...] += jnp.sum((df * (1.0 - sf)).reshape(C // SUBLANE, SUBLANE, HEAD_DIM), axis=0)
            return 0

        @pl.when(kk == 0)
        def _():
            lax.fori_loop(0, cpb, chunk, 0)

        dp_ref[...] = dout_ref[kk]

        @pl.when(jnp.logical_and(i == nb - 1, kk == 3))
        def _():
            dl0 = jnp.sum(dlb_acc[...], axis=0, keepdims=True) * (lb * (1.0 - lb))
            dlb_ref[0:1, :] = dl0
            dlb_ref[1:2, :] = -dl0

        @pl.when(jnp.logical_and(jnp.logical_and(h == heads - 1, i == nb - 1), kk == 3))
        def _():
            dgo_ref[...] = jnp.sum(dgo_acc[...], axis=0, keepdims=True)

    def col(k):
        return pl.BlockSpec((tb, HEAD_DIM), lambda h, i, kk: (nb - 1 - i, cb + k * heads + h))

    rev = pl.BlockSpec((tb, HEAD_DIM), lambda h, i, kk: (nb - 1 - i, h))
    return pl.pallas_call(
        body, name=name, grid=(heads, nb, 4),
        in_specs=[col(0), col(1), col(2), col(3),
                  pl.BlockSpec((2, HEAD_DIM), lambda h, i, kk: (0, h)),
                  pl.BlockSpec((1, HEAD_DIM), lambda h, i, kk: (0, 0)),
                  pl.BlockSpec((nsum, C), lambda h, i, kk: (0, 0)),
                  pl.BlockSpec((nlev, C, C), lambda h, i, kk: (0, 0, 0)),
                  rev,
                  pl.BlockSpec((None, cpb, HEAD_DIM, HEAD_DIM), lambda h, i, kk: (h, nb - 1 - i, 0, 0)),
                  rev,
                  pl.BlockSpec(memory_space=pl.ANY)],
        out_specs=[pl.BlockSpec((tb, HEAD_DIM), lambda h, i, kk: (nb - 1 - i, cb + kk * heads + h)),
                   pl.BlockSpec((2, HEAD_DIM), lambda h, i, kk: (0, h)),
                   pl.BlockSpec((1, HEAD_DIM), lambda h, i, kk: (0, 0))],
        out_shape=[jax.ShapeDtypeStruct(dproj.shape, BF16), jax.ShapeDtypeStruct((2, width), F32),
                   jax.ShapeDtypeStruct((1, HEAD_DIM), F32)],
        input_output_aliases={11: 0},
        scratch_shapes=[pltpu.VMEM((HEAD_DIM, HEAD_DIM), F32), pltpu.VMEM((nsum, HEAD_DIM), F32),
                        pltpu.VMEM((SUBLANE, HEAD_DIM), F32), pltpu.VMEM((SUBLANE, HEAD_DIM), F32),
                        pltpu.VMEM((4, tb, HEAD_DIM), BF16)],
        compiler_params=_params(("arbitrary", "arbitrary", "arbitrary"),
                                tb * HEAD_DIM * 4 * 12 + cpb * HEAD_DIM * HEAD_DIM * 4),
    )(proj, proj, proj, proj, lb_logits, g_out, sums, masks, o_saved, states, dy, dproj)


def _merge_fwd(proj, ma, mb, gate_col0, name):
    T, D = ma.shape
    tr = _tile(T, 256, SUBLANE)
    cw = _tile(D, 1024, LANE)
    nj = D // cw
    assert gate_col0 % cw == 0
    g0 = gate_col0 // cw

    def body(ga_ref, gb_ref, ma_ref, mb_ref, o_ref):
        o_ref[...] = (_sigmoid(ga_ref[...]) * ma_ref[...] + _sigmoid(gb_ref[...]) * mb_ref[...]).astype(BF16)

    blk = pl.BlockSpec((tr, cw), lambda i, j: (i, j))
    return pl.pallas_call(
        body, name=name, grid=(T // tr, nj),
        in_specs=[pl.BlockSpec((tr, cw), lambda i, j: (i, g0 + j)),
                  pl.BlockSpec((tr, cw), lambda i, j: (i, g0 + nj + j)), blk, blk],
        out_specs=blk, out_shape=jax.ShapeDtypeStruct((T, D), BF16),
        compiler_params=_params(("parallel", "parallel"), tr * cw * 18 * 2),
    )(proj, proj, ma, mb)


def _merge_bwd(proj, ma, mb, dm, gate_col0, name):
    T, D = ma.shape
    tr = _tile(T, 256, SUBLANE)
    cw = _tile(D, 1024, LANE)
    nj = D // cw
    g0 = gate_col0 // cw

    def body(g_ref, ma_ref, mb_ref, dm_ref, dp_ref, dmm_ref):
        j = pl.program_id(1)
        s = _sigmoid(g_ref[...])
        dmv = dm_ref[...]
        mm = jnp.where(j < nj, ma_ref[...], mb_ref[...])
        dp_ref[...] = (dmv * mm * (s * (1.0 - s))).astype(BF16)
        dmm_ref[...] = (dmv * s).astype(BF16)

    blk = pl.BlockSpec((tr, cw), lambda i, j: (i, j % nj))
    return pl.pallas_call(
        body, name=name, grid=(T // tr, 2 * nj),
        in_specs=[pl.BlockSpec((tr, cw), lambda i, j: (i, g0 + j)), blk, blk, blk],
        out_specs=[pl.BlockSpec((tr, cw), lambda i, j: (i, g0 + j)),
                   pl.BlockSpec((None, tr, cw), lambda i, j: (j // nj, i, j % nj))],
        out_shape=[jax.ShapeDtypeStruct(proj.shape, BF16), jax.ShapeDtypeStruct((2, T, D), BF16)],
        compiler_params=_params(("parallel", "parallel"), tr * cw * 20 * 2),
    )(proj, ma, mb, dm)


def _causal_conv(ext, w, b):
    s1 = pltpu.roll(ext, 1, 0)
    s2 = pltpu.roll(ext, 2, 0)
    out = b + w[0:1, :] * s2
    out = out + w[1:2, :] * s1
    out = out + w[2:3, :] * ext
    return out, s1, s2


def _conv_fwd(up, convw, convb, name):
    T, F2 = up.shape
    tc = convw.shape[2]
    half = (F2 // 2) // tc
    assert half * tc * 2 == F2
    tr = _tile(T, 256, SUBLANE)
    hb = tr // SUBLANE

    def body(g_ref, gp_ref, v_ref, vp_ref, wg_ref, wv_ref, bg_ref, bv_ref, o_ref):
        first = pl.program_id(1) == 0

        def conv(cur_ref, prev_ref, w_ref, b_ref):
            prev = jnp.where(first, 0.0, prev_ref[...])
            ext = jnp.concatenate([prev, cur_ref[...]], axis=0)
            return _causal_conv(ext, w_ref[...], b_ref[...])[0][SUBLANE:]

        gate = conv(g_ref, gp_ref, wg_ref, bg_ref)
        val = conv(v_ref, vp_ref, wv_ref, bv_ref)
        o_ref[...] = ((gate * _sigmoid(gate)) * val).astype(BF16)

    def main(off):
        return pl.BlockSpec((tr, tc), lambda j, i: (i, off + j))

    def prev(off):
        return pl.BlockSpec((SUBLANE, tc), lambda j, i: (jnp.maximum(i * hb - 1, 0), off + j))

    def wspec(off):
        return pl.BlockSpec((None, 3, tc), lambda j, i: (off + j, 0, 0))

    def bspec(off):
        return pl.BlockSpec((1, tc), lambda j, i: (0, off + j))

    return pl.pallas_call(
        body, name=name, grid=(half, T // tr),
        in_specs=[main(0), prev(0), main(half), prev(half), wspec(0), wspec(half), bspec(0), bspec(half)],
        out_specs=pl.BlockSpec((tr, tc), lambda j, i: (i, j)),
        out_shape=jax.ShapeDtypeStruct((T, F2 // 2), BF16),
        compiler_params=_params(("parallel", "parallel"), tr * tc * 4 * 12),
    )(up, up, up, up, convw, convw, convb, convb)


def _conv_bwd(da, up, convw, convb, name):
    T, F2 = up.shape
    tc = convw.shape[2]
    half = (F2 // 2) // tc
    tr = _tile(T, 128, SUBLANE)
    hb = tr // SUBLANE
    nrow = T // tr
    n = tr + 2 * SUBLANE

    def body(g_ref, gp_ref, gn_ref, v_ref, vp_ref, vn_ref, da_ref, dan_ref, wg_ref, wv_ref, bg_ref, bv_ref,
             dup_ref, gw_ref, acc_ref):
        j = pl.program_id(0)
        i = pl.program_id(1)
        first = i == 0
        last = i == nrow - 1
        is_gate = j < half

        def conv(cur_ref, prev_ref, next_ref, w_ref, b_ref):
            prev = jnp.where(first, 0.0, prev_ref[...])
            ext = jnp.concatenate([prev, cur_ref[...], next_ref[...]], axis=0)
            return (ext,) + _causal_conv(ext, w_ref[...], b_ref[...])

        g_ext, gate, g_s1, g_s2 = conv(g_ref, gp_ref, gn_ref, wg_ref, bg_ref)
        v_ext, val, v_s1, v_s2 = conv(v_ref, vp_ref, vn_ref, wv_ref, bv_ref)
        da_ext = jnp.concatenate([jnp.zeros((SUBLANE, tc), F32), da_ref[...],
                                  jnp.where(last, 0.0, dan_ref[...])], axis=0)
        sg = _sigmoid(gate)
        d_gate = da_ext * val * (sg * (1.0 + gate * (1.0 - sg)))
        d_val = da_ext * (gate * sg)
        mine = jnp.where(is_gate, d_gate, d_val)
        w = jnp.where(is_gate, wg_ref[...], wv_ref[...])
        dup = w[2:3, :] * mine + w[1:2, :] * pltpu.roll(mine, n - 1, 0) + w[0:1, :] * pltpu.roll(mine, n - 2, 0)
        dup_ref[...] = dup[SUBLANE:SUBLANE + tr].astype(BF16)

        rows = slice(SUBLANE, SUBLANE + tr)
        mine_m = mine[rows]
        taps = (jnp.where(is_gate, g_s2, v_s2)[rows], jnp.where(is_gate, g_s1, v_s1)[rows],
                jnp.where(is_gate, g_ext, v_ext)[rows])

        def fold(t):
            return jnp.sum(t.reshape(tr // SUBLANE, SUBLANE, tc), axis=0)

        @pl.when(first)
        def _():
            acc_ref[...] = jnp.zeros_like(acc_ref)

        acc_ref[0] += fold(mine_m)
        for k in range(3):
            acc_ref[1 + k] += fold(mine_m * taps[k])

        @pl.when(last)
        def _():
            gw_ref[...] = jnp.sum(acc_ref[...], axis=1)

    def main(off):
        return pl.BlockSpec((tr, tc), lambda j, i: (i, off + j % half))

    def prev(off):
        return pl.BlockSpec((SUBLANE, tc), lambda j, i: (jnp.maximum(i * hb - 1, 0), off + j % half))

    def nxt(off):
        return pl.BlockSpec((SUBLANE, tc), lambda j, i: (jnp.minimum((i + 1) * hb, T // SUBLANE - 1), off + j % half))

    def wspec(off):
        return pl.BlockSpec((None, 3, tc), lambda j, i: (off + j % half, 0, 0))

    def bspec(off):
        return pl.BlockSpec((1, tc), lambda j, i: (0, off + j % half))

    return pl.pallas_call(
        body, name=name, grid=(2 * half, nrow),
        in_specs=[main(0), prev(0), nxt(0), main(half), prev(half), nxt(half), main(0), nxt(0),
                  wspec(0), wspec(half), bspec(0), bspec(half)],
        out_specs=[pl.BlockSpec((tr, tc), lambda j, i: (i, j)), pl.BlockSpec((4, tc), lambda j, i: (0, j))],
        out_shape=[jax.ShapeDtypeStruct((T, F2), BF16), jax.ShapeDtypeStruct((4, F2), F32)],
        scratch_shapes=[pltpu.VMEM((4, SUBLANE, tc), F32)],
        compiler_params=_params(("parallel", "arbitrary"), n * tc * 4 * 24),
    )(up, up, up, up, up, up, da, da, convw, convw, convb, convb)


def _loss_head(y, target, name):
    T, D = y.shape
    tr = _tile(T, 256, SUBLANE)
    nrow = T // tr

    def body(y_ref, t_ref, d_ref, db_ref, l_ref, acc_ref):
        i = pl.program_id(0)
        diff = y_ref[...] - t_ref[...]
        dy = diff / D
        d_ref[...] = dy
        db_ref[...] = dy.astype(BF16)
        part = jnp.sum((diff * diff).reshape(tr // SUBLANE, SUBLANE, D), axis=0)

        @pl.when(i == 0)
        def _():
            acc_ref[...] = part

        @pl.when(i > 0)
        def _():
            acc_ref[...] += part

        @pl.when(i == nrow - 1)
        def _():
            col = jnp.sum(acc_ref[...], axis=0, keepdims=True)
            l_ref[...] = jnp.broadcast_to(0.5 * (jnp.sum(col, axis=1, keepdims=True) / D), (1, LANE))

    row = pl.BlockSpec((tr, D), lambda i: (i, 0))
    return pl.pallas_call(
        body, name=name, grid=(nrow,), in_specs=[row, row],
        out_specs=[row, row, pl.BlockSpec((1, LANE), lambda i: (0, 0))],
        out_shape=[jax.ShapeDtypeStruct((T, D), F32), jax.ShapeDtypeStruct((T, D), BF16),
                   jax.ShapeDtypeStruct((1, LANE), F32)],
        scratch_shapes=[pltpu.VMEM((SUBLANE, D), F32)],
        compiler_params=_params(("arbitrary",), tr * D * 14 * 2),
    )(y, target)


def _adamw(w, parts, m, v, name):
    R, C = w.shape
    P = parts.shape[0]
    tr = _tile(R, 64, SUBLANE)
    tc = _tile(C, 2048, LANE)

    def body(w_ref, p_ref, m_ref, v_ref, g_ref, d_ref, nm_ref, nv_ref):
        g = p_ref[0]
        for s in range(1, P):
            g = g + p_ref[s]
        wv = w_ref[...]
        nm = ADAM_B1 * m_ref[...] + (1.0 - ADAM_B1) * g
        nv = ADAM_B2 * v_ref[...] + (1.0 - ADAM_B2) * (g * g)
        m_hat = nm / (1.0 - ADAM_B1 ** ADAM_STEP)
        v_hat = nv / (1.0 - ADAM_B2 ** ADAM_STEP)
        g_ref[...] = g
        d_ref[...] = -ADAM_LR * (m_hat / (jnp.sqrt(v_hat) + ADAM_EPS) + ADAM_WD * wv)
        nm_ref[...] = nm
        nv_ref[...] = nv

    blk = pl.BlockSpec((tr, tc), lambda i, j: (i, j))
    shp = jax.ShapeDtypeStruct((R, C), F32)
    return pl.pallas_call(
        body, name=name, grid=(R // tr, C // tc),
        in_specs=[blk, pl.BlockSpec((P, tr, tc), lambda i, j: (0, i, j)), blk, blk],
        out_specs=[blk] * 4, out_shape=[shp] * 4,
        compiler_params=_params(("parallel", "parallel"), tr * tc * 4 * (P + 8) * 2),
    )(w, parts, m, v)


def _place():
    x, y, c = (lax.axis_index(a) for a in MESH_AXES)
    return x, y, c, 4 * x + 2 * y + c


def _peers(x, y, c):
    out = []
    for d in range(1, N_DEV):
        px = x + (d >> 2 & 1) - 2 * x * (d >> 2 & 1)
        py = y + (d >> 1 & 1) - 2 * y * (d >> 1 & 1)
        pc = c + (d & 1) - 2 * c * (d & 1)
        out.append(((px, py, pc), 4 * px + 2 * py + pc))
    return out


def _exchange(arrays, scatter, name):
    n = len(arrays)

    def body(*refs):
        ins, outs = refs[:n], refs[n:2 * n]
        send_sems, recv_sems, local_sems = refs[2 * n:]
        x, y, c, me = _place()
        peers = _peers(x, y, c)
        local, sends, recvs = [], [], []
        for t in range(n):
            src_me = ins[t].at[me] if scatter else ins[t]
            cp = pltpu.make_async_copy(src_me, outs[t].at[me], local_sems.at[t])
            cp.start()
            local.append(cp)
            for d, (peer, pidx) in enumerate(peers):
                src = ins[t].at[pidx] if scatter else ins[t]
                cp = pltpu.make_async_remote_copy(
                    src_ref=src, dst_ref=outs[t].at[me], send_sem=send_sems.at[t, d], recv_sem=recv_sems.at[t, d],
                    device_id=peer, device_id_type=MESH_ID)
                cp.start()
                sends.append(cp)
                recvs.append(pltpu.make_async_remote_copy(
                    src_ref=src, dst_ref=outs[t].at[pidx], send_sem=send_sems.at[t, d], recv_sem=recv_sems.at[t, d],
                    device_id=peer, device_id_type=MESH_ID))
        for cp in recvs:
            cp.wait_recv()
        for cp in sends:
            cp.wait_send()
        for cp in local:
            cp.wait()

    hbm = pl.BlockSpec(memory_space=pltpu.HBM)
    out_shape = [jax.ShapeDtypeStruct(a.shape if scatter else (N_DEV,) + a.shape, a.dtype) for a in arrays]
    return pl.pallas_call(
        body, name=name, in_specs=[hbm] * n, out_specs=[hbm] * n, out_shape=out_shape,
        scratch_shapes=[pltpu.SemaphoreType.DMA((n, N_DEV - 1)), pltpu.SemaphoreType.DMA((n, N_DEV - 1)),
                        pltpu.SemaphoreType.DMA((n,))],
    )(*arrays)


def _all_reduce_small(vec, name):
    R = vec.shape[0]

    def body(v_ref, o_ref, gath_ref, send_sems, recv_sems):
        x, y, c, me = _place()
        sends, recvs = [], []
        for d, (peer, pidx) in enumerate(_peers(x, y, c)):
            cp = pltpu.make_async_remote_copy(
                src_ref=v_ref, dst_ref=gath_ref.at[me], send_sem=send_sems.at[d], recv_sem=recv_sems.at[d],
                device_id=peer, device_id_type=MESH_ID)
            cp.start()
            sends.append(cp)
            recvs.append(pltpu.make_async_remote_copy(
                src_ref=v_ref, dst_ref=gath_ref.at[pidx], send_sem=send_sems.at[d], recv_sem=recv_sems.at[d],
                device_id=peer, device_id_type=MESH_ID))
        gath_ref[me] = v_ref[...]
        for cp in recvs:
            cp.wait_recv()
        for cp in sends:
            cp.wait_send()
        total = gath_ref[0]
        for s in range(1, N_DEV):
            total = total + gath_ref[s]
        o_ref[...] = total

    vm = pl.BlockSpec(memory_space=pltpu.VMEM)
    return pl.pallas_call(
        body, name=name, in_specs=[vm], out_specs=vm, out_shape=jax.ShapeDtypeStruct(vec.shape, F32),
        scratch_shapes=[pltpu.VMEM((N_DEV, R, LANE), F32), pltpu.SemaphoreType.DMA((N_DEV - 1,)),
                        pltpu.SemaphoreType.DMA((N_DEV - 1,))],
        compiler_params=pltpu.CompilerParams(vmem_limit_bytes=_vmem(R * LANE * 4 * 12)),
    )(vec)


def _pack(parts):
    flat = jnp.concatenate([p.reshape(-1).astype(F32) for p in parts])
    rows = -(-flat.shape[0] // (LANE * SUBLANE)) * SUBLANE
    return jnp.pad(flat, (0, rows * LANE - flat.shape[0])).reshape(rows, LANE)


def _unpack(packed, shapes):
    flat = packed.reshape(-1)
    out, at = [], 0
    for s in shapes:
        size = int(np.prod(s))
        out.append(flat[at:at + size].reshape(s))
        at += size
    return out


def kernel(x, g_mix, w_in, g_q, g_k, lb_logits, g_hg_out, p_a, p_b, w_o, g_ffn, w_up, conv_w, conv_b, w_down, loss_target, m_g_mix, m_w_in, m_g_q, m_g_k, m_lb_logits, m_g_hg_out, m_p_a, m_p_b, m_w_o, m_g_ffn, m_w_up, m_conv_w, m_conv_b, m_w_down, v_g_mix, v_w_in, v_g_q, v_g_k, v_lb_logits, v_g_hg_out, v_p_a, v_p_b, v_w_o, v_g_ffn, v_w_up, v_conv_w, v_conv_b, v_w_down):
    assert x.shape[0] == 1 and lb_logits.shape[0] == 2
    xs, target = x[0], loss_target[0]
    T, D = xs.shape
    A = p_a.shape[1]
    HW = p_b.shape[1]
    hg_col0 = 3 * A
    gate_col0 = 3 * A + 4 * HW
    F = w_down.shape[1] * N_DEV
    assert w_in.shape[2] * N_DEV == gate_col0 + 2 * D and w_up.shape[2] * N_DEV == 2 * F

    win_g, wup_g, pa_g, pb_g, wo_g, wdown_g, convw_g = _exchange(
        [w_in[0].astype(BF16), w_up[0].astype(BF16), p_a[0].astype(BF16), p_b[0].astype(BF16),
         w_o[0].astype(BF16), w_down[0].astype(BF16), conv_w[0]], False, "gather_weights")
    wo_full = wo_g.reshape(D, D)
    wdown_full = wdown_g.reshape(F, D)

    u = _rmsnorm_fwd(xs, g_mix, "norm_mix")
    proj = _matmul(u, win_g, mode="nn", b_shards=True, name="proj_in")
    gains = jnp.stack([g_q[0], g_k[0], jnp.ones_like(g_q[0])])[:, None, :]
    qkv = _qkv_prep(proj, gains, A, "qkv_prep")
    ya, ya32 = _sb_fwd(qkv, A, "sb_fwd")
    yb, ob, states = _hg_fwd(proj, lb_logits, g_hg_out, hg_col0, HW, "hg_fwd")
    ma = _matmul(ya, pa_g, mode="nn", b_shards=True, name="proj_a")
    mb = _matmul(yb, pb_g, mode="nn", b_shards=True, name="proj_b")
    m = _merge_fwd(proj, ma, mb, gate_col0, "merge_fwd")
    h1 = _matmul(m, wo_full, mode="nn", add=xs, name="proj_o")
    u2 = _rmsnorm_fwd(h1, g_ffn, "norm_ffn")
    up = _matmul(u2, wup_g, mode="nn", b_shards=True, name="ffn_up")
    act = _conv_fwd(up, convw_g, conv_b, "conv_fwd")
    y = _matmul(act, wdown_full, mode="nn", add=h1, name="ffn_down")
    dy, dyb, loss_part = _loss_head(y, target, "loss_head")
    loss = lax.psum(loss_part[0, 0], MESH_AXES)

    dact = _matmul(dyb, wdown_full, mode="nt", name="d_act")
    g_wdown = _matmul(act, dyb, mode="tn", name="g_w_down")
    dup, g_conv = _conv_bwd(dact, up, convw_g, conv_b, "conv_bwd")
    g_wup = _matmul(u2, dup, mode="tn", out_shards=True, name="g_w_up")
    du2 = _matmul(dup, wup_g, mode="nt", b_shards=True, name="d_u2")
    dh1, dh1b, g_gffn = _rmsnorm_bwd(h1, g_ffn, du2, dy, "norm_ffn_bwd")
    dm = _matmul(dh1b, wo_full, mode="nt", name="d_m")
    g_wo = _matmul(m, dh1b, mode="tn", name="g_w_o")
    dproj, dmab = _merge_bwd(proj, ma, mb, dm, gate_col0, "merge_bwd")
    dya = _matmul(dmab, pa_g, mode="nt", b_shards=True, lead_a=0, out_dtype=BF16, name="d_ya")
    g_pa = _matmul(ya, dmab, mode="tn", out_shards=True, lead_b=0, name="g_p_a")
    dyb_ = _matmul(dmab, pb_g, mode="nt", b_shards=True, lead_a=1, out_dtype=BF16, name="d_yb")
    g_pb = _matmul(yb, dmab, mode="tn", out_shards=True, lead_b=1, name="g_p_b")
    dproj, g_lb, g_ghg = _hg_bwd(proj, lb_logits, g_hg_out, ob, states, dyb_, dproj, hg_col0, HW, "hg_bwd")
    dq, dk, dv = _sb_bwd(qkv, ya32, dya, A, "sb_bwd")
    dproj, g_gains = _qkv_bwd(proj, gains, dq, dk, dv, dproj, "qkv_bwd")
    g_win = _matmul(u, dproj, mode="tn", out_shards=True, name="g_w_in")
    du = _matmul(dproj, win_g, mode="nt", b_shards=True, name="d_u")
    gx, _, g_gmix = _rmsnorm_bwd(xs, g_mix, du, dh1, "norm_mix_bwd")

    r_win, r_wup, r_pa, r_pb, r_wo, r_wdown = _exchange(
        [g_win, g_wup, g_pa, g_pb, g_wo.reshape(N_DEV, D // N_DEV, D), g_wdown.reshape(N_DEV, F // N_DEV, D)],
        True, "scatter_grads")
    small = [g_gmix, g_gains[0], g_gains[1], g_lb, g_ghg, g_gffn, g_conv[0:1], g_conv[1:4]]
    small_shapes = [p.shape for p in small]
    red = _unpack(_all_reduce_small(_pack(small), "reduce_small"), small_shapes)

    _, _, _, me = _place()
    cs = conv_w.shape[2]
    big = {
        "w_in": _adamw(w_in[0], r_win, m_w_in[0], v_w_in[0], "adamw_w_in"),
        "w_up": _adamw(w_up[0], r_wup, m_w_up[0], v_w_up[0], "adamw_w_up"),
        "p_a": _adamw(p_a[0], r_pa, m_p_a[0], v_p_a[0], "adamw_p_a"),
        "p_b": _adamw(p_b[0], r_pb, m_p_b[0], v_p_b[0], "adamw_p_b"),
        "w_o": _adamw(w_o[0], r_wo, m_w_o[0], v_w_o[0], "adamw_w_o"),
        "w_down": _adamw(w_down[0], r_wdown, m_w_down[0], v_w_down[0], "adamw_w_down"),
        "conv_w": _adamw(conv_w[0], lax.dynamic_slice_in_dim(red[7], me * cs, cs, axis=1)[None],
                         m_conv_w[0], v_conv_w[0], "adamw_conv_w"),
    }
    rep_w = [g_mix, g_q, g_k, lb_logits, g_hg_out, g_ffn, conv_b]
    rep_m = [m_g_mix, m_g_q, m_g_k, m_lb_logits, m_g_hg_out, m_g_ffn, m_conv_b]
    rep_v = [v_g_mix, v_g_q, v_g_k, v_lb_logits, v_g_hg_out, v_g_ffn, v_conv_b]
    rep_shapes = [p.shape for p in rep_w]
    rep_out = _adamw(_pack(rep_w), _pack(red[:7])[None], _pack(rep_m), _pack(rep_v), "adamw_small")
    rep = [_unpack(o, rep_shapes) for o in rep_out]
    rep_names = ["g_mix", "g_q", "g_k", "lb_logits", "g_hg_out", "g_ffn", "conv_b"]

    order = ["g_mix", "w_in", "g_q", "g_k", "lb_logits", "g_hg_out", "p_a", "p_b", "w_o", "g_ffn", "w_up",
             "conv_w", "conv_b", "w_down"]

    def leaf(kind, pname):
        if pname in big:
            return big[pname][kind][None]
        return rep[kind][rep_names.index(pname)]

    return (loss, gx[None], *[leaf(kind, p) for kind in range(4) for p in order])
```

```python
import functools

import numpy as np
import jax
import jax.numpy as jnp
from jax import lax
from jax.experimental import pallas as pl
from jax.experimental.pallas import tpu as pltpu

F32 = jnp.float32
BF16 = jnp.bfloat16

N_DEV = 8
HEAD_DIM = 128
HG_CHUNK = 64
HG_LEVELS = (1, 2, 4, 8, 16, 32)
EPS = 1e-6
ADAM_LR = 0.001
ADAM_B1 = 0.9
ADAM_B2 = 0.999
ADAM_EPS = 1e-08
ADAM_WD = 0.01
ADAM_STEP = 10
LANE = 128
SUBLANE = 8
VMEM_CAP = 56 << 20
MESH_AXES = ("x", "y", "c")
MESH_ID = pl.DeviceIdType.MESH


def _tile(dim, pref, align):
    if dim <= pref:
        return dim
    t = (pref // align) * align
    while t >= align:
        if dim % t == 0:
            return t
        t -= align
    return dim


def _vmem(est_bytes):
    return int(min(max(2 * est_bytes + (8 << 20), 32 << 20), VMEM_CAP))


def _params(sem, est_bytes):
    return pltpu.CompilerParams(dimension_semantics=sem, vmem_limit_bytes=_vmem(est_bytes))


def _sigmoid(x):
    return 1.0 / (1.0 + jnp.exp(-x))


_DIMS = {"nn": (((1,), (0,)), ((), ())), "nt": (((1,), (1,)), ((), ())), "tn": (((0,), (0,)), ((), ()))}


def _matmul(a, b, *, mode, name, out_dtype=F32, add=None, b_shards=False, out_shards=False, lead_a=None,
            lead_b=None, side=None, tm_pref=512, tn_pref=1408, tk_pref=2048):
    if mode == "tn":
        K, M = a.shape[-2:]
    else:
        M, K = a.shape[-2:]
    if lead_b is not None:
        assert not b_shards and mode == "tn"
    if b_shards:
        S = b.shape[0]
        if mode == "nn":
            assert b.shape[1] == K
            N, tn, tk = S * b.shape[2], b.shape[2], _tile(K, tk_pref, LANE)
        else:
            assert mode == "nt" and S * b.shape[2] == K
            N, tk, tn = b.shape[1], b.shape[2], _tile(b.shape[1], tn_pref, LANE)
    else:
        if mode == "nt":
            N = b.shape[0]
            assert b.shape[1] == K
        else:
            N = b.shape[-1]
            assert b.shape[-2] == K
        tn = _tile(N, tn_pref, LANE)
        tk = _tile(K, tk_pref, LANE)
    if out_shards:
        assert N % N_DEV == 0
        tn = N // N_DEV
    tm = _tile(M, tm_pref, LANE)
    nm, nn, nk = M // tm, N // tn, K // tk
    assert nm * tm == M and nn * tn == N and nk * tk == K

    if mode == "tn":
        a_spec = pl.BlockSpec((tk, tm), lambda j, i, k: (k, i))
    elif lead_a is not None:
        a_spec = pl.BlockSpec((None, tm, tk), lambda j, i, k: (lead_a, i, k))
    else:
        a_spec = pl.BlockSpec((tm, tk), lambda j, i, k: (i, k))
    if lead_b is not None:
        b_spec = pl.BlockSpec((None, tk, tn), lambda j, i, k: (lead_b, k, j))
    elif b_shards and mode == "nn":
        b_spec = pl.BlockSpec((None, tk, tn), lambda j, i, k: (j, k, 0))
    elif b_shards:
        b_spec = pl.BlockSpec((None, tn, tk), lambda j, i, k: (k, j, 0))
    elif mode == "nt":
        b_spec = pl.BlockSpec((tn, tk), lambda j, i, k: (j, k))
    else:
        b_spec = pl.BlockSpec((tk, tn), lambda j, i, k: (k, j))
    in_specs = [a_spec, b_spec]
    operands = [a, b]
    if add is not None:
        assert not out_shards and add.shape == (M, N)
        in_specs.append(pl.BlockSpec((tm, tn), lambda j, i, k: (i, j)))
        operands.append(add)
    if out_shards:
        out_shape = jax.ShapeDtypeStruct((N_DEV, M, tn), out_dtype)
        out_spec = pl.BlockSpec((None, tm, tn), lambda j, i, k: (j, i, 0))
    else:
        out_shape = jax.ShapeDtypeStruct((M, N), out_dtype)
        out_spec = pl.BlockSpec((tm, tn), lambda j, i, k: (i, j))
    dims = _DIMS[mode]
    has_add = add is not None
    n_in = 3 if has_add else 2
    n_side = side.n if side is not None else 0

    def body(*refs):
        a_ref, b_ref = refs[0], refs[1]
        add_ref = refs[2] if has_add else None
        o_ref = refs[n_in + n_side]
        acc_ref = refs[n_in + 2 * n_side + 1]
        k = pl.program_id(2)
        if side is not None:
            side_refs = (refs[n_in:n_in + n_side], refs[n_in + n_side + 1:n_in + 2 * n_side + 1],
                         refs[n_in + 2 * n_side + 2:])
            first, last = _grid_edges((nn, nm, nk))

            @pl.when(first)
            def _():
                side.start(*side_refs)

        part = lax.dot_general(a_ref[...], b_ref[...], dims, preferred_element_type=F32)

        def finish(total):
            if has_add:
                total = add_ref[...] + total
            o_ref[...] = total.astype(out_dtype)

        if nk == 1:
            finish(part)
        else:
            @pl.when(k == 0)
            def _():
                acc_ref[...] = part

            @pl.when(jnp.logical_and(k > 0, k < nk - 1))
            def _():
                acc_ref[...] += part

            @pl.when(k == nk - 1)
            def _():
                finish(acc_ref[...] + part)

        if side is not None:
            @pl.when(last)
            def _():
                side.wait(*side_refs)

    est = 2 * (tm * tk * 2 + tk * tn * 2 + tm * tn * 4 * (2 if has_add else 1)) + tm * tn * 4 * 2
    acc = pltpu.VMEM((tm, tn) if nk > 1 else (SUBLANE, LANE), F32)
    if side is None:
        return pl.pallas_call(
            body, name=name, grid=(nn, nm, nk), in_specs=in_specs, out_specs=out_spec, out_shape=out_shape,
            scratch_shapes=[acc], compiler_params=_params(("parallel", "parallel", "arbitrary"), est),
        )(*operands)
    outs = pl.pallas_call(
        body, name=name, grid=(nn, nm, nk), in_specs=in_specs + side.in_specs(),
        out_specs=[out_spec] + side.out_specs(), out_shape=[out_shape] + side.out_shape(),
        scratch_shapes=[acc] + side.scratch(), compiler_params=_params(("arbitrary",) * 3, est),
    )(*operands, *side.arrays)
    return outs[0], outs[1:]


def _rmsnorm_fwd(x, g, name):
    T, D = x.shape
    tr = _tile(T, 256, SUBLANE)

    def body(x_ref, g_ref, o_ref):
        xf = x_ref[...]
        r = lax.rsqrt(jnp.mean(xf * xf, axis=-1, keepdims=True) + EPS)
        o_ref[...] = ((xf * r) * g_ref[...]).astype(BF16)

    return pl.pallas_call(
        body, name=name, grid=(T // tr,),
        in_specs=[pl.BlockSpec((tr, D), lambda i: (i, 0)), pl.BlockSpec((1, D), lambda i: (0, 0))],
        out_specs=pl.BlockSpec((tr, D), lambda i: (i, 0)), out_shape=jax.ShapeDtypeStruct((T, D), BF16),
        compiler_params=_params(("parallel",), tr * D * 6 * 2),
    )(x, g)


def _rmsnorm_bwd(x, g, dy, res, name):
    T, D = x.shape
    tr = _tile(T, 256, SUBLANE)
    nsteps = T // tr

    def body(x_ref, g_ref, dy_ref, res_ref, dx_ref, dxb_ref, dg_ref, acc_ref):
        i = pl.program_id(0)
        xf = x_ref[...]
        dyf = dy_ref[...].astype(F32)
        r = lax.rsqrt(jnp.mean(xf * xf, axis=-1, keepdims=True) + EPS)
        w = dyf * g_ref[...]
        s = jnp.mean(xf * w, axis=-1, keepdims=True)
        dx = res_ref[...] + (r * w - xf * (r * r * r * s))
        dx_ref[...] = dx
        dxb_ref[...] = dx.astype(BF16)
        part = jnp.sum((dyf * (xf * r)).reshape(tr // SUBLANE, SUBLANE, D), axis=0)

        @pl.when(i == 0)
        def _():
            acc_ref[...] = part

        @pl.when(i > 0)
        def _():
            acc_ref[...] += part

        @pl.when(i == nsteps - 1)
        def _():
            dg_ref[...] = jnp.sum(acc_ref[...], axis=0, keepdims=True)

    row = pl.BlockSpec((tr, D), lambda i: (i, 0))
    vec = pl.BlockSpec((1, D), lambda i: (0, 0))
    return pl.pallas_call(
        body, name=name, grid=(nsteps,), in_specs=[row, vec, row, row], out_specs=[row, row, vec],
        out_shape=[jax.ShapeDtypeStruct((T, D), F32), jax.ShapeDtypeStruct((T, D), BF16),
                   jax.ShapeDtypeStruct((1, D), F32)],
        scratch_shapes=[pltpu.VMEM((SUBLANE, D), F32)],
        compiler_params=_params(("arbitrary",), tr * D * 18 * 2),
    )(x, g, dy, res)


def _qkv_prep(proj, gains, width, name):
    T = proj.shape[0]
    tr = _tile(T, 256, SUBLANE)
    heads = width // HEAD_DIM

    def body(p_ref, g_ref, o_ref):
        j = pl.program_id(1)
        g = g_ref[...]
        for h in range(heads):
            xh = p_ref[:, h * HEAD_DIM:(h + 1) * HEAD_DIM]
            r = lax.rsqrt(jnp.mean(xh * xh, axis=-1, keepdims=True) + EPS)
            r = jnp.where(j < 2, r, 1.0)
            o_ref[:, h * HEAD_DIM:(h + 1) * HEAD_DIM] = ((xh * r) * g).astype(BF16)

    return pl.pallas_call(
        body, name=name, grid=(T // tr, 3),
        in_specs=[pl.BlockSpec((tr, width), lambda i, j: (i, j)),
                  pl.BlockSpec((None, 1, HEAD_DIM), lambda i, j: (j, 0, 0))],
        out_specs=pl.BlockSpec((tr, width), lambda i, j: (i, j)),
        out_shape=jax.ShapeDtypeStruct((T, 3 * width), BF16),
        compiler_params=_params(("parallel", "parallel"), tr * width * 6 * 2),
    )(proj, gains)


def _qkv_bwd(proj, gains, dq, dk, dv, dproj, name):
    T = proj.shape[0]
    width = dq.shape[1]
    tr = _tile(T, 256, SUBLANE)
    heads = width // HEAD_DIM
    nrow = T // tr

    def body(p_ref, g_ref, dq_ref, dk_ref, dv_ref, dp_in, o_ref, dg_ref, acc_ref):
        del dp_in
        i = pl.program_id(0)
        j = pl.program_id(1)
        g = g_ref[...]

        @pl.when(jnp.logical_and(i == 0, j == 0))
        def _():
            acc_ref[...] = jnp.zeros_like(acc_ref)

        part = jnp.zeros((SUBLANE, HEAD_DIM), F32)
        for h in range(heads):
            sl = slice(h * HEAD_DIM, (h + 1) * HEAD_DIM)
            xh = p_ref[:, sl]
            dyh = jnp.where(j == 0, dq_ref[:, sl], jnp.where(j == 1, dk_ref[:, sl], dv_ref[:, sl]))
            r = lax.rsqrt(jnp.mean(xh * xh, axis=-1, keepdims=True) + EPS)
            w = dyh * g
            s = jnp.mean(xh * w, axis=-1, keepdims=True)
            dx = r * w - xh * (r * r * r * s)
            o_ref[:, sl] = jnp.where(j < 2, dx, dyh).astype(BF16)
            part = part + jnp.sum((dyh * (xh * r)).reshape(tr // SUBLANE, SUBLANE, HEAD_DIM), axis=0)
        acc_ref[j] += part

        @pl.when(jnp.logical_and(i == nrow - 1, j == 2))
        def _():
            dg_ref[...] = jnp.sum(acc_ref[...], axis=1, keepdims=True)

    blk = pl.BlockSpec((tr, width), lambda i, j: (i, j))
    dblk = pl.BlockSpec((tr, width), lambda i, j: (i, 0))
    return pl.pallas_call(
        body, name=name, grid=(nrow, 3),
        in_specs=[blk, pl.BlockSpec((None, 1, HEAD_DIM), lambda i, j: (j, 0, 0)), dblk, dblk, dblk,
                  pl.BlockSpec(memory_space=pl.ANY)],
        out_specs=[blk, pl.BlockSpec((3, 1, HEAD_DIM), lambda i, j: (0, 0, 0))],
        out_shape=[jax.ShapeDtypeStruct(dproj.shape, BF16), jax.ShapeDtypeStruct((3, 1, HEAD_DIM), F32)],
        input_output_aliases={5: 0},
        scratch_shapes=[pltpu.VMEM((3, SUBLANE, HEAD_DIM), F32)],
        compiler_params=_params(("arbitrary", "arbitrary"), tr * width * 18 * 2),
    )(proj, gains, dq, dk, dv, dproj)


def _split2(x):
    hi = x.astype(BF16)
    lo = (x - hi.astype(F32)).astype(BF16)
    return hi, lo


LOG2E = 1.4426950408889634


def _sb_logits(q, kb):
    return lax.dot_general(q, kb, _DIMS["nt"], preferred_element_type=F32) * (HEAD_DIM ** -0.5 * LOG2E)


def _sb_scores(z2, mask):
    lk = -(jnp.maximum(z2, 0.0) + jnp.log2(1.0 + jnp.exp2(-jnp.abs(z2))))
    if mask is not None:
        lk = jnp.where(mask, lk, 0.0)
    return lk, lk + z2


def _sb_block_size(T):
    return _tile(T, 256, LANE)


SB_HEADS_PER_STEP = 2


def _sb_later(lk, upper, carry):
    hi, lo = _split2(lk)
    return (jnp.dot(hi, upper, preferred_element_type=F32) + jnp.dot(lo, upper, preferred_element_type=F32)) + carry


def _sb_fwd(qkv, width, name, side=None):
    T = qkv.shape[0]
    heads = width // HEAD_DIM
    bq = _sb_block_size(T)
    hps = min(2 * SB_HEADS_PER_STEP, heads)
    assert heads % hps == 0
    groups, W = heads // hps, hps * HEAD_DIM

    n_side = side.n if side is not None else 0

    def body(*refs):
        q_ref, k_ref, v_ref = refs[:3]
        o_ref, o32_ref = refs[3 + n_side:5 + n_side]
        if side is not None:
            side_refs = (refs[3:3 + n_side], refs[5 + n_side:5 + 2 * n_side], refs[5 + 2 * n_side:])
            first, last = _grid_edges((groups, T // bq))

            @pl.when(first)
            def _():
                side.start(*side_refs)

        i = pl.program_id(1)
        row = lax.broadcasted_iota(jnp.int32, (bq, bq), 0)
        col = lax.broadcasted_iota(jnp.int32, (bq, bq), 1)
        upper = (row > col).astype(BF16)
        causal = col < row

        def head_block(start, h, carry, acc, mask):
            cols = slice(h * HEAD_DIM, (h + 1) * HEAD_DIM)
            z2 = _sb_logits(q_ref[:, cols], k_ref[pl.ds(start, bq), cols])
            yield
            lk, lb = _sb_scores(z2, mask)
            later = _sb_later(lk, upper, carry)
            yield
            w = jnp.exp2(lb + later)
            if mask is not None:
                w = jnp.where(mask, w, 0.0)
            whi, wlo = _split2(w)
            vb = v_ref[pl.ds(start, bq), cols]
            pv = jnp.dot(whi, vb, preferred_element_type=F32) + jnp.dot(wlo, vb, preferred_element_type=F32)
            yield
            return carry + jnp.sum(lk, axis=1, keepdims=True), acc + pv

        def block(j, state, mask):
            start = pl.multiple_of(j * bq, bq)
            return tuple(_lockstep(head_block(start, h, state[h][0], state[h][1], mask) for h in range(hps)))

        zero = (jnp.zeros((bq, 1), F32), jnp.zeros((bq, HEAD_DIM), F32))
        state = block(i, (zero,) * hps, causal)
        state = lax.fori_loop(0, i, lambda n, st: block(i - 1 - n, st, None), state)
        for h in range(hps):
            cols = slice(h * HEAD_DIM, (h + 1) * HEAD_DIM)
            o_ref[:, cols] = state[h][1].astype(BF16)
            o32_ref[:, cols] = state[h][1]

        if side is not None:
            @pl.when(last)
            def _():
                side.wait(*side_refs)

    oblk = pl.BlockSpec((bq, W), lambda g, i: (i, g))
    in_specs = [oblk, pl.BlockSpec((T, W), lambda g, i: (0, groups + g)),
                pl.BlockSpec((T, W), lambda g, i: (0, 2 * groups + g))]
    out_shape = [jax.ShapeDtypeStruct((T, width), BF16), jax.ShapeDtypeStruct((T, width), F32)]
    est = 2 * T * W * 2 * 2 + hps * 16 * bq * bq * 4
    if side is None:
        return pl.pallas_call(
            body, name=name, grid=(groups, T // bq), in_specs=in_specs, out_specs=[oblk, oblk], out_shape=out_shape,
            compiler_params=_params(("parallel", "arbitrary"), est),
        )(qkv, qkv, qkv)
    outs = pl.pallas_call(
        body, name=name, grid=(groups, T // bq), in_specs=in_specs + side.in_specs(),
        out_specs=[oblk, oblk] + side.out_specs(), out_shape=out_shape + side.out_shape(),
        scratch_shapes=side.scratch(), compiler_params=_params(("arbitrary", "arbitrary"), est),
    )(qkv, qkv, qkv, *side.arrays)
    return outs[0], outs[1], outs[2:]


def _sb_bwd(qkv, y, dy, width, name):
    T = qkv.shape[0]
    heads = width // HEAD_DIM
    bq = _sb_block_size(T)
    nq = T // bq
    scale = HEAD_DIM ** -0.5
    hps = min(SB_HEADS_PER_STEP, heads)
    groups, W = heads // hps, hps * HEAD_DIM

    def body(q_ref, k_ref, v_ref, y_ref, dy_ref, dq_ref, dk_hbm, dv_hbm, dk_acc, dv_acc, out_sems):
        g_id = pl.program_id(0)
        i = pl.program_id(1)
        row = lax.broadcasted_iota(jnp.int32, (bq, bq), 0)
        col = lax.broadcasted_iota(jnp.int32, (bq, bq), 1)
        upper = (row > col).astype(BF16)
        upper_incl = (row >= col).astype(BF16)
        causal = col < row
        deltas = []
        for h in range(hps):
            cols = slice(h * HEAD_DIM, (h + 1) * HEAD_DIM)
            deltas.append(jnp.sum(dy_ref[:, cols].astype(F32) * y_ref[:, cols], axis=1, keepdims=True))

        @pl.when(i == 0)
        def _():
            dk_acc[...] = jnp.zeros_like(dk_acc)
            dv_acc[...] = jnp.zeros_like(dv_acc)

        def head_block(start, h, carry, gcarry, dq, mask):
            cols = slice(h * HEAD_DIM, (h + 1) * HEAD_DIM)
            q = q_ref[:, cols]
            do = dy_ref[:, cols]
            kb = k_ref[pl.ds(start, bq), cols]
            vb = v_ref[pl.ds(start, bq), cols]
            z2 = _sb_logits(q, kb)
            dw = lax.dot_general(do, vb, _DIMS["nt"], preferred_element_type=F32)
            yield
            lk, lb = _sb_scores(z2, mask)
            later = _sb_later(lk, upper, carry)
            yield
            w = jnp.exp2(lb + later)
            if mask is not None:
                w = jnp.where(mask, w, 0.0)
            gw = dw * w
            gsuf = _sb_later(gw, upper_incl, gcarry)
            dvp = lax.dot_general(w.astype(BF16), do, _DIMS["tn"], preferred_element_type=F32)
            yield
            dz = gw - jnp.exp2(lb) * (gw + (deltas[h] - gsuf))
            if mask is not None:
                dz = jnp.where(mask, dz, 0.0)
            dzs = (dz * scale).astype(BF16)
            dqp = jnp.dot(dzs, kb, preferred_element_type=F32)
            dkp = lax.dot_general(dzs, q, _DIMS["tn"], preferred_element_type=F32)
            yield
            dk_acc[pl.ds(start, bq), cols] += dkp
            dv_acc[pl.ds(start, bq), cols] += dvp
            return (carry + jnp.sum(lk, axis=1, keepdims=True), gcarry + jnp.sum(gw, axis=1, keepdims=True), dq + dqp)

        def block(j, state, mask):
            start = pl.multiple_of(j * bq, bq)
            return tuple(_lockstep(head_block(start, h, *state[h], mask) for h in range(hps)))

        zero = jnp.zeros((bq, 1), F32)
        state = block(i, ((zero, zero, jnp.zeros((bq, HEAD_DIM), F32)),) * hps, causal)
        state = lax.fori_loop(0, i, lambda n, st: block(i - 1 - n, st, None), state)
        for h in range(hps):
            dq_ref[:, h * HEAD_DIM:(h + 1) * HEAD_DIM] = state[h][2]

        @pl.when(i == nq - 1)
        def _():
            cols = pl.ds(pl.multiple_of(g_id * W, W), W)
            copies = [pltpu.make_async_copy(dk_acc, dk_hbm.at[:, cols], out_sems.at[0]),
                      pltpu.make_async_copy(dv_acc, dv_hbm.at[:, cols], out_sems.at[1])]
            for cp in copies:
                cp.start()
            for cp in copies:
                cp.wait()

    qblk = pl.BlockSpec((bq, W), lambda g, i: (i, g))
    out = jax.ShapeDtypeStruct((T, width), F32)
    return pl.pallas_call(
        body, name=name, grid=(groups, nq),
        in_specs=[qblk, pl.BlockSpec((T, W), lambda g, i: (0, groups + g)),
                  pl.BlockSpec((T, W), lambda g, i: (0, 2 * groups + g)), qblk, qblk],
        out_specs=[qblk, pl.BlockSpec(memory_space=pl.ANY), pl.BlockSpec(memory_space=pl.ANY)],
        out_shape=[out, out, out],
        scratch_shapes=[pltpu.VMEM((T, W), F32), pltpu.VMEM((T, W), F32), pltpu.SemaphoreType.DMA((2,))],
        compiler_params=_params(("arbitrary", "arbitrary"), 2 * T * W * 2 * 2 + 2 * T * W * 4),
    )(qkv, qkv, qkv, y, dy)


def _hg_constants():
    C = HG_CHUNK
    t = np.arange(C)[:, None]
    j = np.arange(C)[None, :]
    blocks = [(j <= t), (j > t)]
    masks = []
    for n in HG_LEVELS:
        right = (t % (2 * n)) >= n
        mid = (t // (2 * n)) * (2 * n) + n - 1
        blocks.append(right & (j > mid) & (j <= t))
        blocks.append((~right) & (j > t) & (j <= mid))
        tt, ss = np.arange(C)[:, None], np.arange(C)[None, :]
        same = (tt // (2 * n)) == (ss // (2 * n))
        masks.append(same & ((tt % (2 * n)) >= n) & ((ss % (2 * n)) < n))
    sums = np.concatenate(blocks, axis=0).astype(np.float32)
    return jnp.asarray(sums, BF16), jnp.asarray(np.stack(masks).astype(np.float32))


def _split3(x):
    hi = x.astype(BF16)
    r1 = x - hi.astype(F32)
    mid = r1.astype(BF16)
    lo = (r1 - mid.astype(F32)).astype(BF16)
    return hi, mid, lo


def _hg_gates(hq, hf, lb):
    sq = _sigmoid(hq)
    sf = _sigmoid(hf)
    f = lb + (1.0 - lb) * sf
    return hq * sq, sq, f, sf, 1.0 - f, jnp.log(f)


def _hg_exponents(sums, g):
    hi, mid, lo = _split3(g)
    return (jnp.dot(sums, hi, preferred_element_type=F32) + jnp.dot(sums, mid, preferred_element_type=F32)
            + jnp.dot(sums, lo, preferred_element_type=F32))


def _hg_level_scores(q, k, e_all):
    C = HG_CHUNK
    parts, prods = [], []
    for li in range(len(HG_LEVELS)):
        eq = jnp.exp(e_all[(2 + 2 * li) * C:(3 + 2 * li) * C])
        ek = jnp.exp(e_all[(3 + 2 * li) * C:(4 + 2 * li) * C])
        qt = q * eq
        kt = k * ek
        prods.append(lax.dot_general(qt.astype(BF16), kt.astype(BF16), _DIMS["nt"], preferred_element_type=F32))
        parts.append((eq, ek, qt, kt))
    return parts, prods


def _hg_intra(prods, masks_ref):
    a = masks_ref[0] * prods[0]
    for li in range(1, len(HG_LEVELS)):
        a = a + masks_ref[li] * prods[li]
    return a


def _lockstep(gens):
    gens = list(gens)
    results = [None] * len(gens)
    alive = list(range(len(gens)))
    while alive:
        for idx in list(alive):
            try:
                next(gens[idx])
            except StopIteration as done:
                results[idx] = done.value
                alive.remove(idx)
    return results


def _lower_bound(lbl_ref, cols):
    return _sigmoid(lbl_ref[0:1, cols] - lbl_ref[1:2, cols])


HG_HEADS_PER_STEP = 4


def _hg_fwd(proj, lb_logits, g_out, col0, width, name):
    T = proj.shape[0]
    heads = width // HEAD_DIM
    C = HG_CHUNK
    tb = _tile(T, 512, C)
    cpb = tb // C
    nb = T // tb
    sums, masks = _hg_constants()
    hps = min(HG_HEADS_PER_STEP, heads)
    groups, W = heads // hps, hps * HEAD_DIM
    assert col0 % W == 0 and width % W == 0
    cb = col0 // W

    def body(hq_ref, hf_ref, hi_ref, og_ref, lbl_ref, go_ref, sums_ref, masks_ref, y_ref, o_ref, st_ref, s_ref):
        i = pl.program_id(1)

        @pl.when(i == 0)
        def _():
            s_ref[...] = jnp.zeros_like(s_ref)

        go = go_ref[...]

        def chunk_head(c, rows, h):
            cols = slice(h * HEAD_DIM, (h + 1) * HEAD_DIM)
            lb = _lower_bound(lbl_ref, cols)
            q, _, _, _, k, g = _hg_gates(hq_ref[rows, cols], hf_ref[rows, cols], lb)
            v = hi_ref[rows, cols]
            vb = v.astype(BF16)
            st = s_ref[h]
            st_ref[h, c] = st
            e_all = _hg_exponents(sums_ref[...], g)
            yield
            b = e_all[0:C]
            ebl = jnp.exp(b[C - 1:C, :])
            qe = q * jnp.exp(b)
            o_inter = lax.dot_general(qe.astype(BF16), st.astype(BF16), _DIMS["nt"], preferred_element_type=F32)
            kd = k * jnp.exp(e_all[C:2 * C])
            s_new = lax.dot_general(vb, kd.astype(BF16), _DIMS["tn"], preferred_element_type=F32)
            _, prods = _hg_level_scores(q, k, e_all)
            yield
            a = _hg_intra(prods, masks_ref)
            o_intra = jnp.dot(a.astype(BF16), vb, preferred_element_type=F32)
            s_ref[h] = st * ebl + s_new
            yield
            o = (o_inter + o_intra) + jnp.sum(q * k, axis=1, keepdims=True) * v
            o_ref[rows, cols] = o
            r = lax.rsqrt(jnp.mean(o * o, axis=-1, keepdims=True) + EPS)
            og = og_ref[rows, cols]
            y_ref[rows, cols] = (((o * r) * go) * (og * _sigmoid(og))).astype(BF16)

        def chunk(c, _):
            rows = pl.ds(pl.multiple_of(c * C, C), C)
            _lockstep(chunk_head(c, rows, h) for h in range(hps))
            return 0

        lax.fori_loop(0, cpb, chunk, 0)

    def col(k):
        return pl.BlockSpec((tb, W), lambda g, i: (i, cb + k * groups + g))

    nsum = sums.shape[0]
    out_blk = pl.BlockSpec((tb, W), lambda g, i: (i, g))
    return pl.pallas_call(
        body, name=name, grid=(groups, nb),
        in_specs=[col(0), col(1), col(2), col(3),
                  pl.BlockSpec((2, W), lambda g, i: (0, g)),
                  pl.BlockSpec((1, HEAD_DIM), lambda g, i: (0, 0)),
                  pl.BlockSpec((nsum, C), lambda g, i: (0, 0)),
                  pl.BlockSpec((len(HG_LEVELS), C, C), lambda g, i: (0, 0, 0))],
        out_specs=[out_blk, out_blk,
                   pl.BlockSpec((hps, cpb, HEAD_DIM, HEAD_DIM), lambda g, i: (g, i, 0, 0))],
        out_shape=[jax.ShapeDtypeStruct((T, width), BF16), jax.ShapeDtypeStruct((T, width), F32),
                   jax.ShapeDtypeStruct((heads, T // C, HEAD_DIM, HEAD_DIM), F32)],
        scratch_shapes=[pltpu.VMEM((hps, HEAD_DIM, HEAD_DIM), F32)],
        compiler_params=_params(("parallel", "arbitrary"), tb * W * 4 * 7 + hps * cpb * HEAD_DIM * HEAD_DIM * 4),
    )(proj, proj, proj, proj, lb_logits, g_out, sums, masks)


def _hg_bwd(proj, lb_logits, g_out, o_saved, states, dy, dproj, col0, width, name):
    T = proj.shape[0]
    heads = width // HEAD_DIM
    C = HG_CHUNK
    tb = _tile(T, 512, C)
    cpb = tb // C
    nb = T // tb
    sums, masks = _hg_constants()
    nsum = sums.shape[0]
    nlev = len(HG_LEVELS)
    hps = min(HG_HEADS_PER_STEP, heads)
    groups, W = heads // hps, hps * HEAD_DIM
    cb = col0 // W

    def body(hq_ref, hf_ref, hi_ref, og_ref, lbl_ref, go_ref, sums_ref, masks_ref, o_ref, st_ref, dy_ref, dp_in,
             dp_ref, dlb_ref, dgo_ref, ds_ref, de_ref, dlb_acc, dgo_acc, dout_ref):
        del dp_in
        grp = pl.program_id(0)
        i = pl.program_id(1)
        kk = pl.program_id(2)

        @pl.when(jnp.logical_and(i == 0, kk == 0))
        def _():
            ds_ref[...] = jnp.zeros_like(ds_ref)
            dlb_acc[...] = jnp.zeros_like(dlb_acc)

        @pl.when(jnp.logical_and(jnp.logical_and(grp == 0, i == 0), kk == 0))
        def _():
            dgo_acc[...] = jnp.zeros_like(dgo_acc)

        go = go_ref[...]
        last_row = lax.broadcasted_iota(jnp.int32, (C, HEAD_DIM), 0) == C - 1

        def chunk(n, _):
            c = cpb - 1 - n
            rows = pl.ds(pl.multiple_of(c * C, C), C)
            _lockstep(chunk_head(c, rows, h) for h in range(hps))
            return 0

        def chunk_head(c, rows, h):
            cols = slice(h * HEAD_DIM, (h + 1) * HEAD_DIM)
            de_h = de_ref.at[h]
            lb = _lower_bound(lbl_ref, cols)
            hq = hq_ref[rows, cols]
            og = og_ref[rows, cols]
            q, sq, f, sf, k, g = _hg_gates(hq, hf_ref[rows, cols], lb)
            v = hi_ref[rows, cols]
            vb = v.astype(BF16)
            st = st_ref[h, c]
            stb = st.astype(BF16)
            dst = ds_ref[h]
            dstb = dst.astype(BF16)
            o = o_ref[rows, cols]
            dyc = dy_ref[rows, cols].astype(F32)
            sg = _sigmoid(og)
            r = lax.rsqrt(jnp.mean(o * o, axis=-1, keepdims=True) + EPS)
            on = (o * r) * go
            don = dyc * (og * sg)
            dout_ref[3, rows, cols] = (dyc * on * (sg * (1.0 + og * (1.0 - sg)))).astype(BF16)
            dgo_acc[...] += jnp.sum((don * (o * r)).reshape(C // SUBLANE, SUBLANE, HEAD_DIM), axis=0)
            wn = don * go
            do = r * wn - o * (r * r * r * jnp.mean(o * wn, axis=-1, keepdims=True))
            dob = do.astype(BF16)
            e_all = _hg_exponents(sums_ref[...], g)
            da = lax.dot_general(dob, vb, _DIMS["nt"], preferred_element_type=F32)
            dqe = jnp.dot(dob, stb, preferred_element_type=F32)
            dkd = jnp.dot(vb, dstb, preferred_element_type=F32)
            yield
            eb = jnp.exp(e_all[0:C])
            esuf = jnp.exp(e_all[C:2 * C])
            ebl = eb[C - 1:C, :]
            qe = q * eb
            kd = k * esuf
            parts, prods = _hg_level_scores(q, k, e_all)
            dv_state = lax.dot_general(kd.astype(BF16), dstb, _DIMS["nt"], preferred_element_type=F32)
            ds_new = lax.dot_general(dob, qe.astype(BF16), _DIMS["tn"], preferred_element_type=F32)
            yield
            a = _hg_intra(prods, masks_ref)
            dv = lax.dot_general(a.astype(BF16), dob, _DIMS["tn"], preferred_element_type=F32)
            dlev = []
            for li in range(nlev):
                _, _, qt, kt = parts[li]
                dan = (masks_ref[li] * da).astype(BF16)
                dlev.append((jnp.dot(dan, kt.astype(BF16), preferred_element_type=F32),
                             lax.dot_general(dan, qt.astype(BF16), _DIMS["tn"], preferred_element_type=F32)))
            yield
            qk = jnp.sum(q * k, axis=1, keepdims=True)
            dv = dv + qk * do + dv_state
            dqk = jnp.sum(do * v, axis=1, keepdims=True)
            dq = dqk * k
            dk = dqk * q
            for li in range(nlev):
                eq, ek, qt, kt = parts[li]
                dqt, dkt = dlev[li]
                dq = dq + dqt * eq
                dk = dk + dkt * ek
                de_h[(2 + 2 * li) * C:(3 + 2 * li) * C, :] = dqt * qt
                de_h[(3 + 2 * li) * C:(4 + 2 * li) * C, :] = dkt * kt
            dq = dq + dqe * eb
            dk = dk + dkd * esuf
            debl = jnp.sum(dst * st, axis=0, keepdims=True)
            de_h[0:C, :] = dqe * qe + jnp.where(last_row, debl * ebl, 0.0)
            de_h[C:2 * C, :] = dkd * kd
            ds_ref[h] = dst * ebl + ds_new
            dehi, delo = _split2(de_h[...])
            dg = (lax.dot_general(sums_ref[...], dehi, _DIMS["tn"], preferred_element_type=F32)
                  + lax.dot_general(sums_ref[...], delo, _DIMS["tn"], preferred_element_type=F32))
            yield
            df = dg / f - dk
            dout_ref[0, rows, cols] = (dq * (sq * (1.0 + hq * (1.0 - sq)))).astype(BF16)
            dout_ref[1, rows, cols] = (df * (1.0 - lb) * (sf * (1.0 - sf))).astype(BF16)
            dout_ref[2, rows, cols] = dv.astype(BF16)
            dlb_acc[:, cols] += jnp.sum((df * (1.0 - sf)).reshape(C // SUBLANE, SUBLANE, HEAD_DIM), axis=0)

        @pl.when(kk == 0)
        def _():
            lax.fori_loop(0, cpb, chunk, 0)

        dp_ref[...] = dout_ref[kk]

        @pl.when(jnp.logical_and(i == nb - 1, kk == 3))
        def _():
            lb = _lower_bound(lbl_ref, slice(None))
            dl0 = jnp.sum(dlb_acc[...], axis=0, keepdims=True) * (lb * (1.0 - lb))
            dlb_ref[0:1, :] = dl0
            dlb_ref[1:2, :] = -dl0

        @pl.when(jnp.logical_and(jnp.logical_and(grp == groups - 1, i == nb - 1), kk == 3))
        def _():
            dgo_ref[...] = jnp.sum(dgo_acc[...], axis=0, keepdims=True)

    def col(k):
        return pl.BlockSpec((tb, W), lambda g, i, kk: (nb - 1 - i, cb + k * groups + g))

    rev = pl.BlockSpec((tb, W), lambda g, i, kk: (nb - 1 - i, g))
    return pl.pallas_call(
        body, name=name, grid=(groups, nb, 4),
        in_specs=[col(0), col(1), col(2), col(3),
                  pl.BlockSpec((2, W), lambda g, i, kk: (0, g)),
                  pl.BlockSpec((1, HEAD_DIM), lambda g, i, kk: (0, 0)),
                  pl.BlockSpec((nsum, C), lambda g, i, kk: (0, 0)),
                  pl.BlockSpec((nlev, C, C), lambda g, i, kk: (0, 0, 0)),
                  rev,
                  pl.BlockSpec((hps, cpb, HEAD_DIM, HEAD_DIM), lambda g, i, kk: (g, nb - 1 - i, 0, 0)),
                  rev,
                  pl.BlockSpec(memory_space=pl.ANY)],
        out_specs=[pl.BlockSpec((tb, W), lambda g, i, kk: (nb - 1 - i, cb + kk * groups + g)),
                   pl.BlockSpec((2, W), lambda g, i, kk: (0, g)),
                   pl.BlockSpec((1, HEAD_DIM), lambda g, i, kk: (0, 0))],
        out_shape=[jax.ShapeDtypeStruct(dproj.shape, BF16), jax.ShapeDtypeStruct((2, width), F32),
                   jax.ShapeDtypeStruct((1, HEAD_DIM), F32)],
        input_output_aliases={11: 0},
        scratch_shapes=[pltpu.VMEM((hps, HEAD_DIM, HEAD_DIM), F32), pltpu.VMEM((hps, nsum, HEAD_DIM), F32),
                        pltpu.VMEM((SUBLANE, W), F32), pltpu.VMEM((SUBLANE, HEAD_DIM), F32),
                        pltpu.VMEM((4, tb, W), BF16)],
        compiler_params=_params(("arbitrary", "arbitrary", "arbitrary"),
                                tb * W * 4 * 12 + hps * cpb * HEAD_DIM * HEAD_DIM * 4),
    )(proj, proj, proj, proj, lb_logits, g_out, sums, masks, o_saved, states, dy, dproj)


def _merge_fwd(proj, ma, mb, gate_col0, name):
    T, D = ma.shape
    tr = _tile(T, 256, SUBLANE)
    cw = _tile(D, 1024, LANE)
    nj = D // cw
    assert gate_col0 % cw == 0
    g0 = gate_col0 // cw

    def body(ga_ref, gb_ref, ma_ref, mb_ref, o_ref):
        o_ref[...] = (_sigmoid(ga_ref[...]) * ma_ref[...] + _sigmoid(gb_ref[...]) * mb_ref[...]).astype(BF16)

    blk = pl.BlockSpec((tr, cw), lambda i, j: (i, j))
    return pl.pallas_call(
        body, name=name, grid=(T // tr, nj),
        in_specs=[pl.BlockSpec((tr, cw), lambda i, j: (i, g0 + j)),
                  pl.BlockSpec((tr, cw), lambda i, j: (i, g0 + nj + j)), blk, blk],
        out_specs=blk, out_shape=jax.ShapeDtypeStruct((T, D), BF16),
        compiler_params=_params(("parallel", "parallel"), tr * cw * 18 * 2),
    )(proj, proj, ma, mb)


def _merge_bwd(proj, ma, mb, dm, gate_col0, name):
    T, D = ma.shape
    tr = _tile(T, 256, SUBLANE)
    cw = _tile(D, 1024, LANE)
    nj = D // cw
    g0 = gate_col0 // cw

    def body(g_ref, ma_ref, mb_ref, dm_ref, dp_ref, dmm_ref):
        j = pl.program_id(1)
        s = _sigmoid(g_ref[...])
        dmv = dm_ref[...]
        mm = jnp.where(j < nj, ma_ref[...], mb_ref[...])
        dp_ref[...] = (dmv * mm * (s * (1.0 - s))).astype(BF16)
        dmm_ref[...] = (dmv * s).astype(BF16)

    blk = pl.BlockSpec((tr, cw), lambda i, j: (i, j % nj))
    return pl.pallas_call(
        body, name=name, grid=(T // tr, 2 * nj),
        in_specs=[pl.BlockSpec((tr, cw), lambda i, j: (i, g0 + j)), blk, blk, blk],
        out_specs=[pl.BlockSpec((tr, cw), lambda i, j: (i, g0 + j)),
                   pl.BlockSpec((None, tr, cw), lambda i, j: (j // nj, i, j % nj))],
        out_shape=[jax.ShapeDtypeStruct(proj.shape, BF16), jax.ShapeDtypeStruct((2, T, D), BF16)],
        compiler_params=_params(("parallel", "parallel"), tr * cw * 20 * 2),
    )(proj, ma, mb, dm)


def _causal_conv(ext, w, b):
    s1 = pltpu.roll(ext, 1, 0)
    s2 = pltpu.roll(ext, 2, 0)
    out = b + w[0:1, :] * s2
    out = out + w[1:2, :] * s1
    out = out + w[2:3, :] * ext
    return out, s1, s2


def _conv_fwd(up, convw, convb, name):
    T, F2 = up.shape
    tc = convw.shape[2]
    half = (F2 // 2) // tc
    assert half * tc * 2 == F2
    tr = _tile(T, 256, SUBLANE)
    hb = tr // SUBLANE

    def body(g_ref, gp_ref, v_ref, vp_ref, wg_ref, wv_ref, bg_ref, bv_ref, o_ref):
        first = pl.program_id(1) == 0

        def conv(cur_ref, prev_ref, w_ref, b_ref):
            prev = jnp.where(first, 0.0, prev_ref[...])
            ext = jnp.concatenate([prev, cur_ref[...]], axis=0)
            return _causal_conv(ext, w_ref[...], b_ref[...])[0][SUBLANE:]

        gate = conv(g_ref, gp_ref, wg_ref, bg_ref)
        val = conv(v_ref, vp_ref, wv_ref, bv_ref)
        o_ref[...] = ((gate * _sigmoid(gate)) * val).astype(BF16)

    def main(off):
        return pl.BlockSpec((tr, tc), lambda j, i: (i, off + j))

    def prev(off):
        return pl.BlockSpec((SUBLANE, tc), lambda j, i: (jnp.maximum(i * hb - 1, 0), off + j))

    def wspec(off):
        return pl.BlockSpec((None, 3, tc), lambda j, i: (off + j, 0, 0))

    def bspec(off):
        return pl.BlockSpec((1, tc), lambda j, i: (0, off + j))

    return pl.pallas_call(
        body, name=name, grid=(half, T // tr),
        in_specs=[main(0), prev(0), main(half), prev(half), wspec(0), wspec(half), bspec(0), bspec(half)],
        out_specs=pl.BlockSpec((tr, tc), lambda j, i: (i, j)),
        out_shape=jax.ShapeDtypeStruct((T, F2 // 2), BF16),
        compiler_params=_params(("parallel", "parallel"), tr * tc * 4 * 12),
    )(up, up, up, up, convw, convw, convb, convb)


def _conv_bwd(da, up, convw, convb, name):
    T, F2 = up.shape
    tc = convw.shape[2]
    half = (F2 // 2) // tc
    tr = _tile(T, 128, SUBLANE)
    hb = tr // SUBLANE
    nrow = T // tr
    n = tr + 2 * SUBLANE

    def body(g_ref, gp_ref, gn_ref, v_ref, vp_ref, vn_ref, da_ref, dan_ref, wg_ref, wv_ref, bg_ref, bv_ref,
             dup_ref, gw_ref, acc_ref):
        j = pl.program_id(0)
        i = pl.program_id(1)
        first = i == 0
        last = i == nrow - 1
        is_gate = j < half

        def conv(cur_ref, prev_ref, next_ref, w_ref, b_ref):
            prev = jnp.where(first, 0.0, prev_ref[...])
            ext = jnp.concatenate([prev, cur_ref[...], next_ref[...]], axis=0)
            return (ext,) + _causal_conv(ext, w_ref[...], b_ref[...])

        g_ext, gate, g_s1, g_s2 = conv(g_ref, gp_ref, gn_ref, wg_ref, bg_ref)
        v_ext, val, v_s1, v_s2 = conv(v_ref, vp_ref, vn_ref, wv_ref, bv_ref)
        da_ext = jnp.concatenate([jnp.zeros((SUBLANE, tc), F32), da_ref[...],
                                  jnp.where(last, 0.0, dan_ref[...])], axis=0)
        sg = _sigmoid(gate)
        d_gate = da_ext * val * (sg * (1.0 + gate * (1.0 - sg)))
        d_val = da_ext * (gate * sg)
        mine = jnp.where(is_gate, d_gate, d_val)
        w = jnp.where(is_gate, wg_ref[...], wv_ref[...])
        dup = w[2:3, :] * mine + w[1:2, :] * pltpu.roll(mine, n - 1, 0) + w[0:1, :] * pltpu.roll(mine, n - 2, 0)
        dup_ref[...] = dup[SUBLANE:SUBLANE + tr].astype(BF16)

        rows = slice(SUBLANE, SUBLANE + tr)
        mine_m = mine[rows]
        taps = (jnp.where(is_gate, g_s2, v_s2)[rows], jnp.where(is_gate, g_s1, v_s1)[rows],
                jnp.where(is_gate, g_ext, v_ext)[rows])

        def fold(t):
            return jnp.sum(t.reshape(tr // SUBLANE, SUBLANE, tc), axis=0)

        @pl.when(first)
        def _():
            acc_ref[...] = jnp.zeros_like(acc_ref)

        acc_ref[0] += fold(mine_m)
        for k in range(3):
            acc_ref[1 + k] += fold(mine_m * taps[k])

        @pl.when(last)
        def _():
            gw_ref[...] = jnp.sum(acc_ref[...], axis=1)

    def main(off):
        return pl.BlockSpec((tr, tc), lambda j, i: (i, off + j % half))

    def prev(off):
        return pl.BlockSpec((SUBLANE, tc), lambda j, i: (jnp.maximum(i * hb - 1, 0), off + j % half))

    def nxt(off):
        return pl.BlockSpec((SUBLANE, tc), lambda j, i: (jnp.minimum((i + 1) * hb, T // SUBLANE - 1), off + j % half))

    def wspec(off):
        return pl.BlockSpec((None, 3, tc), lambda j, i: (off + j % half, 0, 0))

    def bspec(off):
        return pl.BlockSpec((1, tc), lambda j, i: (0, off + j % half))

    return pl.pallas_call(
        body, name=name, grid=(2 * half, nrow),
        in_specs=[main(0), prev(0), nxt(0), main(half), prev(half), nxt(half), main(0), nxt(0),
                  wspec(0), wspec(half), bspec(0), bspec(half)],
        out_specs=[pl.BlockSpec((tr, tc), lambda j, i: (i, j)), pl.BlockSpec((4, tc), lambda j, i: (0, j))],
        out_shape=[jax.ShapeDtypeStruct((T, F2), BF16), jax.ShapeDtypeStruct((4, F2), F32)],
        scratch_shapes=[pltpu.VMEM((4, SUBLANE, tc), F32)],
        compiler_params=_params(("parallel", "arbitrary"), n * tc * 4 * 24),
    )(up, up, up, up, up, up, da, da, convw, convw, convb, convb)


def _loss_head(y, target, name):
    T, D = y.shape
    tr = _tile(T, 256, SUBLANE)
    nrow = T // tr

    def body(y_ref, t_ref, d_ref, db_ref, l_ref, acc_ref):
        i = pl.program_id(0)
        diff = y_ref[...] - t_ref[...]
        dy = diff / D
        d_ref[...] = dy
        db_ref[...] = dy.astype(BF16)
        part = jnp.sum((diff * diff).reshape(tr // SUBLANE, SUBLANE, D), axis=0)

        @pl.when(i == 0)
        def _():
            acc_ref[...] = part

        @pl.when(i > 0)
        def _():
            acc_ref[...] += part

        @pl.when(i == nrow - 1)
        def _():
            col = jnp.sum(acc_ref[...], axis=0, keepdims=True)
            l_ref[...] = jnp.broadcast_to(0.5 * (jnp.sum(col, axis=1, keepdims=True) / D), (1, LANE))

    row = pl.BlockSpec((tr, D), lambda i: (i, 0))
    return pl.pallas_call(
        body, name=name, grid=(nrow,), in_specs=[row, row],
        out_specs=[row, row, pl.BlockSpec((1, LANE), lambda i: (0, 0))],
        out_shape=[jax.ShapeDtypeStruct((T, D), F32), jax.ShapeDtypeStruct((T, D), BF16),
                   jax.ShapeDtypeStruct((1, LANE), F32)],
        scratch_shapes=[pltpu.VMEM((SUBLANE, D), F32)],
        compiler_params=_params(("arbitrary",), tr * D * 14 * 2),
    )(y, target)


def _adamw(w, parts, m, v, name):
    R, C = w.shape
    P = parts.shape[0]
    tr = _tile(R, 64, SUBLANE)
    tc = _tile(C, 2048, LANE)

    def body(w_ref, p_ref, m_ref, v_ref, g_ref, d_ref, nm_ref, nv_ref):
        g = p_ref[0].astype(F32)
        for s in range(1, P):
            g = g + p_ref[s].astype(F32)
        wv = w_ref[...]
        nm = ADAM_B1 * m_ref[...] + (1.0 - ADAM_B1) * g
        nv = ADAM_B2 * v_ref[...] + (1.0 - ADAM_B2) * (g * g)
        m_hat = nm / (1.0 - ADAM_B1 ** ADAM_STEP)
        v_hat = nv / (1.0 - ADAM_B2 ** ADAM_STEP)
        g_ref[...] = g
        d_ref[...] = -ADAM_LR * (m_hat / (jnp.sqrt(v_hat) + ADAM_EPS) + ADAM_WD * wv)
        nm_ref[...] = nm
        nv_ref[...] = nv

    blk = pl.BlockSpec((tr, tc), lambda i, j: (i, j))
    shp = jax.ShapeDtypeStruct((R, C), F32)
    return pl.pallas_call(
        body, name=name, grid=(R // tr, C // tc),
        in_specs=[blk, pl.BlockSpec((P, tr, tc), lambda i, j: (0, i, j)), blk, blk],
        out_specs=[blk] * 4, out_shape=[shp] * 4,
        compiler_params=_params(("parallel", "parallel"), tr * tc * 4 * (P + 8) * 2),
    )(w, parts, m, v)


def _place():
    x, y, c = (lax.axis_index(a) for a in MESH_AXES)
    return x, y, c, 4 * x + 2 * y + c


def _peers(x, y, c):
    out = []
    for d in range(1, N_DEV):
        px = x + (d >> 2 & 1) - 2 * x * (d >> 2 & 1)
        py = y + (d >> 1 & 1) - 2 * y * (d >> 1 & 1)
        pc = c + (d & 1) - 2 * c * (d & 1)
        out.append(((px, py, pc), 4 * px + 2 * py + pc))
    return out


class _Side:
    def __init__(self, arrays, scatter):
        self.arrays, self.scatter, self.n = list(arrays), scatter, len(arrays)

    def in_specs(self):
        return [pl.BlockSpec(memory_space=pltpu.HBM)] * self.n

    out_specs = in_specs

    def out_shape(self):
        return [jax.ShapeDtypeStruct(a.shape if self.scatter else (N_DEV,) + a.shape, a.dtype) for a in self.arrays]

    def scratch(self):
        return [pltpu.SemaphoreType.DMA((self.n, N_DEV - 1)), pltpu.SemaphoreType.DMA((self.n, N_DEV - 1)),
                pltpu.SemaphoreType.DMA((self.n,))]

    def _copies(self, ins, outs, sems):
        send_sems, recv_sems, local_sems = sems
        x, y, c, me = _place()
        peers = _peers(x, y, c)
        local, sends, recvs = [], [], []
        for t in range(self.n):
            src_me = ins[t].at[me] if self.scatter else ins[t]
            local.append(pltpu.make_async_copy(src_me, outs[t].at[me], local_sems.at[t]))
            for d, (peer, pidx) in enumerate(peers):
                src = ins[t].at[pidx] if self.scatter else ins[t]
                sends.append(pltpu.make_async_remote_copy(
                    src_ref=src, dst_ref=outs[t].at[me], send_sem=send_sems.at[t, d], recv_sem=recv_sems.at[t, d],
                    device_id=peer, device_id_type=MESH_ID))
                recvs.append(pltpu.make_async_remote_copy(
                    src_ref=src, dst_ref=outs[t].at[pidx], send_sem=send_sems.at[t, d], recv_sem=recv_sems.at[t, d],
                    device_id=peer, device_id_type=MESH_ID))
        return local, sends, recvs

    def start(self, ins, outs, sems):
        local, sends, _ = self._copies(ins, outs, sems)
        for cp in local + sends:
            cp.start()

    def wait(self, ins, outs, sems):
        local, sends, recvs = self._copies(ins, outs, sems)
        for cp in recvs:
            cp.wait_recv()
        for cp in sends:
            cp.wait_send()
        for cp in local:
            cp.wait()


def _grid_edges(sizes):
    ids = [pl.program_id(a) for a in range(len(sizes))]
    first = functools.reduce(jnp.logical_and, [i == 0 for i in ids])
    last = functools.reduce(jnp.logical_and, [i == s - 1 for i, s in zip(ids, sizes)])
    return first, last


def _exchange(arrays, scatter, name):
    side = _Side(arrays, scatter)
    n = side.n

    def body(*refs):
        ins, outs, sems = refs[:n], refs[n:2 * n], refs[2 * n:]
        side.start(ins, outs, sems)
        side.wait(ins, outs, sems)

    return pl.pallas_call(
        body, name=name, in_specs=side.in_specs(), out_specs=side.out_specs(), out_shape=side.out_shape(),
        scratch_shapes=side.scratch(),
    )(*arrays)


def _all_reduce_small(vec, name):
    R = vec.shape[0]

    def body(v_ref, o_ref, gath_ref, send_sems, recv_sems):
        x, y, c, me = _place()
        sends, recvs = [], []
        for d, (peer, pidx) in enumerate(_peers(x, y, c)):
            cp = pltpu.make_async_remote_copy(
                src_ref=v_ref, dst_ref=gath_ref.at[me], send_sem=send_sems.at[d], recv_sem=recv_sems.at[d],
                device_id=peer, device_id_type=MESH_ID)
            cp.start()
            sends.append(cp)
            recvs.append(pltpu.make_async_remote_copy(
                src_ref=v_ref, dst_ref=gath_ref.at[pidx], send_sem=send_sems.at[d], recv_sem=recv_sems.at[d],
                device_id=peer, device_id_type=MESH_ID))
        gath_ref[me] = v_ref[...]
        for cp in recvs:
            cp.wait_recv()
        for cp in sends:
            cp.wait_send()
        total = gath_ref[0]
        for s in range(1, N_DEV):
            total = total + gath_ref[s]
        o_ref[...] = total

    vm = pl.BlockSpec(memory_space=pltpu.VMEM)
    return pl.pallas_call(
        body, name=name, in_specs=[vm], out_specs=vm, out_shape=jax.ShapeDtypeStruct(vec.shape, F32),
        scratch_shapes=[pltpu.VMEM((N_DEV, R, LANE), F32), pltpu.SemaphoreType.DMA((N_DEV - 1,)),
                        pltpu.SemaphoreType.DMA((N_DEV - 1,))],
        compiler_params=pltpu.CompilerParams(vmem_limit_bytes=_vmem(R * LANE * 4 * 12)),
    )(vec)


def _pack(parts):
    flat = jnp.concatenate([p.reshape(-1).astype(F32) for p in parts])
    rows = -(-flat.shape[0] // (LANE * SUBLANE)) * SUBLANE
    return jnp.pad(flat, (0, rows * LANE - flat.shape[0])).reshape(rows, LANE)


def _unpack(packed, shapes):
    flat = packed.reshape(-1)
    out, at = [], 0
    for s in shapes:
        size = int(np.prod(s))
        out.append(flat[at:at + size].reshape(s))
        at += size
    return out


def kernel(x, g_mix, w_in, g_q, g_k, lb_logits, g_hg_out, p_a, p_b, w_o, g_ffn, w_up, conv_w, conv_b, w_down, loss_target, m_g_mix, m_w_in, m_g_q, m_g_k, m_lb_logits, m_g_hg_out, m_p_a, m_p_b, m_w_o, m_g_ffn, m_w_up, m_conv_w, m_conv_b, m_w_down, v_g_mix, v_w_in, v_g_q, v_g_k, v_lb_logits, v_g_hg_out, v_p_a, v_p_b, v_w_o, v_g_ffn, v_w_up, v_conv_w, v_conv_b, v_w_down):
    assert x.shape[0] == 1 and lb_logits.shape[0] == 2
    xs, target = x[0], loss_target[0]
    T, D = xs.shape
    A = p_a.shape[1]
    HW = p_b.shape[1]
    hg_col0 = 3 * A
    gate_col0 = 3 * A + 4 * HW
    F = w_down.shape[1] * N_DEV
    assert w_in.shape[2] * N_DEV == gate_col0 + 2 * D and w_up.shape[2] * N_DEV == 2 * F

    (win_g,) = _exchange([w_in[0].astype(BF16)], False, "gather_w_in")
    later_weights = _Side([w_up[0].astype(BF16), p_a[0].astype(BF16), p_b[0].astype(BF16), w_o[0].astype(BF16),
                           w_down[0].astype(BF16), conv_w[0]], False)

    u = _rmsnorm_fwd(xs, g_mix, "norm_mix")
    proj = _matmul(u, win_g, mode="nn", b_shards=True, name="proj_in")
    gains = jnp.stack([g_q[0], g_k[0], jnp.ones_like(g_q[0])])[:, None, :]
    qkv = _qkv_prep(proj, gains, A, "qkv_prep")
    ya, ya32, (wup_g, pa_g, pb_g, wo_g, wdown_g, convw_g) = _sb_fwd(qkv, A, "sb_fwd", side=later_weights)
    wo_full = wo_g.reshape(D, D)
    wdown_full = wdown_g.reshape(F, D)
    yb, ob, states = _hg_fwd(proj, lb_logits, g_hg_out, hg_col0, HW, "hg_fwd")
    ma = _matmul(ya, pa_g, mode="nn", b_shards=True, name="proj_a")
    mb = _matmul(yb, pb_g, mode="nn", b_shards=True, name="proj_b")
    m = _merge_fwd(proj, ma, mb, gate_col0, "merge_fwd")
    h1 = _matmul(m, wo_full, mode="nn", add=xs, name="proj_o")
    u2 = _rmsnorm_fwd(h1, g_ffn, "norm_ffn")
    up = _matmul(u2, wup_g, mode="nn", b_shards=True, name="ffn_up")
    act = _conv_fwd(up, convw_g, conv_b, "conv_fwd")
    y = _matmul(act, wdown_full, mode="nn", add=h1, name="ffn_down")
    dy, dyb, loss_part = _loss_head(y, target, "loss_head")
    loss = lax.psum(loss_part[0, 0], MESH_AXES)

    dact = _matmul(dyb, wdown_full, mode="nt", name="d_act")
    g_wdown = _matmul(act, dyb, mode="tn", out_dtype=BF16, name="g_w_down")
    dup, g_conv = _conv_bwd(dact, up, convw_g, conv_b, "conv_bwd")
    g_wup, (r_wdown,) = _matmul(u2, dup, mode="tn", out_shards=True, out_dtype=BF16, tm_pref=1024, name="g_w_up",
                                side=_Side([g_wdown.reshape(N_DEV, F // N_DEV, D)], True))
    du2, (r_wup,) = _matmul(dup, wup_g, mode="nt", b_shards=True, name="d_u2", side=_Side([g_wup], True))
    dh1, dh1b, g_gffn = _rmsnorm_bwd(h1, g_ffn, du2, dy, "norm_ffn_bwd")
    dm = _matmul(dh1b, wo_full, mode="nt", name="d_m")
    g_wo = _matmul(m, dh1b, mode="tn", out_dtype=BF16, tm_pref=1024, name="g_w_o")
    dproj, dmab = _merge_bwd(proj, ma, mb, dm, gate_col0, "merge_bwd")
    dya = _matmul(dmab, pa_g, mode="nt", b_shards=True, lead_a=0, out_dtype=BF16, name="d_ya")
    g_pa = _matmul(ya, dmab, mode="tn", out_shards=True, out_dtype=BF16, lead_b=0, name="g_p_a")
    dyb_ = _matmul(dmab, pb_g, mode="nt", b_shards=True, lead_a=1, out_dtype=BF16, name="d_yb")
    g_pb = _matmul(yb, dmab, mode="tn", out_shards=True, out_dtype=BF16, lead_b=1, name="g_p_b")
    dproj, g_lb, g_ghg = _hg_bwd(proj, lb_logits, g_hg_out, ob, states, dyb_, dproj, hg_col0, HW, "hg_bwd")
    dq, dk, dv = _sb_bwd(qkv, ya32, dya, A, "sb_bwd")
    dproj, g_gains = _qkv_bwd(proj, gains, dq, dk, dv, dproj, "qkv_bwd")
    g_win, (r_wo, r_pa, r_pb) = _matmul(
        u, dproj, mode="tn", out_shards=True, out_dtype=BF16, tm_pref=1024, name="g_w_in",
        side=_Side([g_wo.reshape(N_DEV, D // N_DEV, D), g_pa, g_pb], True))
    du, (r_win,) = _matmul(dproj, win_g, mode="nt", b_shards=True, name="d_u", side=_Side([g_win], True))
    gx, _, g_gmix = _rmsnorm_bwd(xs, g_mix, du, dh1, "norm_mix_bwd")

    small = [g_gmix, g_gains[0], g_gains[1], g_lb, g_ghg, g_gffn, g_conv[0:1], g_conv[1:4]]
    small_shapes = [p.shape for p in small]
    red = _unpack(_all_reduce_small(_pack(small), "reduce_small"), small_shapes)

    _, _, _, me = _place()
    cs = conv_w.shape[2]
    big = {
        "w_in": _adamw(w_in[0], r_win, m_w_in[0], v_w_in[0], "adamw_w_in"),
        "w_up": _adamw(w_up[0], r_wup, m_w_up[0], v_w_up[0], "adamw_w_up"),
        "p_a": _adamw(p_a[0], r_pa, m_p_a[0], v_p_a[0], "adamw_p_a"),
        "p_b": _adamw(p_b[0], r_pb, m_p_b[0], v_p_b[0], "adamw_p_b"),
        "w_o": _adamw(w_o[0], r_wo, m_w_o[0], v_w_o[0], "adamw_w_o"),
        "w_down": _adamw(w_down[0], r_wdown, m_w_down[0], v_w_down[0], "adamw_w_down"),
        "conv_w": _adamw(conv_w[0], lax.dynamic_slice_in_dim(red[7], me * cs, cs, axis=1)[None],
                         m_conv_w[0], v_conv_w[0], "adamw_conv_w"),
    }
    rep_w = [g_mix, g_q, g_k, lb_logits, g_hg_out, g_ffn, conv_b]
    rep_m = [m_g_mix, m_g_q, m_g_k, m_lb_logits, m_g_hg_out, m_g_ffn, m_conv_b]
    rep_v = [v_g_mix, v_g_q, v_g_k, v_lb_logits, v_g_hg_out, v_g_ffn, v_conv_b]
    rep_shapes = [p.shape for p in rep_w]
    rep_out = _adamw(_pack(rep_w), _pack(red[:7])[None], _pack(rep_m), _pack(rep_v), "adamw_small")
    rep = [_unpack(o, rep_shapes) for o in rep_out]
    rep_names = ["g_mix", "g_q", "g_k", "lb_logits", "g_hg_out", "g_ffn", "conv_b"]

    order = ["g_mix", "w_in", "g_q", "g_k", "lb_logits", "g_hg_out", "p_a", "p_b", "w_o", "g_ffn", "w_up",
             "conv_w", "conv_b", "w_down"]

    def leaf(kind, pname):
        if pname in big:
            return big[pname][kind][None]
        return rep[kind][rep_names.index(pname)]

    return (loss, gx[None], *[leaf(kind, p) for kind in range(4) for p in order])
```

```python
import functools

import numpy as np
import jax
import jax.numpy as jnp
from jax import lax
from jax.experimental import pallas as pl
from jax.experimental.pallas import tpu as pltpu

F32 = jnp.float32
BF16 = jnp.bfloat16

N_DEV = 8
HEAD_DIM = 128
HG_CHUNK = 64
HG_LEVELS = (1, 2, 4, 8, 16, 32)
EPS = 1e-6
ADAM_LR = 0.001
ADAM_B1 = 0.9
ADAM_B2 = 0.999
ADAM_EPS = 1e-08
ADAM_WD = 0.01
ADAM_STEP = 10
LANE = 128
SUBLANE = 8
VMEM_CAP = 56 << 20
MESH_AXES = ("x", "y", "c")
MESH_ID = pl.DeviceIdType.MESH


def _tile(dim, pref, align):
    if dim <= pref:
        return dim
    t = (pref // align) * align
    while t >= align:
        if dim % t == 0:
            return t
        t -= align
    return dim


def _vmem(est_bytes):
    return int(min(max(2 * est_bytes + (8 << 20), 32 << 20), VMEM_CAP))


def _params(sem, est_bytes):
    return pltpu.CompilerParams(dimension_semantics=sem, vmem_limit_bytes=_vmem(est_bytes))


def _sigmoid(x):
    return 1.0 / (1.0 + jnp.exp(-x))


_DIMS = {"nn": (((1,), (0,)), ((), ())), "nt": (((1,), (1,)), ((), ())), "tn": (((0,), (0,)), ((), ()))}


def _matmul(a, b, *, mode, name, out_dtype=F32, add=None, b_shards=False, out_shards=False, lead_a=None,
            lead_b=None, a_halves=False, b_halves=False, side=None, tm_pref=512, tn_pref=1408, tk_pref=2048):
    if mode == "tn":
        K, M = a.shape[-2:]
    else:
        M, K = a.shape[-2:]
    if a_halves:
        assert mode == "nt" and b_shards and lead_a is None
        K = 2 * K
    if lead_b is not None or b_halves:
        assert not b_shards and mode == "tn"
    if b_shards:
        S = b.shape[0]
        if mode == "nn":
            assert b.shape[1] == K
            N, tn, tk = S * b.shape[2], b.shape[2], _tile(K, tk_pref, LANE)
        else:
            assert mode == "nt" and S * b.shape[2] == K
            N, tk, tn = b.shape[1], b.shape[2], _tile(b.shape[1], tn_pref, LANE)
    else:
        if mode == "nt":
            N = b.shape[0]
            assert b.shape[1] == K
        else:
            N = b.shape[-1] * (2 if b_halves else 1)
            assert b.shape[-2] == K
        tn = _tile(N, tn_pref, LANE)
        tk = _tile(K, tk_pref, LANE)
    if out_shards:
        assert N % N_DEV == 0
        tn = N // N_DEV
    per_half_n, per_half_k = (N // 2) // tn, (K // 2) // tk
    assert not b_halves or per_half_n * tn * 2 == N
    assert not a_halves or per_half_k * tk * 2 == K
    tm = _tile(M, tm_pref, LANE)
    nm, nn, nk = M // tm, N // tn, K // tk
    assert nm * tm == M and nn * tn == N and nk * tk == K

    if mode == "tn":
        a_spec = pl.BlockSpec((tk, tm), lambda j, i, k: (k, i))
    elif lead_a is not None:
        a_spec = pl.BlockSpec((None, tm, tk), lambda j, i, k: (lead_a, i, k))
    elif a_halves:
        a_spec = pl.BlockSpec((None, tm, tk), lambda j, i, k: (k // per_half_k, i, k % per_half_k))
    else:
        a_spec = pl.BlockSpec((tm, tk), lambda j, i, k: (i, k))
    if b_halves:
        b_spec = pl.BlockSpec((None, tk, tn), lambda j, i, k: (j // per_half_n, k, j % per_half_n))
    elif lead_b is not None:
        b_spec = pl.BlockSpec((None, tk, tn), lambda j, i, k: (lead_b, k, j))
    elif b_shards and mode == "nn":
        b_spec = pl.BlockSpec((None, tk, tn), lambda j, i, k: (j, k, 0))
    elif b_shards:
        b_spec = pl.BlockSpec((None, tn, tk), lambda j, i, k: (k, j, 0))
    elif mode == "nt":
        b_spec = pl.BlockSpec((tn, tk), lambda j, i, k: (j, k))
    else:
        b_spec = pl.BlockSpec((tk, tn), lambda j, i, k: (k, j))
    in_specs = [a_spec, b_spec]
    operands = [a, b]
    if add is not None:
        assert not out_shards and add.shape == (M, N)
        in_specs.append(pl.BlockSpec((tm, tn), lambda j, i, k: (i, j)))
        operands.append(add)
    if out_shards:
        out_shape = jax.ShapeDtypeStruct((N_DEV, M, tn), out_dtype)
        out_spec = pl.BlockSpec((None, tm, tn), lambda j, i, k: (j, i, 0))
    else:
        out_shape = jax.ShapeDtypeStruct((M, N), out_dtype)
        out_spec = pl.BlockSpec((tm, tn), lambda j, i, k: (i, j))
    dims = _DIMS[mode]
    has_add = add is not None
    n_in = 3 if has_add else 2
    n_side = side.n if side is not None else 0

    def body(*refs):
        a_ref, b_ref = refs[0], refs[1]
        add_ref = refs[2] if has_add else None
        o_ref = refs[n_in + n_side]
        acc_ref = refs[n_in + 2 * n_side + 1]
        k = pl.program_id(2)
        if side is not None:
            side_refs = (refs[n_in:n_in + n_side], refs[n_in + n_side + 1:n_in + 2 * n_side + 1],
                         refs[n_in + 2 * n_side + 2:])
            first, last = _grid_edges((nn, nm, nk))

            @pl.when(first)
            def _():
                side.start(*side_refs)

        part = lax.dot_general(a_ref[...], b_ref[...], dims, preferred_element_type=F32)

        def finish(total):
            if has_add:
                total = add_ref[...] + total
            o_ref[...] = total.astype(out_dtype)

        if nk == 1:
            finish(part)
        else:
            @pl.when(k == 0)
            def _():
                acc_ref[...] = part

            @pl.when(jnp.logical_and(k > 0, k < nk - 1))
            def _():
                acc_ref[...] += part

            @pl.when(k == nk - 1)
            def _():
                finish(acc_ref[...] + part)

        if side is not None:
            @pl.when(last)
            def _():
                side.wait(*side_refs)

    est = 2 * (tm * tk * 2 + tk * tn * 2 + tm * tn * 4 * (2 if has_add else 1)) + tm * tn * 4 * 2
    acc = pltpu.VMEM((tm, tn) if nk > 1 else (SUBLANE, LANE), F32)
    if side is None:
        return pl.pallas_call(
            body, name=name, grid=(nn, nm, nk), in_specs=in_specs, out_specs=out_spec, out_shape=out_shape,
            scratch_shapes=[acc], compiler_params=_params(("parallel", "parallel", "arbitrary"), est),
        )(*operands)
    outs = pl.pallas_call(
        body, name=name, grid=(nn, nm, nk), in_specs=in_specs + side.in_specs(),
        out_specs=[out_spec] + side.out_specs(), out_shape=[out_shape] + side.out_shape(),
        scratch_shapes=[acc] + side.scratch(), compiler_params=_params(("arbitrary",) * 3, est),
    )(*operands, *side.arrays)
    return outs[0], outs[1:]


def _rmsnorm_fwd(x, g, name):
    T, D = x.shape
    tr = _tile(T, 256, SUBLANE)

    def body(x_ref, g_ref, o_ref):
        xf = x_ref[...]
        r = lax.rsqrt(jnp.mean(xf * xf, axis=-1, keepdims=True) + EPS)
        o_ref[...] = ((xf * r) * g_ref[...]).astype(BF16)

    return pl.pallas_call(
        body, name=name, grid=(T // tr,),
        in_specs=[pl.BlockSpec((tr, D), lambda i: (i, 0)), pl.BlockSpec((1, D), lambda i: (0, 0))],
        out_specs=pl.BlockSpec((tr, D), lambda i: (i, 0)), out_shape=jax.ShapeDtypeStruct((T, D), BF16),
        compiler_params=_params(("parallel",), tr * D * 6 * 2),
    )(x, g)


def _rmsnorm_bwd(x, g, dy, res, name):
    T, D = x.shape
    tr = _tile(T, 256, SUBLANE)
    nsteps = T // tr

    def body(x_ref, g_ref, dy_ref, res_ref, dx_ref, dxb_ref, dg_ref, acc_ref):
        i = pl.program_id(0)
        xf = x_ref[...]
        dyf = dy_ref[...].astype(F32)
        r = lax.rsqrt(jnp.mean(xf * xf, axis=-1, keepdims=True) + EPS)
        w = dyf * g_ref[...]
        s = jnp.mean(xf * w, axis=-1, keepdims=True)
        dx = res_ref[...] + (r * w - xf * (r * r * r * s))
        dx_ref[...] = dx
        dxb_ref[...] = dx.astype(BF16)
        part = jnp.sum((dyf * (xf * r)).reshape(tr // SUBLANE, SUBLANE, D), axis=0)

        @pl.when(i == 0)
        def _():
            acc_ref[...] = part

        @pl.when(i > 0)
        def _():
            acc_ref[...] += part

        @pl.when(i == nsteps - 1)
        def _():
            dg_ref[...] = jnp.sum(acc_ref[...], axis=0, keepdims=True)

    row = pl.BlockSpec((tr, D), lambda i: (i, 0))
    vec = pl.BlockSpec((1, D), lambda i: (0, 0))
    return pl.pallas_call(
        body, name=name, grid=(nsteps,), in_specs=[row, vec, row, row], out_specs=[row, row, vec],
        out_shape=[jax.ShapeDtypeStruct((T, D), F32), jax.ShapeDtypeStruct((T, D), BF16),
                   jax.ShapeDtypeStruct((1, D), F32)],
        scratch_shapes=[pltpu.VMEM((SUBLANE, D), F32)],
        compiler_params=_params(("arbitrary",), tr * D * 18 * 2),
    )(x, g, dy, res)


def _qkv_prep(proj, gains, width, name):
    T = proj.shape[0]
    tr = _tile(T, 256, SUBLANE)
    heads = width // HEAD_DIM

    def body(p_ref, g_ref, o_ref):
        j = pl.program_id(1)
        g = g_ref[...]
        for h in range(heads):
            xh = p_ref[:, h * HEAD_DIM:(h + 1) * HEAD_DIM]
            r = lax.rsqrt(jnp.mean(xh * xh, axis=-1, keepdims=True) + EPS)
            r = jnp.where(j < 2, r, 1.0)
            o_ref[:, h * HEAD_DIM:(h + 1) * HEAD_DIM] = ((xh * r) * g).astype(BF16)

    return pl.pallas_call(
        body, name=name, grid=(T // tr, 3),
        in_specs=[pl.BlockSpec((tr, width), lambda i, j: (i, j)),
                  pl.BlockSpec((None, 1, HEAD_DIM), lambda i, j: (j, 0, 0))],
        out_specs=pl.BlockSpec((tr, width), lambda i, j: (i, j)),
        out_shape=jax.ShapeDtypeStruct((T, 3 * width), BF16),
        compiler_params=_params(("parallel", "parallel"), tr * width * 6 * 2),
    )(proj, gains)


def _qkv_bwd(proj, gains, dq, dk, dv, dproj, name):
    T = proj.shape[0]
    width = dq.shape[1]
    tr = _tile(T, 256, SUBLANE)
    heads = width // HEAD_DIM
    nrow = T // tr

    def body(p_ref, g_ref, dq_ref, dk_ref, dv_ref, dp_in, o_ref, dg_ref, acc_ref):
        del dp_in
        i = pl.program_id(0)
        j = pl.program_id(1)
        g = g_ref[...]

        @pl.when(jnp.logical_and(i == 0, j == 0))
        def _():
            acc_ref[...] = jnp.zeros_like(acc_ref)

        part = jnp.zeros((SUBLANE, HEAD_DIM), F32)
        for h in range(heads):
            sl = slice(h * HEAD_DIM, (h + 1) * HEAD_DIM)
            xh = p_ref[:, sl]
            dyh = jnp.where(j == 0, dq_ref[:, sl], jnp.where(j == 1, dk_ref[:, sl], dv_ref[:, sl]))
            r = lax.rsqrt(jnp.mean(xh * xh, axis=-1, keepdims=True) + EPS)
            w = dyh * g
            s = jnp.mean(xh * w, axis=-1, keepdims=True)
            dx = r * w - xh * (r * r * r * s)
            o_ref[:, sl] = jnp.where(j < 2, dx, dyh).astype(BF16)
            part = part + jnp.sum((dyh * (xh * r)).reshape(tr // SUBLANE, SUBLANE, HEAD_DIM), axis=0)
        acc_ref[j] += part

        @pl.when(jnp.logical_and(i == nrow - 1, j == 2))
        def _():
            dg_ref[...] = jnp.sum(acc_ref[...], axis=1, keepdims=True)

    blk = pl.BlockSpec((tr, width), lambda i, j: (i, j))
    dblk = pl.BlockSpec((tr, width), lambda i, j: (i, 0))
    return pl.pallas_call(
        body, name=name, grid=(nrow, 3),
        in_specs=[blk, pl.BlockSpec((None, 1, HEAD_DIM), lambda i, j: (j, 0, 0)), dblk, dblk, dblk,
                  pl.BlockSpec(memory_space=pl.ANY)],
        out_specs=[blk, pl.BlockSpec((3, 1, HEAD_DIM), lambda i, j: (0, 0, 0))],
        out_shape=[jax.ShapeDtypeStruct(dproj.shape, BF16), jax.ShapeDtypeStruct((3, 1, HEAD_DIM), F32)],
        input_output_aliases={5: 0},
        scratch_shapes=[pltpu.VMEM((3, SUBLANE, HEAD_DIM), F32)],
        compiler_params=_params(("arbitrary", "arbitrary"), tr * width * 18 * 2),
    )(proj, gains, dq, dk, dv, dproj)


def _split2(x):
    hi = x.astype(BF16)
    lo = (x - hi.astype(F32)).astype(BF16)
    return hi, lo


LOG2E = 1.4426950408889634


def _sb_logits(q, kb):
    return lax.dot_general(q, kb, _DIMS["nt"], preferred_element_type=F32) * (HEAD_DIM ** -0.5 * LOG2E)


def _sb_scores(z2, mask):
    lk = -(jnp.maximum(z2, 0.0) + jnp.log2(1.0 + jnp.exp2(-jnp.abs(z2))))
    if mask is not None:
        lk = jnp.where(mask, lk, 0.0)
    return lk, lk + z2


def _sb_block_size(T):
    return _tile(T, 256, LANE)


SB_HEADS_PER_STEP = 2


def _sb_later(lk, upper, carry):
    hi, lo = _split2(lk)
    return (jnp.dot(hi, upper, preferred_element_type=F32) + jnp.dot(lo, upper, preferred_element_type=F32)) + carry


def _sb_fwd(qkv, width, name, side=None):
    T = qkv.shape[0]
    heads = width // HEAD_DIM
    bq = _sb_block_size(T)
    hps = min(2 * SB_HEADS_PER_STEP, heads)
    assert heads % hps == 0
    groups, W = heads // hps, hps * HEAD_DIM

    n_side = side.n if side is not None else 0

    def body(*refs):
        q_ref, k_ref, v_ref = refs[:3]
        o_ref, o32_ref = refs[3 + n_side:5 + n_side]
        if side is not None:
            side_refs = (refs[3:3 + n_side], refs[5 + n_side:5 + 2 * n_side], refs[5 + 2 * n_side:])
            first, last = _grid_edges((groups, T // bq))

            @pl.when(first)
            def _():
                side.start(*side_refs)

        i = pl.program_id(1)
        row = lax.broadcasted_iota(jnp.int32, (bq, bq), 0)
        col = lax.broadcasted_iota(jnp.int32, (bq, bq), 1)
        upper = (row > col).astype(BF16)
        causal = col < row

        def head_block(start, h, carry, acc, mask):
            cols = slice(h * HEAD_DIM, (h + 1) * HEAD_DIM)
            z2 = _sb_logits(q_ref[:, cols], k_ref[pl.ds(start, bq), cols])
            yield
            lk, lb = _sb_scores(z2, mask)
            later = _sb_later(lk, upper, carry)
            yield
            w = jnp.exp2(lb + later)
            if mask is not None:
                w = jnp.where(mask, w, 0.0)
            pv = jnp.dot(w.astype(BF16), v_ref[pl.ds(start, bq), cols], preferred_element_type=F32)
            yield
            return carry + jnp.sum(lk, axis=1, keepdims=True), acc + pv

        def block(j, state, mask):
            start = pl.multiple_of(j * bq, bq)
            return tuple(_lockstep(head_block(start, h, state[h][0], state[h][1], mask) for h in range(hps)))

        zero = (jnp.zeros((bq, 1), F32), jnp.zeros((bq, HEAD_DIM), F32))
        state = block(i, (zero,) * hps, causal)
        state = lax.fori_loop(0, i, lambda n, st: block(i - 1 - n, st, None), state)
        for h in range(hps):
            cols = slice(h * HEAD_DIM, (h + 1) * HEAD_DIM)
            o_ref[:, cols] = state[h][1].astype(BF16)
            o32_ref[:, cols] = state[h][1]

        if side is not None:
            @pl.when(last)
            def _():
                side.wait(*side_refs)

    oblk = pl.BlockSpec((bq, W), lambda g, i: (i, g))
    in_specs = [oblk, pl.BlockSpec((T, W), lambda g, i: (0, groups + g)),
                pl.BlockSpec((T, W), lambda g, i: (0, 2 * groups + g))]
    out_shape = [jax.ShapeDtypeStruct((T, width), BF16), jax.ShapeDtypeStruct((T, width), F32)]
    est = 2 * T * W * 2 * 2 + hps * 16 * bq * bq * 4
    if side is None:
        return pl.pallas_call(
            body, name=name, grid=(groups, T // bq), in_specs=in_specs, out_specs=[oblk, oblk], out_shape=out_shape,
            compiler_params=_params(("parallel", "arbitrary"), est),
        )(qkv, qkv, qkv)
    outs = pl.pallas_call(
        body, name=name, grid=(groups, T // bq), in_specs=in_specs + side.in_specs(),
        out_specs=[oblk, oblk] + side.out_specs(), out_shape=out_shape + side.out_shape(),
        scratch_shapes=side.scratch(), compiler_params=_params(("arbitrary", "arbitrary"), est),
    )(qkv, qkv, qkv, *side.arrays)
    return outs[0], outs[1], outs[2:]


def _sb_bwd(qkv, y, dy, width, name):
    T = qkv.shape[0]
    heads = width // HEAD_DIM
    bq = _sb_block_size(T)
    nq = T // bq
    scale = HEAD_DIM ** -0.5
    hps = min(SB_HEADS_PER_STEP, heads)
    groups, W = heads // hps, hps * HEAD_DIM

    def body(q_ref, k_ref, v_ref, y_ref, dy_ref, dq_ref, dk_hbm, dv_hbm, dk_acc, dv_acc, out_sems):
        g_id = pl.program_id(0)
        i = pl.program_id(1)
        row = lax.broadcasted_iota(jnp.int32, (bq, bq), 0)
        col = lax.broadcasted_iota(jnp.int32, (bq, bq), 1)
        upper = (row > col).astype(BF16)
        upper_incl = (row >= col).astype(BF16)
        causal = col < row
        deltas = []
        for h in range(hps):
            cols = slice(h * HEAD_DIM, (h + 1) * HEAD_DIM)
            deltas.append(jnp.sum(dy_ref[:, cols].astype(F32) * y_ref[:, cols], axis=1, keepdims=True))

        @pl.when(i == 0)
        def _():
            dk_acc[...] = jnp.zeros_like(dk_acc)
            dv_acc[...] = jnp.zeros_like(dv_acc)

        def head_block(start, h, carry, gcarry, dq, mask):
            cols = slice(h * HEAD_DIM, (h + 1) * HEAD_DIM)
            q = q_ref[:, cols]
            do = dy_ref[:, cols]
            kb = k_ref[pl.ds(start, bq), cols]
            vb = v_ref[pl.ds(start, bq), cols]
            z2 = _sb_logits(q, kb)
            dw = lax.dot_general(do, vb, _DIMS["nt"], preferred_element_type=F32)
            yield
            lk, lb = _sb_scores(z2, mask)
            later = _sb_later(lk, upper, carry)
            yield
            w = jnp.exp2(lb + later)
            if mask is not None:
                w = jnp.where(mask, w, 0.0)
            wb = w.astype(BF16)
            gw = dw * wb.astype(F32)
            gsuf = _sb_later(gw, upper_incl, gcarry)
            dvp = lax.dot_general(wb, do, _DIMS["tn"], preferred_element_type=F32)
            yield
            dz = gw - jnp.exp2(lb) * (gw + (deltas[h] - gsuf))
            if mask is not None:
                dz = jnp.where(mask, dz, 0.0)
            dzs = (dz * scale).astype(BF16)
            dqp = jnp.dot(dzs, kb, preferred_element_type=F32)
            dkp = lax.dot_general(dzs, q, _DIMS["tn"], preferred_element_type=F32)
            yield
            dk_acc[pl.ds(start, bq), cols] += dkp
            dv_acc[pl.ds(start, bq), cols] += dvp
            return (carry + jnp.sum(lk, axis=1, keepdims=True), gcarry + jnp.sum(gw, axis=1, keepdims=True), dq + dqp)

        def block(j, state, mask):
            start = pl.multiple_of(j * bq, bq)
            return tuple(_lockstep(head_block(start, h, *state[h], mask) for h in range(hps)))

        zero = jnp.zeros((bq, 1), F32)
        state = block(i, ((zero, zero, jnp.zeros((bq, HEAD_DIM), F32)),) * hps, causal)
        state = lax.fori_loop(0, i, lambda n, st: block(i - 1 - n, st, None), state)
        for h in range(hps):
            dq_ref[:, h * HEAD_DIM:(h + 1) * HEAD_DIM] = state[h][2]

        @pl.when(i == nq - 1)
        def _():
            cols = pl.ds(pl.multiple_of(g_id * W, W), W)
            copies = [pltpu.make_async_copy(dk_acc, dk_hbm.at[:, cols], out_sems.at[0]),
                      pltpu.make_async_copy(dv_acc, dv_hbm.at[:, cols], out_sems.at[1])]
            for cp in copies:
                cp.start()
            for cp in copies:
                cp.wait()

    qblk = pl.BlockSpec((bq, W), lambda g, i: (i, g))
    out = jax.ShapeDtypeStruct((T, width), F32)
    return pl.pallas_call(
        body, name=name, grid=(groups, nq),
        in_specs=[qblk, pl.BlockSpec((T, W), lambda g, i: (0, groups + g)),
                  pl.BlockSpec((T, W), lambda g, i: (0, 2 * groups + g)), qblk, qblk],
        out_specs=[qblk, pl.BlockSpec(memory_space=pl.ANY), pl.BlockSpec(memory_space=pl.ANY)],
        out_shape=[out, out, out],
        scratch_shapes=[pltpu.VMEM((T, W), F32), pltpu.VMEM((T, W), F32), pltpu.SemaphoreType.DMA((2,))],
        compiler_params=_params(("arbitrary", "arbitrary"), 2 * T * W * 2 * 2 + 2 * T * W * 4),
    )(qkv, qkv, qkv, y, dy)


def _hg_constants():
    C = HG_CHUNK
    t = np.arange(C)[:, None]
    j = np.arange(C)[None, :]
    blocks = [(j <= t), (j > t)]
    masks = []
    for n in HG_LEVELS:
        right = (t % (2 * n)) >= n
        mid = (t // (2 * n)) * (2 * n) + n - 1
        blocks.append(right & (j > mid) & (j <= t))
        blocks.append((~right) & (j > t) & (j <= mid))
        tt, ss = np.arange(C)[:, None], np.arange(C)[None, :]
        same = (tt // (2 * n)) == (ss // (2 * n))
        masks.append(same & ((tt % (2 * n)) >= n) & ((ss % (2 * n)) < n))
    sums = np.concatenate(blocks, axis=0).astype(np.float32)
    return jnp.asarray(sums, BF16), jnp.asarray(np.stack(masks).astype(np.float32))


def _split3(x):
    hi = x.astype(BF16)
    r1 = x - hi.astype(F32)
    mid = r1.astype(BF16)
    lo = (r1 - mid.astype(F32)).astype(BF16)
    return hi, mid, lo


def _hg_gates(hq, hf, lb):
    sq = _sigmoid(hq)
    sf = _sigmoid(hf)
    f = lb + (1.0 - lb) * sf
    return hq * sq, sq, f, sf, 1.0 - f, jnp.log(f)


def _hg_exponents(sums, g):
    hi, mid, lo = _split3(g)
    return (jnp.dot(sums, hi, preferred_element_type=F32) + jnp.dot(sums, mid, preferred_element_type=F32)
            + jnp.dot(sums, lo, preferred_element_type=F32))


def _hg_level_scores(q, k, e_all):
    C = HG_CHUNK
    parts, prods = [], []
    for li in range(len(HG_LEVELS)):
        eq = jnp.exp(e_all[(2 + 2 * li) * C:(3 + 2 * li) * C])
        ek = jnp.exp(e_all[(3 + 2 * li) * C:(4 + 2 * li) * C])
        qt = q * eq
        kt = k * ek
        prods.append(lax.dot_general(qt.astype(BF16), kt.astype(BF16), _DIMS["nt"], preferred_element_type=F32))
        parts.append((eq, ek, qt, kt))
    return parts, prods


def _hg_intra(prods, masks_ref):
    a = masks_ref[0] * prods[0]
    for li in range(1, len(HG_LEVELS)):
        a = a + masks_ref[li] * prods[li]
    return a


def _lockstep(gens):
    gens = list(gens)
    results = [None] * len(gens)
    alive = list(range(len(gens)))
    while alive:
        for idx in list(alive):
            try:
                next(gens[idx])
            except StopIteration as done:
                results[idx] = done.value
                alive.remove(idx)
    return results


def _lower_bound(lbl_ref, cols):
    return _sigmoid(lbl_ref[0:1, cols] - lbl_ref[1:2, cols])


HG_HEADS_PER_STEP = 4


def _hg_fwd(proj, lb_logits, g_out, col0, width, name):
    T = proj.shape[0]
    heads = width // HEAD_DIM
    C = HG_CHUNK
    tb = _tile(T, 512, C)
    cpb = tb // C
    nb = T // tb
    sums, masks = _hg_constants()
    hps = min(HG_HEADS_PER_STEP, heads)
    groups, W = heads // hps, hps * HEAD_DIM
    assert col0 % W == 0 and width % W == 0
    cb = col0 // W

    def body(hq_ref, hf_ref, hi_ref, og_ref, lbl_ref, go_ref, sums_ref, masks_ref, y_ref, o_ref, st_ref, s_ref):
        i = pl.program_id(1)

        @pl.when(i == 0)
        def _():
            s_ref[...] = jnp.zeros_like(s_ref)

        go = go_ref[...]

        def chunk_head(c, rows, h):
            cols = slice(h * HEAD_DIM, (h + 1) * HEAD_DIM)
            lb = _lower_bound(lbl_ref, cols)
            q, _, _, _, k, g = _hg_gates(hq_ref[rows, cols], hf_ref[rows, cols], lb)
            v = hi_ref[rows, cols]
            vb = v.astype(BF16)
            st = s_ref[h]
            st_ref[h, c] = st
            e_all = _hg_exponents(sums_ref[...], g)
            yield
            b = e_all[0:C]
            ebl = jnp.exp(b[C - 1:C, :])
            qe = q * jnp.exp(b)
            o_inter = lax.dot_general(qe.astype(BF16), st.astype(BF16), _DIMS["nt"], preferred_element_type=F32)
            kd = k * jnp.exp(e_all[C:2 * C])
            s_new = lax.dot_general(vb, kd.astype(BF16), _DIMS["tn"], preferred_element_type=F32)
            _, prods = _hg_level_scores(q, k, e_all)
            yield
            a = _hg_intra(prods, masks_ref)
            o_intra = jnp.dot(a.astype(BF16), vb, preferred_element_type=F32)
            s_ref[h] = st * ebl + s_new
            yield
            o = (o_inter + o_intra) + jnp.sum(q * k, axis=1, keepdims=True) * v
            o_ref[rows, cols] = o
            r = lax.rsqrt(jnp.mean(o * o, axis=-1, keepdims=True) + EPS)
            og = og_ref[rows, cols]
            y_ref[rows, cols] = (((o * r) * go) * (og * _sigmoid(og))).astype(BF16)

        def chunk(c, _):
            rows = pl.ds(pl.multiple_of(c * C, C), C)
            _lockstep(chunk_head(c, rows, h) for h in range(hps))
            return 0

        lax.fori_loop(0, cpb, chunk, 0)

    def col(k):
        return pl.BlockSpec((tb, W), lambda g, i: (i, cb + k * groups + g))

    nsum = sums.shape[0]
    out_blk = pl.BlockSpec((tb, W), lambda g, i: (i, g))
    return pl.pallas_call(
        body, name=name, grid=(groups, nb),
        in_specs=[col(0), col(1), col(2), col(3),
                  pl.BlockSpec((2, W), lambda g, i: (0, g)),
                  pl.BlockSpec((1, HEAD_DIM), lambda g, i: (0, 0)),
                  pl.BlockSpec((nsum, C), lambda g, i: (0, 0)),
                  pl.BlockSpec((len(HG_LEVELS), C, C), lambda g, i: (0, 0, 0))],
        out_specs=[out_blk, out_blk,
                   pl.BlockSpec((hps, cpb, HEAD_DIM, HEAD_DIM), lambda g, i: (g, i, 0, 0))],
        out_shape=[jax.ShapeDtypeStruct((T, width), BF16), jax.ShapeDtypeStruct((T, width), F32),
                   jax.ShapeDtypeStruct((heads, T // C, HEAD_DIM, HEAD_DIM), F32)],
        scratch_shapes=[pltpu.VMEM((hps, HEAD_DIM, HEAD_DIM), F32)],
        compiler_params=_params(("parallel", "arbitrary"), tb * W * 4 * 7 + hps * cpb * HEAD_DIM * HEAD_DIM * 4),
    )(proj, proj, proj, proj, lb_logits, g_out, sums, masks)


def _hg_bwd(proj, lb_logits, g_out, o_saved, states, dy, dproj, col0, width, name):
    T = proj.shape[0]
    heads = width // HEAD_DIM
    C = HG_CHUNK
    tb = _tile(T, 512, C)
    cpb = tb // C
    nb = T // tb
    sums, masks = _hg_constants()
    nsum = sums.shape[0]
    nlev = len(HG_LEVELS)
    hps = min(HG_HEADS_PER_STEP, heads)
    groups, W = heads // hps, hps * HEAD_DIM
    cb = col0 // W

    def body(hq_ref, hf_ref, hi_ref, og_ref, lbl_ref, go_ref, sums_ref, masks_ref, o_ref, st_ref, dy_ref, dp_in,
             dp_ref, dlb_ref, dgo_ref, ds_ref, de_ref, dlb_acc, dgo_acc, dout_ref):
        del dp_in
        grp = pl.program_id(0)
        i = pl.program_id(1)
        kk = pl.program_id(2)

        @pl.when(jnp.logical_and(i == 0, kk == 0))
        def _():
            ds_ref[...] = jnp.zeros_like(ds_ref)
            dlb_acc[...] = jnp.zeros_like(dlb_acc)

        @pl.when(jnp.logical_and(jnp.logical_and(grp == 0, i == 0), kk == 0))
        def _():
            dgo_acc[...] = jnp.zeros_like(dgo_acc)

        go = go_ref[...]
        last_row = lax.broadcasted_iota(jnp.int32, (C, HEAD_DIM), 0) == C - 1

        def chunk(n, _):
            c = cpb - 1 - n
            rows = pl.ds(pl.multiple_of(c * C, C), C)
            _lockstep(chunk_head(c, rows, h) for h in range(hps))
            return 0

        def chunk_head(c, rows, h):
            cols = slice(h * HEAD_DIM, (h + 1) * HEAD_DIM)
            de_h = de_ref.at[h]
            lb = _lower_bound(lbl_ref, cols)
            hq = hq_ref[rows, cols]
            og = og_ref[rows, cols]
            q, sq, f, sf, k, g = _hg_gates(hq, hf_ref[rows, cols], lb)
            v = hi_ref[rows, cols]
            vb = v.astype(BF16)
            st = st_ref[h, c]
            stb = st.astype(BF16)
            dst = ds_ref[h]
            dstb = dst.astype(BF16)
            o = o_ref[rows, cols]
            dyc = dy_ref[rows, cols].astype(F32)
            sg = _sigmoid(og)
            r = lax.rsqrt(jnp.mean(o * o, axis=-1, keepdims=True) + EPS)
            on = (o * r) * go
            don = dyc * (og * sg)
            dout_ref[3, rows, cols] = (dyc * on * (sg * (1.0 + og * (1.0 - sg)))).astype(BF16)
            dgo_acc[...] += jnp.sum((don * (o * r)).reshape(C // SUBLANE, SUBLANE, HEAD_DIM), axis=0)
            wn = don * go
            do = r * wn - o * (r * r * r * jnp.mean(o * wn, axis=-1, keepdims=True))
            dob = do.astype(BF16)
            e_all = _hg_exponents(sums_ref[...], g)
            da = lax.dot_general(dob, vb, _DIMS["nt"], preferred_element_type=F32)
            dqe = jnp.dot(dob, stb, preferred_element_type=F32)
            dkd = jnp.dot(vb, dstb, preferred_element_type=F32)
            yield
            eb = jnp.exp(e_all[0:C])
            esuf = jnp.exp(e_all[C:2 * C])
            ebl = eb[C - 1:C, :]
            qe = q * eb
            kd = k * esuf
            parts, prods = _hg_level_scores(q, k, e_all)
            dv_state = lax.dot_general(kd.astype(BF16), dstb, _DIMS["nt"], preferred_element_type=F32)
            ds_new = lax.dot_general(dob, qe.astype(BF16), _DIMS["tn"], preferred_element_type=F32)
            yield
            a = _hg_intra(prods, masks_ref)
            dv = lax.dot_general(a.astype(BF16), dob, _DIMS["tn"], preferred_element_type=F32)
            dlev = []
            for li in range(nlev):
                _, _, qt, kt = parts[li]
                dan = (masks_ref[li] * da).astype(BF16)
                dlev.append((jnp.dot(dan, kt.astype(BF16), preferred_element_type=F32),
                             lax.dot_general(dan, qt.astype(BF16), _DIMS["tn"], preferred_element_type=F32)))
            yield
            qk = jnp.sum(q * k, axis=1, keepdims=True)
            dv = dv + qk * do + dv_state
            dqk = jnp.sum(do * v, axis=1, keepdims=True)
            dq = dqk * k
            dk = dqk * q
            for li in range(nlev):
                eq, ek, qt, kt = parts[li]
                dqt, dkt = dlev[li]
                dq = dq + dqt * eq
                dk = dk + dkt * ek
                de_h[(2 + 2 * li) * C:(3 + 2 * li) * C, :] = dqt * qt
                de_h[(3 + 2 * li) * C:(4 + 2 * li) * C, :] = dkt * kt
            dq = dq + dqe * eb
            dk = dk + dkd * esuf
            debl = jnp.sum(dst * st, axis=0, keepdims=True)
            de_h[0:C, :] = dqe * qe + jnp.where(last_row, debl * ebl, 0.0)
            de_h[C:2 * C, :] = dkd * kd
            ds_ref[h] = dst * ebl + ds_new
            dehi, delo = _split2(de_h[...])
            dg = (lax.dot_general(sums_ref[...], dehi, _DIMS["tn"], preferred_element_type=F32)
                  + lax.dot_general(sums_ref[...], delo, _DIMS["tn"], preferred_element_type=F32))
            yield
            df = dg / f - dk
            dout_ref[0, rows, cols] = (dq * (sq * (1.0 + hq * (1.0 - sq)))).astype(BF16)
            dout_ref[1, rows, cols] = (df * (1.0 - lb) * (sf * (1.0 - sf))).astype(BF16)
            dout_ref[2, rows, cols] = dv.astype(BF16)
            dlb_acc[:, cols] += jnp.sum((df * (1.0 - sf)).reshape(C // SUBLANE, SUBLANE, HEAD_DIM), axis=0)

        @pl.when(kk == 0)
        def _():
            lax.fori_loop(0, cpb, chunk, 0)

        dp_ref[...] = dout_ref[kk]

        @pl.when(jnp.logical_and(i == nb - 1, kk == 3))
        def _():
            lb = _lower_bound(lbl_ref, slice(None))
            dl0 = jnp.sum(dlb_acc[...], axis=0, keepdims=True) * (lb * (1.0 - lb))
            dlb_ref[0:1, :] = dl0
            dlb_ref[1:2, :] = -dl0

        @pl.when(jnp.logical_and(jnp.logical_and(grp == groups - 1, i == nb - 1), kk == 3))
        def _():
            dgo_ref[...] = jnp.sum(dgo_acc[...], axis=0, keepdims=True)

    def col(k):
        return pl.BlockSpec((tb, W), lambda g, i, kk: (nb - 1 - i, cb + k * groups + g))

    rev = pl.BlockSpec((tb, W), lambda g, i, kk: (nb - 1 - i, g))
    return pl.pallas_call(
        body, name=name, grid=(groups, nb, 4),
        in_specs=[col(0), col(1), col(2), col(3),
                  pl.BlockSpec((2, W), lambda g, i, kk: (0, g)),
                  pl.BlockSpec((1, HEAD_DIM), lambda g, i, kk: (0, 0)),
                  pl.BlockSpec((nsum, C), lambda g, i, kk: (0, 0)),
                  pl.BlockSpec((nlev, C, C), lambda g, i, kk: (0, 0, 0)),
                  rev,
                  pl.BlockSpec((hps, cpb, HEAD_DIM, HEAD_DIM), lambda g, i, kk: (g, nb - 1 - i, 0, 0)),
                  rev,
                  pl.BlockSpec(memory_space=pl.ANY)],
        out_specs=[pl.BlockSpec((tb, W), lambda g, i, kk: (nb - 1 - i, cb + kk * groups + g)),
                   pl.BlockSpec((2, W), lambda g, i, kk: (0, g)),
                   pl.BlockSpec((1, HEAD_DIM), lambda g, i, kk: (0, 0))],
        out_shape=[jax.ShapeDtypeStruct(dproj.shape, BF16), jax.ShapeDtypeStruct((2, width), F32),
                   jax.ShapeDtypeStruct((1, HEAD_DIM), F32)],
        input_output_aliases={11: 0},
        scratch_shapes=[pltpu.VMEM((hps, HEAD_DIM, HEAD_DIM), F32), pltpu.VMEM((hps, nsum, HEAD_DIM), F32),
                        pltpu.VMEM((SUBLANE, W), F32), pltpu.VMEM((SUBLANE, HEAD_DIM), F32),
                        pltpu.VMEM((4, tb, W), BF16)],
        compiler_params=_params(("arbitrary", "arbitrary", "arbitrary"),
                                tb * W * 4 * 12 + hps * cpb * HEAD_DIM * HEAD_DIM * 4),
    )(proj, proj, proj, proj, lb_logits, g_out, sums, masks, o_saved, states, dy, dproj)


def _merge_fwd(proj, ma, mb, gate_col0, name):
    T, D = ma.shape
    tr = _tile(T, 256, SUBLANE)
    cw = _tile(D, 1024, LANE)
    nj = D // cw
    assert gate_col0 % cw == 0
    g0 = gate_col0 // cw

    def body(ga_ref, gb_ref, ma_ref, mb_ref, o_ref):
        o_ref[...] = (_sigmoid(ga_ref[...]) * ma_ref[...] + _sigmoid(gb_ref[...]) * mb_ref[...]).astype(BF16)

    blk = pl.BlockSpec((tr, cw), lambda i, j: (i, j))
    return pl.pallas_call(
        body, name=name, grid=(T // tr, nj),
        in_specs=[pl.BlockSpec((tr, cw), lambda i, j: (i, g0 + j)),
                  pl.BlockSpec((tr, cw), lambda i, j: (i, g0 + nj + j)), blk, blk],
        out_specs=blk, out_shape=jax.ShapeDtypeStruct((T, D), BF16),
        compiler_params=_params(("parallel", "parallel"), tr * cw * 18 * 2),
    )(proj, proj, ma, mb)


def _merge_bwd(proj, ma, mb, dm, gate_col0, name):
    T, D = ma.shape
    tr = _tile(T, 256, SUBLANE)
    cw = _tile(D, 1024, LANE)
    nj = D // cw
    g0 = gate_col0 // cw

    def body(g_ref, ma_ref, mb_ref, dm_ref, dp_ref, dmm_ref):
        j = pl.program_id(1)
        s = _sigmoid(g_ref[...])
        dmv = dm_ref[...]
        mm = jnp.where(j < nj, ma_ref[...], mb_ref[...])
        dp_ref[...] = (dmv * mm * (s * (1.0 - s))).astype(BF16)
        dmm_ref[...] = (dmv * s).astype(BF16)

    blk = pl.BlockSpec((tr, cw), lambda i, j: (i, j % nj))
    return pl.pallas_call(
        body, name=name, grid=(T // tr, 2 * nj),
        in_specs=[pl.BlockSpec((tr, cw), lambda i, j: (i, g0 + j)), blk, blk, blk],
        out_specs=[pl.BlockSpec((tr, cw), lambda i, j: (i, g0 + j)),
                   pl.BlockSpec((None, tr, cw), lambda i, j: (j // nj, i, j % nj))],
        out_shape=[jax.ShapeDtypeStruct(proj.shape, BF16), jax.ShapeDtypeStruct((2, T, D), BF16)],
        compiler_params=_params(("parallel", "parallel"), tr * cw * 20 * 2),
    )(proj, ma, mb, dm)


def _causal_conv(ext, w, b):
    s1 = pltpu.roll(ext, 1, 0)
    s2 = pltpu.roll(ext, 2, 0)
    out = b + w[0:1, :] * s2
    out = out + w[1:2, :] * s1
    out = out + w[2:3, :] * ext
    return out, s1, s2


def _conv_fwd(up, convw, convb, name):
    T, F2 = up.shape
    tc = convw.shape[2]
    half = (F2 // 2) // tc
    assert half * tc * 2 == F2
    tr = _tile(T, 256, SUBLANE)
    hb = tr // SUBLANE

    def body(g_ref, gp_ref, v_ref, vp_ref, wg_ref, wv_ref, bg_ref, bv_ref, o_ref):
        first = pl.program_id(1) == 0

        def conv(cur_ref, prev_ref, w_ref, b_ref):
            prev = jnp.where(first, 0.0, prev_ref[...])
            ext = jnp.concatenate([prev, cur_ref[...]], axis=0)
            return _causal_conv(ext, w_ref[...], b_ref[...])[0][SUBLANE:]

        gate = conv(g_ref, gp_ref, wg_ref, bg_ref)
        val = conv(v_ref, vp_ref, wv_ref, bv_ref)
        o_ref[...] = ((gate * _sigmoid(gate)) * val).astype(BF16)

    def main(off):
        return pl.BlockSpec((tr, tc), lambda j, i: (i, off + j))

    def prev(off):
        return pl.BlockSpec((SUBLANE, tc), lambda j, i: (jnp.maximum(i * hb - 1, 0), off + j))

    def wspec(off):
        return pl.BlockSpec((None, 3, tc), lambda j, i: (off + j, 0, 0))

    def bspec(off):
        return pl.BlockSpec((1, tc), lambda j, i: (0, off + j))

    return pl.pallas_call(
        body, name=name, grid=(half, T // tr),
        in_specs=[main(0), prev(0), main(half), prev(half), wspec(0), wspec(half), bspec(0), bspec(half)],
        out_specs=pl.BlockSpec((tr, tc), lambda j, i: (i, j)),
        out_shape=jax.ShapeDtypeStruct((T, F2 // 2), BF16),
        compiler_params=_params(("parallel", "parallel"), tr * tc * 4 * 12),
    )(up, up, up, up, convw, convw, convb, convb)


def _conv_bwd(da, up, convw, convb, name):
    T, F2 = up.shape
    tc = convw.shape[2]
    half = (F2 // 2) // tc
    tr = _tile(T, 128, SUBLANE)
    hb = tr // SUBLANE
    nrow = T // tr
    n = tr + 2 * SUBLANE

    def body(g_ref, gp_ref, gn_ref, v_ref, vp_ref, vn_ref, da_ref, dan_ref, wg_ref, wv_ref, bg_ref, bv_ref,
             dup_ref, gw_ref, acc_ref):
        i = pl.program_id(1)
        first = i == 0
        last = i == nrow - 1

        @pl.when(first)
        def _():
            acc_ref[...] = jnp.zeros_like(acc_ref)

        rows = slice(SUBLANE, SUBLANE + tr)

        def fold(t):
            return jnp.sum(t.reshape(tr // SUBLANE, SUBLANE, LANE), axis=0)

        def lanes(c, _):
            cols = pl.ds(pl.multiple_of(c * LANE, LANE), LANE)

            def conv(cur_ref, prev_ref, next_ref, w_ref, b_ref):
                prev = jnp.where(first, 0.0, prev_ref[:, cols])
                ext = jnp.concatenate([prev, cur_ref[:, cols], next_ref[:, cols]], axis=0)
                return (ext,) + _causal_conv(ext, w_ref[:, cols], b_ref[:, cols])

            g_ext, gate, g_s1, g_s2 = conv(g_ref, gp_ref, gn_ref, wg_ref, bg_ref)
            v_ext, val, v_s1, v_s2 = conv(v_ref, vp_ref, vn_ref, wv_ref, bv_ref)
            da_ext = jnp.concatenate([jnp.zeros((SUBLANE, LANE), F32), da_ref[:, cols],
                                      jnp.where(last, 0.0, dan_ref[:, cols])], axis=0)
            sg = _sigmoid(gate)
            d_gate = da_ext * val * (sg * (1.0 + gate * (1.0 - sg)))
            d_val = da_ext * (gate * sg)
            halves = ((d_gate, wg_ref[:, cols], (g_s2, g_s1, g_ext)), (d_val, wv_ref[:, cols], (v_s2, v_s1, v_ext)))
            for hf, (dc, w, taps) in enumerate(halves):
                dup = w[2:3, :] * dc + w[1:2, :] * pltpu.roll(dc, n - 1, 0) + w[0:1, :] * pltpu.roll(dc, n - 2, 0)
                dup_ref[hf, :, cols] = dup[rows].astype(BF16)
                dc_m = dc[rows]
                acc_ref[hf, 0, :, cols] += fold(dc_m)
                for k in range(3):
                    acc_ref[hf, 1 + k, :, cols] += fold(dc_m * taps[k][rows])
            return 0

        lax.fori_loop(0, tc // LANE, lanes, 0)

        @pl.when(last)
        def _():
            gw_ref[...] = jnp.sum(acc_ref[...], axis=2)

    def main(off):
        return pl.BlockSpec((tr, tc), lambda j, i: (i, off + j))

    def prev(off):
        return pl.BlockSpec((SUBLANE, tc), lambda j, i: (jnp.maximum(i * hb - 1, 0), off + j))

    def nxt(off):
        return pl.BlockSpec((SUBLANE, tc), lambda j, i: (jnp.minimum((i + 1) * hb, T // SUBLANE - 1), off + j))

    def wspec(off):
        return pl.BlockSpec((None, 3, tc), lambda j, i: (off + j, 0, 0))

    def bspec(off):
        return pl.BlockSpec((1, tc), lambda j, i: (0, off + j))

    return pl.pallas_call(
        body, name=name, grid=(half, nrow),
        in_specs=[main(0), prev(0), nxt(0), main(half), prev(half), nxt(half), main(0), nxt(0),
                  wspec(0), wspec(half), bspec(0), bspec(half)],
        out_specs=[pl.BlockSpec((2, tr, tc), lambda j, i: (0, i, j)), pl.BlockSpec((2, 4, tc), lambda j, i: (0, 0, j))],
        out_shape=[jax.ShapeDtypeStruct((2, T, F2 // 2), BF16), jax.ShapeDtypeStruct((2, 4, F2 // 2), F32)],
        scratch_shapes=[pltpu.VMEM((2, 4, SUBLANE, tc), F32)],
        compiler_params=_params(("parallel", "arbitrary"), n * tc * 4 * 24),
    )(up, up, up, up, up, up, da, da, convw, convw, convb, convb)


def _loss_head(y, target, name):
    T, D = y.shape
    tr = _tile(T, 256, SUBLANE)
    nrow = T // tr

    def body(y_ref, t_ref, d_ref, db_ref, l_ref, acc_ref):
        i = pl.program_id(0)
        diff = y_ref[...] - t_ref[...]
        dy = diff / D
        d_ref[...] = dy
        db_ref[...] = dy.astype(BF16)
        part = jnp.sum((diff * diff).reshape(tr // SUBLANE, SUBLANE, D), axis=0)

        @pl.when(i == 0)
        def _():
            acc_ref[...] = part

        @pl.when(i > 0)
        def _():
            acc_ref[...] += part

        @pl.when(i == nrow - 1)
        def _():
            col = jnp.sum(acc_ref[...], axis=0, keepdims=True)
            l_ref[...] = jnp.broadcast_to(0.5 * (jnp.sum(col, axis=1, keepdims=True) / D), (1, LANE))

    row = pl.BlockSpec((tr, D), lambda i: (i, 0))
    return pl.pallas_call(
        body, name=name, grid=(nrow,), in_specs=[row, row],
        out_specs=[row, row, pl.BlockSpec((1, LANE), lambda i: (0, 0))],
        out_shape=[jax.ShapeDtypeStruct((T, D), F32), jax.ShapeDtypeStruct((T, D), BF16),
                   jax.ShapeDtypeStruct((1, LANE), F32)],
        scratch_shapes=[pltpu.VMEM((SUBLANE, D), F32)],
        compiler_params=_params(("arbitrary",), tr * D * 14 * 2),
    )(y, target)


def _adamw(w, parts, m, v, name):
    R, C = w.shape
    P = parts.shape[0]
    tr = _tile(R, 64, SUBLANE)
    tc = _tile(C, 2048, LANE)

    def body(w_ref, p_ref, m_ref, v_ref, g_ref, d_ref, nm_ref, nv_ref):
        g = p_ref[0].astype(F32)
        for s in range(1, P):
            g = g + p_ref[s].astype(F32)
        wv = w_ref[...]
        nm = ADAM_B1 * m_ref[...] + (1.0 - ADAM_B1) * g
        nv = ADAM_B2 * v_ref[...] + (1.0 - ADAM_B2) * (g * g)
        m_hat = nm / (1.0 - ADAM_B1 ** ADAM_STEP)
        v_hat = nv / (1.0 - ADAM_B2 ** ADAM_STEP)
        g_ref[...] = g
        d_ref[...] = -ADAM_LR * (m_hat / (jnp.sqrt(v_hat) + ADAM_EPS) + ADAM_WD * wv)
        nm_ref[...] = nm
        nv_ref[...] = nv

    blk = pl.BlockSpec((tr, tc), lambda i, j: (i, j))
    shp = jax.ShapeDtypeStruct((R, C), F32)
    return pl.pallas_call(
        body, name=name, grid=(R // tr, C // tc),
        in_specs=[blk, pl.BlockSpec((P, tr, tc), lambda i, j: (0, i, j)), blk, blk],
        out_specs=[blk] * 4, out_shape=[shp] * 4,
        compiler_params=_params(("parallel", "parallel"), tr * tc * 4 * (P + 8) * 2),
    )(w, parts, m, v)


def _place():
    x, y, c = (lax.axis_index(a) for a in MESH_AXES)
    return x, y, c, 4 * x + 2 * y + c


def _peers(x, y, c):
    out = []
    for d in range(1, N_DEV):
        px = x + (d >> 2 & 1) - 2 * x * (d >> 2 & 1)
        py = y + (d >> 1 & 1) - 2 * y * (d >> 1 & 1)
        pc = c + (d & 1) - 2 * c * (d & 1)
        out.append(((px, py, pc), 4 * px + 2 * py + pc))
    return out


class _Side:
    def __init__(self, arrays, scatter):
        self.arrays, self.scatter, self.n = list(arrays), scatter, len(arrays)

    def in_specs(self):
        return [pl.BlockSpec(memory_space=pltpu.HBM)] * self.n

    out_specs = in_specs

    def out_shape(self):
        return [jax.ShapeDtypeStruct(a.shape if self.scatter else (N_DEV,) + a.shape, a.dtype) for a in self.arrays]

    def scratch(self):
        return [pltpu.SemaphoreType.DMA((self.n, N_DEV - 1)), pltpu.SemaphoreType.DMA((self.n, N_DEV - 1)),
                pltpu.SemaphoreType.DMA((self.n,))]

    def _copies(self, ins, outs, sems):
        send_sems, recv_sems, local_sems = sems
        x, y, c, me = _place()
        peers = _peers(x, y, c)
        local, sends, recvs = [], [], []
        for t in range(self.n):
            src_me = ins[t].at[me] if self.scatter else ins[t]
            local.append(pltpu.make_async_copy(src_me, outs[t].at[me], local_sems.at[t]))
            for d, (peer, pidx) in enumerate(peers):
                src = ins[t].at[pidx] if self.scatter else ins[t]
                sends.append(pltpu.make_async_remote_copy(
                    src_ref=src, dst_ref=outs[t].at[me], send_sem=send_sems.at[t, d], recv_sem=recv_sems.at[t, d],
                    device_id=peer, device_id_type=MESH_ID))
                recvs.append(pltpu.make_async_remote_copy(
                    src_ref=src, dst_ref=outs[t].at[pidx], send_sem=send_sems.at[t, d], recv_sem=recv_sems.at[t, d],
                    device_id=peer, device_id_type=MESH_ID))
        return local, sends, recvs

    def start(self, ins, outs, sems):
        local, sends, _ = self._copies(ins, outs, sems)
        for cp in local + sends:
            cp.start()

    def wait(self, ins, outs, sems):
        local, sends, recvs = self._copies(ins, outs, sems)
        for cp in recvs:
            cp.wait_recv()
        for cp in sends:
            cp.wait_send()
        for cp in local:
            cp.wait()


def _grid_edges(sizes):
    ids = [pl.program_id(a) for a in range(len(sizes))]
    first = functools.reduce(jnp.logical_and, [i == 0 for i in ids])
    last = functools.reduce(jnp.logical_and, [i == s - 1 for i, s in zip(ids, sizes)])
    return first, last


def _exchange(arrays, scatter, name):
    side = _Side(arrays, scatter)
    n = side.n

    def body(*refs):
        ins, outs, sems = refs[:n], refs[n:2 * n], refs[2 * n:]
        side.start(ins, outs, sems)
        side.wait(ins, outs, sems)

    return pl.pallas_call(
        body, name=name, in_specs=side.in_specs(), out_specs=side.out_specs(), out_shape=side.out_shape(),
        scratch_shapes=side.scratch(),
    )(*arrays)


def _all_reduce_small(vec, name):
    R = vec.shape[0]

    def body(v_ref, o_ref, gath_ref, send_sems, recv_sems):
        x, y, c, me = _place()
        sends, recvs = [], []
        for d, (peer, pidx) in enumerate(_peers(x, y, c)):
            cp = pltpu.make_async_remote_copy(
                src_ref=v_ref, dst_ref=gath_ref.at[me], send_sem=send_sems.at[d], recv_sem=recv_sems.at[d],
                device_id=peer, device_id_type=MESH_ID)
            cp.start()
            sends.append(cp)
            recvs.append(pltpu.make_async_remote_copy(
                src_ref=v_ref, dst_ref=gath_ref.at[pidx], send_sem=send_sems.at[d], recv_sem=recv_sems.at[d],
                device_id=peer, device_id_type=MESH_ID))
        gath_ref[me] = v_ref[...]
        for cp in recvs:
            cp.wait_recv()
        for cp in sends:
            cp.wait_send()
        total = gath_ref[0]
        for s in range(1, N_DEV):
            total = total + gath_ref[s]
        o_ref[...] = total

    vm = pl.BlockSpec(memory_space=pltpu.VMEM)
    return pl.pallas_call(
        body, name=name, in_specs=[vm], out_specs=vm, out_shape=jax.ShapeDtypeStruct(vec.shape, F32),
        scratch_shapes=[pltpu.VMEM((N_DEV, R, LANE), F32), pltpu.SemaphoreType.DMA((N_DEV - 1,)),
                        pltpu.SemaphoreType.DMA((N_DEV - 1,))],
        compiler_params=pltpu.CompilerParams(vmem_limit_bytes=_vmem(R * LANE * 4 * 12)),
    )(vec)


def _pack(parts):
    flat = jnp.concatenate([p.reshape(-1).astype(F32) for p in parts])
    rows = -(-flat.shape[0] // (LANE * SUBLANE)) * SUBLANE
    return jnp.pad(flat, (0, rows * LANE - flat.shape[0])).reshape(rows, LANE)


def _unpack(packed, shapes):
    flat = packed.reshape(-1)
    out, at = [], 0
    for s in shapes:
        size = int(np.prod(s))
        out.append(flat[at:at + size].reshape(s))
        at += size
    return out


def kernel(x, g_mix, w_in, g_q, g_k, lb_logits, g_hg_out, p_a, p_b, w_o, g_ffn, w_up, conv_w, conv_b, w_down, loss_target, m_g_mix, m_w_in, m_g_q, m_g_k, m_lb_logits, m_g_hg_out, m_p_a, m_p_b, m_w_o, m_g_ffn, m_w_up, m_conv_w, m_conv_b, m_w_down, v_g_mix, v_w_in, v_g_q, v_g_k, v_lb_logits, v_g_hg_out, v_p_a, v_p_b, v_w_o, v_g_ffn, v_w_up, v_conv_w, v_conv_b, v_w_down):
    assert x.shape[0] == 1 and lb_logits.shape[0] == 2
    xs, target = x[0], loss_target[0]
    T, D = xs.shape
    A = p_a.shape[1]
    HW = p_b.shape[1]
    hg_col0 = 3 * A
    gate_col0 = 3 * A + 4 * HW
    F = w_down.shape[1] * N_DEV
    assert w_in.shape[2] * N_DEV == gate_col0 + 2 * D and w_up.shape[2] * N_DEV == 2 * F

    (win_g,) = _exchange([w_in[0].astype(BF16)], False, "gather_w_in")
    later_weights = _Side([w_up[0].astype(BF16), p_a[0].astype(BF16), p_b[0].astype(BF16), w_o[0].astype(BF16),
                           w_down[0].astype(BF16), conv_w[0]], False)

    u = _rmsnorm_fwd(xs, g_mix, "norm_mix")
    proj = _matmul(u, win_g, mode="nn", b_shards=True, name="proj_in")
    gains = jnp.stack([g_q[0], g_k[0], jnp.ones_like(g_q[0])])[:, None, :]
    qkv = _qkv_prep(proj, gains, A, "qkv_prep")
    ya, ya32, (wup_g, pa_g, pb_g, wo_g, wdown_g, convw_g) = _sb_fwd(qkv, A, "sb_fwd", side=later_weights)
    wo_full = wo_g.reshape(D, D)
    wdown_full = wdown_g.reshape(F, D)
    yb, ob, states = _hg_fwd(proj, lb_logits, g_hg_out, hg_col0, HW, "hg_fwd")
    ma = _matmul(ya, pa_g, mode="nn", b_shards=True, name="proj_a")
    mb = _matmul(yb, pb_g, mode="nn", b_shards=True, name="proj_b")
    m = _merge_fwd(proj, ma, mb, gate_col0, "merge_fwd")
    h1 = _matmul(m, wo_full, mode="nn", add=xs, name="proj_o")
    u2 = _rmsnorm_fwd(h1, g_ffn, "norm_ffn")
    up = _matmul(u2, wup_g, mode="nn", b_shards=True, name="ffn_up")
    act = _conv_fwd(up, convw_g, conv_b, "conv_fwd")
    y = _matmul(act, wdown_full, mode="nn", add=h1, name="ffn_down")
    dy, dyb, loss_part = _loss_head(y, target, "loss_head")
    loss = lax.psum(loss_part[0, 0], MESH_AXES)

    dact = _matmul(dyb, wdown_full, mode="nt", name="d_act")
    g_wdown = _matmul(act, dyb, mode="tn", out_dtype=BF16, name="g_w_down")
    dup, g_conv = _conv_bwd(dact, up, convw_g, conv_b, "conv_bwd")
    g_wup, (r_wdown,) = _matmul(u2, dup, mode="tn", b_halves=True, out_shards=True, out_dtype=BF16, tm_pref=1024,
                                name="g_w_up",
                                side=_Side([g_wdown.reshape(N_DEV, F // N_DEV, D)], True))
    du2, (r_wup,) = _matmul(dup, wup_g, mode="nt", a_halves=True, b_shards=True, tn_pref=2048, name="d_u2",
                            side=_Side([g_wup], True))
    dh1, dh1b, g_gffn = _rmsnorm_bwd(h1, g_ffn, du2, dy, "norm_ffn_bwd")
    dm = _matmul(dh1b, wo_full, mode="nt", name="d_m")
    g_wo = _matmul(m, dh1b, mode="tn", out_dtype=BF16, tm_pref=1024, name="g_w_o")
    dproj, dmab = _merge_bwd(proj, ma, mb, dm, gate_col0, "merge_bwd")
    dya = _matmul(dmab, pa_g, mode="nt", b_shards=True, lead_a=0, out_dtype=BF16, name="d_ya")
    g_pa = _matmul(ya, dmab, mode="tn", out_shards=True, out_dtype=BF16, lead_b=0, name="g_p_a")
    dyb_ = _matmul(dmab, pb_g, mode="nt", b_shards=True, lead_a=1, out_dtype=BF16, name="d_yb")
    g_pb = _matmul(yb, dmab, mode="tn", out_shards=True, out_dtype=BF16, lead_b=1, name="g_p_b")
    dproj, g_lb, g_ghg = _hg_bwd(proj, lb_logits, g_hg_out, ob, states, dyb_, dproj, hg_col0, HW, "hg_bwd")
    dq, dk, dv = _sb_bwd(qkv, ya32, dya, A, "sb_bwd")
    dproj, g_gains = _qkv_bwd(proj, gains, dq, dk, dv, dproj, "qkv_bwd")
    g_win, (r_wo, r_pa, r_pb) = _matmul(
        u, dproj, mode="tn", out_shards=True, out_dtype=BF16, tm_pref=1024, name="g_w_in",
        side=_Side([g_wo.reshape(N_DEV, D // N_DEV, D), g_pa, g_pb], True))
    du, (r_win,) = _matmul(dproj, win_g, mode="nt", b_shards=True, tn_pref=2048, name="d_u",
                           side=_Side([g_win], True))
    gx, _, g_gmix = _rmsnorm_bwd(xs, g_mix, du, dh1, "norm_mix_bwd")

    g_convb = g_conv[:, 0].reshape(1, 2 * F)
    g_convw = g_conv[:, 1:4].transpose(1, 0, 2).reshape(3, 2 * F)
    small = [g_gmix, g_gains[0], g_gains[1], g_lb, g_ghg, g_gffn, g_convb, g_convw]
    small_shapes = [p.shape for p in small]
    red = _unpack(_all_reduce_small(_pack(small), "reduce_small"), small_shapes)

    _, _, _, me = _place()
    cs = conv_w.shape[2]
    big = {
        "w_in": _adamw(w_in[0], r_win, m_w_in[0], v_w_in[0], "adamw_w_in"),
        "w_up": _adamw(w_up[0], r_wup, m_w_up[0], v_w_up[0], "adamw_w_up"),
        "p_a": _adamw(p_a[0], r_pa, m_p_a[0], v_p_a[0], "adamw_p_a"),
        "p_b": _adamw(p_b[0], r_pb, m_p_b[0], v_p_b[0], "adamw_p_b"),
        "w_o": _adamw(w_o[0], r_wo, m_w_o[0], v_w_o[0], "adamw_w_o"),
        "w_down": _adamw(w_down[0], r_wdown, m_w_down[0], v_w_down[0], "adamw_w_down"),
        "conv_w": _adamw(conv_w[0], lax.dynamic_slice_in_dim(red[7], me * cs, cs, axis=1)[None],
                         m_conv_w[0], v_conv_w[0], "adamw_conv_w"),
    }
    rep_w = [g_mix, g_q, g_k, lb_logits, g_hg_out, g_ffn, conv_b]
    rep_m = [m_g_mix, m_g_q, m_g_k, m_lb_logits, m_g_hg_out, m_g_ffn, m_conv_b]
    rep_v = [v_g_mix, v_g_q, v_g_k, v_lb_logits, v_g_hg_out, v_g_ffn, v_conv_b]
    rep_shapes = [p.shape for p in rep_w]
    rep_out = _adamw(_pack(rep_w), _pack(red[:7])[None], _pack(rep_m), _pack(rep_v), "adamw_small")
    rep = [_unpack(o, rep_shapes) for o in rep_out]
    rep_names = ["g_mix", "g_q", "g_k", "lb_logits", "g_hg_out", "g_ffn", "conv_b"]

    order = ["g_mix", "w_in", "g_q", "g_k", "lb_logits", "g_hg_out", "p_a", "p_b", "w_o", "g_ffn", "w_up",
             "conv_w", "conv_b", "w_down"]

    def leaf(kind, pname):
        if pname in big:
            return big[pname][kind][None]
        return rep[kind][rep_names.index(pname)]

    return (loss, gx[None], *[leaf(kind, p) for kind in range(4) for p in order])
```

```python
import functools

import numpy as np
import jax
import jax.numpy as jnp
from jax import lax
from jax.experimental import pallas as pl
from jax.experimental.pallas import tpu as pltpu

F32 = jnp.float32
BF16 = jnp.bfloat16

N_DEV = 8
HEAD_DIM = 128
HG_CHUNK = 64
HG_LEVELS = (1, 2, 4, 8, 16, 32)
EPS = 1e-6
ADAM_LR = 0.001
ADAM_B1 = 0.9
ADAM_B2 = 0.999
ADAM_EPS = 1e-08
ADAM_WD = 0.01
ADAM_STEP = 10
LANE = 128
SUBLANE = 8
VMEM_CAP = 56 << 20
MESH_AXES = ("x", "y", "c")
MESH_ID = pl.DeviceIdType.MESH


def _tile(dim, pref, align):
    if dim <= pref:
        return dim
    t = (pref // align) * align
    while t >= align:
        if dim % t == 0:
            return t
        t -= align
    return dim


def _vmem(est_bytes):
    return int(min(max(2 * est_bytes + (8 << 20), 32 << 20), VMEM_CAP))


def _params(sem, est_bytes):
    return pltpu.CompilerParams(dimension_semantics=sem, vmem_limit_bytes=_vmem(est_bytes))


def _sigmoid(x):
    return 1.0 / (1.0 + jnp.exp(-x))


_DIMS = {"nn": (((1,), (0,)), ((), ())), "nt": (((1,), (1,)), ((), ())), "tn": (((0,), (0,)), ((), ()))}


def _matmul(a, b, *, mode, name, out_dtype=F32, add=None, b_shards=False, out_shards=False, lead_a=None,
            lead_b=None, a_halves=False, b_halves=False, side=None, tm_pref=512, tn_pref=1408, tk_pref=2048):
    if mode == "tn":
        K, M = a.shape[-2:]
    else:
        M, K = a.shape[-2:]
    if a_halves:
        assert mode == "nt" and b_shards and lead_a is None
        K = 2 * K
    if lead_b is not None or b_halves:
        assert not b_shards and mode == "tn"
    if b_shards:
        S = b.shape[0]
        if mode == "nn":
            assert b.shape[1] == K
            N, tn, tk = S * b.shape[2], b.shape[2], _tile(K, tk_pref, LANE)
        else:
            assert mode == "nt" and S * b.shape[2] == K
            N, tk, tn = b.shape[1], b.shape[2], _tile(b.shape[1], tn_pref, LANE)
    else:
        if mode == "nt":
            N = b.shape[0]
            assert b.shape[1] == K
        else:
            N = b.shape[-1] * (2 if b_halves else 1)
            assert b.shape[-2] == K
        tn = _tile(N, tn_pref, LANE)
        tk = _tile(K, tk_pref, LANE)
    if out_shards:
        assert N % N_DEV == 0
        tn = N // N_DEV
    per_half_n, per_half_k = (N // 2) // tn, (K // 2) // tk
    assert not b_halves or per_half_n * tn * 2 == N
    assert not a_halves or per_half_k * tk * 2 == K
    tm = _tile(M, tm_pref, LANE)
    nm, nn, nk = M // tm, N // tn, K // tk
    assert nm * tm == M and nn * tn == N and nk * tk == K

    if mode == "tn":
        a_spec = pl.BlockSpec((tk, tm), lambda j, i, k: (k, i))
    elif lead_a is not None:
        a_spec = pl.BlockSpec((None, tm, tk), lambda j, i, k: (lead_a, i, k))
    elif a_halves:
        a_spec = pl.BlockSpec((None, tm, tk), lambda j, i, k: (k // per_half_k, i, k % per_half_k))
    else:
        a_spec = pl.BlockSpec((tm, tk), lambda j, i, k: (i, k))
    if b_halves:
        b_spec = pl.BlockSpec((None, tk, tn), lambda j, i, k: (j // per_half_n, k, j % per_half_n))
    elif lead_b is not None:
        b_spec = pl.BlockSpec((None, tk, tn), lambda j, i, k: (lead_b, k, j))
    elif b_shards and mode == "nn":
        b_spec = pl.BlockSpec((None, tk, tn), lambda j, i, k: (j, k, 0))
    elif b_shards:
        b_spec = pl.BlockSpec((None, tn, tk), lambda j, i, k: (k, j, 0))
    elif mode == "nt":
        b_spec = pl.BlockSpec((tn, tk), lambda j, i, k: (j, k))
    else:
        b_spec = pl.BlockSpec((tk, tn), lambda j, i, k: (k, j))
    in_specs = [a_spec, b_spec]
    operands = [a, b]
    if add is not None:
        assert not out_shards and add.shape == (M, N)
        in_specs.append(pl.BlockSpec((tm, tn), lambda j, i, k: (i, j)))
        operands.append(add)
    if out_shards:
        out_shape = jax.ShapeDtypeStruct((N_DEV, M, tn), out_dtype)
        out_spec = pl.BlockSpec((None, tm, tn), lambda j, i, k: (j, i, 0))
    else:
        out_shape = jax.ShapeDtypeStruct((M, N), out_dtype)
        out_spec = pl.BlockSpec((tm, tn), lambda j, i, k: (i, j))
    dims = _DIMS[mode]
    has_add = add is not None
    n_in = 3 if has_add else 2
    n_side = side.n if side is not None else 0

    def body(*refs):
        a_ref, b_ref = refs[0], refs[1]
        add_ref = refs[2] if has_add else None
        o_ref = refs[n_in + n_side]
        acc_ref = refs[n_in + 2 * n_side + 1]
        k = pl.program_id(2)
        if side is not None:
            side_refs = (refs[n_in:n_in + n_side], refs[n_in + n_side + 1:n_in + 2 * n_side + 1],
                         refs[n_in + 2 * n_side + 2:])
            first, last = _grid_edges((nn, nm, nk))

            @pl.when(first)
            def _():
                side.start(*side_refs)

        part = lax.dot_general(a_ref[...], b_ref[...], dims, preferred_element_type=F32)

        def finish(total):
            if has_add:
                total = add_ref[...] + total
            o_ref[...] = total.astype(out_dtype)

        if nk == 1:
            finish(part)
        else:
            @pl.when(k == 0)
            def _():
                acc_ref[...] = part

            @pl.when(jnp.logical_and(k > 0, k < nk - 1))
            def _():
                acc_ref[...] += part

            @pl.when(k == nk - 1)
            def _():
                finish(acc_ref[...] + part)

        if side is not None:
            @pl.when(last)
            def _():
                side.wait(*side_refs)

    est = 2 * (tm * tk * 2 + tk * tn * 2 + tm * tn * 4 * (2 if has_add else 1)) + tm * tn * 4 * 2
    acc = pltpu.VMEM((tm, tn) if nk > 1 else (SUBLANE, LANE), F32)
    if side is None:
        return pl.pallas_call(
            body, name=name, grid=(nn, nm, nk), in_specs=in_specs, out_specs=out_spec, out_shape=out_shape,
            scratch_shapes=[acc], compiler_params=_params(("parallel", "parallel", "arbitrary"), est),
        )(*operands)
    outs = pl.pallas_call(
        body, name=name, grid=(nn, nm, nk), in_specs=in_specs + side.in_specs(),
        out_specs=[out_spec] + side.out_specs(), out_shape=[out_shape] + side.out_shape(),
        scratch_shapes=[acc] + side.scratch(), compiler_params=_params(("arbitrary",) * 3, est),
    )(*operands, *side.arrays)
    return outs[0], outs[1:]


def _rmsnorm_fwd(x, g, name):
    T, D = x.shape
    tr = _tile(T, 256, SUBLANE)

    def body(x_ref, g_ref, o_ref):
        xf = x_ref[...]
        r = lax.rsqrt(jnp.mean(xf * xf, axis=-1, keepdims=True) + EPS)
        o_ref[...] = ((xf * r) * g_ref[...]).astype(BF16)

    return pl.pallas_call(
        body, name=name, grid=(T // tr,),
        in_specs=[pl.BlockSpec((tr, D), lambda i: (i, 0)), pl.BlockSpec((1, D), lambda i: (0, 0))],
        out_specs=pl.BlockSpec((tr, D), lambda i: (i, 0)), out_shape=jax.ShapeDtypeStruct((T, D), BF16),
        compiler_params=_params(("parallel",), tr * D * 6 * 2),
    )(x, g)


def _rmsnorm_bwd(x, g, dy, res, name):
    T, D = x.shape
    tr = _tile(T, 256, SUBLANE)
    nsteps = T // tr

    def body(x_ref, g_ref, dy_ref, res_ref, dx_ref, dxb_ref, dg_ref, acc_ref):
        i = pl.program_id(0)
        xf = x_ref[...]
        dyf = dy_ref[...].astype(F32)
        r = lax.rsqrt(jnp.mean(xf * xf, axis=-1, keepdims=True) + EPS)
        w = dyf * g_ref[...]
        s = jnp.mean(xf * w, axis=-1, keepdims=True)
        dx = res_ref[...] + (r * w - xf * (r * r * r * s))
        dx_ref[...] = dx
        dxb_ref[...] = dx.astype(BF16)
        part = jnp.sum((dyf * (xf * r)).reshape(tr // SUBLANE, SUBLANE, D), axis=0)

        @pl.when(i == 0)
        def _():
            acc_ref[...] = part

        @pl.when(i > 0)
        def _():
            acc_ref[...] += part

        @pl.when(i == nsteps - 1)
        def _():
            dg_ref[...] = jnp.sum(acc_ref[...], axis=0, keepdims=True)

    row = pl.BlockSpec((tr, D), lambda i: (i, 0))
    vec = pl.BlockSpec((1, D), lambda i: (0, 0))
    return pl.pallas_call(
        body, name=name, grid=(nsteps,), in_specs=[row, vec, row, row], out_specs=[row, row, vec],
        out_shape=[jax.ShapeDtypeStruct((T, D), F32), jax.ShapeDtypeStruct((T, D), BF16),
                   jax.ShapeDtypeStruct((1, D), F32)],
        scratch_shapes=[pltpu.VMEM((SUBLANE, D), F32)],
        compiler_params=_params(("arbitrary",), tr * D * 18 * 2),
    )(x, g, dy, res)


def _qkv_prep(proj, gains, width, name):
    T = proj.shape[0]
    tr = _tile(T, 256, SUBLANE)
    heads = width // HEAD_DIM

    def body(p_ref, g_ref, o_ref):
        j = pl.program_id(1)
        g = g_ref[...]
        for h in range(heads):
            xh = p_ref[:, h * HEAD_DIM:(h + 1) * HEAD_DIM]
            r = lax.rsqrt(jnp.mean(xh * xh, axis=-1, keepdims=True) + EPS)
            r = jnp.where(j < 2, r, 1.0)
            o_ref[:, h * HEAD_DIM:(h + 1) * HEAD_DIM] = ((xh * r) * g).astype(BF16)

    return pl.pallas_call(
        body, name=name, grid=(T // tr, 3),
        in_specs=[pl.BlockSpec((tr, width), lambda i, j: (i, j)),
                  pl.BlockSpec((None, 1, HEAD_DIM), lambda i, j: (j, 0, 0))],
        out_specs=pl.BlockSpec((tr, width), lambda i, j: (i, j)),
        out_shape=jax.ShapeDtypeStruct((T, 3 * width), BF16),
        compiler_params=_params(("parallel", "parallel"), tr * width * 6 * 2),
    )(proj, gains)


def _qkv_bwd(proj, gains, dq, dk, dv, dproj, name):
    T = proj.shape[0]
    width = dq.shape[1]
    tr = _tile(T, 256, SUBLANE)
    heads = width // HEAD_DIM
    nrow = T // tr

    def body(p_ref, g_ref, dq_ref, dk_ref, dv_ref, dp_in, o_ref, dg_ref, acc_ref):
        del dp_in
        i = pl.program_id(0)
        j = pl.program_id(1)
        g = g_ref[...]

        @pl.when(jnp.logical_and(i == 0, j == 0))
        def _():
            acc_ref[...] = jnp.zeros_like(acc_ref)

        part = jnp.zeros((SUBLANE, HEAD_DIM), F32)
        for h in range(heads):
            sl = slice(h * HEAD_DIM, (h + 1) * HEAD_DIM)
            xh = p_ref[:, sl]
            dyh = jnp.where(j == 0, dq_ref[:, sl], jnp.where(j == 1, dk_ref[:, sl], dv_ref[:, sl]))
            r = lax.rsqrt(jnp.mean(xh * xh, axis=-1, keepdims=True) + EPS)
            w = dyh * g
            s = jnp.mean(xh * w, axis=-1, keepdims=True)
            dx = r * w - xh * (r * r * r * s)
            o_ref[:, sl] = jnp.where(j < 2, dx, dyh).astype(BF16)
            part = part + jnp.sum((dyh * (xh * r)).reshape(tr // SUBLANE, SUBLANE, HEAD_DIM), axis=0)
        acc_ref[j] += part

        @pl.when(jnp.logical_and(i == nrow - 1, j == 2))
        def _():
            dg_ref[...] = jnp.sum(acc_ref[...], axis=1, keepdims=True)

    blk = pl.BlockSpec((tr, width), lambda i, j: (i, j))
    dblk = pl.BlockSpec((tr, width), lambda i, j: (i, 0))
    return pl.pallas_call(
        body, name=name, grid=(nrow, 3),
        in_specs=[blk, pl.BlockSpec((None, 1, HEAD_DIM), lambda i, j: (j, 0, 0)), dblk, dblk, dblk,
                  pl.BlockSpec(memory_space=pl.ANY)],
        out_specs=[blk, pl.BlockSpec((3, 1, HEAD_DIM), lambda i, j: (0, 0, 0))],
        out_shape=[jax.ShapeDtypeStruct(dproj.shape, BF16), jax.ShapeDtypeStruct((3, 1, HEAD_DIM), F32)],
        input_output_aliases={5: 0},
        scratch_shapes=[pltpu.VMEM((3, SUBLANE, HEAD_DIM), F32)],
        compiler_params=_params(("arbitrary", "arbitrary"), tr * width * 18 * 2),
    )(proj, gains, dq, dk, dv, dproj)


def _split2(x):
    hi = x.astype(BF16)
    lo = (x - hi.astype(F32)).astype(BF16)
    return hi, lo


LOG2E = 1.4426950408889634


def _sb_logits(q, kb):
    return lax.dot_general(q, kb, _DIMS["nt"], preferred_element_type=F32) * (HEAD_DIM ** -0.5 * LOG2E)


def _sb_scores(z2, mask):
    lk = -(jnp.maximum(z2, 0.0) + jnp.log2(1.0 + jnp.exp2(-jnp.abs(z2))))
    if mask is not None:
        lk = jnp.where(mask, lk, 0.0)
    return lk, lk + z2


def _sb_block_size(T):
    return _tile(T, 256, LANE)


SB_HEADS_PER_STEP = 2


def _sb_later(lk, upper, carry):
    hi, lo = _split2(lk)
    return (jnp.dot(hi, upper, preferred_element_type=F32) + jnp.dot(lo, upper, preferred_element_type=F32)) + carry


def _sb_fwd(qkv, width, name, side=None):
    T = qkv.shape[0]
    heads = width // HEAD_DIM
    bq = _sb_block_size(T)
    hps = min(2 * SB_HEADS_PER_STEP, heads)
    assert heads % hps == 0
    groups, W = heads // hps, hps * HEAD_DIM

    n_side = side.n if side is not None else 0

    def body(*refs):
        q_ref, k_ref, v_ref = refs[:3]
        o_ref, o32_ref = refs[3 + n_side:5 + n_side]
        if side is not None:
            side_refs = (refs[3:3 + n_side], refs[5 + n_side:5 + 2 * n_side], refs[5 + 2 * n_side:])
            first, last = _grid_edges((groups, T // bq))

            @pl.when(first)
            def _():
                side.start(*side_refs)

        i = pl.program_id(1)
        row = lax.broadcasted_iota(jnp.int32, (bq, bq), 0)
        col = lax.broadcasted_iota(jnp.int32, (bq, bq), 1)
        upper = (row > col).astype(BF16)
        causal = col < row

        def head_block(start, h, carry, acc, mask):
            cols = slice(h * HEAD_DIM, (h + 1) * HEAD_DIM)
            z2 = _sb_logits(q_ref[:, cols], k_ref[pl.ds(start, bq), cols])
            yield
            lk, lb = _sb_scores(z2, mask)
            later = _sb_later(lk, upper, carry)
            yield
            w = jnp.exp2(lb + later)
            if mask is not None:
                w = jnp.where(mask, w, 0.0)
            pv = jnp.dot(w.astype(BF16), v_ref[pl.ds(start, bq), cols], preferred_element_type=F32)
            yield
            return carry + jnp.sum(lk, axis=1, keepdims=True), acc + pv

        def block(j, state, mask):
            start = pl.multiple_of(j * bq, bq)
            return tuple(_lockstep(head_block(start, h, state[h][0], state[h][1], mask) for h in range(hps)))

        zero = (jnp.zeros((bq, 1), F32), jnp.zeros((bq, HEAD_DIM), F32))
        state = block(i, (zero,) * hps, causal)
        state = lax.fori_loop(0, i, lambda n, st: block(i - 1 - n, st, None), state)
        for h in range(hps):
            cols = slice(h * HEAD_DIM, (h + 1) * HEAD_DIM)
            o_ref[:, cols] = state[h][1].astype(BF16)
            o32_ref[:, cols] = state[h][1]

        if side is not None:
            @pl.when(last)
            def _():
                side.wait(*side_refs)

    oblk = pl.BlockSpec((bq, W), lambda g, i: (i, g))
    in_specs = [oblk, pl.BlockSpec((T, W), lambda g, i: (0, groups + g)),
                pl.BlockSpec((T, W), lambda g, i: (0, 2 * groups + g))]
    out_shape = [jax.ShapeDtypeStruct((T, width), BF16), jax.ShapeDtypeStruct((T, width), F32)]
    est = 2 * T * W * 2 * 2 + hps * 16 * bq * bq * 4
    if side is None:
        return pl.pallas_call(
            body, name=name, grid=(groups, T // bq), in_specs=in_specs, out_specs=[oblk, oblk], out_shape=out_shape,
            compiler_params=_params(("parallel", "arbitrary"), est),
        )(qkv, qkv, qkv)
    outs = pl.pallas_call(
        body, name=name, grid=(groups, T // bq), in_specs=in_specs + side.in_specs(),
        out_specs=[oblk, oblk] + side.out_specs(), out_shape=out_shape + side.out_shape(),
        scratch_shapes=side.scratch(), compiler_params=_params(("arbitrary", "arbitrary"), est),
    )(qkv, qkv, qkv, *side.arrays)
    return outs[0], outs[1], outs[2:]


def _sb_bwd(qkv, y, dy, width, name):
    T = qkv.shape[0]
    heads = width // HEAD_DIM
    bq = _sb_block_size(T)
    nq = T // bq
    scale = HEAD_DIM ** -0.5
    hps = min(2 * SB_HEADS_PER_STEP, heads)
    groups, W = heads // hps, hps * HEAD_DIM

    def body(q_ref, kv_hbm, y_ref, dy_ref, dq_ref, dk_hbm, dv_hbm, dk_acc, dv_acc, kbuf, vbuf, kv_sems, out_sems):
        g_id = pl.program_id(0)
        i = pl.program_id(1)
        kcol = pl.multiple_of((groups + g_id) * W, W)
        vcol = pl.multiple_of((2 * groups + g_id) * W, W)

        def fetch(j, slot):
            rows = pl.ds(pl.multiple_of(j * bq, bq), bq)
            return (pltpu.make_async_copy(kv_hbm.at[rows, pl.ds(kcol, W)], kbuf.at[slot], kv_sems.at[0, slot]),
                    pltpu.make_async_copy(kv_hbm.at[rows, pl.ds(vcol, W)], vbuf.at[slot], kv_sems.at[1, slot]))

        for cp in fetch(i, 0):
            cp.start()
        row = lax.broadcasted_iota(jnp.int32, (bq, bq), 0)
        col = lax.broadcasted_iota(jnp.int32, (bq, bq), 1)
        upper = (row > col).astype(BF16)
        upper_incl = (row >= col).astype(BF16)
        causal = col < row
        deltas = []
        for h in range(hps):
            cols = slice(h * HEAD_DIM, (h + 1) * HEAD_DIM)
            deltas.append(jnp.sum(dy_ref[:, cols].astype(F32) * y_ref[:, cols], axis=1, keepdims=True))

        @pl.when(i == 0)
        def _():
            dk_acc[...] = jnp.zeros_like(dk_acc)
            dv_acc[...] = jnp.zeros_like(dv_acc)

        def head_block(start, slot, h, carry, gcarry, dq, mask):
            cols = slice(h * HEAD_DIM, (h + 1) * HEAD_DIM)
            q = q_ref[:, cols]
            do = dy_ref[:, cols]
            kb = kbuf[slot, :, cols]
            vb = vbuf[slot, :, cols]
            z2 = _sb_logits(q, kb)
            dw = lax.dot_general(do, vb, _DIMS["nt"], preferred_element_type=F32)
            yield
            lk, lb = _sb_scores(z2, mask)
            later = _sb_later(lk, upper, carry)
            yield
            w = jnp.exp2(lb + later)
            if mask is not None:
                w = jnp.where(mask, w, 0.0)
            wb = w.astype(BF16)
            gw = dw * wb.astype(F32)
            gsuf = _sb_later(gw, upper_incl, gcarry)
            dvp = lax.dot_general(wb, do, _DIMS["tn"], preferred_element_type=F32)
            yield
            dz = gw - jnp.exp2(lb) * (gw + (deltas[h] - gsuf))
            if mask is not None:
                dz = jnp.where(mask, dz, 0.0)
            dzs = (dz * scale).astype(BF16)
            dqp = jnp.dot(dzs, kb, preferred_element_type=F32)
            dkp = lax.dot_general(dzs, q, _DIMS["tn"], preferred_element_type=F32)
            yield
            dk_acc[pl.ds(start, bq), cols] += dkp
            dv_acc[pl.ds(start, bq), cols] += dvp
            return (carry + jnp.sum(lk, axis=1, keepdims=True), gcarry + jnp.sum(gw, axis=1, keepdims=True), dq + dqp)

        def block(n, state, mask):
            j = i - n
            slot = n % 2
            for cp in fetch(j, slot):
                cp.wait()

            @pl.when(n < i)
            def _():
                for cp in fetch(j - 1, 1 - slot):
                    cp.start()

            start = pl.multiple_of(j * bq, bq)
            return tuple(_lockstep(head_block(start, slot, h, *state[h], mask) for h in range(hps)))

        zero = jnp.zeros((bq, 1), F32)
        state = block(0, ((zero, zero, jnp.zeros((bq, HEAD_DIM), F32)),) * hps, causal)
        state = lax.fori_loop(1, i + 1, lambda n, st: block(n, st, None), state)
        for h in range(hps):
            dq_ref[:, h * HEAD_DIM:(h + 1) * HEAD_DIM] = state[h][2]

        @pl.when(i == nq - 1)
        def _():
            cols = pl.ds(pl.multiple_of(g_id * W, W), W)
            copies = [pltpu.make_async_copy(dk_acc, dk_hbm.at[:, cols], out_sems.at[0]),
                      pltpu.make_async_copy(dv_acc, dv_hbm.at[:, cols], out_sems.at[1])]
            for cp in copies:
                cp.start()
            for cp in copies:
                cp.wait()

    qblk = pl.BlockSpec((bq, W), lambda g, i: (i, g))
    out = jax.ShapeDtypeStruct((T, width), F32)
    return pl.pallas_call(
        body, name=name, grid=(groups, nq),
        in_specs=[qblk, pl.BlockSpec(memory_space=pl.ANY), qblk, qblk],
        out_specs=[qblk, pl.BlockSpec(memory_space=pl.ANY), pl.BlockSpec(memory_space=pl.ANY)],
        out_shape=[out, out, out],
        scratch_shapes=[pltpu.VMEM((T, W), F32), pltpu.VMEM((T, W), F32), pltpu.VMEM((2, bq, W), BF16),
                        pltpu.VMEM((2, bq, W), BF16), pltpu.SemaphoreType.DMA((2, 2)), pltpu.SemaphoreType.DMA((2,))],
        compiler_params=_params(("arbitrary", "arbitrary"), 2 * T * W * 4),
    )(qkv, qkv, y, dy)


def _hg_constants():
    C = HG_CHUNK
    t = np.arange(C)[:, None]
    j = np.arange(C)[None, :]
    blocks = [(j <= t), (j > t)]
    masks = []
    for n in HG_LEVELS:
        right = (t % (2 * n)) >= n
        mid = (t // (2 * n)) * (2 * n) + n - 1
        blocks.append(right & (j > mid) & (j <= t))
        blocks.append((~right) & (j > t) & (j <= mid))
        tt, ss = np.arange(C)[:, None], np.arange(C)[None, :]
        same = (tt // (2 * n)) == (ss // (2 * n))
        masks.append(same & ((tt % (2 * n)) >= n) & ((ss % (2 * n)) < n))
    sums = np.concatenate(blocks, axis=0).astype(np.float32)
    return jnp.asarray(sums, BF16), jnp.asarray(np.stack(masks).astype(np.float32))


def _split3(x):
    hi = x.astype(BF16)
    r1 = x - hi.astype(F32)
    mid = r1.astype(BF16)
    lo = (r1 - mid.astype(F32)).astype(BF16)
    return hi, mid, lo


def _hg_gates(hq, hf, lb):
    sq = _sigmoid(hq)
    sf = _sigmoid(hf)
    f = lb + (1.0 - lb) * sf
    return hq * sq, sq, f, sf, 1.0 - f, jnp.log(f)


def _hg_exponents(sums, g):
    hi, mid, lo = _split3(g)
    return (jnp.dot(sums, hi, preferred_element_type=F32) + jnp.dot(sums, mid, preferred_element_type=F32)
            + jnp.dot(sums, lo, preferred_element_type=F32))


def _hg_level_scores(q, k, e_all):
    C = HG_CHUNK
    parts, prods = [], []
    for li in range(len(HG_LEVELS)):
        eq = jnp.exp(e_all[(2 + 2 * li) * C:(3 + 2 * li) * C])
        ek = jnp.exp(e_all[(3 + 2 * li) * C:(4 + 2 * li) * C])
        qt = q * eq
        kt = k * ek
        prods.append(lax.dot_general(qt.astype(BF16), kt.astype(BF16), _DIMS["nt"], preferred_element_type=F32))
        parts.append((eq, ek, qt, kt))
    return parts, prods


def _hg_intra(prods, masks_ref):
    a = masks_ref[0] * prods[0]
    for li in range(1, len(HG_LEVELS)):
        a = a + masks_ref[li] * prods[li]
    return a


def _lockstep(gens):
    gens = list(gens)
    results = [None] * len(gens)
    alive = list(range(len(gens)))
    while alive:
        for idx in list(alive):
            try:
                next(gens[idx])
            except StopIteration as done:
                results[idx] = done.value
                alive.remove(idx)
    return results


def _lower_bound(lbl_ref, cols):
    return _sigmoid(lbl_ref[0:1, cols] - lbl_ref[1:2, cols])


HG_HEADS_PER_STEP = 4


def _hg_fwd(proj, lb_logits, g_out, col0, width, name):
    T = proj.shape[0]
    heads = width // HEAD_DIM
    C = HG_CHUNK
    tb = _tile(T, 512, C)
    cpb = tb // C
    nb = T // tb
    sums, masks = _hg_constants()
    hps = min(HG_HEADS_PER_STEP, heads)
    groups, W = heads // hps, hps * HEAD_DIM
    assert col0 % W == 0 and width % W == 0
    cb = col0 // W

    def body(hq_ref, hf_ref, hi_ref, og_ref, lbl_ref, go_ref, sums_ref, masks_ref, y_ref, o_ref, st_ref, s_ref):
        i = pl.program_id(1)

        @pl.when(i == 0)
        def _():
            s_ref[...] = jnp.zeros_like(s_ref)

        go = go_ref[...]

        def chunk_head(c, rows, h):
            cols = slice(h * HEAD_DIM, (h + 1) * HEAD_DIM)
            lb = _lower_bound(lbl_ref, cols)
            q, _, _, _, k, g = _hg_gates(hq_ref[rows, cols], hf_ref[rows, cols], lb)
            v = hi_ref[rows, cols]
            vb = v.astype(BF16)
            st = s_ref[h]
            st_ref[h, c] = st
            e_all = _hg_exponents(sums_ref[...], g)
            yield
            b = e_all[0:C]
            ebl = jnp.exp(b[C - 1:C, :])
            qe = q * jnp.exp(b)
            o_inter = lax.dot_general(qe.astype(BF16), st.astype(BF16), _DIMS["nt"], preferred_element_type=F32)
            kd = k * jnp.exp(e_all[C:2 * C])
            s_new = lax.dot_general(vb, kd.astype(BF16), _DIMS["tn"], preferred_element_type=F32)
            _, prods = _hg_level_scores(q, k, e_all)
            yield
            a = _hg_intra(prods, masks_ref)
            o_intra = jnp.dot(a.astype(BF16), vb, preferred_element_type=F32)
            s_ref[h] = st * ebl + s_new
            yield
            o = (o_inter + o_intra) + jnp.sum(q * k, axis=1, keepdims=True) * v
            o_ref[rows, cols] = o
            r = lax.rsqrt(jnp.mean(o * o, axis=-1, keepdims=True) + EPS)
            og = og_ref[rows, cols]
            y_ref[rows, cols] = (((o * r) * go) * (og * _sigmoid(og))).astype(BF16)

        def chunk(c, _):
            rows = pl.ds(pl.multiple_of(c * C, C), C)
            _lockstep(chunk_head(c, rows, h) for h in range(hps))
            return 0

        lax.fori_loop(0, cpb, chunk, 0)

    def col(k):
        return pl.BlockSpec((tb, W), lambda g, i: (i, cb + k * groups + g))

    nsum = sums.shape[0]
    out_blk = pl.BlockSpec((tb, W), lambda g, i: (i, g))
    return pl.pallas_call(
        body, name=name, grid=(groups, nb),
        in_specs=[col(0), col(1), col(2), col(3),
                  pl.BlockSpec((2, W), lambda g, i: (0, g)),
                  pl.BlockSpec((1, HEAD_DIM), lambda g, i: (0, 0)),
                  pl.BlockSpec((nsum, C), lambda g, i: (0, 0)),
                  pl.BlockSpec((len(HG_LEVELS), C, C), lambda g, i: (0, 0, 0))],
        out_specs=[out_blk, out_blk,
                   pl.BlockSpec((hps, cpb, HEAD_DIM, HEAD_DIM), lambda g, i: (g, i, 0, 0))],
        out_shape=[jax.ShapeDtypeStruct((T, width), BF16), jax.ShapeDtypeStruct((T, width), F32),
                   jax.ShapeDtypeStruct((heads, T // C, HEAD_DIM, HEAD_DIM), F32)],
        scratch_shapes=[pltpu.VMEM((hps, HEAD_DIM, HEAD_DIM), F32)],
        compiler_params=_params(("parallel", "arbitrary"), tb * W * 4 * 7 + hps * cpb * HEAD_DIM * HEAD_DIM * 4),
    )(proj, proj, proj, proj, lb_logits, g_out, sums, masks)


def _hg_bwd(proj, lb_logits, g_out, o_saved, states, dy, dproj, col0, width, name):
    T = proj.shape[0]
    heads = width // HEAD_DIM
    C = HG_CHUNK
    tb = _tile(T, 512, C)
    cpb = tb // C
    nb = T // tb
    sums, masks = _hg_constants()
    nsum = sums.shape[0]
    nlev = len(HG_LEVELS)
    hps = min(HG_HEADS_PER_STEP, heads)
    groups, W = heads // hps, hps * HEAD_DIM
    cb = col0 // W

    def body(hq_ref, hf_ref, hi_ref, og_ref, lbl_ref, go_ref, sums_ref, masks_ref, o_ref, st_ref, dy_ref, dp_in,
             dp_ref, dlb_ref, dgo_ref, ds_ref, de_ref, dlb_acc, dgo_acc, dout_ref):
        del dp_in
        grp = pl.program_id(0)
        i = pl.program_id(1)
        kk = pl.program_id(2)

        @pl.when(jnp.logical_and(i == 0, kk == 0))
        def _():
            ds_ref[...] = jnp.zeros_like(ds_ref)
            dlb_acc[...] = jnp.zeros_like(dlb_acc)

        @pl.when(jnp.logical_and(jnp.logical_and(grp == 0, i == 0), kk == 0))
        def _():
            dgo_acc[...] = jnp.zeros_like(dgo_acc)

        go = go_ref[...]
        last_row = lax.broadcasted_iota(jnp.int32, (C, HEAD_DIM), 0) == C - 1

        def chunk(n, _):
            c = cpb - 1 - n
            rows = pl.ds(pl.multiple_of(c * C, C), C)
            _lockstep(chunk_head(c, rows, h) for h in range(hps))
            return 0

        def chunk_head(c, rows, h):
            cols = slice(h * HEAD_DIM, (h + 1) * HEAD_DIM)
            de_h = de_ref.at[h]
            lb = _lower_bound(lbl_ref, cols)
            hq = hq_ref[rows, cols]
            og = og_ref[rows, cols]
            q, sq, f, sf, k, g = _hg_gates(hq, hf_ref[rows, cols], lb)
            v = hi_ref[rows, cols]
            vb = v.astype(BF16)
            st = st_ref[h, c]
            stb = st.astype(BF16)
            dst = ds_ref[h]
            dstb = dst.astype(BF16)
            o = o_ref[rows, cols]
            dyc = dy_ref[rows, cols].astype(F32)
            sg = _sigmoid(og)
            r = lax.rsqrt(jnp.mean(o * o, axis=-1, keepdims=True) + EPS)
            on = (o * r) * go
            don = dyc * (og * sg)
            dout_ref[3, rows, cols] = (dyc * on * (sg * (1.0 + og * (1.0 - sg)))).astype(BF16)
            dgo_acc[...] += jnp.sum((don * (o * r)).reshape(C // SUBLANE, SUBLANE, HEAD_DIM), axis=0)
            wn = don * go
            do = r * wn - o * (r * r * r * jnp.mean(o * wn, axis=-1, keepdims=True))
            dob = do.astype(BF16)
            e_all = _hg_exponents(sums_ref[...], g)
            da = lax.dot_general(dob, vb, _DIMS["nt"], preferred_element_type=F32)
            dqe = jnp.dot(dob, stb, preferred_element_type=F32)
            dkd = jnp.dot(vb, dstb, preferred_element_type=F32)
            yield
            eb = jnp.exp(e_all[0:C])
            esuf = jnp.exp(e_all[C:2 * C])
            ebl = eb[C - 1:C, :]
            qe = q * eb
            kd = k * esuf
            parts, prods = _hg_level_scores(q, k, e_all)
            dv_state = lax.dot_general(kd.astype(BF16), dstb, _DIMS["nt"], preferred_element_type=F32)
            ds_new = lax.dot_general(dob, qe.astype(BF16), _DIMS["tn"], preferred_element_type=F32)
            yield
            a = _hg_intra(prods, masks_ref)
            dv = lax.dot_general(a.astype(BF16), dob, _DIMS["tn"], preferred_element_type=F32)
            dlev = []
            for li in range(nlev):
                _, _, qt, kt = parts[li]
                dan = (masks_ref[li] * da).astype(BF16)
                dlev.append((jnp.dot(dan, kt.astype(BF16), preferred_element_type=F32),
                             lax.dot_general(dan, qt.astype(BF16), _DIMS["tn"], preferred_element_type=F32)))
            yield
            qk = jnp.sum(q * k, axis=1, keepdims=True)
            dv = dv + qk * do + dv_state
            dqk = jnp.sum(do * v, axis=1, keepdims=True)
            dq = dqk * k
            dk = dqk * q
            for li in range(nlev):
                eq, ek, qt, kt = parts[li]
                dqt, dkt = dlev[li]
                dq = dq + dqt * eq
                dk = dk + dkt * ek
                de_h[(2 + 2 * li) * C:(3 + 2 * li) * C, :] = dqt * qt
                de_h[(3 + 2 * li) * C:(4 + 2 * li) * C, :] = dkt * kt
            dq = dq + dqe * eb
            dk = dk + dkd * esuf
            debl = jnp.sum(dst * st, axis=0, keepdims=True)
            de_h[0:C, :] = dqe * qe + jnp.where(last_row, debl * ebl, 0.0)
            de_h[C:2 * C, :] = dkd * kd
            ds_ref[h] = dst * ebl + ds_new
            dehi, delo = _split2(de_h[...])
            dg = (lax.dot_general(sums_ref[...], dehi, _DIMS["tn"], preferred_element_type=F32)
                  + lax.dot_general(sums_ref[...], delo, _DIMS["tn"], preferred_element_type=F32))
            yield
            df = dg / f - dk
            dout_ref[0, rows, cols] = (dq * (sq * (1.0 + hq * (1.0 - sq)))).astype(BF16)
            dout_ref[1, rows, cols] = (df * (1.0 - lb) * (sf * (1.0 - sf))).astype(BF16)
            dout_ref[2, rows, cols] = dv.astype(BF16)
            dlb_acc[:, cols] += jnp.sum((df * (1.0 - sf)).reshape(C // SUBLANE, SUBLANE, HEAD_DIM), axis=0)

        @pl.when(kk == 0)
        def _():
            lax.fori_loop(0, cpb, chunk, 0)

        dp_ref[...] = dout_ref[kk]

        @pl.when(jnp.logical_and(i == nb - 1, kk == 3))
        def _():
            lb = _lower_bound(lbl_ref, slice(None))
            dl0 = jnp.sum(dlb_acc[...], axis=0, keepdims=True) * (lb * (1.0 - lb))
            dlb_ref[0:1, :] = dl0
            dlb_ref[1:2, :] = -dl0

        @pl.when(jnp.logical_and(jnp.logical_and(grp == groups - 1, i == nb - 1), kk == 3))
        def _():
            dgo_ref[...] = jnp.sum(dgo_acc[...], axis=0, keepdims=True)

    def col(k):
        return pl.BlockSpec((tb, W), lambda g, i, kk: (nb - 1 - i, cb + k * groups + g))

    rev = pl.BlockSpec((tb, W), lambda g, i, kk: (nb - 1 - i, g))
    return pl.pallas_call(
        body, name=name, grid=(groups, nb, 4),
        in_specs=[col(0), col(1), col(2), col(3),
                  pl.BlockSpec((2, W), lambda g, i, kk: (0, g)),
                  pl.BlockSpec((1, HEAD_DIM), lambda g, i, kk: (0, 0)),
                  pl.BlockSpec((nsum, C), lambda g, i, kk: (0, 0)),
                  pl.BlockSpec((nlev, C, C), lambda g, i, kk: (0, 0, 0)),
                  rev,
                  pl.BlockSpec((hps, cpb, HEAD_DIM, HEAD_DIM), lambda g, i, kk: (g, nb - 1 - i, 0, 0)),
                  rev,
                  pl.BlockSpec(memory_space=pl.ANY)],
        out_specs=[pl.BlockSpec((tb, W), lambda g, i, kk: (nb - 1 - i, cb + kk * groups + g)),
                   pl.BlockSpec((2, W), lambda g, i, kk: (0, g)),
                   pl.BlockSpec((1, HEAD_DIM), lambda g, i, kk: (0, 0))],
        out_shape=[jax.ShapeDtypeStruct(dproj.shape, BF16), jax.ShapeDtypeStruct((2, width), F32),
                   jax.ShapeDtypeStruct((1, HEAD_DIM), F32)],
        input_output_aliases={11: 0},
        scratch_shapes=[pltpu.VMEM((hps, HEAD_DIM, HEAD_DIM), F32), pltpu.VMEM((hps, nsum, HEAD_DIM), F32),
                        pltpu.VMEM((SUBLANE, W), F32), pltpu.VMEM((SUBLANE, HEAD_DIM), F32),
                        pltpu.VMEM((4, tb, W), BF16)],
        compiler_params=_params(("arbitrary", "arbitrary", "arbitrary"),
                                tb * W * 4 * 12 + hps * cpb * HEAD_DIM * HEAD_DIM * 4),
    )(proj, proj, proj, proj, lb_logits, g_out, sums, masks, o_saved, states, dy, dproj)


def _merge_fwd(proj, ma, mb, gate_col0, name):
    T, D = ma.shape
    tr = _tile(T, 256, SUBLANE)
    cw = _tile(D, 1024, LANE)
    nj = D // cw
    assert gate_col0 % cw == 0
    g0 = gate_col0 // cw

    def body(ga_ref, gb_ref, ma_ref, mb_ref, o_ref):
        o_ref[...] = (_sigmoid(ga_ref[...]) * ma_ref[...] + _sigmoid(gb_ref[...]) * mb_ref[...]).astype(BF16)

    blk = pl.BlockSpec((tr, cw), lambda i, j: (i, j))
    return pl.pallas_call(
        body, name=name, grid=(T // tr, nj),
        in_specs=[pl.BlockSpec((tr, cw), lambda i, j: (i, g0 + j)),
                  pl.BlockSpec((tr, cw), lambda i, j: (i, g0 + nj + j)), blk, blk],
        out_specs=blk, out_shape=jax.ShapeDtypeStruct((T, D), BF16),
        compiler_params=_params(("parallel", "parallel"), tr * cw * 18 * 2),
    )(proj, proj, ma, mb)


def _merge_bwd(proj, ma, mb, dm, gate_col0, name):
    T, D = ma.shape
    tr = _tile(T, 256, SUBLANE)
    cw = _tile(D, 1024, LANE)
    nj = D // cw
    g0 = gate_col0 // cw

    def body(g_ref, ma_ref, mb_ref, dm_ref, dp_ref, dmm_ref):
        j = pl.program_id(1)
        s = _sigmoid(g_ref[...])
        dmv = dm_ref[...]
        mm = jnp.where(j < nj, ma_ref[...], mb_ref[...])
        dp_ref[...] = (dmv * mm * (s * (1.0 - s))).astype(BF16)
        dmm_ref[...] = (dmv * s).astype(BF16)

    blk = pl.BlockSpec((tr, cw), lambda i, j: (i, j % nj))
    return pl.pallas_call(
        body, name=name, grid=(T // tr, 2 * nj),
        in_specs=[pl.BlockSpec((tr, cw), lambda i, j: (i, g0 + j)), blk, blk, blk],
        out_specs=[pl.BlockSpec((tr, cw), lambda i, j: (i, g0 + j)),
                   pl.BlockSpec((None, tr, cw), lambda i, j: (j // nj, i, j % nj))],
        out_shape=[jax.ShapeDtypeStruct(proj.shape, BF16), jax.ShapeDtypeStruct((2, T, D), BF16)],
        compiler_params=_params(("parallel", "parallel"), tr * cw * 20 * 2),
    )(proj, ma, mb, dm)


def _causal_conv(ext, w, b):
    s1 = pltpu.roll(ext, 1, 0)
    s2 = pltpu.roll(ext, 2, 0)
    out = b + w[0:1, :] * s2
    out = out + w[1:2, :] * s1
    out = out + w[2:3, :] * ext
    return out, s1, s2


def _conv_fwd(up, convw, convb, name):
    T, F2 = up.shape
    tc = convw.shape[2]
    half = (F2 // 2) // tc
    assert half * tc * 2 == F2
    tr = _tile(T, 256, SUBLANE)
    hb = tr // SUBLANE

    def body(g_ref, gp_ref, v_ref, vp_ref, wg_ref, wv_ref, bg_ref, bv_ref, o_ref):
        first = pl.program_id(1) == 0

        def lanes(c, _):
            cols = pl.ds(pl.multiple_of(c * LANE, LANE), LANE)

            def conv(cur_ref, prev_ref, w_ref, b_ref):
                prev = jnp.where(first, 0.0, prev_ref[:, cols])
                ext = jnp.concatenate([prev, cur_ref[:, cols]], axis=0)
                return _causal_conv(ext, w_ref[:, cols], b_ref[:, cols])[0][SUBLANE:]

            gate = conv(g_ref, gp_ref, wg_ref, bg_ref)
            val = conv(v_ref, vp_ref, wv_ref, bv_ref)
            o_ref[:, cols] = ((gate * _sigmoid(gate)) * val).astype(BF16)
            return 0

        lax.fori_loop(0, tc // LANE, lanes, 0)

    def main(off):
        return pl.BlockSpec((tr, tc), lambda j, i: (i, off + j))

    def prev(off):
        return pl.BlockSpec((SUBLANE, tc), lambda j, i: (jnp.maximum(i * hb - 1, 0), off + j))

    def wspec(off):
        return pl.BlockSpec((None, 3, tc), lambda j, i: (off + j, 0, 0))

    def bspec(off):
        return pl.BlockSpec((1, tc), lambda j, i: (0, off + j))

    return pl.pallas_call(
        body, name=name, grid=(half, T // tr),
        in_specs=[main(0), prev(0), main(half), prev(half), wspec(0), wspec(half), bspec(0), bspec(half)],
        out_specs=pl.BlockSpec((tr, tc), lambda j, i: (i, j)),
        out_shape=jax.ShapeDtypeStruct((T, F2 // 2), BF16),
        compiler_params=_params(("parallel", "parallel"), tr * tc * 4 * 12),
    )(up, up, up, up, convw, convw, convb, convb)


def _conv_bwd(da, up, convw, convb, name):
    T, F2 = up.shape
    tc = convw.shape[2]
    half = (F2 // 2) // tc
    tr = _tile(T, 128, SUBLANE)
    hb = tr // SUBLANE
    nrow = T // tr
    n = tr + 2 * SUBLANE

    def body(g_ref, gp_ref, gn_ref, v_ref, vp_ref, vn_ref, da_ref, dan_ref, wg_ref, wv_ref, bg_ref, bv_ref,
             dup_ref, gw_ref, acc_ref):
        i = pl.program_id(1)
        first = i == 0
        last = i == nrow - 1

        @pl.when(first)
        def _():
            acc_ref[...] = jnp.zeros_like(acc_ref)

        rows = slice(SUBLANE, SUBLANE + tr)

        def fold(t):
            return jnp.sum(t.reshape(tr // SUBLANE, SUBLANE, LANE), axis=0)

        def lanes(c, _):
            cols = pl.ds(pl.multiple_of(c * LANE, LANE), LANE)

            def conv(cur_ref, prev_ref, next_ref, w_ref, b_ref):
                prev = jnp.where(first, 0.0, prev_ref[:, cols])
                ext = jnp.concatenate([prev, cur_ref[:, cols], next_ref[:, cols]], axis=0)
                return (ext,) + _causal_conv(ext, w_ref[:, cols], b_ref[:, cols])

            g_ext, gate, g_s1, g_s2 = conv(g_ref, gp_ref, gn_ref, wg_ref, bg_ref)
            v_ext, val, v_s1, v_s2 = conv(v_ref, vp_ref, vn_ref, wv_ref, bv_ref)
            da_ext = jnp.concatenate([jnp.zeros((SUBLANE, LANE), F32), da_ref[:, cols],
                                      jnp.where(last, 0.0, dan_ref[:, cols])], axis=0)
            sg = _sigmoid(gate)
            d_gate = da_ext * val * (sg * (1.0 + gate * (1.0 - sg)))
            d_val = da_ext * (gate * sg)
            halves = ((d_gate, wg_ref[:, cols], (g_s2, g_s1, g_ext)), (d_val, wv_ref[:, cols], (v_s2, v_s1, v_ext)))
            for hf, (dc, w, taps) in enumerate(halves):
                dup = w[2:3, :] * dc + w[1:2, :] * pltpu.roll(dc, n - 1, 0) + w[0:1, :] * pltpu.roll(dc, n - 2, 0)
                dup_ref[hf, :, cols] = dup[rows].astype(BF16)
                dc_m = dc[rows]
                acc_ref[hf, 0, :, cols] += fold(dc_m)
                for k in range(3):
                    acc_ref[hf, 1 + k, :, cols] += fold(dc_m * taps[k][rows])
            return 0

        lax.fori_loop(0, tc // LANE, lanes, 0)

        @pl.when(last)
        def _():
            gw_ref[...] = jnp.sum(acc_ref[...], axis=2)

    def main(off):
        return pl.BlockSpec((tr, tc), lambda j, i: (i, off + j))

    def prev(off):
        return pl.BlockSpec((SUBLANE, tc), lambda j, i: (jnp.maximum(i * hb - 1, 0), off + j))

    def nxt(off):
        return pl.BlockSpec((SUBLANE, tc), lambda j, i: (jnp.minimum((i + 1) * hb, T // SUBLANE - 1), off + j))

    def wspec(off):
        return pl.BlockSpec((None, 3, tc), lambda j, i: (off + j, 0, 0))

    def bspec(off):
        return pl.BlockSpec((1, tc), lambda j, i: (0, off + j))

    return pl.pallas_call(
        body, name=name, grid=(half, nrow),
        in_specs=[main(0), prev(0), nxt(0), main(half), prev(half), nxt(half), main(0), nxt(0),
                  wspec(0), wspec(half), bspec(0), bspec(half)],
        out_specs=[pl.BlockSpec((2, tr, tc), lambda j, i: (0, i, j)), pl.BlockSpec((2, 4, tc), lambda j, i: (0, 0, j))],
        out_shape=[jax.ShapeDtypeStruct((2, T, F2 // 2), BF16), jax.ShapeDtypeStruct((2, 4, F2 // 2), F32)],
        scratch_shapes=[pltpu.VMEM((2, 4, SUBLANE, tc), F32)],
        compiler_params=_params(("parallel", "arbitrary"), n * tc * 4 * 24),
    )(up, up, up, up, up, up, da, da, convw, convw, convb, convb)


def _loss_head(y, target, name):
    T, D = y.shape
    tr = _tile(T, 256, SUBLANE)
    nrow = T // tr

    def body(y_ref, t_ref, d_ref, db_ref, l_ref, acc_ref):
        i = pl.program_id(0)
        diff = y_ref[...] - t_ref[...]
        dy = diff / D
        d_ref[...] = dy
        db_ref[...] = dy.astype(BF16)
        part = jnp.sum((diff * diff).reshape(tr // SUBLANE, SUBLANE, D), axis=0)

        @pl.when(i == 0)
        def _():
            acc_ref[...] = part

        @pl.when(i > 0)
        def _():
            acc_ref[...] += part

        @pl.when(i == nrow - 1)
        def _():
            col = jnp.sum(acc_ref[...], axis=0, keepdims=True)
            l_ref[...] = jnp.broadcast_to(0.5 * (jnp.sum(col, axis=1, keepdims=True) / D), (1, LANE))

    row = pl.BlockSpec((tr, D), lambda i: (i, 0))
    return pl.pallas_call(
        body, name=name, grid=(nrow,), in_specs=[row, row],
        out_specs=[row, row, pl.BlockSpec((1, LANE), lambda i: (0, 0))],
        out_shape=[jax.ShapeDtypeStruct((T, D), F32), jax.ShapeDtypeStruct((T, D), BF16),
                   jax.ShapeDtypeStruct((1, LANE), F32)],
        scratch_shapes=[pltpu.VMEM((SUBLANE, D), F32)],
        compiler_params=_params(("arbitrary",), tr * D * 14 * 2),
    )(y, target)


def _adamw(w, parts, m, v, name):
    R, C = w.shape
    P = parts.shape[0]
    tr = _tile(R, 64, SUBLANE)
    tc = _tile(C, 2048, LANE)

    def body(w_ref, p_ref, m_ref, v_ref, g_ref, d_ref, nm_ref, nv_ref):
        g = p_ref[0].astype(F32)
        for s in range(1, P):
            g = g + p_ref[s].astype(F32)
        wv = w_ref[...]
        nm = ADAM_B1 * m_ref[...] + (1.0 - ADAM_B1) * g
        nv = ADAM_B2 * v_ref[...] + (1.0 - ADAM_B2) * (g * g)
        m_hat = nm / (1.0 - ADAM_B1 ** ADAM_STEP)
        v_hat = nv / (1.0 - ADAM_B2 ** ADAM_STEP)
        g_ref[...] = g
        d_ref[...] = -ADAM_LR * (m_hat / (jnp.sqrt(v_hat) + ADAM_EPS) + ADAM_WD * wv)
        nm_ref[...] = nm
        nv_ref[...] = nv

    blk = pl.BlockSpec((tr, tc), lambda i, j: (i, j))
    shp = jax.ShapeDtypeStruct((R, C), F32)
    return pl.pallas_call(
        body, name=name, grid=(R // tr, C // tc),
        in_specs=[blk, pl.BlockSpec((P, tr, tc), lambda i, j: (0, i, j)), blk, blk],
        out_specs=[blk] * 4, out_shape=[shp] * 4,
        compiler_params=_params(("parallel", "parallel"), tr * tc * 4 * (P + 8) * 2),
    )(w, parts, m, v)


def _place():
    x, y, c = (lax.axis_index(a) for a in MESH_AXES)
    return x, y, c, 4 * x + 2 * y + c


def _peers(x, y, c):
    out = []
    for d in range(1, N_DEV):
        px = x + (d >> 2 & 1) - 2 * x * (d >> 2 & 1)
        py = y + (d >> 1 & 1) - 2 * y * (d >> 1 & 1)
        pc = c + (d & 1) - 2 * c * (d & 1)
        out.append(((px, py, pc), 4 * px + 2 * py + pc))
    return out


class _Side:
    def __init__(self, arrays, scatter):
        self.arrays, self.scatter, self.n = list(arrays), scatter, len(arrays)

    def in_specs(self):
        return [pl.BlockSpec(memory_space=pltpu.HBM)] * self.n

    out_specs = in_specs

    def out_shape(self):
        return [jax.ShapeDtypeStruct(a.shape if self.scatter else (N_DEV,) + a.shape, a.dtype) for a in self.arrays]

    def scratch(self):
        return [pltpu.SemaphoreType.DMA((self.n, N_DEV - 1)), pltpu.SemaphoreType.DMA((self.n, N_DEV - 1)),
                pltpu.SemaphoreType.DMA((self.n,))]

    def _copies(self, ins, outs, sems):
        send_sems, recv_sems, local_sems = sems
        x, y, c, me = _place()
        peers = _peers(x, y, c)
        local, sends, recvs = [], [], []
        for t in range(self.n):
            src_me = ins[t].at[me] if self.scatter else ins[t]
            local.append(pltpu.make_async_copy(src_me, outs[t].at[me], local_sems.at[t]))
            for d, (peer, pidx) in enumerate(peers):
                src = ins[t].at[pidx] if self.scatter else ins[t]
                sends.append(pltpu.make_async_remote_copy(
                    src_ref=src, dst_ref=outs[t].at[me], send_sem=send_sems.at[t, d], recv_sem=recv_sems.at[t, d],
                    device_id=peer, device_id_type=MESH_ID))
                recvs.append(pltpu.make_async_remote_copy(
                    src_ref=src, dst_ref=outs[t].at[pidx], send_sem=send_sems.at[t, d], recv_sem=recv_sems.at[t, d],
                    device_id=peer, device_id_type=MESH_ID))
        return local, sends, recvs

    def start(self, ins, outs, sems):
        local, sends, _ = self._copies(ins, outs, sems)
        for cp in local + sends:
            cp.start()

    def wait(self, ins, outs, sems):
        local, sends, recvs = self._copies(ins, outs, sems)
        for cp in recvs:
            cp.wait_recv()
        for cp in sends:
            cp.wait_send()
        for cp in local:
            cp.wait()


def _grid_edges(sizes):
    ids = [pl.program_id(a) for a in range(len(sizes))]
    first = functools.reduce(jnp.logical_and, [i == 0 for i in ids])
    last = functools.reduce(jnp.logical_and, [i == s - 1 for i, s in zip(ids, sizes)])
    return first, last


def _exchange(arrays, scatter, name):
    side = _Side(arrays, scatter)
    n = side.n

    def body(*refs):
        ins, outs, sems = refs[:n], refs[n:2 * n], refs[2 * n:]
        side.start(ins, outs, sems)
        side.wait(ins, outs, sems)

    return pl.pallas_call(
        body, name=name, in_specs=side.in_specs(), out_specs=side.out_specs(), out_shape=side.out_shape(),
        scratch_shapes=side.scratch(),
    )(*arrays)


def _all_reduce_small(vec, name):
    R = vec.shape[0]

    def body(v_ref, o_ref, gath_ref, send_sems, recv_sems):
        x, y, c, me = _place()
        sends, recvs = [], []
        for d, (peer, pidx) in enumerate(_peers(x, y, c)):
            cp = pltpu.make_async_remote_copy(
                src_ref=v_ref, dst_ref=gath_ref.at[me], send_sem=send_sems.at[d], recv_sem=recv_sems.at[d],
                device_id=peer, device_id_type=MESH_ID)
            cp.start()
            sends.append(cp)
            recvs.append(pltpu.make_async_remote_copy(
                src_ref=v_ref, dst_ref=gath_ref.at[pidx], send_sem=send_sems.at[d], recv_sem=recv_sems.at[d],
                device_id=peer, device_id_type=MESH_ID))
        gath_ref[me] = v_ref[...]
        for cp in recvs:
            cp.wait_recv()
        for cp in sends:
            cp.wait_send()
        total = gath_ref[0]
        for s in range(1, N_DEV):
            total = total + gath_ref[s]
        o_ref[...] = total

    vm = pl.BlockSpec(memory_space=pltpu.VMEM)
    return pl.pallas_call(
        body, name=name, in_specs=[vm], out_specs=vm, out_shape=jax.ShapeDtypeStruct(vec.shape, F32),
        scratch_shapes=[pltpu.VMEM((N_DEV, R, LANE), F32), pltpu.SemaphoreType.DMA((N_DEV - 1,)),
                        pltpu.SemaphoreType.DMA((N_DEV - 1,))],
        compiler_params=pltpu.CompilerParams(vmem_limit_bytes=_vmem(R * LANE * 4 * 12)),
    )(vec)


def _pack(parts):
    flat = jnp.concatenate([p.reshape(-1).astype(F32) for p in parts])
    rows = -(-flat.shape[0] // (LANE * SUBLANE)) * SUBLANE
    return jnp.pad(flat, (0, rows * LANE - flat.shape[0])).reshape(rows, LANE)


def _unpack(packed, shapes):
    flat = packed.reshape(-1)
    out, at = [], 0
    for s in shapes:
        size = int(np.prod(s))
        out.append(flat[at:at + size].reshape(s))
        at += size
    return out


def kernel(x, g_mix, w_in, g_q, g_k, lb_logits, g_hg_out, p_a, p_b, w_o, g_ffn, w_up, conv_w, conv_b, w_down, loss_target, m_g_mix, m_w_in, m_g_q, m_g_k, m_lb_logits, m_g_hg_out, m_p_a, m_p_b, m_w_o, m_g_ffn, m_w_up, m_conv_w, m_conv_b, m_w_down, v_g_mix, v_w_in, v_g_q, v_g_k, v_lb_logits, v_g_hg_out, v_p_a, v_p_b, v_w_o, v_g_ffn, v_w_up, v_conv_w, v_conv_b, v_w_down):
    assert x.shape[0] == 1 and lb_logits.shape[0] == 2
    xs, target = x[0], loss_target[0]
    T, D = xs.shape
    A = p_a.shape[1]
    HW = p_b.shape[1]
    hg_col0 = 3 * A
    gate_col0 = 3 * A + 4 * HW
    F = w_down.shape[1] * N_DEV
    assert w_in.shape[2] * N_DEV == gate_col0 + 2 * D and w_up.shape[2] * N_DEV == 2 * F

    (win_g,) = _exchange([w_in[0].astype(BF16)], False, "gather_w_in")
    later_weights = _Side([w_up[0].astype(BF16), p_a[0].astype(BF16), p_b[0].astype(BF16), w_o[0].astype(BF16),
                           w_down[0].astype(BF16), conv_w[0]], False)

    u = _rmsnorm_fwd(xs, g_mix, "norm_mix")
    proj = _matmul(u, win_g, mode="nn", b_shards=True, name="proj_in")
    gains = jnp.stack([g_q[0], g_k[0], jnp.ones_like(g_q[0])])[:, None, :]
    qkv = _qkv_prep(proj, gains, A, "qkv_prep")
    ya, ya32, (wup_g, pa_g, pb_g, wo_g, wdown_g, convw_g) = _sb_fwd(qkv, A, "sb_fwd", side=later_weights)
    wo_full = wo_g.reshape(D, D)
    wdown_full = wdown_g.reshape(F, D)
    yb, ob, states = _hg_fwd(proj, lb_logits, g_hg_out, hg_col0, HW, "hg_fwd")
    ma = _matmul(ya, pa_g, mode="nn", b_shards=True, name="proj_a")
    mb = _matmul(yb, pb_g, mode="nn", b_shards=True, name="proj_b")
    m = _merge_fwd(proj, ma, mb, gate_col0, "merge_fwd")
    h1 = _matmul(m, wo_full, mode="nn", add=xs, name="proj_o")
    u2 = _rmsnorm_fwd(h1, g_ffn, "norm_ffn")
    up = _matmul(u2, wup_g, mode="nn", b_shards=True, name="ffn_up")
    act = _conv_fwd(up, convw_g, conv_b, "conv_fwd")
    y = _matmul(act, wdown_full, mode="nn", add=h1, tn_pref=2048, name="ffn_down")
    dy, dyb, loss_part = _loss_head(y, target, "loss_head")
    loss = lax.psum(loss_part[0, 0], MESH_AXES)

    dact = _matmul(dyb, wdown_full, mode="nt", name="d_act")
    g_wdown = _matmul(act, dyb, mode="tn", out_dtype=BF16, name="g_w_down")
    dup, g_conv = _conv_bwd(dact, up, convw_g, conv_b, "conv_bwd")
    g_wup, (r_wdown,) = _matmul(u2, dup, mode="tn", b_halves=True, out_shards=True, out_dtype=BF16, tm_pref=1024,
                                name="g_w_up",
                                side=_Side([g_wdown.reshape(N_DEV, F // N_DEV, D)], True))
    du2, (r_wup,) = _matmul(dup, wup_g, mode="nt", a_halves=True, b_shards=True, tn_pref=2048, name="d_u2",
                            side=_Side([g_wup], True))
    dh1, dh1b, g_gffn = _rmsnorm_bwd(h1, g_ffn, du2, dy, "norm_ffn_bwd")
    dm = _matmul(dh1b, wo_full, mode="nt", name="d_m")
    g_wo = _matmul(m, dh1b, mode="tn", out_dtype=BF16, tm_pref=1024, name="g_w_o")
    dproj, dmab = _merge_bwd(proj, ma, mb, dm, gate_col0, "merge_bwd")
    dya = _matmul(dmab, pa_g, mode="nt", b_shards=True, lead_a=0, out_dtype=BF16, name="d_ya")
    g_pa = _matmul(ya, dmab, mode="tn", out_shards=True, out_dtype=BF16, lead_b=0, name="g_p_a")
    dyb_ = _matmul(dmab, pb_g, mode="nt", b_shards=True, lead_a=1, out_dtype=BF16, name="d_yb")
    g_pb = _matmul(yb, dmab, mode="tn", out_shards=True, out_dtype=BF16, lead_b=1, name="g_p_b")
    dproj, g_lb, g_ghg = _hg_bwd(proj, lb_logits, g_hg_out, ob, states, dyb_, dproj, hg_col0, HW, "hg_bwd")
    dq, dk, dv = _sb_bwd(qkv, ya32, dya, A, "sb_bwd")
    dproj, g_gains = _qkv_bwd(proj, gains, dq, dk, dv, dproj, "qkv_bwd")
    g_win, (r_wo, r_pa, r_pb) = _matmul(
        u, dproj, mode="tn", out_shards=True, out_dtype=BF16, tm_pref=1024, name="g_w_in",
        side=_Side([g_wo.reshape(N_DEV, D // N_DEV, D), g_pa, g_pb], True))
    du, (r_win,) = _matmul(dproj, win_g, mode="nt", b_shards=True, tn_pref=2048, name="d_u",
                           side=_Side([g_win], True))
    gx, _, g_gmix = _rmsnorm_bwd(xs, g_mix, du, dh1, "norm_mix_bwd")

    g_convb = g_conv[:, 0].reshape(1, 2 * F)
    g_convw = g_conv[:, 1:4].transpose(1, 0, 2).reshape(3, 2 * F)
    small = [g_gmix, g_gains[0], g_gains[1], g_lb, g_ghg, g_gffn, g_convb, g_convw]
    small_shapes = [p.shape for p in small]
    red = _unpack(_all_reduce_small(_pack(small), "reduce_small"), small_shapes)

    _, _, _, me = _place()
    cs = conv_w.shape[2]
    big = {
        "w_in": _adamw(w_in[0], r_win, m_w_in[0], v_w_in[0], "adamw_w_in"),
        "w_up": _adamw(w_up[0], r_wup, m_w_up[0], v_w_up[0], "adamw_w_up"),
        "p_a": _adamw(p_a[0], r_pa, m_p_a[0], v_p_a[0], "adamw_p_a"),
        "p_b": _adamw(p_b[0], r_pb, m_p_b[0], v_p_b[0], "adamw_p_b"),
        "w_o": _adamw(w_o[0], r_wo, m_w_o[0], v_w_o[0], "adamw_w_o"),
        "w_down": _adamw(w_down[0], r_wdown, m_w_down[0], v_w_down[0], "adamw_w_down"),
        "conv_w": _adamw(conv_w[0], lax.dynamic_slice_in_dim(red[7], me * cs, cs, axis=1)[None],
                         m_conv_w[0], v_conv_w[0], "adamw_conv_w"),
    }
    rep_w = [g_mix, g_q, g_k, lb_logits, g_hg_out, g_ffn, conv_b]
    rep_m = [m_g_mix, m_g_q, m_g_k, m_lb_logits, m_g_hg_out, m_g_ffn, m_conv_b]
    rep_v = [v_g_mix, v_g_q, v_g_k, v_lb_logits, v_g_hg_out, v_g_ffn, v_conv_b]
    rep_shapes = [p.shape for p in rep_w]
    rep_out = _adamw(_pack(rep_w), _pack(red[:7])[None], _pack(rep_m), _pack(rep_v), "adamw_small")
    rep = [_unpack(o, rep_shapes) for o in rep_out]
    rep_names = ["g_mix", "g_q", "g_k", "lb_logits", "g_hg_out", "g_ffn", "conv_b"]

    order = ["g_mix", "w_in", "g_q", "g_k", "lb_logits", "g_hg_out", "p_a", "p_b", "w_o", "g_ffn", "w_up",
             "conv_w", "conv_b", "w_down"]

    def leaf(kind, pname):
        if pname in big:
            return big[pname][kind][None]
        return rep[kind][rep_names.index(pname)]

    return (loss, gx[None], *[leaf(kind, p) for kind in range(4) for p in order])
```

```python
import functools

import numpy as np
import jax
import jax.numpy as jnp
from jax import lax
from jax.experimental import pallas as pl
from jax.experimental.pallas import tpu as pltpu

F32 = jnp.float32
BF16 = jnp.bfloat16

N_DEV = 8
HEAD_DIM = 128
HG_CHUNK = 64
HG_LEVELS = (1, 2, 4, 8, 16, 32)
EPS = 1e-6
ADAM_LR = 0.001
ADAM_B1 = 0.9
ADAM_B2 = 0.999
ADAM_EPS = 1e-08
ADAM_WD = 0.01
ADAM_STEP = 10
LANE = 128
SUBLANE = 8
VMEM_CAP = 56 << 20
MESH_AXES = ("x", "y", "c")
MESH_ID = pl.DeviceIdType.MESH


def _tile(dim, pref, align):
    if dim <= pref:
        return dim
    t = (pref // align) * align
    while t >= align:
        if dim % t == 0:
            return t
        t -= align
    return dim


def _vmem(est_bytes):
    return int(min(max(2 * est_bytes + (8 << 20), 32 << 20), VMEM_CAP))


def _params(sem, est_bytes):
    return pltpu.CompilerParams(dimension_semantics=sem, vmem_limit_bytes=_vmem(est_bytes))


def _sigmoid(x):
    return 1.0 / (1.0 + jnp.exp(-x))


_DIMS = {"nn": (((1,), (0,)), ((), ())), "nt": (((1,), (1,)), ((), ())), "tn": (((0,), (0,)), ((), ()))}


def _matmul(a, b, *, mode, name, out_dtype=F32, add=None, b_shards=False, out_shards=False, lead_a=None,
            lead_b=None, a_halves=False, b_halves=False, side=None, tm_pref=512, tn_pref=1408, tk_pref=2048):
    if mode == "tn":
        K, M = a.shape[-2:]
    else:
        M, K = a.shape[-2:]
    if a_halves:
        assert mode == "nt" and b_shards and lead_a is None
        K = 2 * K
    if lead_b is not None or b_halves:
        assert not b_shards and mode == "tn"
    if b_shards:
        S = b.shape[0]
        if mode == "nn":
            assert b.shape[1] == K
            N, tn, tk = S * b.shape[2], b.shape[2], _tile(K, tk_pref, LANE)
        else:
            assert mode == "nt" and S * b.shape[2] == K
            N, tk, tn = b.shape[1], b.shape[2], _tile(b.shape[1], tn_pref, LANE)
    else:
        if mode == "nt":
            N = b.shape[0]
            assert b.shape[1] == K
        else:
            N = b.shape[-1] * (2 if b_halves else 1)
            assert b.shape[-2] == K
        tn = _tile(N, tn_pref, LANE)
        tk = _tile(K, tk_pref, LANE)
    if out_shards:
        assert N % N_DEV == 0
        tn = N // N_DEV
    per_half_n, per_half_k = (N // 2) // tn, (K // 2) // tk
    assert not b_halves or per_half_n * tn * 2 == N
    assert not a_halves or per_half_k * tk * 2 == K
    tm = _tile(M, tm_pref, LANE)
    nm, nn, nk = M // tm, N // tn, K // tk
    assert nm * tm == M and nn * tn == N and nk * tk == K

    if mode == "tn":
        a_spec = pl.BlockSpec((tk, tm), lambda j, i, k: (k, i))
    elif lead_a is not None:
        a_spec = pl.BlockSpec((None, tm, tk), lambda j, i, k: (lead_a, i, k))
    elif a_halves:
        a_spec = pl.BlockSpec((None, tm, tk), lambda j, i, k: (k // per_half_k, i, k % per_half_k))
    else:
        a_spec = pl.BlockSpec((tm, tk), lambda j, i, k: (i, k))
    if b_halves:
        b_spec = pl.BlockSpec((None, tk, tn), lambda j, i, k: (j // per_half_n, k, j % per_half_n))
    elif lead_b is not None:
        b_spec = pl.BlockSpec((None, tk, tn), lambda j, i, k: (lead_b, k, j))
    elif b_shards and mode == "nn":
        b_spec = pl.BlockSpec((None, tk, tn), lambda j, i, k: (j, k, 0))
    elif b_shards:
        b_spec = pl.BlockSpec((None, tn, tk), lambda j, i, k: (k, j, 0))
    elif mode == "nt":
        b_spec = pl.BlockSpec((tn, tk), lambda j, i, k: (j, k))
    else:
        b_spec = pl.BlockSpec((tk, tn), lambda j, i, k: (k, j))
    in_specs = [a_spec, b_spec]
    operands = [a, b]
    if add is not None:
        assert not out_shards and add.shape == (M, N)
        in_specs.append(pl.BlockSpec((tm, tn), lambda j, i, k: (i, j)))
        operands.append(add)
    if out_shards:
        out_shape = jax.ShapeDtypeStruct((N_DEV, M, tn), out_dtype)
        out_spec = pl.BlockSpec((None, tm, tn), lambda j, i, k: (j, i, 0))
    else:
        out_shape = jax.ShapeDtypeStruct((M, N), out_dtype)
        out_spec = pl.BlockSpec((tm, tn), lambda j, i, k: (i, j))
    dims = _DIMS[mode]
    has_add = add is not None
    n_in = 3 if has_add else 2
    n_side = side.n if side is not None else 0

    def body(*refs):
        a_ref, b_ref = refs[0], refs[1]
        add_ref = refs[2] if has_add else None
        o_ref = refs[n_in + n_side]
        acc_ref = refs[n_in + 2 * n_side + 1]
        k = pl.program_id(2)
        if side is not None:
            side_refs = (refs[n_in:n_in + n_side], refs[n_in + n_side + 1:n_in + 2 * n_side + 1],
                         refs[n_in + 2 * n_side + 2:])
            first, last = _grid_edges((nn, nm, nk))

            @pl.when(first)
            def _():
                side.start(*side_refs)

        part = lax.dot_general(a_ref[...], b_ref[...], dims, preferred_element_type=F32)

        def finish(total):
            if has_add:
                total = add_ref[...] + total
            o_ref[...] = total.astype(out_dtype)

        if nk == 1:
            finish(part)
        else:
            @pl.when(k == 0)
            def _():
                acc_ref[...] = part

            @pl.when(jnp.logical_and(k > 0, k < nk - 1))
            def _():
                acc_ref[...] += part

            @pl.when(k == nk - 1)
            def _():
                finish(acc_ref[...] + part)

        if side is not None:
            @pl.when(last)
            def _():
                side.wait(*side_refs)

    est = 2 * (tm * tk * 2 + tk * tn * 2 + tm * tn * 4 * (2 if has_add else 1)) + tm * tn * 4 * 2
    acc = pltpu.VMEM((tm, tn) if nk > 1 else (SUBLANE, LANE), F32)
    if side is None:
        return pl.pallas_call(
            body, name=name, grid=(nn, nm, nk), in_specs=in_specs, out_specs=out_spec, out_shape=out_shape,
            scratch_shapes=[acc], compiler_params=_params(("parallel", "parallel", "arbitrary"), est),
        )(*operands)
    outs = pl.pallas_call(
        body, name=name, grid=(nn, nm, nk), in_specs=in_specs + side.in_specs(),
        out_specs=[out_spec] + side.out_specs(), out_shape=[out_shape] + side.out_shape(),
        scratch_shapes=[acc] + side.scratch(), compiler_params=_params(("arbitrary",) * 3, est),
    )(*operands, *side.arrays)
    return outs[0], outs[1:]


def _rmsnorm_fwd(x, g, name):
    T, D = x.shape
    tr = _tile(T, 256, SUBLANE)

    def body(x_ref, g_ref, o_ref):
        xf = x_ref[...]
        r = lax.rsqrt(jnp.mean(xf * xf, axis=-1, keepdims=True) + EPS)
        o_ref[...] = ((xf * r) * g_ref[...]).astype(BF16)

    return pl.pallas_call(
        body, name=name, grid=(T // tr,),
        in_specs=[pl.BlockSpec((tr, D), lambda i: (i, 0)), pl.BlockSpec((1, D), lambda i: (0, 0))],
        out_specs=pl.BlockSpec((tr, D), lambda i: (i, 0)), out_shape=jax.ShapeDtypeStruct((T, D), BF16),
        compiler_params=_params(("parallel",), tr * D * 6 * 2),
    )(x, g)


def _rmsnorm_bwd(x, g, dy, res, name):
    T, D = x.shape
    tr = _tile(T, 256, SUBLANE)
    nsteps = T // tr

    def body(x_ref, g_ref, dy_ref, res_ref, dx_ref, dxb_ref, dg_ref, acc_ref):
        i = pl.program_id(0)
        xf = x_ref[...]
        dyf = dy_ref[...].astype(F32)
        r = lax.rsqrt(jnp.mean(xf * xf, axis=-1, keepdims=True) + EPS)
        w = dyf * g_ref[...]
        s = jnp.mean(xf * w, axis=-1, keepdims=True)
        dx = res_ref[...] + (r * w - xf * (r * r * r * s))
        dx_ref[...] = dx
        dxb_ref[...] = dx.astype(BF16)
        part = jnp.sum((dyf * (xf * r)).reshape(tr // SUBLANE, SUBLANE, D), axis=0)

        @pl.when(i == 0)
        def _():
            acc_ref[...] = part

        @pl.when(i > 0)
        def _():
            acc_ref[...] += part

        @pl.when(i == nsteps - 1)
        def _():
            dg_ref[...] = jnp.sum(acc_ref[...], axis=0, keepdims=True)

    row = pl.BlockSpec((tr, D), lambda i: (i, 0))
    vec = pl.BlockSpec((1, D), lambda i: (0, 0))
    return pl.pallas_call(
        body, name=name, grid=(nsteps,), in_specs=[row, vec, row, row], out_specs=[row, row, vec],
        out_shape=[jax.ShapeDtypeStruct((T, D), F32), jax.ShapeDtypeStruct((T, D), BF16),
                   jax.ShapeDtypeStruct((1, D), F32)],
        scratch_shapes=[pltpu.VMEM((SUBLANE, D), F32)],
        compiler_params=_params(("arbitrary",), tr * D * 18 * 2),
    )(x, g, dy, res)


def _qkv_prep(proj, gains, width, name):
    T = proj.shape[0]
    tr = _tile(T, 256, SUBLANE)
    heads = width // HEAD_DIM

    def body(p_ref, g_ref, o_ref):
        j = pl.program_id(1)
        g = g_ref[...]
        for h in range(heads):
            xh = p_ref[:, h * HEAD_DIM:(h + 1) * HEAD_DIM]
            r = lax.rsqrt(jnp.mean(xh * xh, axis=-1, keepdims=True) + EPS)
            r = jnp.where(j < 2, r, 1.0)
            o_ref[:, h * HEAD_DIM:(h + 1) * HEAD_DIM] = ((xh * r) * g).astype(BF16)

    return pl.pallas_call(
        body, name=name, grid=(T // tr, 3),
        in_specs=[pl.BlockSpec((tr, width), lambda i, j: (i, j)),
                  pl.BlockSpec((None, 1, HEAD_DIM), lambda i, j: (j, 0, 0))],
        out_specs=pl.BlockSpec((tr, width), lambda i, j: (i, j)),
        out_shape=jax.ShapeDtypeStruct((T, 3 * width), BF16),
        compiler_params=_params(("parallel", "parallel"), tr * width * 6 * 2),
    )(proj, gains)


def _qkv_bwd(proj, gains, dq, dk, dv, dproj, name):
    T = proj.shape[0]
    width = dq.shape[1]
    tr = _tile(T, 256, SUBLANE)
    heads = width // HEAD_DIM
    nrow = T // tr

    def body(p_ref, g_ref, dq_ref, dk_ref, dv_ref, dp_in, o_ref, dg_ref, acc_ref):
        del dp_in
        i = pl.program_id(0)
        j = pl.program_id(1)
        g = g_ref[...]

        @pl.when(jnp.logical_and(i == 0, j == 0))
        def _():
            acc_ref[...] = jnp.zeros_like(acc_ref)

        part = jnp.zeros((SUBLANE, HEAD_DIM), F32)
        for h in range(heads):
            sl = slice(h * HEAD_DIM, (h + 1) * HEAD_DIM)
            xh = p_ref[:, sl]
            dyh = jnp.where(j == 0, dq_ref[:, sl], jnp.where(j == 1, dk_ref[:, sl], dv_ref[:, sl]))
            r = lax.rsqrt(jnp.mean(xh * xh, axis=-1, keepdims=True) + EPS)
            w = dyh * g
            s = jnp.mean(xh * w, axis=-1, keepdims=True)
            dx = r * w - xh * (r * r * r * s)
            o_ref[:, sl] = jnp.where(j < 2, dx, dyh).astype(BF16)
            part = part + jnp.sum((dyh * (xh * r)).reshape(tr // SUBLANE, SUBLANE, HEAD_DIM), axis=0)
        acc_ref[j] += part

        @pl.when(jnp.logical_and(i == nrow - 1, j == 2))
        def _():
            dg_ref[...] = jnp.sum(acc_ref[...], axis=1, keepdims=True)

    blk = pl.BlockSpec((tr, width), lambda i, j: (i, j))
    dblk = pl.BlockSpec((tr, width), lambda i, j: (i, 0))
    return pl.pallas_call(
        body, name=name, grid=(nrow, 3),
        in_specs=[blk, pl.BlockSpec((None, 1, HEAD_DIM), lambda i, j: (j, 0, 0)), dblk, dblk, dblk,
                  pl.BlockSpec(memory_space=pl.ANY)],
        out_specs=[blk, pl.BlockSpec((3, 1, HEAD_DIM), lambda i, j: (0, 0, 0))],
        out_shape=[jax.ShapeDtypeStruct(dproj.shape, BF16), jax.ShapeDtypeStruct((3, 1, HEAD_DIM), F32)],
        input_output_aliases={5: 0},
        scratch_shapes=[pltpu.VMEM((3, SUBLANE, HEAD_DIM), F32)],
        compiler_params=_params(("arbitrary", "arbitrary"), tr * width * 18 * 2),
    )(proj, gains, dq, dk, dv, dproj)


def _split2(x):
    hi = x.astype(BF16)
    lo = (x - hi.astype(F32)).astype(BF16)
    return hi, lo


LOG2E = 1.4426950408889634


def _sb_logits(q, kb):
    return lax.dot_general(q, kb, _DIMS["nt"], preferred_element_type=F32) * (HEAD_DIM ** -0.5 * LOG2E)


def _sb_scores(z2, mask):
    lk = -(jnp.maximum(z2, 0.0) + jnp.log2(1.0 + jnp.exp2(-jnp.abs(z2))))
    if mask is not None:
        lk = jnp.where(mask, lk, 0.0)
    return lk, lk + z2


def _sb_block_size(T):
    return _tile(T, 256, LANE)


SB_HEADS_PER_STEP = 2


def _sb_later(lk, upper, carry):
    hi, lo = _split2(lk)
    return (jnp.dot(hi, upper, preferred_element_type=F32) + jnp.dot(lo, upper, preferred_element_type=F32)) + carry


def _sb_fwd(qkv, width, name, side=None):
    T = qkv.shape[0]
    heads = width // HEAD_DIM
    bq = _sb_block_size(T)
    hps = min(2 * SB_HEADS_PER_STEP, heads)
    assert heads % hps == 0
    groups, W = heads // hps, hps * HEAD_DIM

    n_side = side.n if side is not None else 0

    def body(*refs):
        q_ref, k_ref, v_ref = refs[:3]
        o_ref, o32_ref = refs[3 + n_side:5 + n_side]
        if side is not None:
            side_refs = (refs[3:3 + n_side], refs[5 + n_side:5 + 2 * n_side], refs[5 + 2 * n_side:])
            first, last = _grid_edges((groups, T // bq))

            @pl.when(first)
            def _():
                side.start(*side_refs)

        i = pl.program_id(1)
        row = lax.broadcasted_iota(jnp.int32, (bq, bq), 0)
        col = lax.broadcasted_iota(jnp.int32, (bq, bq), 1)
        upper = (row > col).astype(BF16)
        causal = col < row

        def head_block(start, h, carry, acc, mask):
            cols = slice(h * HEAD_DIM, (h + 1) * HEAD_DIM)
            z2 = _sb_logits(q_ref[:, cols], k_ref[pl.ds(start, bq), cols])
            yield
            lk, lb = _sb_scores(z2, mask)
            later = _sb_later(lk, upper, carry)
            yield
            w = jnp.exp2(lb + later)
            if mask is not None:
                w = jnp.where(mask, w, 0.0)
            pv = jnp.dot(w.astype(BF16), v_ref[pl.ds(start, bq), cols], preferred_element_type=F32)
            yield
            return carry + jnp.sum(lk, axis=1, keepdims=True), acc + pv

        def block(j, state, mask):
            start = pl.multiple_of(j * bq, bq)
            return tuple(_lockstep(head_block(start, h, state[h][0], state[h][1], mask) for h in range(hps)))

        zero = (jnp.zeros((bq, 1), F32), jnp.zeros((bq, HEAD_DIM), F32))
        state = block(i, (zero,) * hps, causal)
        state = lax.fori_loop(0, i, lambda n, st: block(i - 1 - n, st, None), state)
        for h in range(hps):
            cols = slice(h * HEAD_DIM, (h + 1) * HEAD_DIM)
            o_ref[:, cols] = state[h][1].astype(BF16)
            o32_ref[:, cols] = state[h][1]

        if side is not None:
            @pl.when(last)
            def _():
                side.wait(*side_refs)

    oblk = pl.BlockSpec((bq, W), lambda g, i: (i, g))
    in_specs = [oblk, pl.BlockSpec((T, W), lambda g, i: (0, groups + g)),
                pl.BlockSpec((T, W), lambda g, i: (0, 2 * groups + g))]
    out_shape = [jax.ShapeDtypeStruct((T, width), BF16), jax.ShapeDtypeStruct((T, width), F32)]
    est = 2 * T * W * 2 * 2 + hps * 16 * bq * bq * 4
    if side is None:
        return pl.pallas_call(
            body, name=name, grid=(groups, T // bq), in_specs=in_specs, out_specs=[oblk, oblk], out_shape=out_shape,
            compiler_params=_params(("parallel", "arbitrary"), est),
        )(qkv, qkv, qkv)
    outs = pl.pallas_call(
        body, name=name, grid=(groups, T // bq), in_specs=in_specs + side.in_specs(),
        out_specs=[oblk, oblk] + side.out_specs(), out_shape=out_shape + side.out_shape(),
        scratch_shapes=side.scratch(), compiler_params=_params(("arbitrary", "arbitrary"), est),
    )(qkv, qkv, qkv, *side.arrays)
    return outs[0], outs[1], outs[2:]


def _sb_bwd(qkv, y, dy, width, name):
    T = qkv.shape[0]
    heads = width // HEAD_DIM
    bq = _sb_block_size(T)
    nq = T // bq
    scale = HEAD_DIM ** -0.5
    hps = min(2 * SB_HEADS_PER_STEP, heads)
    groups, W = heads // hps, hps * HEAD_DIM

    def body(q_ref, kv_hbm, y_ref, dy_ref, dq_ref, dk_hbm, dv_hbm, dk_acc, dv_acc, kbuf, vbuf, kv_sems, out_sems):
        g_id = pl.program_id(0)
        i = pl.program_id(1)
        kcol = pl.multiple_of((groups + g_id) * W, W)
        vcol = pl.multiple_of((2 * groups + g_id) * W, W)

        def fetch(j, slot):
            rows = pl.ds(pl.multiple_of(j * bq, bq), bq)
            return (pltpu.make_async_copy(kv_hbm.at[rows, pl.ds(kcol, W)], kbuf.at[slot], kv_sems.at[0, slot]),
                    pltpu.make_async_copy(kv_hbm.at[rows, pl.ds(vcol, W)], vbuf.at[slot], kv_sems.at[1, slot]))

        for cp in fetch(i, 0):
            cp.start()
        row = lax.broadcasted_iota(jnp.int32, (bq, bq), 0)
        col = lax.broadcasted_iota(jnp.int32, (bq, bq), 1)
        upper = (row > col).astype(BF16)
        upper_incl = (row >= col).astype(BF16)
        causal = col < row
        deltas = []
        for h in range(hps):
            cols = slice(h * HEAD_DIM, (h + 1) * HEAD_DIM)
            deltas.append(jnp.sum(dy_ref[:, cols].astype(F32) * y_ref[:, cols], axis=1, keepdims=True))

        @pl.when(i == 0)
        def _():
            dk_acc[...] = jnp.zeros_like(dk_acc)
            dv_acc[...] = jnp.zeros_like(dv_acc)

        def head_block(start, slot, h, carry, gcarry, dq, mask):
            cols = slice(h * HEAD_DIM, (h + 1) * HEAD_DIM)
            q = q_ref[:, cols]
            do = dy_ref[:, cols]
            kb = kbuf[slot, :, cols]
            vb = vbuf[slot, :, cols]
            z2 = _sb_logits(q, kb)
            dw = lax.dot_general(do, vb, _DIMS["nt"], preferred_element_type=F32)
            yield
            lk, lb = _sb_scores(z2, mask)
            later = _sb_later(lk, upper, carry)
            yield
            w = jnp.exp2(lb + later)
            if mask is not None:
                w = jnp.where(mask, w, 0.0)
            wb = w.astype(BF16)
            gw = dw * wb.astype(F32)
            gsuf = _sb_later(gw, upper_incl, gcarry)
            dvp = lax.dot_general(wb, do, _DIMS["tn"], preferred_element_type=F32)
            yield
            dz = gw - jnp.exp2(lb) * (gw + (deltas[h] - gsuf))
            if mask is not None:
                dz = jnp.where(mask, dz, 0.0)
            dzs = (dz * scale).astype(BF16)
            dqp = jnp.dot(dzs, kb, preferred_element_type=F32)
            dkp = lax.dot_general(dzs, q, _DIMS["tn"], preferred_element_type=F32)
            yield
            dk_acc[pl.ds(start, bq), cols] += dkp
            dv_acc[pl.ds(start, bq), cols] += dvp
            return (carry + jnp.sum(lk, axis=1, keepdims=True), gcarry + jnp.sum(gw, axis=1, keepdims=True), dq + dqp)

        def block(n, state, mask):
            j = i - n
            slot = n % 2
            for cp in fetch(j, slot):
                cp.wait()

            @pl.when(n < i)
            def _():
                for cp in fetch(j - 1, 1 - slot):
                    cp.start()

            start = pl.multiple_of(j * bq, bq)
            return tuple(_lockstep(head_block(start, slot, h, *state[h], mask) for h in range(hps)))

        zero = jnp.zeros((bq, 1), F32)
        state = block(0, ((zero, zero, jnp.zeros((bq, HEAD_DIM), F32)),) * hps, causal)
        state = lax.fori_loop(1, i + 1, lambda n, st: block(n, st, None), state)
        for h in range(hps):
            dq_ref[:, h * HEAD_DIM:(h + 1) * HEAD_DIM] = state[h][2]

        @pl.when(i == nq - 1)
        def _():
            cols = pl.ds(pl.multiple_of(g_id * W, W), W)
            copies = [pltpu.make_async_copy(dk_acc, dk_hbm.at[:, cols], out_sems.at[0]),
                      pltpu.make_async_copy(dv_acc, dv_hbm.at[:, cols], out_sems.at[1])]
            for cp in copies:
                cp.start()
            for cp in copies:
                cp.wait()

    qblk = pl.BlockSpec((bq, W), lambda g, i: (i, g))
    out = jax.ShapeDtypeStruct((T, width), F32)
    return pl.pallas_call(
        body, name=name, grid=(groups, nq),
        in_specs=[qblk, pl.BlockSpec(memory_space=pl.ANY), qblk, qblk],
        out_specs=[qblk, pl.BlockSpec(memory_space=pl.ANY), pl.BlockSpec(memory_space=pl.ANY)],
        out_shape=[out, out, out],
        scratch_shapes=[pltpu.VMEM((T, W), F32), pltpu.VMEM((T, W), F32), pltpu.VMEM((2, bq, W), BF16),
                        pltpu.VMEM((2, bq, W), BF16), pltpu.SemaphoreType.DMA((2, 2)), pltpu.SemaphoreType.DMA((2,))],
        compiler_params=_params(("arbitrary", "arbitrary"), 2 * T * W * 4),
    )(qkv, qkv, y, dy)


def _hg_constants():
    C = HG_CHUNK
    t = np.arange(C)[:, None]
    j = np.arange(C)[None, :]
    blocks = [(j <= t), (j > t)]
    masks = []
    for n in HG_LEVELS:
        right = (t % (2 * n)) >= n
        mid = (t // (2 * n)) * (2 * n) + n - 1
        blocks.append(right & (j > mid) & (j <= t))
        blocks.append((~right) & (j > t) & (j <= mid))
        tt, ss = np.arange(C)[:, None], np.arange(C)[None, :]
        same = (tt // (2 * n)) == (ss // (2 * n))
        masks.append(same & ((tt % (2 * n)) >= n) & ((ss % (2 * n)) < n))
    sums = np.concatenate(blocks, axis=0).astype(np.float32)
    return jnp.asarray(sums, BF16), jnp.asarray(np.stack(masks).astype(np.float32))


def _split3(x):
    hi = x.astype(BF16)
    r1 = x - hi.astype(F32)
    mid = r1.astype(BF16)
    lo = (r1 - mid.astype(F32)).astype(BF16)
    return hi, mid, lo


def _hg_gates(hq, hf, lb):
    sq = _sigmoid(hq)
    sf = _sigmoid(hf)
    f = lb + (1.0 - lb) * sf
    return hq * sq, sq, f, sf, 1.0 - f, jnp.log(f)


def _hg_exponents(sums, g):
    hi, mid, lo = _split3(g)
    return (jnp.dot(sums, hi, preferred_element_type=F32) + jnp.dot(sums, mid, preferred_element_type=F32)
            + jnp.dot(sums, lo, preferred_element_type=F32))


def _hg_level_scores(q, k, e_all):
    C = HG_CHUNK
    parts, prods = [], []
    for li in range(len(HG_LEVELS)):
        eq = jnp.exp(e_all[(2 + 2 * li) * C:(3 + 2 * li) * C])
        ek = jnp.exp(e_all[(3 + 2 * li) * C:(4 + 2 * li) * C])
        qt = q * eq
        kt = k * ek
        prods.append(lax.dot_general(qt.astype(BF16), kt.astype(BF16), _DIMS["nt"], preferred_element_type=F32))
        parts.append((eq, ek, qt, kt))
    return parts, prods


def _hg_intra(prods, masks_ref):
    a = masks_ref[0] * prods[0]
    for li in range(1, len(HG_LEVELS)):
        a = a + masks_ref[li] * prods[li]
    return a


def _lockstep(gens):
    gens = list(gens)
    results = [None] * len(gens)
    alive = list(range(len(gens)))
    while alive:
        for idx in list(alive):
            try:
                next(gens[idx])
            except StopIteration as done:
                results[idx] = done.value
                alive.remove(idx)
    return results


def _lower_bound(lbl_ref, cols):
    return _sigmoid(lbl_ref[0:1, cols] - lbl_ref[1:2, cols])


HG_HEADS_PER_STEP = 4


def _hg_fwd(proj, lb_logits, g_out, col0, width, name):
    T = proj.shape[0]
    heads = width // HEAD_DIM
    C = HG_CHUNK
    tb = _tile(T, 512, C)
    cpb = tb // C
    nb = T // tb
    sums, masks = _hg_constants()
    hps = min(HG_HEADS_PER_STEP, heads)
    groups, W = heads // hps, hps * HEAD_DIM
    assert col0 % W == 0 and width % W == 0
    cb = col0 // W

    def body(hq_ref, hf_ref, hi_ref, og_ref, lbl_ref, go_ref, sums_ref, masks_ref, y_ref, o_ref, st_ref, s_ref):
        i = pl.program_id(1)

        @pl.when(i == 0)
        def _():
            s_ref[...] = jnp.zeros_like(s_ref)

        go = go_ref[...]

        def chunk_head(c, rows, h):
            cols = slice(h * HEAD_DIM, (h + 1) * HEAD_DIM)
            lb = _lower_bound(lbl_ref, cols)
            q, _, _, _, k, g = _hg_gates(hq_ref[rows, cols], hf_ref[rows, cols], lb)
            v = hi_ref[rows, cols]
            vb = v.astype(BF16)
            st = s_ref[h]
            st_ref[h, c] = st
            e_all = _hg_exponents(sums_ref[...], g)
            yield
            b = e_all[0:C]
            ebl = jnp.exp(b[C - 1:C, :])
            qe = q * jnp.exp(b)
            o_inter = lax.dot_general(qe.astype(BF16), st.astype(BF16), _DIMS["nt"], preferred_element_type=F32)
            kd = k * jnp.exp(e_all[C:2 * C])
            s_new = lax.dot_general(vb, kd.astype(BF16), _DIMS["tn"], preferred_element_type=F32)
            _, prods = _hg_level_scores(q, k, e_all)
            yield
            a = _hg_intra(prods, masks_ref)
            o_intra = jnp.dot(a.astype(BF16), vb, preferred_element_type=F32)
            s_ref[h] = st * ebl + s_new
            yield
            o = (o_inter + o_intra) + jnp.sum(q * k, axis=1, keepdims=True) * v
            o_ref[rows, cols] = o
            r = lax.rsqrt(jnp.mean(o * o, axis=-1, keepdims=True) + EPS)
            og = og_ref[rows, cols]
            y_ref[rows, cols] = (((o * r) * go) * (og * _sigmoid(og))).astype(BF16)

        def chunk(c, _):
            rows = pl.ds(pl.multiple_of(c * C, C), C)
            _lockstep(chunk_head(c, rows, h) for h in range(hps))
            return 0

        lax.fori_loop(0, cpb, chunk, 0)

    def col(k):
        return pl.BlockSpec((tb, W), lambda g, i: (i, cb + k * groups + g))

    nsum = sums.shape[0]
    out_blk = pl.BlockSpec((tb, W), lambda g, i: (i, g))
    return pl.pallas_call(
        body, name=name, grid=(groups, nb),
        in_specs=[col(0), col(1), col(2), col(3),
                  pl.BlockSpec((2, W), lambda g, i: (0, g)),
                  pl.BlockSpec((1, HEAD_DIM), lambda g, i: (0, 0)),
                  pl.BlockSpec((nsum, C), lambda g, i: (0, 0)),
                  pl.BlockSpec((len(HG_LEVELS), C, C), lambda g, i: (0, 0, 0))],
        out_specs=[out_blk, out_blk,
                   pl.BlockSpec((hps, cpb, HEAD_DIM, HEAD_DIM), lambda g, i: (g, i, 0, 0))],
        out_shape=[jax.ShapeDtypeStruct((T, width), BF16), jax.ShapeDtypeStruct((T, width), F32),
                   jax.ShapeDtypeStruct((heads, T // C, HEAD_DIM, HEAD_DIM), F32)],
        scratch_shapes=[pltpu.VMEM((hps, HEAD_DIM, HEAD_DIM), F32)],
        compiler_params=_params(("parallel", "arbitrary"), tb * W * 4 * 7 + hps * cpb * HEAD_DIM * HEAD_DIM * 4),
    )(proj, proj, proj, proj, lb_logits, g_out, sums, masks)


def _hg_bwd(proj, lb_logits, g_out, o_saved, states, dy, dproj, col0, width, name):
    T = proj.shape[0]
    heads = width // HEAD_DIM
    C = HG_CHUNK
    tb = _tile(T, 512, C)
    cpb = tb // C
    nb = T // tb
    sums, masks = _hg_constants()
    nsum = sums.shape[0]
    nlev = len(HG_LEVELS)
    hps = min(HG_HEADS_PER_STEP, heads)
    groups, W = heads // hps, hps * HEAD_DIM
    cb = col0 // W

    def body(hq_ref, hf_ref, hi_ref, og_ref, lbl_ref, go_ref, sums_ref, masks_ref, o_ref, st_ref, dy_ref, dp_in,
             dp_ref, dlb_ref, dgo_ref, ds_ref, de_ref, dlb_acc, dgo_acc, dout_ref):
        del dp_in
        grp = pl.program_id(0)
        i = pl.program_id(1)
        kk = pl.program_id(2)

        @pl.when(jnp.logical_and(i == 0, kk == 0))
        def _():
            ds_ref[...] = jnp.zeros_like(ds_ref)
            dlb_acc[...] = jnp.zeros_like(dlb_acc)

        @pl.when(jnp.logical_and(jnp.logical_and(grp == 0, i == 0), kk == 0))
        def _():
            dgo_acc[...] = jnp.zeros_like(dgo_acc)

        go = go_ref[...]
        last_row = lax.broadcasted_iota(jnp.int32, (C, HEAD_DIM), 0) == C - 1

        def chunk(n, _):
            c = cpb - 1 - n
            rows = pl.ds(pl.multiple_of(c * C, C), C)
            _lockstep(chunk_head(c, rows, h) for h in range(hps))
            return 0

        def chunk_head(c, rows, h):
            cols = slice(h * HEAD_DIM, (h + 1) * HEAD_DIM)
            de_h = de_ref.at[h]
            lb = _lower_bound(lbl_ref, cols)
            hq = hq_ref[rows, cols]
            og = og_ref[rows, cols]
            q, sq, f, sf, k, g = _hg_gates(hq, hf_ref[rows, cols], lb)
            v = hi_ref[rows, cols]
            vb = v.astype(BF16)
            st = st_ref[h, c]
            stb = st.astype(BF16)
            dst = ds_ref[h]
            dstb = dst.astype(BF16)
            o = o_ref[rows, cols]
            dyc = dy_ref[rows, cols].astype(F32)
            sg = _sigmoid(og)
            r = lax.rsqrt(jnp.mean(o * o, axis=-1, keepdims=True) + EPS)
            on = (o * r) * go
            don = dyc * (og * sg)
            dout_ref[3, rows, cols] = (dyc * on * (sg * (1.0 + og * (1.0 - sg)))).astype(BF16)
            dgo_acc[...] += jnp.sum((don * (o * r)).reshape(C // SUBLANE, SUBLANE, HEAD_DIM), axis=0)
            wn = don * go
            do = r * wn - o * (r * r * r * jnp.mean(o * wn, axis=-1, keepdims=True))
            dob = do.astype(BF16)
            e_all = _hg_exponents(sums_ref[...], g)
            da = lax.dot_general(dob, vb, _DIMS["nt"], preferred_element_type=F32)
            dqe = jnp.dot(dob, stb, preferred_element_type=F32)
            dkd = jnp.dot(vb, dstb, preferred_element_type=F32)
            yield
            eb = jnp.exp(e_all[0:C])
            esuf = jnp.exp(e_all[C:2 * C])
            ebl = eb[C - 1:C, :]
            qe = q * eb
            kd = k * esuf
            parts, prods = _hg_level_scores(q, k, e_all)
            dv_state = lax.dot_general(kd.astype(BF16), dstb, _DIMS["nt"], preferred_element_type=F32)
            ds_new = lax.dot_general(dob, qe.astype(BF16), _DIMS["tn"], preferred_element_type=F32)
            yield
            a = _hg_intra(prods, masks_ref)
            dv = lax.dot_general(a.astype(BF16), dob, _DIMS["tn"], preferred_element_type=F32)
            dlev = []
            for li in range(nlev):
                _, _, qt, kt = parts[li]
                dan = (masks_ref[li] * da).astype(BF16)
                dlev.append((jnp.dot(dan, kt.astype(BF16), preferred_element_type=F32),
                             lax.dot_general(dan, qt.astype(BF16), _DIMS["tn"], preferred_element_type=F32)))
            yield
            qk = jnp.sum(q * k, axis=1, keepdims=True)
            dv = dv + qk * do + dv_state
            dqk = jnp.sum(do * v, axis=1, keepdims=True)
            dq = dqk * k
            dk = dqk * q
            for li in range(nlev):
                eq, ek, qt, kt = parts[li]
                dqt, dkt = dlev[li]
                dq = dq + dqt * eq
                dk = dk + dkt * ek
                de_h[(2 + 2 * li) * C:(3 + 2 * li) * C, :] = dqt * qt
                de_h[(3 + 2 * li) * C:(4 + 2 * li) * C, :] = dkt * kt
            dq = dq + dqe * eb
            dk = dk + dkd * esuf
            debl = jnp.sum(dst * st, axis=0, keepdims=True)
            de_h[0:C, :] = dqe * qe + jnp.where(last_row, debl * ebl, 0.0)
            de_h[C:2 * C, :] = dkd * kd
            ds_ref[h] = dst * ebl + ds_new
            dehi, delo = _split2(de_h[...])
            dg = (lax.dot_general(sums_ref[...], dehi, _DIMS["tn"], preferred_element_type=F32)
                  + lax.dot_general(sums_ref[...], delo, _DIMS["tn"], preferred_element_type=F32))
            yield
            df = dg / f - dk
            dout_ref[0, rows, cols] = (dq * (sq * (1.0 + hq * (1.0 - sq)))).astype(BF16)
            dout_ref[1, rows, cols] = (df * (1.0 - lb) * (sf * (1.0 - sf))).astype(BF16)
            dout_ref[2, rows, cols] = dv.astype(BF16)
            dlb_acc[:, cols] += jnp.sum((df * (1.0 - sf)).reshape(C // SUBLANE, SUBLANE, HEAD_DIM), axis=0)

        @pl.when(kk == 0)
        def _():
            lax.fori_loop(0, cpb, chunk, 0)

        dp_ref[...] = dout_ref[kk]

        @pl.when(jnp.logical_and(i == nb - 1, kk == 3))
        def _():
            lb = _lower_bound(lbl_ref, slice(None))
            dl0 = jnp.sum(dlb_acc[...], axis=0, keepdims=True) * (lb * (1.0 - lb))
            dlb_ref[0:1, :] = dl0
            dlb_ref[1:2, :] = -dl0

        @pl.when(jnp.logical_and(jnp.logical_and(grp == groups - 1, i == nb - 1), kk == 3))
        def _():
            dgo_ref[...] = jnp.sum(dgo_acc[...], axis=0, keepdims=True)

    def col(k):
        return pl.BlockSpec((tb, W), lambda g, i, kk: (nb - 1 - i, cb + k * groups + g))

    rev = pl.BlockSpec((tb, W), lambda g, i, kk: (nb - 1 - i, g))
    return pl.pallas_call(
        body, name=name, grid=(groups, nb, 4),
        in_specs=[col(0), col(1), col(2), col(3),
                  pl.BlockSpec((2, W), lambda g, i, kk: (0, g)),
                  pl.BlockSpec((1, HEAD_DIM), lambda g, i, kk: (0, 0)),
                  pl.BlockSpec((nsum, C), lambda g, i, kk: (0, 0)),
                  pl.BlockSpec((nlev, C, C), lambda g, i, kk: (0, 0, 0)),
                  rev,
                  pl.BlockSpec((hps, cpb, HEAD_DIM, HEAD_DIM), lambda g, i, kk: (g, nb - 1 - i, 0, 0)),
                  rev,
                  pl.BlockSpec(memory_space=pl.ANY)],
        out_specs=[pl.BlockSpec((tb, W), lambda g, i, kk: (nb - 1 - i, cb + kk * groups + g)),
                   pl.BlockSpec((2, W), lambda g, i, kk: (0, g)),
                   pl.BlockSpec((1, HEAD_DIM), lambda g, i, kk: (0, 0))],
        out_shape=[jax.ShapeDtypeStruct(dproj.shape, BF16), jax.ShapeDtypeStruct((2, width), F32),
                   jax.ShapeDtypeStruct((1, HEAD_DIM), F32)],
        input_output_aliases={11: 0},
        scratch_shapes=[pltpu.VMEM((hps, HEAD_DIM, HEAD_DIM), F32), pltpu.VMEM((hps, nsum, HEAD_DIM), F32),
                        pltpu.VMEM((SUBLANE, W), F32), pltpu.VMEM((SUBLANE, HEAD_DIM), F32),
                        pltpu.VMEM((4, tb, W), BF16)],
        compiler_params=_params(("arbitrary", "arbitrary", "arbitrary"),
                                tb * W * 4 * 12 + hps * cpb * HEAD_DIM * HEAD_DIM * 4),
    )(proj, proj, proj, proj, lb_logits, g_out, sums, masks, o_saved, states, dy, dproj)


def _merge_fwd(proj, ma, mb, gate_col0, name):
    T, D = ma.shape
    tr = _tile(T, 256, SUBLANE)
    cw = _tile(D, 1024, LANE)
    nj = D // cw
    assert gate_col0 % cw == 0
    g0 = gate_col0 // cw

    def body(ga_ref, gb_ref, ma_ref, mb_ref, o_ref):
        o_ref[...] = (_sigmoid(ga_ref[...]) * ma_ref[...] + _sigmoid(gb_ref[...]) * mb_ref[...]).astype(BF16)

    blk = pl.BlockSpec((tr, cw), lambda i, j: (i, j))
    return pl.pallas_call(
        body, name=name, grid=(T // tr, nj),
        in_specs=[pl.BlockSpec((tr, cw), lambda i, j: (i, g0 + j)),
                  pl.BlockSpec((tr, cw), lambda i, j: (i, g0 + nj + j)), blk, blk],
        out_specs=blk, out_shape=jax.ShapeDtypeStruct((T, D), BF16),
        compiler_params=_params(("parallel", "parallel"), tr * cw * 18 * 2),
    )(proj, proj, ma, mb)


def _merge_bwd(proj, ma, mb, dm, gate_col0, name):
    T, D = ma.shape
    tr = _tile(T, 256, SUBLANE)
    cw = _tile(D, 1024, LANE)
    nj = D // cw
    g0 = gate_col0 // cw

    def body(g_ref, ma_ref, mb_ref, dm_ref, dp_ref, dmm_ref):
        j = pl.program_id(1)
        s = _sigmoid(g_ref[...])
        dmv = dm_ref[...]
        mm = jnp.where(j < nj, ma_ref[...], mb_ref[...])
        dp_ref[...] = (dmv * mm * (s * (1.0 - s))).astype(BF16)
        dmm_ref[...] = (dmv * s).astype(BF16)

    blk = pl.BlockSpec((tr, cw), lambda i, j: (i, j % nj))
    return pl.pallas_call(
        body, name=name, grid=(T // tr, 2 * nj),
        in_specs=[pl.BlockSpec((tr, cw), lambda i, j: (i, g0 + j)), blk, blk, blk],
        out_specs=[pl.BlockSpec((tr, cw), lambda i, j: (i, g0 + j)),
                   pl.BlockSpec((None, tr, cw), lambda i, j: (j // nj, i, j % nj))],
        out_shape=[jax.ShapeDtypeStruct(proj.shape, BF16), jax.ShapeDtypeStruct((2, T, D), BF16)],
        compiler_params=_params(("parallel", "parallel"), tr * cw * 20 * 2),
    )(proj, ma, mb, dm)


def _causal_conv(ext, w, b):
    s1 = pltpu.roll(ext, 1, 0)
    s2 = pltpu.roll(ext, 2, 0)
    out = b + w[0:1, :] * s2
    out = out + w[1:2, :] * s1
    out = out + w[2:3, :] * ext
    return out, s1, s2


def _conv_fwd(up, convw, convb, name):
    T, F2 = up.shape
    tc = convw.shape[2]
    half = (F2 // 2) // tc
    assert half * tc * 2 == F2
    tr = _tile(T, 256, SUBLANE)
    hb = tr // SUBLANE

    def body(g_ref, gp_ref, v_ref, vp_ref, wg_ref, wv_ref, bg_ref, bv_ref, o_ref):
        first = pl.program_id(1) == 0

        def lanes(c, _):
            cols = pl.ds(pl.multiple_of(c * LANE, LANE), LANE)

            def conv(cur_ref, prev_ref, w_ref, b_ref):
                prev = jnp.where(first, 0.0, prev_ref[:, cols])
                ext = jnp.concatenate([prev, cur_ref[:, cols]], axis=0)
                return _causal_conv(ext, w_ref[:, cols], b_ref[:, cols])[0][SUBLANE:]

            gate = conv(g_ref, gp_ref, wg_ref, bg_ref)
            val = conv(v_ref, vp_ref, wv_ref, bv_ref)
            o_ref[:, cols] = ((gate * _sigmoid(gate)) * val).astype(BF16)
            return 0

        lax.fori_loop(0, tc // LANE, lanes, 0)

    def main(off):
        return pl.BlockSpec((tr, tc), lambda j, i: (i, off + j))

    def prev(off):
        return pl.BlockSpec((SUBLANE, tc), lambda j, i: (jnp.maximum(i * hb - 1, 0), off + j))

    def wspec(off):
        return pl.BlockSpec((None, 3, tc), lambda j, i: (off + j, 0, 0))

    def bspec(off):
        return pl.BlockSpec((1, tc), lambda j, i: (0, off + j))

    return pl.pallas_call(
        body, name=name, grid=(half, T // tr),
        in_specs=[main(0), prev(0), main(half), prev(half), wspec(0), wspec(half), bspec(0), bspec(half)],
        out_specs=pl.BlockSpec((tr, tc), lambda j, i: (i, j)),
        out_shape=jax.ShapeDtypeStruct((T, F2 // 2), BF16),
        compiler_params=_params(("parallel", "parallel"), tr * tc * 4 * 12),
    )(up, up, up, up, convw, convw, convb, convb)


def _conv_bwd(da, up, convw, convb, name):
    T, F2 = up.shape
    tc = convw.shape[2]
    half = (F2 // 2) // tc
    tr = _tile(T, 128, SUBLANE)
    hb = tr // SUBLANE
    nrow = T // tr
    n = tr + 2 * SUBLANE

    def body(g_ref, gp_ref, gn_ref, v_ref, vp_ref, vn_ref, da_ref, dan_ref, wg_ref, wv_ref, bg_ref, bv_ref,
             dup_ref, gw_ref, acc_ref):
        i = pl.program_id(1)
        first = i == 0
        last = i == nrow - 1

        @pl.when(first)
        def _():
            acc_ref[...] = jnp.zeros_like(acc_ref)

        rows = slice(SUBLANE, SUBLANE + tr)

        def fold(t):
            return jnp.sum(t.reshape(tr // SUBLANE, SUBLANE, LANE), axis=0)

        def lanes(c, _):
            cols = pl.ds(pl.multiple_of(c * LANE, LANE), LANE)

            def conv(cur_ref, prev_ref, next_ref, w_ref, b_ref):
                prev = jnp.where(first, 0.0, prev_ref[:, cols])
                ext = jnp.concatenate([prev, cur_ref[:, cols], next_ref[:, cols]], axis=0)
                return (ext,) + _causal_conv(ext, w_ref[:, cols], b_ref[:, cols])

            g_ext, gate, g_s1, g_s2 = conv(g_ref, gp_ref, gn_ref, wg_ref, bg_ref)
            v_ext, val, v_s1, v_s2 = conv(v_ref, vp_ref, vn_ref, wv_ref, bv_ref)
            da_ext = jnp.concatenate([jnp.zeros((SUBLANE, LANE), F32), da_ref[:, cols],
                                      jnp.where(last, 0.0, dan_ref[:, cols])], axis=0)
            sg = _sigmoid(gate)
            d_gate = da_ext * val * (sg * (1.0 + gate * (1.0 - sg)))
            d_val = da_ext * (gate * sg)
            halves = ((d_gate, wg_ref[:, cols], (g_s2, g_s1, g_ext)), (d_val, wv_ref[:, cols], (v_s2, v_s1, v_ext)))
            for hf, (dc, w, taps) in enumerate(halves):
                dup = w[2:3, :] * dc + w[1:2, :] * pltpu.roll(dc, n - 1, 0) + w[0:1, :] * pltpu.roll(dc, n - 2, 0)
                dup_ref[hf, :, cols] = dup[rows].astype(BF16)
                dc_m = dc[rows]
                acc_ref[hf, 0, :, cols] += fold(dc_m)
                for k in range(3):
                    acc_ref[hf, 1 + k, :, cols] += fold(dc_m * taps[k][rows])
            return 0

        lax.fori_loop(0, tc // LANE, lanes, 0)

        @pl.when(last)
        def _():
            gw_ref[...] = jnp.sum(acc_ref[...], axis=2)

    def main(off):
        return pl.BlockSpec((tr, tc), lambda j, i: (i, off + j))

    def prev(off):
        return pl.BlockSpec((SUBLANE, tc), lambda j, i: (jnp.maximum(i * hb - 1, 0), off + j))

    def nxt(off):
        return pl.BlockSpec((SUBLANE, tc), lambda j, i: (jnp.minimum((i + 1) * hb, T // SUBLANE - 1), off + j))

    def wspec(off):
        return pl.BlockSpec((None, 3, tc), lambda j, i: (off + j, 0, 0))

    def bspec(off):
        return pl.BlockSpec((1, tc), lambda j, i: (0, off + j))

    return pl.pallas_call(
        body, name=name, grid=(half, nrow),
        in_specs=[main(0), prev(0), nxt(0), main(half), prev(half), nxt(half), main(0), nxt(0),
                  wspec(0), wspec(half), bspec(0), bspec(half)],
        out_specs=[pl.BlockSpec((2, tr, tc), lambda j, i: (0, i, j)), pl.BlockSpec((2, 4, tc), lambda j, i: (0, 0, j))],
        out_shape=[jax.ShapeDtypeStruct((2, T, F2 // 2), BF16), jax.ShapeDtypeStruct((2, 4, F2 // 2), F32)],
        scratch_shapes=[pltpu.VMEM((2, 4, SUBLANE, tc), F32)],
        compiler_params=_params(("parallel", "arbitrary"), n * tc * 4 * 24),
    )(up, up, up, up, up, up, da, da, convw, convw, convb, convb)


def _loss_head(y, target, name):
    T, D = y.shape
    tr = _tile(T, 256, SUBLANE)
    nrow = T // tr

    def body(y_ref, t_ref, d_ref, db_ref, l_ref, acc_ref):
        i = pl.program_id(0)
        diff = y_ref[...] - t_ref[...]
        dy = diff / D
        d_ref[...] = dy
        db_ref[...] = dy.astype(BF16)
        part = jnp.sum((diff * diff).reshape(tr // SUBLANE, SUBLANE, D), axis=0)

        @pl.when(i == 0)
        def _():
            acc_ref[...] = part

        @pl.when(i > 0)
        def _():
            acc_ref[...] += part

        @pl.when(i == nrow - 1)
        def _():
            col = jnp.sum(acc_ref[...], axis=0, keepdims=True)
            l_ref[...] = jnp.broadcast_to(0.5 * (jnp.sum(col, axis=1, keepdims=True) / D), (1, LANE))

    row = pl.BlockSpec((tr, D), lambda i: (i, 0))
    return pl.pallas_call(
        body, name=name, grid=(nrow,), in_specs=[row, row],
        out_specs=[row, row, pl.BlockSpec((1, LANE), lambda i: (0, 0))],
        out_shape=[jax.ShapeDtypeStruct((T, D), F32), jax.ShapeDtypeStruct((T, D), BF16),
                   jax.ShapeDtypeStruct((1, LANE), F32)],
        scratch_shapes=[pltpu.VMEM((SUBLANE, D), F32)],
        compiler_params=_params(("arbitrary",), tr * D * 14 * 2),
    )(y, target)


def _adamw(w, parts, m, v, name):
    R, C = w.shape
    P = parts.shape[0]
    tr = _tile(R, 64, SUBLANE)
    tc = _tile(C, 2048, LANE)

    def body(w_ref, p_ref, m_ref, v_ref, g_ref, d_ref, nm_ref, nv_ref):
        g = p_ref[0].astype(F32)
        for s in range(1, P):
            g = g + p_ref[s].astype(F32)
        wv = w_ref[...]
        nm = ADAM_B1 * m_ref[...] + (1.0 - ADAM_B1) * g
        nv = ADAM_B2 * v_ref[...] + (1.0 - ADAM_B2) * (g * g)
        m_hat = nm / (1.0 - ADAM_B1 ** ADAM_STEP)
        v_hat = nv / (1.0 - ADAM_B2 ** ADAM_STEP)
        g_ref[...] = g
        d_ref[...] = -ADAM_LR * (m_hat / (jnp.sqrt(v_hat) + ADAM_EPS) + ADAM_WD * wv)
        nm_ref[...] = nm
        nv_ref[...] = nv

    blk = pl.BlockSpec((tr, tc), lambda i, j: (i, j))
    shp = jax.ShapeDtypeStruct((R, C), F32)
    return pl.pallas_call(
        body, name=name, grid=(R // tr, C // tc),
        in_specs=[blk, pl.BlockSpec((P, tr, tc), lambda i, j: (0, i, j)), blk, blk],
        out_specs=[blk] * 4, out_shape=[shp] * 4,
        compiler_params=_params(("parallel", "parallel"), tr * tc * 4 * (P + 8) * 2),
    )(w, parts, m, v)


def _place():
    x, y, c = (lax.axis_index(a) for a in MESH_AXES)
    return x, y, c, 4 * x + 2 * y + c


def _peers(x, y, c):
    out = []
    for d in range(1, N_DEV):
        px = x + (d >> 2 & 1) - 2 * x * (d >> 2 & 1)
        py = y + (d >> 1 & 1) - 2 * y * (d >> 1 & 1)
        pc = c + (d & 1) - 2 * c * (d & 1)
        out.append(((px, py, pc), 4 * px + 2 * py + pc))
    return out


class _Side:
    def __init__(self, arrays, scatter, side_by_side=()):
        self.arrays, self.scatter, self.n = list(arrays), scatter, len(arrays)
        self.side_by_side = set(side_by_side)
        assert not (scatter and self.side_by_side)

    def in_specs(self):
        return [pl.BlockSpec(memory_space=pltpu.HBM)] * self.n

    out_specs = in_specs

    def out_shape(self):
        def shape(t, a):
            if t in self.side_by_side:
                return (a.shape[0], N_DEV * a.shape[1])
            return a.shape if self.scatter else (N_DEV,) + a.shape
        return [jax.ShapeDtypeStruct(shape(t, a), a.dtype) for t, a in enumerate(self.arrays)]

    def _slot(self, t, out, idx):
        if t in self.side_by_side:
            w = self.arrays[t].shape[1]
            return out.at[:, pl.ds(pl.multiple_of(idx * w, w), w)]
        return out.at[idx]

    def scratch(self):
        return [pltpu.SemaphoreType.DMA((self.n, N_DEV - 1)), pltpu.SemaphoreType.DMA((self.n, N_DEV - 1)),
                pltpu.SemaphoreType.DMA((self.n,))]

    def _copies(self, ins, outs, sems):
        send_sems, recv_sems, local_sems = sems
        x, y, c, me = _place()
        peers = _peers(x, y, c)
        local, sends, recvs = [], [], []
        for t in range(self.n):
            src_me = ins[t].at[me] if self.scatter else ins[t]
            local.append(pltpu.make_async_copy(src_me, self._slot(t, outs[t], me), local_sems.at[t]))
            for d, (peer, pidx) in enumerate(peers):
                src = ins[t].at[pidx] if self.scatter else ins[t]
                sends.append(pltpu.make_async_remote_copy(
                    src_ref=src, dst_ref=self._slot(t, outs[t], me), send_sem=send_sems.at[t, d],
                    recv_sem=recv_sems.at[t, d], device_id=peer, device_id_type=MESH_ID))
                recvs.append(pltpu.make_async_remote_copy(
                    src_ref=src, dst_ref=self._slot(t, outs[t], pidx), send_sem=send_sems.at[t, d],
                    recv_sem=recv_sems.at[t, d], device_id=peer, device_id_type=MESH_ID))
        return local, sends, recvs

    def start(self, ins, outs, sems):
        local, sends, _ = self._copies(ins, outs, sems)
        for cp in local + sends:
            cp.start()

    def wait(self, ins, outs, sems):
        local, sends, recvs = self._copies(ins, outs, sems)
        for cp in recvs:
            cp.wait_recv()
        for cp in sends:
            cp.wait_send()
        for cp in local:
            cp.wait()


def _grid_edges(sizes):
    ids = [pl.program_id(a) for a in range(len(sizes))]
    first = functools.reduce(jnp.logical_and, [i == 0 for i in ids])
    last = functools.reduce(jnp.logical_and, [i == s - 1 for i, s in zip(ids, sizes)])
    return first, last


def _exchange(arrays, scatter, name):
    side = _Side(arrays, scatter)
    n = side.n

    def body(*refs):
        ins, outs, sems = refs[:n], refs[n:2 * n], refs[2 * n:]
        side.start(ins, outs, sems)
        side.wait(ins, outs, sems)

    return pl.pallas_call(
        body, name=name, in_specs=side.in_specs(), out_specs=side.out_specs(), out_shape=side.out_shape(),
        scratch_shapes=side.scratch(),
    )(*arrays)


def _gather_two_level(arrays, name):
    n = len(arrays)

    def body(*refs):
        ins, outs = refs[:n], refs[n:2 * n]
        send_sems, recv_sems, local_sems = refs[2 * n:]
        x, y, c, me = _place()
        sibling = (x, y, 1 - c)
        chips = [(1 - x, y), (x, 1 - y), (1 - x, 1 - y)]

        def slot(px, py, pc):
            return 4 * px + 2 * py + pc

        def copy(t, k, block, to, src=None):
            return pltpu.make_async_remote_copy(
                src_ref=outs[t].at[slot(*block)] if src is None else src, dst_ref=outs[t].at[slot(*block)],
                send_sem=send_sems.at[t, k], recv_sem=recv_sems.at[t, k], device_id=to, device_id_type=MESH_ID)

        mine = [pltpu.make_async_copy(ins[t], outs[t].at[me], local_sems.at[t]) for t in range(n)]
        first = [copy(t, 0, (x, y, c), sibling, src=ins[t]) for t in range(n)]
        first += [copy(t, 1 + j, (x, y, c), (*chip, c), src=ins[t]) for t in range(n) for j, chip in enumerate(chips)]
        for cp in mine + first:
            cp.start()
        passed = []
        for t in range(n):
            for j, chip in enumerate(chips):
                copy(t, 1 + j, (*chip, c), (x, y, c)).wait_recv()
                cp = copy(t, 4 + j, (*chip, c), sibling)
                cp.start()
                passed.append(cp)
        for t in range(n):
            copy(t, 0, sibling, (x, y, c)).wait_recv()
            for j, chip in enumerate(chips):
                copy(t, 4 + j, (*chip, 1 - c), (x, y, c)).wait_recv()
        for cp in first + passed:
            cp.wait_send()
        for cp in mine:
            cp.wait()

    hbm = pl.BlockSpec(memory_space=pltpu.HBM)
    return pl.pallas_call(
        body, name=name, in_specs=[hbm] * n, out_specs=[hbm] * n,
        out_shape=[jax.ShapeDtypeStruct((N_DEV,) + a.shape, a.dtype) for a in arrays],
        scratch_shapes=[pltpu.SemaphoreType.DMA((n, N_DEV - 1)), pltpu.SemaphoreType.DMA((n, N_DEV - 1)),
                        pltpu.SemaphoreType.DMA((n,))],
    )(*arrays)


def _all_reduce_small(vec, name):
    R = vec.shape[0]

    def body(v_ref, o_ref, gath_ref, send_sems, recv_sems):
        x, y, c, me = _place()
        sends, recvs = [], []
        for d, (peer, pidx) in enumerate(_peers(x, y, c)):
            cp = pltpu.make_async_remote_copy(
                src_ref=v_ref, dst_ref=gath_ref.at[me], send_sem=send_sems.at[d], recv_sem=recv_sems.at[d],
                device_id=peer, device_id_type=MESH_ID)
            cp.start()
            sends.append(cp)
            recvs.append(pltpu.make_async_remote_copy(
                src_ref=v_ref, dst_ref=gath_ref.at[pidx], send_sem=send_sems.at[d], recv_sem=recv_sems.at[d],
                device_id=peer, device_id_type=MESH_ID))
        gath_ref[me] = v_ref[...]
        for cp in recvs:
            cp.wait_recv()
        for cp in sends:
            cp.wait_send()
        total = gath_ref[0]
        for s in range(1, N_DEV):
            total = total + gath_ref[s]
        o_ref[...] = total

    vm = pl.BlockSpec(memory_space=pltpu.VMEM)
    return pl.pallas_call(
        body, name=name, in_specs=[vm], out_specs=vm, out_shape=jax.ShapeDtypeStruct(vec.shape, F32),
        scratch_shapes=[pltpu.VMEM((N_DEV, R, LANE), F32), pltpu.SemaphoreType.DMA((N_DEV - 1,)),
                        pltpu.SemaphoreType.DMA((N_DEV - 1,))],
        compiler_params=pltpu.CompilerParams(vmem_limit_bytes=_vmem(R * LANE * 4 * 12)),
    )(vec)


def _pack(parts):
    flat = jnp.concatenate([p.reshape(-1).astype(F32) for p in parts])
    rows = -(-flat.shape[0] // (LANE * SUBLANE)) * SUBLANE
    return jnp.pad(flat, (0, rows * LANE - flat.shape[0])).reshape(rows, LANE)


def _unpack(packed, shapes):
    flat = packed.reshape(-1)
    out, at = [], 0
    for s in shapes:
        size = int(np.prod(s))
        out.append(flat[at:at + size].reshape(s))
        at += size
    return out


def kernel(x, g_mix, w_in, g_q, g_k, lb_logits, g_hg_out, p_a, p_b, w_o, g_ffn, w_up, conv_w, conv_b, w_down, loss_target, m_g_mix, m_w_in, m_g_q, m_g_k, m_lb_logits, m_g_hg_out, m_p_a, m_p_b, m_w_o, m_g_ffn, m_w_up, m_conv_w, m_conv_b, m_w_down, v_g_mix, v_w_in, v_g_q, v_g_k, v_lb_logits, v_g_hg_out, v_p_a, v_p_b, v_w_o, v_g_ffn, v_w_up, v_conv_w, v_conv_b, v_w_down):
    assert x.shape[0] == 1 and lb_logits.shape[0] == 2
    xs, target = x[0], loss_target[0]
    T, D = xs.shape
    A = p_a.shape[1]
    HW = p_b.shape[1]
    hg_col0 = 3 * A
    gate_col0 = 3 * A + 4 * HW
    F = w_down.shape[1] * N_DEV
    assert w_in.shape[2] * N_DEV == gate_col0 + 2 * D and w_up.shape[2] * N_DEV == 2 * F

    (win_g,) = _gather_two_level([w_in[0].astype(BF16)], "gather_w_in")
    later_weights = _Side([w_up[0].astype(BF16), p_a[0].astype(BF16), p_b[0].astype(BF16), w_o[0].astype(BF16),
                           w_down[0].astype(BF16), conv_w[0]], False, side_by_side=(1, 2))

    u = _rmsnorm_fwd(xs, g_mix, "norm_mix")
    proj = _matmul(u, win_g, mode="nn", b_shards=True, name="proj_in")
    gains = jnp.stack([g_q[0], g_k[0], jnp.ones_like(g_q[0])])[:, None, :]
    qkv = _qkv_prep(proj, gains, A, "qkv_prep")
    ya, ya32, (wup_g, pa_g, pb_g, wo_g, wdown_g, convw_g) = _sb_fwd(qkv, A, "sb_fwd", side=later_weights)
    wo_full = wo_g.reshape(D, D)
    wdown_full = wdown_g.reshape(F, D)
    yb, ob, states = _hg_fwd(proj, lb_logits, g_hg_out, hg_col0, HW, "hg_fwd")
    ma = _matmul(ya, pa_g, mode="nn", name="proj_a")
    mb = _matmul(yb, pb_g, mode="nn", name="proj_b")
    m = _merge_fwd(proj, ma, mb, gate_col0, "merge_fwd")
    h1 = _matmul(m, wo_full, mode="nn", add=xs, name="proj_o")
    u2 = _rmsnorm_fwd(h1, g_ffn, "norm_ffn")
    up = _matmul(u2, wup_g, mode="nn", b_shards=True, name="ffn_up")
    act = _conv_fwd(up, convw_g, conv_b, "conv_fwd")
    y = _matmul(act, wdown_full, mode="nn", add=h1, tn_pref=2048, name="ffn_down")
    dy, dyb, loss_part = _loss_head(y, target, "loss_head")
    loss = lax.psum(loss_part[0, 0], MESH_AXES)

    dact = _matmul(dyb, wdown_full, mode="nt", name="d_act")
    g_wdown = _matmul(act, dyb, mode="tn", out_dtype=BF16, name="g_w_down")
    dup, g_conv = _conv_bwd(dact, up, convw_g, conv_b, "conv_bwd")
    g_wup, (r_wdown,) = _matmul(u2, dup, mode="tn", b_halves=True, out_shards=True, out_dtype=BF16, tm_pref=1024,
                                name="g_w_up",
                                side=_Side([g_wdown.reshape(N_DEV, F // N_DEV, D)], True))
    du2, (r_wup,) = _matmul(dup, wup_g, mode="nt", a_halves=True, b_shards=True, tn_pref=2048, name="d_u2",
                            side=_Side([g_wup], True))
    dh1, dh1b, g_gffn = _rmsnorm_bwd(h1, g_ffn, du2, dy, "norm_ffn_bwd")
    dm = _matmul(dh1b, wo_full, mode="nt", name="d_m")
    g_wo = _matmul(m, dh1b, mode="tn", out_dtype=BF16, tm_pref=1024, name="g_w_o")
    dproj, dmab = _merge_bwd(proj, ma, mb, dm, gate_col0, "merge_bwd")
    dya = _matmul(dmab, pa_g, mode="nt", lead_a=0, out_dtype=BF16, name="d_ya")
    g_pa = _matmul(ya, dmab, mode="tn", out_shards=True, out_dtype=BF16, lead_b=0, name="g_p_a")
    dyb_ = _matmul(dmab, pb_g, mode="nt", lead_a=1, out_dtype=BF16, name="d_yb")
    g_pb = _matmul(yb, dmab, mode="tn", out_shards=True, out_dtype=BF16, lead_b=1, name="g_p_b")
    dproj, g_lb, g_ghg = _hg_bwd(proj, lb_logits, g_hg_out, ob, states, dyb_, dproj, hg_col0, HW, "hg_bwd")
    dq, dk, dv = _sb_bwd(qkv, ya32, dya, A, "sb_bwd")
    dproj, g_gains = _qkv_bwd(proj, gains, dq, dk, dv, dproj, "qkv_bwd")
    g_win, (r_wo, r_pa, r_pb) = _matmul(
        u, dproj, mode="tn", out_shards=True, out_dtype=BF16, tm_pref=1024, name="g_w_in",
        side=_Side([g_wo.reshape(N_DEV, D // N_DEV, D), g_pa, g_pb], True))
    du, (r_win,) = _matmul(dproj, win_g, mode="nt", b_shards=True, tn_pref=2048, name="d_u",
                           side=_Side([g_win], True))
    gx, _, g_gmix = _rmsnorm_bwd(xs, g_mix, du, dh1, "norm_mix_bwd")

    g_convb = g_conv[:, 0].reshape(1, 2 * F)
    g_convw = g_conv[:, 1:4].transpose(1, 0, 2).reshape(3, 2 * F)
    small = [g_gmix, g_gains[0], g_gains[1], g_lb, g_ghg, g_gffn, g_convb, g_convw]
    small_shapes = [p.shape for p in small]
    red = _unpack(_all_reduce_small(_pack(small), "reduce_small"), small_shapes)

    _, _, _, me = _place()
    cs = conv_w.shape[2]
    big = {
        "w_in": _adamw(w_in[0], r_win, m_w_in[0], v_w_in[0], "adamw_w_in"),
        "w_up": _adamw(w_up[0], r_wup, m_w_up[0], v_w_up[0], "adamw_w_up"),
        "p_a": _adamw(p_a[0], r_pa, m_p_a[0], v_p_a[0], "adamw_p_a"),
        "p_b": _adamw(p_b[0], r_pb, m_p_b[0], v_p_b[0], "adamw_p_b"),
        "w_o": _adamw(w_o[0], r_wo, m_w_o[0], v_w_o[0], "adamw_w_o"),
        "w_down": _adamw(w_down[0], r_wdown, m_w_down[0], v_w_down[0], "adamw_w_down"),
        "conv_w": _adamw(conv_w[0], lax.dynamic_slice_in_dim(red[7], me * cs, cs, axis=1)[None],
                         m_conv_w[0], v_conv_w[0], "adamw_conv_w"),
    }
    rep_w = [g_mix, g_q, g_k, lb_logits, g_hg_out, g_ffn, conv_b]
    rep_m = [m_g_mix, m_g_q, m_g_k, m_lb_logits, m_g_hg_out, m_g_ffn, m_conv_b]
    rep_v = [v_g_mix, v_g_q, v_g_k, v_lb_logits, v_g_hg_out, v_g_ffn, v_conv_b]
    rep_shapes = [p.shape for p in rep_w]
    rep_out = _adamw(_pack(rep_w), _pack(red[:7])[None], _pack(rep_m), _pack(rep_v), "adamw_small")
    rep = [_unpack(o, rep_shapes) for o in rep_out]
    rep_names = ["g_mix", "g_q", "g_k", "lb_logits", "g_hg_out", "g_ffn", "conv_b"]

    order = ["g_mix", "w_in", "g_q", "g_k", "lb_logits", "g_hg_out", "p_a", "p_b", "w_o", "g_ffn", "w_up",
             "conv_w", "conv_b", "w_down"]

    def leaf(kind, pname):
        if pname in big:
            return big[pname][kind][None]
        return rep[kind][rep_names.index(pname)]

    return (loss, gx[None], *[leaf(kind, p) for kind in range(4) for p in order])
```

```python
import functools

import numpy as np
import jax
import jax.numpy as jnp
from jax import lax
from jax.experimental import pallas as pl
from jax.experimental.pallas import tpu as pltpu

F32 = jnp.float32
BF16 = jnp.bfloat16

N_DEV = 8
HEAD_DIM = 128
HG_CHUNK = 64
HG_LEVELS = (1, 2, 4, 8, 16, 32)
EPS = 1e-6
ADAM_LR = 0.001
ADAM_B1 = 0.9
ADAM_B2 = 0.999
ADAM_EPS = 1e-08
ADAM_WD = 0.01
ADAM_STEP = 10
LANE = 128
SUBLANE = 8
VMEM_CAP = 56 << 20
MESH_AXES = ("x", "y", "c")
MESH_ID = pl.DeviceIdType.MESH


def _tile(dim, pref, align):
    if dim <= pref:
        return dim
    t = (pref // align) * align
    while t >= align:
        if dim % t == 0:
            return t
        t -= align
    return dim


def _vmem(est_bytes):
    return int(min(max(2 * est_bytes + (8 << 20), 32 << 20), VMEM_CAP))


def _params(sem, est_bytes):
    return pltpu.CompilerParams(dimension_semantics=sem, vmem_limit_bytes=_vmem(est_bytes))


def _sigmoid(x):
    return 1.0 / (1.0 + jnp.exp(-x))


_DIMS = {"nn": (((1,), (0,)), ((), ())), "nt": (((1,), (1,)), ((), ())), "tn": (((0,), (0,)), ((), ()))}


def _matmul(a, b, *, mode, name, out_dtype=F32, add=None, b_shards=False, out_shards=False, lead_a=None,
            lead_b=None, a_halves=False, b_halves=False, side=None, tm_pref=512, tn_pref=1408, tk_pref=2048):
    if mode == "tn":
        K, M = a.shape[-2:]
    else:
        M, K = a.shape[-2:]
    if a_halves:
        assert mode == "nt" and b_shards and lead_a is None
        K = 2 * K
    if lead_b is not None or b_halves:
        assert not b_shards and mode == "tn"
    if b_shards:
        S = b.shape[0]
        if mode == "nn":
            assert b.shape[1] == K
            N, tn, tk = S * b.shape[2], b.shape[2], _tile(K, tk_pref, LANE)
        else:
            assert mode == "nt" and S * b.shape[2] == K
            N, tk, tn = b.shape[1], b.shape[2], _tile(b.shape[1], tn_pref, LANE)
    else:
        if mode == "nt":
            N = b.shape[0]
            assert b.shape[1] == K
        else:
            N = b.shape[-1] * (2 if b_halves else 1)
            assert b.shape[-2] == K
        tn = _tile(N, tn_pref, LANE)
        tk = _tile(K, tk_pref, LANE)
    if out_shards:
        assert N % N_DEV == 0
        tn = N // N_DEV
    per_half_n, per_half_k = (N // 2) // tn, (K // 2) // tk
    assert not b_halves or per_half_n * tn * 2 == N
    assert not a_halves or per_half_k * tk * 2 == K
    tm = _tile(M, tm_pref, LANE)
    nm, nn, nk = M // tm, N // tn, K // tk
    assert nm * tm == M and nn * tn == N and nk * tk == K

    if mode == "tn":
        a_spec = pl.BlockSpec((tk, tm), lambda j, i, k: (k, i))
    elif lead_a is not None:
        a_spec = pl.BlockSpec((None, tm, tk), lambda j, i, k: (lead_a, i, k))
    elif a_halves:
        a_spec = pl.BlockSpec((None, tm, tk), lambda j, i, k: (k // per_half_k, i, k % per_half_k))
    else:
        a_spec = pl.BlockSpec((tm, tk), lambda j, i, k: (i, k))
    if b_halves:
        b_spec = pl.BlockSpec((None, tk, tn), lambda j, i, k: (j // per_half_n, k, j % per_half_n))
    elif lead_b is not None:
        b_spec = pl.BlockSpec((None, tk, tn), lambda j, i, k: (lead_b, k, j))
    elif b_shards and mode == "nn":
        b_spec = pl.BlockSpec((None, tk, tn), lambda j, i, k: (j, k, 0))
    elif b_shards:
        b_spec = pl.BlockSpec((None, tn, tk), lambda j, i, k: (k, j, 0))
    elif mode == "nt":
        b_spec = pl.BlockSpec((tn, tk), lambda j, i, k: (j, k))
    else:
        b_spec = pl.BlockSpec((tk, tn), lambda j, i, k: (k, j))
    in_specs = [a_spec, b_spec]
    operands = [a, b]
    if add is not None:
        assert not out_shards and add.shape == (M, N)
        in_specs.append(pl.BlockSpec((tm, tn), lambda j, i, k: (i, j)))
        operands.append(add)
    if out_shards:
        out_shape = jax.ShapeDtypeStruct((N_DEV, M, tn), out_dtype)
        out_spec = pl.BlockSpec((None, tm, tn), lambda j, i, k: (j, i, 0))
    else:
        out_shape = jax.ShapeDtypeStruct((M, N), out_dtype)
        out_spec = pl.BlockSpec((tm, tn), lambda j, i, k: (i, j))
    dims = _DIMS[mode]
    has_add = add is not None
    n_in = 3 if has_add else 2
    n_side = side.n if side is not None else 0

    def body(*refs):
        a_ref, b_ref = refs[0], refs[1]
        add_ref = refs[2] if has_add else None
        o_ref = refs[n_in + n_side]
        acc_ref = refs[n_in + 2 * n_side + 1]
        k = pl.program_id(2)
        if side is not None:
            side_refs = (refs[n_in:n_in + n_side], refs[n_in + n_side + 1:n_in + 2 * n_side + 1],
                         refs[n_in + 2 * n_side + 2:])
            first, last = _grid_edges((nn, nm, nk))

            @pl.when(first)
            def _():
                side.start(*side_refs)

        part = lax.dot_general(a_ref[...], b_ref[...], dims, preferred_element_type=F32)

        def finish(total):
            if has_add:
                total = add_ref[...] + total
            o_ref[...] = total.astype(out_dtype)

        if nk == 1:
            finish(part)
        else:
            @pl.when(k == 0)
            def _():
                acc_ref[...] = part

            @pl.when(jnp.logical_and(k > 0, k < nk - 1))
            def _():
                acc_ref[...] += part

            @pl.when(k == nk - 1)
            def _():
                finish(acc_ref[...] + part)

        if side is not None:
            @pl.when(last)
            def _():
                side.wait(*side_refs)

    est = 2 * (tm * tk * 2 + tk * tn * 2 + tm * tn * 4 * (2 if has_add else 1)) + tm * tn * 4 * 2
    acc = pltpu.VMEM((tm, tn) if nk > 1 else (SUBLANE, LANE), F32)
    if side is None:
        return pl.pallas_call(
            body, name=name, grid=(nn, nm, nk), in_specs=in_specs, out_specs=out_spec, out_shape=out_shape,
            scratch_shapes=[acc], compiler_params=_params(("parallel", "parallel", "arbitrary"), est),
        )(*operands)
    outs = pl.pallas_call(
        body, name=name, grid=(nn, nm, nk), in_specs=in_specs + side.in_specs(),
        out_specs=[out_spec] + side.out_specs(), out_shape=[out_shape] + side.out_shape(),
        scratch_shapes=[acc] + side.scratch(), compiler_params=_params(("arbitrary",) * 3, est),
    )(*operands, *side.arrays)
    return outs[0], outs[1:]


def _rmsnorm_fwd(x, g, name):
    T, D = x.shape
    tr = _tile(T, 256, SUBLANE)

    def body(x_ref, g_ref, o_ref):
        xf = x_ref[...]
        r = lax.rsqrt(jnp.mean(xf * xf, axis=-1, keepdims=True) + EPS)
        o_ref[...] = ((xf * r) * g_ref[...]).astype(BF16)

    return pl.pallas_call(
        body, name=name, grid=(T // tr,),
        in_specs=[pl.BlockSpec((tr, D), lambda i: (i, 0)), pl.BlockSpec((1, D), lambda i: (0, 0))],
        out_specs=pl.BlockSpec((tr, D), lambda i: (i, 0)), out_shape=jax.ShapeDtypeStruct((T, D), BF16),
        compiler_params=_params(("parallel",), tr * D * 6 * 2),
    )(x, g)


def _rmsnorm_bwd(x, g, dy, res, name):
    T, D = x.shape
    tr = _tile(T, 256, SUBLANE)
    nsteps = T // tr

    def body(x_ref, g_ref, dy_ref, res_ref, dx_ref, dxb_ref, dg_ref, acc_ref):
        i = pl.program_id(0)
        xf = x_ref[...]
        dyf = dy_ref[...].astype(F32)
        r = lax.rsqrt(jnp.mean(xf * xf, axis=-1, keepdims=True) + EPS)
        w = dyf * g_ref[...]
        s = jnp.mean(xf * w, axis=-1, keepdims=True)
        dx = res_ref[...] + (r * w - xf * (r * r * r * s))
        dx_ref[...] = dx
        dxb_ref[...] = dx.astype(BF16)
        part = jnp.sum((dyf * (xf * r)).reshape(tr // SUBLANE, SUBLANE, D), axis=0)

        @pl.when(i == 0)
        def _():
            acc_ref[...] = part

        @pl.when(i > 0)
        def _():
            acc_ref[...] += part

        @pl.when(i == nsteps - 1)
        def _():
            dg_ref[...] = jnp.sum(acc_ref[...], axis=0, keepdims=True)

    row = pl.BlockSpec((tr, D), lambda i: (i, 0))
    vec = pl.BlockSpec((1, D), lambda i: (0, 0))
    return pl.pallas_call(
        body, name=name, grid=(nsteps,), in_specs=[row, vec, row, row], out_specs=[row, row, vec],
        out_shape=[jax.ShapeDtypeStruct((T, D), F32), jax.ShapeDtypeStruct((T, D), BF16),
                   jax.ShapeDtypeStruct((1, D), F32)],
        scratch_shapes=[pltpu.VMEM((SUBLANE, D), F32)],
        compiler_params=_params(("arbitrary",), tr * D * 18 * 2),
    )(x, g, dy, res)


def _qkv_prep(proj, gains, width, name):
    T = proj.shape[0]
    tr = _tile(T, 256, SUBLANE)
    heads = width // HEAD_DIM

    def body(p_ref, g_ref, o_ref):
        j = pl.program_id(1)
        g = g_ref[...]
        for h in range(heads):
            xh = p_ref[:, h * HEAD_DIM:(h + 1) * HEAD_DIM]
            r = lax.rsqrt(jnp.mean(xh * xh, axis=-1, keepdims=True) + EPS)
            r = jnp.where(j < 2, r, 1.0)
            o_ref[:, h * HEAD_DIM:(h + 1) * HEAD_DIM] = ((xh * r) * g).astype(BF16)

    return pl.pallas_call(
        body, name=name, grid=(T // tr, 3),
        in_specs=[pl.BlockSpec((tr, width), lambda i, j: (i, j)),
                  pl.BlockSpec((None, 1, HEAD_DIM), lambda i, j: (j, 0, 0))],
        out_specs=pl.BlockSpec((tr, width), lambda i, j: (i, j)),
        out_shape=jax.ShapeDtypeStruct((T, 3 * width), BF16),
        compiler_params=_params(("parallel", "parallel"), tr * width * 6 * 2),
    )(proj, gains)


def _qkv_bwd(proj, gains, dq, dk, dv, dproj, name):
    T = proj.shape[0]
    width = dq.shape[1]
    tr = _tile(T, 256, SUBLANE)
    heads = width // HEAD_DIM
    nrow = T // tr

    def body(p_ref, g_ref, dq_ref, dk_ref, dv_ref, dp_in, o_ref, dg_ref, acc_ref):
        del dp_in
        i = pl.program_id(0)
        j = pl.program_id(1)
        g = g_ref[...]

        @pl.when(jnp.logical_and(i == 0, j == 0))
        def _():
            acc_ref[...] = jnp.zeros_like(acc_ref)

        part = jnp.zeros((SUBLANE, HEAD_DIM), F32)
        for h in range(heads):
            sl = slice(h * HEAD_DIM, (h + 1) * HEAD_DIM)
            xh = p_ref[:, sl]
            dyh = jnp.where(j == 0, dq_ref[:, sl], jnp.where(j == 1, dk_ref[:, sl], dv_ref[:, sl]))
            r = lax.rsqrt(jnp.mean(xh * xh, axis=-1, keepdims=True) + EPS)
            w = dyh * g
            s = jnp.mean(xh * w, axis=-1, keepdims=True)
            dx = r * w - xh * (r * r * r * s)
            o_ref[:, sl] = jnp.where(j < 2, dx, dyh).astype(BF16)
            part = part + jnp.sum((dyh * (xh * r)).reshape(tr // SUBLANE, SUBLANE, HEAD_DIM), axis=0)
        acc_ref[j] += part

        @pl.when(jnp.logical_and(i == nrow - 1, j == 2))
        def _():
            dg_ref[...] = jnp.sum(acc_ref[...], axis=1, keepdims=True)

    blk = pl.BlockSpec((tr, width), lambda i, j: (i, j))
    dblk = pl.BlockSpec((tr, width), lambda i, j: (i, 0))
    return pl.pallas_call(
        body, name=name, grid=(nrow, 3),
        in_specs=[blk, pl.BlockSpec((None, 1, HEAD_DIM), lambda i, j: (j, 0, 0)), dblk, dblk, dblk,
                  pl.BlockSpec(memory_space=pl.ANY)],
        out_specs=[blk, pl.BlockSpec((3, 1, HEAD_DIM), lambda i, j: (0, 0, 0))],
        out_shape=[jax.ShapeDtypeStruct(dproj.shape, BF16), jax.ShapeDtypeStruct((3, 1, HEAD_DIM), F32)],
        input_output_aliases={5: 0},
        scratch_shapes=[pltpu.VMEM((3, SUBLANE, HEAD_DIM), F32)],
        compiler_params=_params(("arbitrary", "arbitrary"), tr * width * 18 * 2),
    )(proj, gains, dq, dk, dv, dproj)


def _split2(x):
    hi = x.astype(BF16)
    lo = (x - hi.astype(F32)).astype(BF16)
    return hi, lo


LOG2E = 1.4426950408889634


def _sb_logits(q, kb):
    return lax.dot_general(q, kb, _DIMS["nt"], preferred_element_type=F32) * (HEAD_DIM ** -0.5 * LOG2E)


def _sb_scores(z2, mask):
    lk = -(jnp.maximum(z2, 0.0) + jnp.log2(1.0 + jnp.exp2(-jnp.abs(z2))))
    if mask is not None:
        lk = jnp.where(mask, lk, 0.0)
    return lk, lk + z2


def _sb_block_size(T):
    return _tile(T, 256, LANE)


SB_HEADS_PER_STEP = 2


def _sb_later(lk, upper, carry):
    hi, lo = _split2(lk)
    return (jnp.dot(hi, upper, preferred_element_type=F32) + jnp.dot(lo, upper, preferred_element_type=F32)) + carry


def _sb_fwd(qkv, width, name, side=None):
    T = qkv.shape[0]
    heads = width // HEAD_DIM
    bq = _sb_block_size(T)
    hps = min(2 * SB_HEADS_PER_STEP, heads)
    assert heads % hps == 0
    groups, W = heads // hps, hps * HEAD_DIM

    n_side = side.n if side is not None else 0

    def body(*refs):
        q_ref, k_ref, v_ref = refs[:3]
        o_ref, o32_ref = refs[3 + n_side:5 + n_side]
        if side is not None:
            side_refs = (refs[3:3 + n_side], refs[5 + n_side:5 + 2 * n_side], refs[5 + 2 * n_side:])
            first, last = _grid_edges((groups, T // bq))

            @pl.when(first)
            def _():
                side.start(*side_refs)

        i = pl.program_id(1)
        row = lax.broadcasted_iota(jnp.int32, (bq, bq), 0)
        col = lax.broadcasted_iota(jnp.int32, (bq, bq), 1)
        upper = (row > col).astype(BF16)
        causal = col < row

        def head_block(start, h, carry, acc, mask):
            cols = slice(h * HEAD_DIM, (h + 1) * HEAD_DIM)
            z2 = _sb_logits(q_ref[:, cols], k_ref[pl.ds(start, bq), cols])
            yield
            lk, lb = _sb_scores(z2, mask)
            later = _sb_later(lk, upper, carry)
            yield
            w = jnp.exp2(lb + later)
            if mask is not None:
                w = jnp.where(mask, w, 0.0)
            pv = jnp.dot(w.astype(BF16), v_ref[pl.ds(start, bq), cols], preferred_element_type=F32)
            yield
            return carry + jnp.sum(lk, axis=1, keepdims=True), acc + pv

        def block(j, state, mask):
            start = pl.multiple_of(j * bq, bq)
            return tuple(_lockstep(head_block(start, h, state[h][0], state[h][1], mask) for h in range(hps)))

        zero = (jnp.zeros((bq, 1), F32), jnp.zeros((bq, HEAD_DIM), F32))
        state = block(i, (zero,) * hps, causal)
        state = lax.fori_loop(0, i, lambda n, st: block(i - 1 - n, st, None), state)
        for h in range(hps):
            cols = slice(h * HEAD_DIM, (h + 1) * HEAD_DIM)
            o_ref[:, cols] = state[h][1].astype(BF16)
            o32_ref[:, cols] = state[h][1]

        if side is not None:
            @pl.when(last)
            def _():
                side.wait(*side_refs)

    oblk = pl.BlockSpec((bq, W), lambda g, i: (i, g))
    in_specs = [oblk, pl.BlockSpec((T, W), lambda g, i: (0, groups + g)),
                pl.BlockSpec((T, W), lambda g, i: (0, 2 * groups + g))]
    out_shape = [jax.ShapeDtypeStruct((T, width), BF16), jax.ShapeDtypeStruct((T, width), F32)]
    est = 2 * T * W * 2 * 2 + hps * 16 * bq * bq * 4
    if side is None:
        return pl.pallas_call(
            body, name=name, grid=(groups, T // bq), in_specs=in_specs, out_specs=[oblk, oblk], out_shape=out_shape,
            compiler_params=_params(("parallel", "arbitrary"), est),
        )(qkv, qkv, qkv)
    outs = pl.pallas_call(
        body, name=name, grid=(groups, T // bq), in_specs=in_specs + side.in_specs(),
        out_specs=[oblk, oblk] + side.out_specs(), out_shape=out_shape + side.out_shape(),
        scratch_shapes=side.scratch(), compiler_params=_params(("arbitrary", "arbitrary"), est),
    )(qkv, qkv, qkv, *side.arrays)
    return outs[0], outs[1], outs[2:]


def _sb_bwd(qkv, y, dy, width, name):
    T = qkv.shape[0]
    heads = width // HEAD_DIM
    bq = _sb_block_size(T)
    nq = T // bq
    scale = HEAD_DIM ** -0.5
    hps = min(2 * SB_HEADS_PER_STEP, heads)
    groups, W = heads // hps, hps * HEAD_DIM

    def body(q_ref, kv_hbm, y_ref, dy_ref, dq_ref, dk_hbm, dv_hbm, dk_acc, dv_acc, kbuf, vbuf, kv_sems, out_sems):
        g_id = pl.program_id(0)
        i = pl.program_id(1)
        kcol = pl.multiple_of((groups + g_id) * W, W)
        vcol = pl.multiple_of((2 * groups + g_id) * W, W)

        def fetch(j, slot):
            rows = pl.ds(pl.multiple_of(j * bq, bq), bq)
            return (pltpu.make_async_copy(kv_hbm.at[rows, pl.ds(kcol, W)], kbuf.at[slot], kv_sems.at[0, slot]),
                    pltpu.make_async_copy(kv_hbm.at[rows, pl.ds(vcol, W)], vbuf.at[slot], kv_sems.at[1, slot]))

        for cp in fetch(i, 0):
            cp.start()
        row = lax.broadcasted_iota(jnp.int32, (bq, bq), 0)
        col = lax.broadcasted_iota(jnp.int32, (bq, bq), 1)
        upper = (row > col).astype(BF16)
        upper_incl = (row >= col).astype(BF16)
        causal = col < row
        deltas = []
        for h in range(hps):
            cols = slice(h * HEAD_DIM, (h + 1) * HEAD_DIM)
            deltas.append(jnp.sum(dy_ref[:, cols].astype(F32) * y_ref[:, cols], axis=1, keepdims=True))

        @pl.when(i == 0)
        def _():
            dk_acc[...] = jnp.zeros_like(dk_acc)
            dv_acc[...] = jnp.zeros_like(dv_acc)

        def head_block(start, slot, h, carry, gcarry, dq, mask):
            cols = slice(h * HEAD_DIM, (h + 1) * HEAD_DIM)
            q = q_ref[:, cols]
            do = dy_ref[:, cols]
            kb = kbuf[slot, :, cols]
            vb = vbuf[slot, :, cols]
            z2 = _sb_logits(q, kb)
            dw = lax.dot_general(do, vb, _DIMS["nt"], preferred_element_type=F32)
            yield
            lk, lb = _sb_scores(z2, mask)
            later = _sb_later(lk, upper, carry)
            yield
            w = jnp.exp2(lb + later)
            if mask is not None:
                w = jnp.where(mask, w, 0.0)
            wb = w.astype(BF16)
            gw = dw * wb.astype(F32)
            gsuf = _sb_later(gw, upper_incl, gcarry)
            dvp = lax.dot_general(wb, do, _DIMS["tn"], preferred_element_type=F32)
            yield
            dz = gw - jnp.exp2(lb) * (gw + (deltas[h] - gsuf))
            if mask is not None:
                dz = jnp.where(mask, dz, 0.0)
            dzs = (dz * scale).astype(BF16)
            dqp = jnp.dot(dzs, kb, preferred_element_type=F32)
            dkp = lax.dot_general(dzs, q, _DIMS["tn"], preferred_element_type=F32)
            yield
            dk_acc[pl.ds(start, bq), cols] += dkp
            dv_acc[pl.ds(start, bq), cols] += dvp
            return (carry + jnp.sum(lk, axis=1, keepdims=True), gcarry + jnp.sum(gw, axis=1, keepdims=True), dq + dqp)

        def block(n, state, mask):
            j = i - n
            slot = n % 2
            for cp in fetch(j, slot):
                cp.wait()

            @pl.when(n < i)
            def _():
                for cp in fetch(j - 1, 1 - slot):
                    cp.start()

            start = pl.multiple_of(j * bq, bq)
            return tuple(_lockstep(head_block(start, slot, h, *state[h], mask) for h in range(hps)))

        zero = jnp.zeros((bq, 1), F32)
        state = block(0, ((zero, zero, jnp.zeros((bq, HEAD_DIM), F32)),) * hps, causal)
        state = lax.fori_loop(1, i + 1, lambda n, st: block(n, st, None), state)
        for h in range(hps):
            dq_ref[:, h * HEAD_DIM:(h + 1) * HEAD_DIM] = state[h][2]

        @pl.when(i == nq - 1)
        def _():
            cols = pl.ds(pl.multiple_of(g_id * W, W), W)
            copies = [pltpu.make_async_copy(dk_acc, dk_hbm.at[:, cols], out_sems.at[0]),
                      pltpu.make_async_copy(dv_acc, dv_hbm.at[:, cols], out_sems.at[1])]
            for cp in copies:
                cp.start()
            for cp in copies:
                cp.wait()

    qblk = pl.BlockSpec((bq, W), lambda g, i: (i, g))
    out = jax.ShapeDtypeStruct((T, width), F32)
    return pl.pallas_call(
        body, name=name, grid=(groups, nq),
        in_specs=[qblk, pl.BlockSpec(memory_space=pl.ANY), qblk, qblk],
        out_specs=[qblk, pl.BlockSpec(memory_space=pl.ANY), pl.BlockSpec(memory_space=pl.ANY)],
        out_shape=[out, out, out],
        scratch_shapes=[pltpu.VMEM((T, W), F32), pltpu.VMEM((T, W), F32), pltpu.VMEM((2, bq, W), BF16),
                        pltpu.VMEM((2, bq, W), BF16), pltpu.SemaphoreType.DMA((2, 2)), pltpu.SemaphoreType.DMA((2,))],
        compiler_params=_params(("arbitrary", "arbitrary"), 2 * T * W * 4),
    )(qkv, qkv, y, dy)


def _hg_constants():
    C = HG_CHUNK
    t = np.arange(C)[:, None]
    j = np.arange(C)[None, :]
    blocks = [(j <= t), (j > t)]
    masks = []
    for n in HG_LEVELS:
        right = (t % (2 * n)) >= n
        mid = (t // (2 * n)) * (2 * n) + n - 1
        blocks.append(right & (j > mid) & (j <= t))
        blocks.append((~right) & (j > t) & (j <= mid))
        tt, ss = np.arange(C)[:, None], np.arange(C)[None, :]
        same = (tt // (2 * n)) == (ss // (2 * n))
        masks.append(same & ((tt % (2 * n)) >= n) & ((ss % (2 * n)) < n))
    sums = np.concatenate(blocks, axis=0).astype(np.float32)
    return jnp.asarray(sums, BF16), jnp.asarray(np.stack(masks).astype(np.float32))


def _split3(x):
    hi = x.astype(BF16)
    r1 = x - hi.astype(F32)
    mid = r1.astype(BF16)
    lo = (r1 - mid.astype(F32)).astype(BF16)
    return hi, mid, lo


def _hg_gates(hq, hf, lb):
    sq = _sigmoid(hq)
    sf = _sigmoid(hf)
    f = lb + (1.0 - lb) * sf
    return hq * sq, sq, f, sf, 1.0 - f, jnp.log(f)


def _hg_exponents(sums, g):
    hi, mid, lo = _split3(g)
    return (jnp.dot(sums, hi, preferred_element_type=F32) + jnp.dot(sums, mid, preferred_element_type=F32)
            + jnp.dot(sums, lo, preferred_element_type=F32))


def _hg_level_scores(q, k, e_all):
    C = HG_CHUNK
    parts, prods = [], []
    for li in range(len(HG_LEVELS)):
        eq = jnp.exp(e_all[(2 + 2 * li) * C:(3 + 2 * li) * C])
        ek = jnp.exp(e_all[(3 + 2 * li) * C:(4 + 2 * li) * C])
        qt = q * eq
        kt = k * ek
        prods.append(lax.dot_general(qt.astype(BF16), kt.astype(BF16), _DIMS["nt"], preferred_element_type=F32))
        parts.append((eq, ek, qt, kt))
    return parts, prods


def _hg_intra(prods, masks_ref):
    a = masks_ref[0] * prods[0]
    for li in range(1, len(HG_LEVELS)):
        a = a + masks_ref[li] * prods[li]
    return a


def _lockstep(gens):
    gens = list(gens)
    results = [None] * len(gens)
    alive = list(range(len(gens)))
    while alive:
        for idx in list(alive):
            try:
                next(gens[idx])
            except StopIteration as done:
                results[idx] = done.value
                alive.remove(idx)
    return results


def _lower_bound(lbl_ref, cols):
    return _sigmoid(lbl_ref[0:1, cols] - lbl_ref[1:2, cols])


HG_HEADS_PER_STEP = 4


def _hg_fwd(proj, lb_logits, g_out, col0, width, name):
    T = proj.shape[0]
    heads = width // HEAD_DIM
    C = HG_CHUNK
    tb = _tile(T, 512, C)
    cpb = tb // C
    nb = T // tb
    sums, masks = _hg_constants()
    hps = min(HG_HEADS_PER_STEP, heads)
    groups, W = heads // hps, hps * HEAD_DIM
    assert col0 % W == 0 and width % W == 0
    cb = col0 // W

    def body(hq_ref, hf_ref, hi_ref, og_ref, lbl_ref, go_ref, sums_ref, masks_ref, y_ref, o_ref, st_ref, s_ref):
        i = pl.program_id(1)

        @pl.when(i == 0)
        def _():
            s_ref[...] = jnp.zeros_like(s_ref)

        go = go_ref[...]

        def chunk_head(c, rows, h):
            cols = slice(h * HEAD_DIM, (h + 1) * HEAD_DIM)
            lb = _lower_bound(lbl_ref, cols)
            q, _, _, _, k, g = _hg_gates(hq_ref[rows, cols], hf_ref[rows, cols], lb)
            v = hi_ref[rows, cols]
            vb = v.astype(BF16)
            st = s_ref[h]
            st_ref[h, c] = st
            e_all = _hg_exponents(sums_ref[...], g)
            yield
            b = e_all[0:C]
            ebl = jnp.exp(b[C - 1:C, :])
            qe = q * jnp.exp(b)
            o_inter = lax.dot_general(qe.astype(BF16), st.astype(BF16), _DIMS["nt"], preferred_element_type=F32)
            kd = k * jnp.exp(e_all[C:2 * C])
            s_new = lax.dot_general(vb, kd.astype(BF16), _DIMS["tn"], preferred_element_type=F32)
            _, prods = _hg_level_scores(q, k, e_all)
            yield
            a = _hg_intra(prods, masks_ref)
            o_intra = jnp.dot(a.astype(BF16), vb, preferred_element_type=F32)
            s_ref[h] = st * ebl + s_new
            yield
            o = (o_inter + o_intra) + jnp.sum(q * k, axis=1, keepdims=True) * v
            o_ref[rows, cols] = o
            r = lax.rsqrt(jnp.mean(o * o, axis=-1, keepdims=True) + EPS)
            og = og_ref[rows, cols]
            y_ref[rows, cols] = (((o * r) * go) * (og * _sigmoid(og))).astype(BF16)

        def chunk(c, _):
            rows = pl.ds(pl.multiple_of(c * C, C), C)
            _lockstep(chunk_head(c, rows, h) for h in range(hps))
            return 0

        lax.fori_loop(0, cpb, chunk, 0)

    def col(k):
        return pl.BlockSpec((tb, W), lambda g, i: (i, cb + k * groups + g))

    nsum = sums.shape[0]
    out_blk = pl.BlockSpec((tb, W), lambda g, i: (i, g))
    return pl.pallas_call(
        body, name=name, grid=(groups, nb),
        in_specs=[col(0), col(1), col(2), col(3),
                  pl.BlockSpec((2, W), lambda g, i: (0, g)),
                  pl.BlockSpec((1, HEAD_DIM), lambda g, i: (0, 0)),
                  pl.BlockSpec((nsum, C), lambda g, i: (0, 0)),
                  pl.BlockSpec((len(HG_LEVELS), C, C), lambda g, i: (0, 0, 0))],
        out_specs=[out_blk, out_blk,
                   pl.BlockSpec((hps, cpb, HEAD_DIM, HEAD_DIM), lambda g, i: (g, i, 0, 0))],
        out_shape=[jax.ShapeDtypeStruct((T, width), BF16), jax.ShapeDtypeStruct((T, width), F32),
                   jax.ShapeDtypeStruct((heads, T // C, HEAD_DIM, HEAD_DIM), F32)],
        scratch_shapes=[pltpu.VMEM((hps, HEAD_DIM, HEAD_DIM), F32)],
        compiler_params=_params(("parallel", "arbitrary"), tb * W * 4 * 7 + hps * cpb * HEAD_DIM * HEAD_DIM * 4),
    )(proj, proj, proj, proj, lb_logits, g_out, sums, masks)


def _hg_bwd(proj, lb_logits, g_out, o_saved, states, dy, dproj, col0, width, name):
    T = proj.shape[0]
    heads = width // HEAD_DIM
    C = HG_CHUNK
    tb = _tile(T, 512, C)
    cpb = tb // C
    nb = T // tb
    sums, masks = _hg_constants()
    nsum = sums.shape[0]
    nlev = len(HG_LEVELS)
    hps = min(HG_HEADS_PER_STEP, heads)
    groups, W = heads // hps, hps * HEAD_DIM
    cb = col0 // W

    def body(hq_ref, hf_ref, hi_ref, og_ref, lbl_ref, go_ref, sums_ref, masks_ref, o_ref, st_ref, dy_ref, dp_in,
             dp_ref, dlb_ref, dgo_ref, ds_ref, de_ref, dlb_acc, dgo_acc, dout_ref):
        del dp_in
        grp = pl.program_id(0)
        i = pl.program_id(1)
        kk = pl.program_id(2)

        @pl.when(jnp.logical_and(i == 0, kk == 0))
        def _():
            ds_ref[...] = jnp.zeros_like(ds_ref)
            dlb_acc[...] = jnp.zeros_like(dlb_acc)

        @pl.when(jnp.logical_and(jnp.logical_and(grp == 0, i == 0), kk == 0))
        def _():
            dgo_acc[...] = jnp.zeros_like(dgo_acc)

        go = go_ref[...]
        last_row = lax.broadcasted_iota(jnp.int32, (C, HEAD_DIM), 0) == C - 1

        def chunk(n, _):
            c = cpb - 1 - n
            rows = pl.ds(pl.multiple_of(c * C, C), C)
            _lockstep(chunk_head(c, rows, h) for h in range(hps))
            return 0

        def chunk_head(c, rows, h):
            cols = slice(h * HEAD_DIM, (h + 1) * HEAD_DIM)
            de_h = de_ref.at[h]
            lb = _lower_bound(lbl_ref, cols)
            hq = hq_ref[rows, cols]
            og = og_ref[rows, cols]
            q, sq, f, sf, k, g = _hg_gates(hq, hf_ref[rows, cols], lb)
            v = hi_ref[rows, cols]
            vb = v.astype(BF16)
            st = st_ref[h, c]
            stb = st.astype(BF16)
            dst = ds_ref[h]
            dstb = dst.astype(BF16)
            o = o_ref[rows, cols]
            dyc = dy_ref[rows, cols].astype(F32)
            sg = _sigmoid(og)
            r = lax.rsqrt(jnp.mean(o * o, axis=-1, keepdims=True) + EPS)
            on = (o * r) * go
            don = dyc * (og * sg)
            dout_ref[3, rows, cols] = (dyc * on * (sg * (1.0 + og * (1.0 - sg)))).astype(BF16)
            dgo_acc[...] += jnp.sum((don * (o * r)).reshape(C // SUBLANE, SUBLANE, HEAD_DIM), axis=0)
            wn = don * go
            do = r * wn - o * (r * r * r * jnp.mean(o * wn, axis=-1, keepdims=True))
            dob = do.astype(BF16)
            e_all = _hg_exponents(sums_ref[...], g)
            da = lax.dot_general(dob, vb, _DIMS["nt"], preferred_element_type=F32)
            dqe = jnp.dot(dob, stb, preferred_element_type=F32)
            dkd = jnp.dot(vb, dstb, preferred_element_type=F32)
            yield
            eb = jnp.exp(e_all[0:C])
            esuf = jnp.exp(e_all[C:2 * C])
            ebl = eb[C - 1:C, :]
            qe = q * eb
            kd = k * esuf
            parts, prods = _hg_level_scores(q, k, e_all)
            dv_state = lax.dot_general(kd.astype(BF16), dstb, _DIMS["nt"], preferred_element_type=F32)
            ds_new = lax.dot_general(dob, qe.astype(BF16), _DIMS["tn"], preferred_element_type=F32)
            yield
            a = _hg_intra(prods, masks_ref)
            dv = lax.dot_general(a.astype(BF16), dob, _DIMS["tn"], preferred_element_type=F32)
            dlev = []
            for li in range(nlev):
                _, _, qt, kt = parts[li]
                dan = (masks_ref[li] * da).astype(BF16)
                dlev.append((jnp.dot(dan, kt.astype(BF16), preferred_element_type=F32),
                             lax.dot_general(dan, qt.astype(BF16), _DIMS["tn"], preferred_element_type=F32)))
            yield
            qk = jnp.sum(q * k, axis=1, keepdims=True)
            dv = dv + qk * do + dv_state
            dqk = jnp.sum(do * v, axis=1, keepdims=True)
            dq = dqk * k
            dk = dqk * q
            for li in range(nlev):
                eq, ek, qt, kt = parts[li]
                dqt, dkt = dlev[li]
                dq = dq + dqt * eq
                dk = dk + dkt * ek
                de_h[(2 + 2 * li) * C:(3 + 2 * li) * C, :] = dqt * qt
                de_h[(3 + 2 * li) * C:(4 + 2 * li) * C, :] = dkt * kt
            dq = dq + dqe * eb
            dk = dk + dkd * esuf
            debl = jnp.sum(dst * st, axis=0, keepdims=True)
            de_h[0:C, :] = dqe * qe + jnp.where(last_row, debl * ebl, 0.0)
            de_h[C:2 * C, :] = dkd * kd
            ds_ref[h] = dst * ebl + ds_new
            dehi, delo = _split2(de_h[...])
            dg = (lax.dot_general(sums_ref[...], dehi, _DIMS["tn"], preferred_element_type=F32)
                  + lax.dot_general(sums_ref[...], delo, _DIMS["tn"], preferred_element_type=F32))
            yield
            df = dg / f - dk
            dout_ref[0, rows, cols] = (dq * (sq * (1.0 + hq * (1.0 - sq)))).astype(BF16)
            dout_ref[1, rows, cols] = (df * (1.0 - lb) * (sf * (1.0 - sf))).astype(BF16)
            dout_ref[2, rows, cols] = dv.astype(BF16)
            dlb_acc[:, cols] += jnp.sum((df * (1.0 - sf)).reshape(C // SUBLANE, SUBLANE, HEAD_DIM), axis=0)

        @pl.when(kk == 0)
        def _():
            lax.fori_loop(0, cpb, chunk, 0)

        dp_ref[...] = dout_ref[kk]

        @pl.when(jnp.logical_and(i == nb - 1, kk == 3))
        def _():
            lb = _lower_bound(lbl_ref, slice(None))
            dl0 = jnp.sum(dlb_acc[...], axis=0, keepdims=True) * (lb * (1.0 - lb))
            dlb_ref[0:1, :] = dl0
            dlb_ref[1:2, :] = -dl0

        @pl.when(jnp.logical_and(jnp.logical_and(grp == groups - 1, i == nb - 1), kk == 3))
        def _():
            dgo_ref[...] = jnp.sum(dgo_acc[...], axis=0, keepdims=True)

    def col(k):
        return pl.BlockSpec((tb, W), lambda g, i, kk: (nb - 1 - i, cb + k * groups + g))

    rev = pl.BlockSpec((tb, W), lambda g, i, kk: (nb - 1 - i, g))
    return pl.pallas_call(
        body, name=name, grid=(groups, nb, 4),
        in_specs=[col(0), col(1), col(2), col(3),
                  pl.BlockSpec((2, W), lambda g, i, kk: (0, g)),
                  pl.BlockSpec((1, HEAD_DIM), lambda g, i, kk: (0, 0)),
                  pl.BlockSpec((nsum, C), lambda g, i, kk: (0, 0)),
                  pl.BlockSpec((nlev, C, C), lambda g, i, kk: (0, 0, 0)),
                  rev,
                  pl.BlockSpec((hps, cpb, HEAD_DIM, HEAD_DIM), lambda g, i, kk: (g, nb - 1 - i, 0, 0)),
                  rev,
                  pl.BlockSpec(memory_space=pl.ANY)],
        out_specs=[pl.BlockSpec((tb, W), lambda g, i, kk: (nb - 1 - i, cb + kk * groups + g)),
                   pl.BlockSpec((2, W), lambda g, i, kk: (0, g)),
                   pl.BlockSpec((1, HEAD_DIM), lambda g, i, kk: (0, 0))],
        out_shape=[jax.ShapeDtypeStruct(dproj.shape, BF16), jax.ShapeDtypeStruct((2, width), F32),
                   jax.ShapeDtypeStruct((1, HEAD_DIM), F32)],
        input_output_aliases={11: 0},
        scratch_shapes=[pltpu.VMEM((hps, HEAD_DIM, HEAD_DIM), F32), pltpu.VMEM((hps, nsum, HEAD_DIM), F32),
                        pltpu.VMEM((SUBLANE, W), F32), pltpu.VMEM((SUBLANE, HEAD_DIM), F32),
                        pltpu.VMEM((4, tb, W), BF16)],
        compiler_params=_params(("arbitrary", "arbitrary", "arbitrary"),
                                tb * W * 4 * 12 + hps * cpb * HEAD_DIM * HEAD_DIM * 4),
    )(proj, proj, proj, proj, lb_logits, g_out, sums, masks, o_saved, states, dy, dproj)


def _merge_fwd(proj, ma, mb, gate_col0, name):
    T, D = ma.shape
    tr = _tile(T, 256, SUBLANE)
    cw = _tile(D, 1024, LANE)
    nj = D // cw
    assert gate_col0 % cw == 0
    g0 = gate_col0 // cw

    def body(ga_ref, gb_ref, ma_ref, mb_ref, o_ref):
        o_ref[...] = (_sigmoid(ga_ref[...]) * ma_ref[...] + _sigmoid(gb_ref[...]) * mb_ref[...]).astype(BF16)

    blk = pl.BlockSpec((tr, cw), lambda i, j: (i, j))
    return pl.pallas_call(
        body, name=name, grid=(T // tr, nj),
        in_specs=[pl.BlockSpec((tr, cw), lambda i, j: (i, g0 + j)),
                  pl.BlockSpec((tr, cw), lambda i, j: (i, g0 + nj + j)), blk, blk],
        out_specs=blk, out_shape=jax.ShapeDtypeStruct((T, D), BF16),
        compiler_params=_params(("parallel", "parallel"), tr * cw * 18 * 2),
    )(proj, proj, ma, mb)


def _merge_bwd(proj, ma, mb, dm, gate_col0, name):
    T, D = ma.shape
    tr = _tile(T, 256, SUBLANE)
    cw = _tile(D, 1024, LANE)
    nj = D // cw
    g0 = gate_col0 // cw

    def body(g_ref, ma_ref, mb_ref, dm_ref, dp_ref, dmm_ref):
        branch = pl.program_id(2)
        s = _sigmoid(g_ref[...])
        dmv = dm_ref[...]
        mm = jnp.where(branch == 0, ma_ref[...], mb_ref[...])
        dp_ref[...] = (dmv * mm * (s * (1.0 - s))).astype(BF16)
        dmm_ref[...] = (dmv * s).astype(BF16)

    blk = pl.BlockSpec((tr, cw), lambda i, j, b: (i, j))
    gate = pl.BlockSpec((tr, cw), lambda i, j, b: (i, g0 + b * nj + j))
    return pl.pallas_call(
        body, name=name, grid=(T // tr, nj, 2),
        in_specs=[gate, blk, blk, blk],
        out_specs=[gate, pl.BlockSpec((None, tr, cw), lambda i, j, b: (b, i, j))],
        out_shape=[jax.ShapeDtypeStruct(proj.shape, BF16), jax.ShapeDtypeStruct((2, T, D), BF16)],
        compiler_params=_params(("parallel", "parallel", "arbitrary"), tr * cw * 20 * 2),
    )(proj, ma, mb, dm)


def _causal_conv(ext, w, b):
    s1 = pltpu.roll(ext, 1, 0)
    s2 = pltpu.roll(ext, 2, 0)
    out = b + w[0:1, :] * s2
    out = out + w[1:2, :] * s1
    out = out + w[2:3, :] * ext
    return out, s1, s2


def _conv_fwd(up, convw, convb, name):
    T, F2 = up.shape
    tc = convw.shape[2]
    half = (F2 // 2) // tc
    assert half * tc * 2 == F2
    tr = _tile(T, 256, SUBLANE)
    hb = tr // SUBLANE

    def body(g_ref, gp_ref, v_ref, vp_ref, wg_ref, wv_ref, bg_ref, bv_ref, o_ref, gx_ref, vx_ref):
        first = pl.program_id(1) == 0

        def lanes(c, _):
            cols = pl.ds(pl.multiple_of(c * LANE, LANE), LANE)

            def conv(cur_ref, prev_ref, w_ref, b_ref, x_ref):
                x_ref[0:SUBLANE, :] = jnp.where(first, 0.0, prev_ref[:, cols])
                x_ref[SUBLANE:, :] = cur_ref[:, cols]
                w = w_ref[:, cols]
                out = b_ref[:, cols] + w[0:1, :] * x_ref[pl.ds(SUBLANE - 2, tr), :]
                out = out + w[1:2, :] * x_ref[pl.ds(SUBLANE - 1, tr), :]
                return out + w[2:3, :] * x_ref[pl.ds(SUBLANE, tr), :]

            gate = conv(g_ref, gp_ref, wg_ref, bg_ref, gx_ref)
            val = conv(v_ref, vp_ref, wv_ref, bv_ref, vx_ref)
            o_ref[:, cols] = ((gate * _sigmoid(gate)) * val).astype(BF16)
            return 0

        lax.fori_loop(0, tc // LANE, lanes, 0)

    def main(off):
        return pl.BlockSpec((tr, tc), lambda j, i: (i, off + j))

    def prev(off):
        return pl.BlockSpec((SUBLANE, tc), lambda j, i: (jnp.maximum(i * hb - 1, 0), off + j))

    def wspec(off):
        return pl.BlockSpec((None, 3, tc), lambda j, i: (off + j, 0, 0))

    def bspec(off):
        return pl.BlockSpec((1, tc), lambda j, i: (0, off + j))

    return pl.pallas_call(
        body, name=name, grid=(half, T // tr),
        in_specs=[main(0), prev(0), main(half), prev(half), wspec(0), wspec(half), bspec(0), bspec(half)],
        out_specs=pl.BlockSpec((tr, tc), lambda j, i: (i, j)),
        out_shape=jax.ShapeDtypeStruct((T, F2 // 2), BF16),
        scratch_shapes=[pltpu.VMEM((SUBLANE + tr, LANE), F32), pltpu.VMEM((SUBLANE + tr, LANE), F32)],
        compiler_params=_params(("parallel", "parallel"), tr * tc * 4 * 12),
    )(up, up, up, up, convw, convw, convb, convb)


def _conv_bwd(da, up, convw, convb, name):
    T, F2 = up.shape
    tc = convw.shape[2]
    half = (F2 // 2) // tc
    tr = _tile(T, 128, SUBLANE)
    hb = tr // SUBLANE
    nrow = T // tr
    n = tr + 2 * SUBLANE

    def body(g_ref, gp_ref, gn_ref, v_ref, vp_ref, vn_ref, da_ref, dan_ref, wg_ref, wv_ref, bg_ref, bv_ref,
             dup_ref, gw_ref, acc_ref, gx_ref, vx_ref, dg_ref, dv_ref):
        i = pl.program_id(1)
        first = i == 0
        last = i == nrow - 1

        @pl.when(first)
        def _():
            acc_ref[...] = jnp.zeros_like(acc_ref)

        m = tr + SUBLANE

        def fold(t):
            return jnp.sum(t.reshape(tr // SUBLANE, SUBLANE, LANE), axis=0)

        def lanes(c, _):
            cols = pl.ds(pl.multiple_of(c * LANE, LANE), LANE)

            def conv(cur_ref, prev_ref, next_ref, w_ref, b_ref, x_ref):
                x_ref[0:SUBLANE, :] = jnp.where(first, 0.0, prev_ref[:, cols])
                x_ref[SUBLANE:SUBLANE + tr, :] = cur_ref[:, cols]
                x_ref[SUBLANE + tr:, :] = next_ref[:, cols]
                w = w_ref[:, cols]
                taps = tuple(x_ref[pl.ds(SUBLANE - 2 + k, m), :] for k in range(3))
                out = b_ref[:, cols] + w[0:1, :] * taps[0]
                out = out + w[1:2, :] * taps[1]
                return out + w[2:3, :] * taps[2], taps, w

            gate, g_taps, wg = conv(g_ref, gp_ref, gn_ref, wg_ref, bg_ref, gx_ref)
            val, v_taps, wv = conv(v_ref, vp_ref, vn_ref, wv_ref, bv_ref, vx_ref)
            da_m = jnp.concatenate([da_ref[:, cols], jnp.where(last, 0.0, dan_ref[:, cols])], axis=0)
            sg = _sigmoid(gate)
            dg_ref[...] = da_m * val * (sg * (1.0 + gate * (1.0 - sg)))
            dv_ref[...] = da_m * (gate * sg)
            for hf, (d_ref, w, taps) in enumerate(((dg_ref, wg, g_taps), (dv_ref, wv, v_taps))):
                dc = d_ref[pl.ds(0, tr), :]
                dup = w[2:3, :] * dc + w[1:2, :] * d_ref[pl.ds(1, tr), :] + w[0:1, :] * d_ref[pl.ds(2, tr), :]
                dup_ref[hf, :, cols] = dup.astype(BF16)
                acc_ref[hf, 0, :, cols] += fold(dc)
                for k in range(3):
                    acc_ref[hf, 1 + k, :, cols] += fold(dc * taps[k][0:tr])
            return 0

        lax.fori_loop(0, tc // LANE, lanes, 0)

        @pl.when(last)
        def _():
            gw_ref[...] = jnp.sum(acc_ref[...], axis=2)

    def main(off):
        return pl.BlockSpec((tr, tc), lambda j, i: (i, off + j))

    def prev(off):
        return pl.BlockSpec((SUBLANE, tc), lambda j, i: (jnp.maximum(i * hb - 1, 0), off + j))

    def nxt(off):
        return pl.BlockSpec((SUBLANE, tc), lambda j, i: (jnp.minimum((i + 1) * hb, T // SUBLANE - 1), off + j))

    def wspec(off):
        return pl.BlockSpec((None, 3, tc), lambda j, i: (off + j, 0, 0))

    def bspec(off):
        return pl.BlockSpec((1, tc), lambda j, i: (0, off + j))

    return pl.pallas_call(
        body, name=name, grid=(half, nrow),
        in_specs=[main(0), prev(0), nxt(0), main(half), prev(half), nxt(half), main(0), nxt(0),
                  wspec(0), wspec(half), bspec(0), bspec(half)],
        out_specs=[pl.BlockSpec((2, tr, tc), lambda j, i: (0, i, j)), pl.BlockSpec((2, 4, tc), lambda j, i: (0, 0, j))],
        out_shape=[jax.ShapeDtypeStruct((2, T, F2 // 2), BF16), jax.ShapeDtypeStruct((2, 4, F2 // 2), F32)],
        scratch_shapes=[pltpu.VMEM((2, 4, SUBLANE, tc), F32), pltpu.VMEM((n, LANE), F32), pltpu.VMEM((n, LANE), F32),
                        pltpu.VMEM((tr + SUBLANE, LANE), F32), pltpu.VMEM((tr + SUBLANE, LANE), F32)],
        compiler_params=_params(("parallel", "arbitrary"), n * tc * 4 * 24),
    )(up, up, up, up, up, up, da, da, convw, convw, convb, convb)


def _loss_head(y, target, name):
    T, D = y.shape
    tr = _tile(T, 256, SUBLANE)
    nrow = T // tr

    def body(y_ref, t_ref, d_ref, db_ref, l_ref, acc_ref):
        i = pl.program_id(0)
        diff = y_ref[...] - t_ref[...]
        dy = diff / D
        d_ref[...] = dy
        db_ref[...] = dy.astype(BF16)
        part = jnp.sum((diff * diff).reshape(tr // SUBLANE, SUBLANE, D), axis=0)

        @pl.when(i == 0)
        def _():
            acc_ref[...] = part

        @pl.when(i > 0)
        def _():
            acc_ref[...] += part

        @pl.when(i == nrow - 1)
        def _():
            col = jnp.sum(acc_ref[...], axis=0, keepdims=True)
            l_ref[...] = jnp.broadcast_to(0.5 * (jnp.sum(col, axis=1, keepdims=True) / D), (1, LANE))

    row = pl.BlockSpec((tr, D), lambda i: (i, 0))
    return pl.pallas_call(
        body, name=name, grid=(nrow,), in_specs=[row, row],
        out_specs=[row, row, pl.BlockSpec((1, LANE), lambda i: (0, 0))],
        out_shape=[jax.ShapeDtypeStruct((T, D), F32), jax.ShapeDtypeStruct((T, D), BF16),
                   jax.ShapeDtypeStruct((1, LANE), F32)],
        scratch_shapes=[pltpu.VMEM((SUBLANE, D), F32)],
        compiler_params=_params(("arbitrary",), tr * D * 14 * 2),
    )(y, target)


def _adamw(w, parts, m, v, name):
    R, C = w.shape
    P = parts.shape[0]
    tr = _tile(R, 64, SUBLANE)
    tc = _tile(C, 2048, LANE)

    def body(w_ref, p_ref, m_ref, v_ref, g_ref, d_ref, nm_ref, nv_ref):
        g = p_ref[0].astype(F32)
        for s in range(1, P):
            g = g + p_ref[s].astype(F32)
        wv = w_ref[...]
        nm = ADAM_B1 * m_ref[...] + (1.0 - ADAM_B1) * g
        nv = ADAM_B2 * v_ref[...] + (1.0 - ADAM_B2) * (g * g)
        m_hat = nm / (1.0 - ADAM_B1 ** ADAM_STEP)
        v_hat = nv / (1.0 - ADAM_B2 ** ADAM_STEP)
        g_ref[...] = g
        d_ref[...] = -ADAM_LR * (m_hat / (jnp.sqrt(v_hat) + ADAM_EPS) + ADAM_WD * wv)
        nm_ref[...] = nm
        nv_ref[...] = nv

    blk = pl.BlockSpec((tr, tc), lambda i, j: (i, j))
    shp = jax.ShapeDtypeStruct((R, C), F32)
    return pl.pallas_call(
        body, name=name, grid=(R // tr, C // tc),
        in_specs=[blk, pl.BlockSpec((P, tr, tc), lambda i, j: (0, i, j)), blk, blk],
        out_specs=[blk] * 4, out_shape=[shp] * 4,
        compiler_params=_params(("parallel", "parallel"), tr * tc * 4 * (P + 8) * 2),
    )(w, parts, m, v)


def _place():
    x, y, c = (lax.axis_index(a) for a in MESH_AXES)
    return x, y, c, 4 * x + 2 * y + c


def _peers(x, y, c):
    out = []
    for d in range(1, N_DEV):
        px = x + (d >> 2 & 1) - 2 * x * (d >> 2 & 1)
        py = y + (d >> 1 & 1) - 2 * y * (d >> 1 & 1)
        pc = c + (d & 1) - 2 * c * (d & 1)
        out.append(((px, py, pc), 4 * px + 2 * py + pc))
    return out


class _Side:
    def __init__(self, arrays, scatter, side_by_side=()):
        self.arrays, self.scatter, self.n = list(arrays), scatter, len(arrays)
        self.side_by_side = set(side_by_side)
        assert not (scatter and self.side_by_side)

    def in_specs(self):
        return [pl.BlockSpec(memory_space=pltpu.HBM)] * self.n

    out_specs = in_specs

    def out_shape(self):
        def shape(t, a):
            if t in self.side_by_side:
                return (a.shape[0], N_DEV * a.shape[1])
            return a.shape if self.scatter else (N_DEV,) + a.shape
        return [jax.ShapeDtypeStruct(shape(t, a), a.dtype) for t, a in enumerate(self.arrays)]

    def _slot(self, t, out, idx):
        if t in self.side_by_side:
            w = self.arrays[t].shape[1]
            return out.at[:, pl.ds(pl.multiple_of(idx * w, w), w)]
        return out.at[idx]

    def scratch(self):
        return [pltpu.SemaphoreType.DMA((self.n, N_DEV - 1)), pltpu.SemaphoreType.DMA((self.n, N_DEV - 1)),
                pltpu.SemaphoreType.DMA((self.n,))]

    def _copies(self, ins, outs, sems):
        send_sems, recv_sems, local_sems = sems
        x, y, c, me = _place()
        peers = _peers(x, y, c)
        local, sends, recvs = [], [], []
        for t in range(self.n):
            src_me = ins[t].at[me] if self.scatter else ins[t]
            local.append(pltpu.make_async_copy(src_me, self._slot(t, outs[t], me), local_sems.at[t]))
            for d, (peer, pidx) in enumerate(peers):
                src = ins[t].at[pidx] if self.scatter else ins[t]
                sends.append(pltpu.make_async_remote_copy(
                    src_ref=src, dst_ref=self._slot(t, outs[t], me), send_sem=send_sems.at[t, d],
                    recv_sem=recv_sems.at[t, d], device_id=peer, device_id_type=MESH_ID))
                recvs.append(pltpu.make_async_remote_copy(
                    src_ref=src, dst_ref=self._slot(t, outs[t], pidx), send_sem=send_sems.at[t, d],
                    recv_sem=recv_sems.at[t, d], device_id=peer, device_id_type=MESH_ID))
        return local, sends, recvs

    def start(self, ins, outs, sems):
        local, sends, _ = self._copies(ins, outs, sems)
        for cp in local + sends:
            cp.start()

    def wait(self, ins, outs, sems):
        local, sends, recvs = self._copies(ins, outs, sems)
        for cp in recvs:
            cp.wait_recv()
        for cp in sends:
            cp.wait_send()
        for cp in local:
            cp.wait()


def _grid_edges(sizes):
    ids = [pl.program_id(a) for a in range(len(sizes))]
    first = functools.reduce(jnp.logical_and, [i == 0 for i in ids])
    last = functools.reduce(jnp.logical_and, [i == s - 1 for i, s in zip(ids, sizes)])
    return first, last


def _exchange(arrays, scatter, name):
    side = _Side(arrays, scatter)
    n = side.n

    def body(*refs):
        ins, outs, sems = refs[:n], refs[n:2 * n], refs[2 * n:]
        side.start(ins, outs, sems)
        side.wait(ins, outs, sems)

    return pl.pallas_call(
        body, name=name, in_specs=side.in_specs(), out_specs=side.out_specs(), out_shape=side.out_shape(),
        scratch_shapes=side.scratch(),
    )(*arrays)


def _gather_two_level(arrays, name):
    n = len(arrays)

    def body(*refs):
        ins, outs = refs[:n], refs[n:2 * n]
        send_sems, recv_sems, local_sems = refs[2 * n:]
        x, y, c, me = _place()
        sibling = (x, y, 1 - c)
        chips = [(1 - x, y), (x, 1 - y), (1 - x, 1 - y)]

        def slot(px, py, pc):
            return 4 * px + 2 * py + pc

        def copy(t, k, block, to, src=None):
            return pltpu.make_async_remote_copy(
                src_ref=outs[t].at[slot(*block)] if src is None else src, dst_ref=outs[t].at[slot(*block)],
                send_sem=send_sems.at[t, k], recv_sem=recv_sems.at[t, k], device_id=to, device_id_type=MESH_ID)

        mine = [pltpu.make_async_copy(ins[t], outs[t].at[me], local_sems.at[t]) for t in range(n)]
        first = [copy(t, 0, (x, y, c), sibling, src=ins[t]) for t in range(n)]
        first += [copy(t, 1 + j, (x, y, c), (*chip, c), src=ins[t]) for t in range(n) for j, chip in enumerate(chips)]
        for cp in mine + first:
            cp.start()
        passed = []
        for t in range(n):
            for j, chip in enumerate(chips):
                copy(t, 1 + j, (*chip, c), (x, y, c)).wait_recv()
                cp = copy(t, 4 + j, (*chip, c), sibling)
                cp.start()
                passed.append(cp)
        for t in range(n):
            copy(t, 0, sibling, (x, y, c)).wait_recv()
            for j, chip in enumerate(chips):
                copy(t, 4 + j, (*chip, 1 - c), (x, y, c)).wait_recv()
        for cp in first + passed:
            cp.wait_send()
        for cp in mine:
            cp.wait()

    hbm = pl.BlockSpec(memory_space=pltpu.HBM)
    return pl.pallas_call(
        body, name=name, in_specs=[hbm] * n, out_specs=[hbm] * n,
        out_shape=[jax.ShapeDtypeStruct((N_DEV,) + a.shape, a.dtype) for a in arrays],
        scratch_shapes=[pltpu.SemaphoreType.DMA((n, N_DEV - 1)), pltpu.SemaphoreType.DMA((n, N_DEV - 1)),
                        pltpu.SemaphoreType.DMA((n,))],
    )(*arrays)


def _all_reduce_small(vec, name):
    R = vec.shape[0]

    def body(v_ref, o_ref, gath_ref, send_sems, recv_sems):
        x, y, c, me = _place()
        sends, recvs = [], []
        for d, (peer, pidx) in enumerate(_peers(x, y, c)):
            cp = pltpu.make_async_remote_copy(
                src_ref=v_ref, dst_ref=gath_ref.at[me], send_sem=send_sems.at[d], recv_sem=recv_sems.at[d],
                device_id=peer, device_id_type=MESH_ID)
            cp.start()
            sends.append(cp)
            recvs.append(pltpu.make_async_remote_copy(
                src_ref=v_ref, dst_ref=gath_ref.at[pidx], send_sem=send_sems.at[d], recv_sem=recv_sems.at[d],
                device_id=peer, device_id_type=MESH_ID))
        gath_ref[me] = v_ref[...]
        for cp in recvs:
            cp.wait_recv()
        for cp in sends:
            cp.wait_send()
        total = gath_ref[0]
        for s in range(1, N_DEV):
            total = total + gath_ref[s]
        o_ref[...] = total

    vm = pl.BlockSpec(memory_space=pltpu.VMEM)
    return pl.pallas_call(
        body, name=name, in_specs=[vm], out_specs=vm, out_shape=jax.ShapeDtypeStruct(vec.shape, F32),
        scratch_shapes=[pltpu.VMEM((N_DEV, R, LANE), F32), pltpu.SemaphoreType.DMA((N_DEV - 1,)),
                        pltpu.SemaphoreType.DMA((N_DEV - 1,))],
        compiler_params=pltpu.CompilerParams(vmem_limit_bytes=_vmem(R * LANE * 4 * 12)),
    )(vec)


def _pack(parts):
    flat = jnp.concatenate([p.reshape(-1).astype(F32) for p in parts])
    rows = -(-flat.shape[0] // (LANE * SUBLANE)) * SUBLANE
    return jnp.pad(flat, (0, rows * LANE - flat.shape[0])).reshape(rows, LANE)


def _unpack(packed, shapes):
    flat = packed.reshape(-1)
    out, at = [], 0
    for s in shapes:
        size = int(np.prod(s))
        out.append(flat[at:at + size].reshape(s))
        at += size
    return out


def kernel(x, g_mix, w_in, g_q, g_k, lb_logits, g_hg_out, p_a, p_b, w_o, g_ffn, w_up, conv_w, conv_b, w_down, loss_target, m_g_mix, m_w_in, m_g_q, m_g_k, m_lb_logits, m_g_hg_out, m_p_a, m_p_b, m_w_o, m_g_ffn, m_w_up, m_conv_w, m_conv_b, m_w_down, v_g_mix, v_w_in, v_g_q, v_g_k, v_lb_logits, v_g_hg_out, v_p_a, v_p_b, v_w_o, v_g_ffn, v_w_up, v_conv_w, v_conv_b, v_w_down):
    assert x.shape[0] == 1 and lb_logits.shape[0] == 2
    xs, target = x[0], loss_target[0]
    T, D = xs.shape
    A = p_a.shape[1]
    HW = p_b.shape[1]
    hg_col0 = 3 * A
    gate_col0 = 3 * A + 4 * HW
    F = w_down.shape[1] * N_DEV
    assert w_in.shape[2] * N_DEV == gate_col0 + 2 * D and w_up.shape[2] * N_DEV == 2 * F

    (win_g,) = _gather_two_level([w_in[0].astype(BF16)], "gather_w_in")
    later_weights = _Side([w_up[0].astype(BF16), p_a[0].astype(BF16), p_b[0].astype(BF16), w_o[0].astype(BF16),
                           w_down[0].astype(BF16), conv_w[0]], False, side_by_side=(1, 2))

    u = _rmsnorm_fwd(xs, g_mix, "norm_mix")
    proj = _matmul(u, win_g, mode="nn", b_shards=True, name="proj_in")
    gains = jnp.stack([g_q[0], g_k[0], jnp.ones_like(g_q[0])])[:, None, :]
    qkv = _qkv_prep(proj, gains, A, "qkv_prep")
    ya, ya32, (wup_g, pa_g, pb_g, wo_g, wdown_g, convw_g) = _sb_fwd(qkv, A, "sb_fwd", side=later_weights)
    wo_full = wo_g.reshape(D, D)
    wdown_full = wdown_g.reshape(F, D)
    yb, ob, states = _hg_fwd(proj, lb_logits, g_hg_out, hg_col0, HW, "hg_fwd")
    ma = _matmul(ya, pa_g, mode="nn", name="proj_a")
    mb = _matmul(yb, pb_g, mode="nn", name="proj_b")
    m = _merge_fwd(proj, ma, mb, gate_col0, "merge_fwd")
    h1 = _matmul(m, wo_full, mode="nn", add=xs, name="proj_o")
    u2 = _rmsnorm_fwd(h1, g_ffn, "norm_ffn")
    up = _matmul(u2, wup_g, mode="nn", b_shards=True, name="ffn_up")
    act = _conv_fwd(up, convw_g, conv_b, "conv_fwd")
    y = _matmul(act, wdown_full, mode="nn", add=h1, tn_pref=2048, name="ffn_down")
    dy, dyb, loss_part = _loss_head(y, target, "loss_head")
    loss = lax.psum(loss_part[0, 0], MESH_AXES)

    dact = _matmul(dyb, wdown_full, mode="nt", name="d_act")
    g_wdown = _matmul(act, dyb, mode="tn", out_dtype=BF16, name="g_w_down")
    dup, g_conv = _conv_bwd(dact, up, convw_g, conv_b, "conv_bwd")
    g_wup, (r_wdown,) = _matmul(u2, dup, mode="tn", b_halves=True, out_shards=True, out_dtype=BF16, tm_pref=1024,
                                name="g_w_up",
                                side=_Side([g_wdown.reshape(N_DEV, F // N_DEV, D)], True))
    du2, (r_wup,) = _matmul(dup, wup_g, mode="nt", a_halves=True, b_shards=True, tn_pref=2048, name="d_u2",
                            side=_Side([g_wup], True))
    dh1, dh1b, g_gffn = _rmsnorm_bwd(h1, g_ffn, du2, dy, "norm_ffn_bwd")
    dm = _matmul(dh1b, wo_full, mode="nt", name="d_m")
    g_wo = _matmul(m, dh1b, mode="tn", out_dtype=BF16, tm_pref=1024, name="g_w_o")
    dproj, dmab = _merge_bwd(proj, ma, mb, dm, gate_col0, "merge_bwd")
    dya = _matmul(dmab, pa_g, mode="nt", lead_a=0, out_dtype=BF16, name="d_ya")
    g_pa = _matmul(ya, dmab, mode="tn", out_shards=True, out_dtype=BF16, lead_b=0, name="g_p_a")
    dyb_ = _matmul(dmab, pb_g, mode="nt", lead_a=1, out_dtype=BF16, name="d_yb")
    g_pb = _matmul(yb, dmab, mode="tn", out_shards=True, out_dtype=BF16, lead_b=1, name="g_p_b")
    dproj, g_lb, g_ghg = _hg_bwd(proj, lb_logits, g_hg_out, ob, states, dyb_, dproj, hg_col0, HW, "hg_bwd")
    dq, dk, dv = _sb_bwd(qkv, ya32, dya, A, "sb_bwd")
    dproj, g_gains = _qkv_bwd(proj, gains, dq, dk, dv, dproj, "qkv_bwd")
    g_win, (r_wo, r_pa, r_pb) = _matmul(
        u, dproj, mode="tn", out_shards=True, out_dtype=BF16, tm_pref=1024, name="g_w_in",
        side=_Side([g_wo.reshape(N_DEV, D // N_DEV, D), g_pa, g_pb], True))
    du, (r_win,) = _matmul(dproj, win_g, mode="nt", b_shards=True, tn_pref=2048, name="d_u",
                           side=_Side([g_win], True))
    gx, _, g_gmix = _rmsnorm_bwd(xs, g_mix, du, dh1, "norm_mix_bwd")

    g_convb = g_conv[:, 0].reshape(1, 2 * F)
    g_convw = g_conv[:, 1:4].transpose(1, 0, 2).reshape(3, 2 * F)
    small = [g_gmix, g_gains[0], g_gains[1], g_lb, g_ghg, g_gffn, g_convb, g_convw]
    small_shapes = [p.shape for p in small]
    red = _unpack(_all_reduce_small(_pack(small), "reduce_small"), small_shapes)

    _, _, _, me = _place()
    cs = conv_w.shape[2]
    big = {
        "w_in": _adamw(w_in[0], r_win, m_w_in[0], v_w_in[0], "adamw_w_in"),
        "w_up": _adamw(w_up[0], r_wup, m_w_up[0], v_w_up[0], "adamw_w_up"),
        "p_a": _adamw(p_a[0], r_pa, m_p_a[0], v_p_a[0], "adamw_p_a"),
        "p_b": _adamw(p_b[0], r_pb, m_p_b[0], v_p_b[0], "adamw_p_b"),
        "w_o": _adamw(w_o[0], r_wo, m_w_o[0], v_w_o[0], "adamw_w_o"),
        "w_down": _adamw(w_down[0], r_wdown, m_w_down[0], v_w_down[0], "adamw_w_down"),
        "conv_w": _adamw(conv_w[0], lax.dynamic_slice_in_dim(red[7], me * cs, cs, axis=1)[None],
                         m_conv_w[0], v_conv_w[0], "adamw_conv_w"),
    }
    rep_w = [g_mix, g_q, g_k, lb_logits, g_hg_out, g_ffn, conv_b]
    rep_m = [m_g_mix, m_g_q, m_g_k, m_lb_logits, m_g_hg_out, m_g_ffn, m_conv_b]
    rep_v = [v_g_mix, v_g_q, v_g_k, v_lb_logits, v_g_hg_out, v_g_ffn, v_conv_b]
    rep_shapes = [p.shape for p in rep_w]
    rep_out = _adamw(_pack(rep_w), _pack(red[:7])[None], _pack(rep_m), _pack(rep_v), "adamw_small")
    rep = [_unpack(o, rep_shapes) for o in rep_out]
    rep_names = ["g_mix", "g_q", "g_k", "lb_logits", "g_hg_out", "g_ffn", "conv_b"]

    order = ["g_mix", "w_in", "g_q", "g_k", "lb_logits", "g_hg_out", "p_a", "p_b", "w_o", "g_ffn", "w_up",
             "conv_w", "conv_b", "w_down"]

    def leaf(kind, pname):
        if pname in big:
            return big[pname][kind][None]
        return rep[kind][rep_names.index(pname)]

    return (loss, gx[None], *[leaf(kind, p) for kind in range(4) for p in order])
```

```python
import functools

import numpy as np
import jax
import jax.numpy as jnp
from jax import lax
from jax.experimental import pallas as pl
from jax.experimental.pallas import tpu as pltpu

F32 = jnp.float32
BF16 = jnp.bfloat16

N_DEV = 8
HEAD_DIM = 128
HG_CHUNK = 64
HG_LEVELS = (1, 2, 4, 8, 16, 32)
EPS = 1e-6
ADAM_LR = 0.001
ADAM_B1 = 0.9
ADAM_B2 = 0.999
ADAM_EPS = 1e-08
ADAM_WD = 0.01
ADAM_STEP = 10
LANE = 128
SUBLANE = 8
VMEM_CAP = 56 << 20
MESH_AXES = ("x", "y", "c")
MESH_ID = pl.DeviceIdType.MESH


def _tile(dim, pref, align):
    if dim <= pref:
        return dim
    t = (pref // align) * align
    while t >= align:
        if dim % t == 0:
            return t
        t -= align
    return dim


def _vmem(est_bytes):
    return int(min(max(2 * est_bytes + (8 << 20), 32 << 20), VMEM_CAP))


def _params(sem, est_bytes):
    return pltpu.CompilerParams(dimension_semantics=sem, vmem_limit_bytes=_vmem(est_bytes))


def _sigmoid(x):
    return 1.0 / (1.0 + jnp.exp(-x))


_DIMS = {"nn": (((1,), (0,)), ((), ())), "nt": (((1,), (1,)), ((), ())), "tn": (((0,), (0,)), ((), ()))}


def _matmul(a, b, *, mode, name, out_dtype=F32, add=None, b_shards=False, out_shards=False, lead_a=None,
            lead_b=None, a_halves=False, b_halves=False, side=None, tm_pref=512, tn_pref=1408, tk_pref=2048):
    if mode == "tn":
        K, M = a.shape[-2:]
    else:
        M, K = a.shape[-2:]
    if a_halves:
        assert mode == "nt" and b_shards and lead_a is None
        K = 2 * K
    if lead_b is not None or b_halves:
        assert not b_shards and mode == "tn"
    if b_shards:
        S = b.shape[0]
        if mode == "nn":
            assert b.shape[1] == K
            N, tn, tk = S * b.shape[2], b.shape[2], _tile(K, tk_pref, LANE)
        else:
            assert mode == "nt" and S * b.shape[2] == K
            N, tk, tn = b.shape[1], b.shape[2], _tile(b.shape[1], tn_pref, LANE)
    else:
        if mode == "nt":
            N = b.shape[0]
            assert b.shape[1] == K
        else:
            N = b.shape[-1] * (2 if b_halves else 1)
            assert b.shape[-2] == K
        tn = _tile(N, tn_pref, LANE)
        tk = _tile(K, tk_pref, LANE)
    if out_shards:
        assert N % N_DEV == 0
        tn = N // N_DEV
    per_half_n, per_half_k = (N // 2) // tn, (K // 2) // tk
    assert not b_halves or per_half_n * tn * 2 == N
    assert not a_halves or per_half_k * tk * 2 == K
    tm = _tile(M, tm_pref, LANE)
    nm, nn, nk = M // tm, N // tn, K // tk
    assert nm * tm == M and nn * tn == N and nk * tk == K

    if mode == "tn":
        a_spec = pl.BlockSpec((tk, tm), lambda j, i, k: (k, i))
    elif lead_a is not None:
        a_spec = pl.BlockSpec((None, tm, tk), lambda j, i, k: (lead_a, i, k))
    elif a_halves:
        a_spec = pl.BlockSpec((None, tm, tk), lambda j, i, k: (k // per_half_k, i, k % per_half_k))
    else:
        a_spec = pl.BlockSpec((tm, tk), lambda j, i, k: (i, k))
    if b_halves:
        b_spec = pl.BlockSpec((None, tk, tn), lambda j, i, k: (j // per_half_n, k, j % per_half_n))
    elif lead_b is not None:
        b_spec = pl.BlockSpec((None, tk, tn), lambda j, i, k: (lead_b, k, j))
    elif b_shards and mode == "nn":
        b_spec = pl.BlockSpec((None, tk, tn), lambda j, i, k: (j, k, 0))
    elif b_shards:
        b_spec = pl.BlockSpec((None, tn, tk), lambda j, i, k: (k, j, 0))
    elif mode == "nt":
        b_spec = pl.BlockSpec((tn, tk), lambda j, i, k: (j, k))
    else:
        b_spec = pl.BlockSpec((tk, tn), lambda j, i, k: (k, j))
    in_specs = [a_spec, b_spec]
    operands = [a, b]
    if add is not None:
        assert not out_shards and add.shape == (M, N)
        in_specs.append(pl.BlockSpec((tm, tn), lambda j, i, k: (i, j)))
        operands.append(add)
    if out_shards:
        out_shape = jax.ShapeDtypeStruct((N_DEV, M, tn), out_dtype)
        out_spec = pl.BlockSpec((None, tm, tn), lambda j, i, k: (j, i, 0))
    else:
        out_shape = jax.ShapeDtypeStruct((M, N), out_dtype)
        out_spec = pl.BlockSpec((tm, tn), lambda j, i, k: (i, j))
    dims = _DIMS[mode]
    has_add = add is not None
    n_in = 3 if has_add else 2
    n_side = side.n if side is not None else 0

    def body(*refs):
        a_ref, b_ref = refs[0], refs[1]
        add_ref = refs[2] if has_add else None
        o_ref = refs[n_in + n_side]
        acc_ref = refs[n_in + 2 * n_side + 1]
        k = pl.program_id(2)
        if side is not None:
            side_refs = (refs[n_in:n_in + n_side], refs[n_in + n_side + 1:n_in + 2 * n_side + 1],
                         refs[n_in + 2 * n_side + 2:])
            first, last = _grid_edges((nn, nm, nk))

            @pl.when(first)
            def _():
                side.start(*side_refs)

        part = lax.dot_general(a_ref[...], b_ref[...], dims, preferred_element_type=F32)

        def finish(total):
            if has_add:
                total = add_ref[...] + total
            o_ref[...] = total.astype(out_dtype)

        if nk == 1:
            finish(part)
        else:
            @pl.when(k == 0)
            def _():
                acc_ref[...] = part

            @pl.when(jnp.logical_and(k > 0, k < nk - 1))
            def _():
                acc_ref[...] += part

            @pl.when(k == nk - 1)
            def _():
                finish(acc_ref[...] + part)

        if side is not None:
            @pl.when(last)
            def _():
                side.wait(*side_refs)

    est = 2 * (tm * tk * 2 + tk * tn * 2 + tm * tn * 4 * (2 if has_add else 1)) + tm * tn * 4 * 2
    acc = pltpu.VMEM((tm, tn) if nk > 1 else (SUBLANE, LANE), F32)
    if side is None:
        return pl.pallas_call(
            body, name=name, grid=(nn, nm, nk), in_specs=in_specs, out_specs=out_spec, out_shape=out_shape,
            scratch_shapes=[acc], compiler_params=_params(("parallel", "parallel", "arbitrary"), est),
        )(*operands)
    outs = pl.pallas_call(
        body, name=name, grid=(nn, nm, nk), in_specs=in_specs + side.in_specs(),
        out_specs=[out_spec] + side.out_specs(), out_shape=[out_shape] + side.out_shape(),
        scratch_shapes=[acc] + side.scratch(), compiler_params=_params(("arbitrary",) * 3, est),
    )(*operands, *side.arrays)
    return outs[0], outs[1:]


def _rmsnorm_fwd(x, g, name):
    T, D = x.shape
    tr = _tile(T, 256, SUBLANE)

    def body(x_ref, g_ref, o_ref):
        xf = x_ref[...]
        r = lax.rsqrt(jnp.mean(xf * xf, axis=-1, keepdims=True) + EPS)
        o_ref[...] = ((xf * r) * g_ref[...]).astype(BF16)

    return pl.pallas_call(
        body, name=name, grid=(T // tr,),
        in_specs=[pl.BlockSpec((tr, D), lambda i: (i, 0)), pl.BlockSpec((1, D), lambda i: (0, 0))],
        out_specs=pl.BlockSpec((tr, D), lambda i: (i, 0)), out_shape=jax.ShapeDtypeStruct((T, D), BF16),
        compiler_params=_params(("parallel",), tr * D * 6 * 2),
    )(x, g)


def _rmsnorm_bwd(x, g, dy, res, name):
    T, D = x.shape
    tr = _tile(T, 256, SUBLANE)
    nsteps = T // tr

    def body(x_ref, g_ref, dy_ref, res_ref, dx_ref, dxb_ref, dg_ref, acc_ref):
        i = pl.program_id(0)
        xf = x_ref[...]
        dyf = dy_ref[...].astype(F32)
        r = lax.rsqrt(jnp.mean(xf * xf, axis=-1, keepdims=True) + EPS)
        w = dyf * g_ref[...]
        s = jnp.mean(xf * w, axis=-1, keepdims=True)
        dx = res_ref[...] + (r * w - xf * (r * r * r * s))
        dx_ref[...] = dx
        dxb_ref[...] = dx.astype(BF16)
        part = jnp.sum((dyf * (xf * r)).reshape(tr // SUBLANE, SUBLANE, D), axis=0)

        @pl.when(i == 0)
        def _():
            acc_ref[...] = part

        @pl.when(i > 0)
        def _():
            acc_ref[...] += part

        @pl.when(i == nsteps - 1)
        def _():
            dg_ref[...] = jnp.sum(acc_ref[...], axis=0, keepdims=True)

    row = pl.BlockSpec((tr, D), lambda i: (i, 0))
    vec = pl.BlockSpec((1, D), lambda i: (0, 0))
    return pl.pallas_call(
        body, name=name, grid=(nsteps,), in_specs=[row, vec, row, row], out_specs=[row, row, vec],
        out_shape=[jax.ShapeDtypeStruct((T, D), F32), jax.ShapeDtypeStruct((T, D), BF16),
                   jax.ShapeDtypeStruct((1, D), F32)],
        scratch_shapes=[pltpu.VMEM((SUBLANE, D), F32)],
        compiler_params=_params(("arbitrary",), tr * D * 18 * 2),
    )(x, g, dy, res)


def _qkv_prep(proj, gains, width, name):
    T = proj.shape[0]
    tr = _tile(T, 256, SUBLANE)
    heads = width // HEAD_DIM

    def body(p_ref, g_ref, o_ref):
        j = pl.program_id(1)
        g = g_ref[...]
        for h in range(heads):
            xh = p_ref[:, h * HEAD_DIM:(h + 1) * HEAD_DIM]
            r = lax.rsqrt(jnp.mean(xh * xh, axis=-1, keepdims=True) + EPS)
            r = jnp.where(j < 2, r, 1.0)
            o_ref[:, h * HEAD_DIM:(h + 1) * HEAD_DIM] = ((xh * r) * g).astype(BF16)

    return pl.pallas_call(
        body, name=name, grid=(T // tr, 3),
        in_specs=[pl.BlockSpec((tr, width), lambda i, j: (i, j)),
                  pl.BlockSpec((None, 1, HEAD_DIM), lambda i, j: (j, 0, 0))],
        out_specs=pl.BlockSpec((tr, width), lambda i, j: (i, j)),
        out_shape=jax.ShapeDtypeStruct((T, 3 * width), BF16),
        compiler_params=_params(("parallel", "parallel"), tr * width * 6 * 2),
    )(proj, gains)


def _qkv_bwd(proj, gains, dq, dk, dv, dproj, name):
    T = proj.shape[0]
    width = dq.shape[1]
    tr = _tile(T, 256, SUBLANE)
    heads = width // HEAD_DIM
    nrow = T // tr

    def body(p_ref, g_ref, dq_ref, dk_ref, dv_ref, dp_in, o_ref, dg_ref, acc_ref):
        del dp_in
        i = pl.program_id(0)
        j = pl.program_id(1)
        g = g_ref[...]

        @pl.when(jnp.logical_and(i == 0, j == 0))
        def _():
            acc_ref[...] = jnp.zeros_like(acc_ref)

        part = jnp.zeros((SUBLANE, HEAD_DIM), F32)
        for h in range(heads):
            sl = slice(h * HEAD_DIM, (h + 1) * HEAD_DIM)
            xh = p_ref[:, sl]
            dyh = jnp.where(j == 0, dq_ref[:, sl], jnp.where(j == 1, dk_ref[:, sl], dv_ref[:, sl]))
            r = lax.rsqrt(jnp.mean(xh * xh, axis=-1, keepdims=True) + EPS)
            w = dyh * g
            s = jnp.mean(xh * w, axis=-1, keepdims=True)
            dx = r * w - xh * (r * r * r * s)
            o_ref[:, sl] = jnp.where(j < 2, dx, dyh).astype(BF16)
            part = part + jnp.sum((dyh * (xh * r)).reshape(tr // SUBLANE, SUBLANE, HEAD_DIM), axis=0)
        acc_ref[j] += part

        @pl.when(jnp.logical_and(i == nrow - 1, j == 2))
        def _():
            dg_ref[...] = jnp.sum(acc_ref[...], axis=1, keepdims=True)

    blk = pl.BlockSpec((tr, width), lambda i, j: (i, j))
    dblk = pl.BlockSpec((tr, width), lambda i, j: (i, 0))
    return pl.pallas_call(
        body, name=name, grid=(nrow, 3),
        in_specs=[blk, pl.BlockSpec((None, 1, HEAD_DIM), lambda i, j: (j, 0, 0)), dblk, dblk, dblk,
                  pl.BlockSpec(memory_space=pl.ANY)],
        out_specs=[blk, pl.BlockSpec((3, 1, HEAD_DIM), lambda i, j: (0, 0, 0))],
        out_shape=[jax.ShapeDtypeStruct(dproj.shape, BF16), jax.ShapeDtypeStruct((3, 1, HEAD_DIM), F32)],
        input_output_aliases={5: 0},
        scratch_shapes=[pltpu.VMEM((3, SUBLANE, HEAD_DIM), F32)],
        compiler_params=_params(("arbitrary", "arbitrary"), tr * width * 18 * 2),
    )(proj, gains, dq, dk, dv, dproj)


def _split2(x):
    hi = x.astype(BF16)
    lo = (x - hi.astype(F32)).astype(BF16)
    return hi, lo


LOG2E = 1.4426950408889634


def _sb_logits(q, kb):
    return lax.dot_general(q, kb, _DIMS["nt"], preferred_element_type=F32) * (HEAD_DIM ** -0.5 * LOG2E)


def _sb_scores(z2, mask):
    lk = -(jnp.maximum(z2, 0.0) + jnp.log2(1.0 + jnp.exp2(-jnp.abs(z2))))
    if mask is not None:
        lk = jnp.where(mask, lk, 0.0)
    return lk, lk + z2


def _sb_block_size(T):
    return _tile(T, 256, LANE)


SB_HEADS_PER_STEP = 2


def _sb_later(lk, upper, carry):
    hi, lo = _split2(lk)
    return (jnp.dot(hi, upper, preferred_element_type=F32) + jnp.dot(lo, upper, preferred_element_type=F32)) + carry


def _sb_fwd(qkv, width, name, side=None):
    T = qkv.shape[0]
    heads = width // HEAD_DIM
    bq = _sb_block_size(T)
    hps = min(2 * SB_HEADS_PER_STEP, heads)
    assert heads % hps == 0
    groups, W = heads // hps, hps * HEAD_DIM

    n_side = side.n if side is not None else 0

    def body(*refs):
        q_ref, k_ref, v_ref = refs[:3]
        o_ref, o32_ref = refs[3 + n_side:5 + n_side]
        if side is not None:
            side_refs = (refs[3:3 + n_side], refs[5 + n_side:5 + 2 * n_side], refs[5 + 2 * n_side:])
            first, last = _grid_edges((groups, T // bq))

            @pl.when(first)
            def _():
                side.start(*side_refs)

        i = pl.program_id(1)
        row = lax.broadcasted_iota(jnp.int32, (bq, bq), 0)
        col = lax.broadcasted_iota(jnp.int32, (bq, bq), 1)
        upper = (row > col).astype(BF16)
        causal = col < row

        def head_block(start, h, carry, acc, mask):
            cols = slice(h * HEAD_DIM, (h + 1) * HEAD_DIM)
            z2 = _sb_logits(q_ref[:, cols], k_ref[pl.ds(start, bq), cols])
            yield
            lk, lb = _sb_scores(z2, mask)
            later = _sb_later(lk, upper, carry)
            yield
            w = jnp.exp2(lb + later)
            if mask is not None:
                w = jnp.where(mask, w, 0.0)
            pv = jnp.dot(w.astype(BF16), v_ref[pl.ds(start, bq), cols], preferred_element_type=F32)
            yield
            return carry + jnp.sum(lk, axis=1, keepdims=True), acc + pv

        def block(j, state, mask):
            start = pl.multiple_of(j * bq, bq)
            return tuple(_lockstep(head_block(start, h, state[h][0], state[h][1], mask) for h in range(hps)))

        zero = (jnp.zeros((bq, 1), F32), jnp.zeros((bq, HEAD_DIM), F32))
        state = block(i, (zero,) * hps, causal)
        state = lax.fori_loop(0, i, lambda n, st: block(i - 1 - n, st, None), state)
        for h in range(hps):
            cols = slice(h * HEAD_DIM, (h + 1) * HEAD_DIM)
            o_ref[:, cols] = state[h][1].astype(BF16)
            o32_ref[:, cols] = state[h][1]

        if side is not None:
            @pl.when(last)
            def _():
                side.wait(*side_refs)

    oblk = pl.BlockSpec((bq, W), lambda g, i: (i, g))
    in_specs = [oblk, pl.BlockSpec((T, W), lambda g, i: (0, groups + g)),
                pl.BlockSpec((T, W), lambda g, i: (0, 2 * groups + g))]
    out_shape = [jax.ShapeDtypeStruct((T, width), BF16), jax.ShapeDtypeStruct((T, width), F32)]
    est = 2 * T * W * 2 * 2 + hps * 16 * bq * bq * 4
    if side is None:
        return pl.pallas_call(
            body, name=name, grid=(groups, T // bq), in_specs=in_specs, out_specs=[oblk, oblk], out_shape=out_shape,
            compiler_params=_params(("parallel", "arbitrary"), est),
        )(qkv, qkv, qkv)
    outs = pl.pallas_call(
        body, name=name, grid=(groups, T // bq), in_specs=in_specs + side.in_specs(),
        out_specs=[oblk, oblk] + side.out_specs(), out_shape=out_shape + side.out_shape(),
        scratch_shapes=side.scratch(), compiler_params=_params(("arbitrary", "arbitrary"), est),
    )(qkv, qkv, qkv, *side.arrays)
    return outs[0], outs[1], outs[2:]


def _sb_bwd(qkv, y, dy, width, name):
    T = qkv.shape[0]
    heads = width // HEAD_DIM
    bq = _sb_block_size(T)
    nq = T // bq
    scale = HEAD_DIM ** -0.5
    hps = min(2 * SB_HEADS_PER_STEP, heads)
    groups, W = heads // hps, hps * HEAD_DIM

    def body(q_ref, kv_hbm, y_ref, dy_ref, dq_ref, dk_hbm, dv_hbm, dk_acc, dv_acc, kbuf, vbuf, kv_sems, out_sems):
        g_id = pl.program_id(0)
        i = pl.program_id(1)
        kcol = pl.multiple_of((groups + g_id) * W, W)
        vcol = pl.multiple_of((2 * groups + g_id) * W, W)

        def fetch(j, slot):
            rows = pl.ds(pl.multiple_of(j * bq, bq), bq)
            return (pltpu.make_async_copy(kv_hbm.at[rows, pl.ds(kcol, W)], kbuf.at[slot], kv_sems.at[0, slot]),
                    pltpu.make_async_copy(kv_hbm.at[rows, pl.ds(vcol, W)], vbuf.at[slot], kv_sems.at[1, slot]))

        for cp in fetch(i, 0):
            cp.start()
        row = lax.broadcasted_iota(jnp.int32, (bq, bq), 0)
        col = lax.broadcasted_iota(jnp.int32, (bq, bq), 1)
        upper = (row > col).astype(BF16)
        upper_incl = (row >= col).astype(BF16)
        causal = col < row
        deltas = []
        for h in range(hps):
            cols = slice(h * HEAD_DIM, (h + 1) * HEAD_DIM)
            deltas.append(jnp.sum(dy_ref[:, cols].astype(F32) * y_ref[:, cols], axis=1, keepdims=True))

        @pl.when(i == 0)
        def _():
            dk_acc[...] = jnp.zeros_like(dk_acc)
            dv_acc[...] = jnp.zeros_like(dv_acc)

        def head_block(start, slot, h, carry, gcarry, dq, mask):
            cols = slice(h * HEAD_DIM, (h + 1) * HEAD_DIM)
            q = q_ref[:, cols]
            do = dy_ref[:, cols]
            kb = kbuf[slot, :, cols]
            vb = vbuf[slot, :, cols]
            z2 = _sb_logits(q, kb)
            dw = lax.dot_general(do, vb, _DIMS["nt"], preferred_element_type=F32)
            yield
            lk, lb = _sb_scores(z2, mask)
            later = _sb_later(lk, upper, carry)
            yield
            w = jnp.exp2(lb + later)
            if mask is not None:
                w = jnp.where(mask, w, 0.0)
            wb = w.astype(BF16)
            gw = dw * wb.astype(F32)
            gsuf = _sb_later(gw, upper_incl, gcarry)
            dvp = lax.dot_general(wb, do, _DIMS["tn"], preferred_element_type=F32)
            yield
            dz = gw - jnp.exp2(lb) * (gw + (deltas[h] - gsuf))
            if mask is not None:
                dz = jnp.where(mask, dz, 0.0)
            dzs = (dz * scale).astype(BF16)
            dqp = jnp.dot(dzs, kb, preferred_element_type=F32)
            dkp = lax.dot_general(dzs, q, _DIMS["tn"], preferred_element_type=F32)
            yield
            dk_acc[pl.ds(start, bq), cols] += dkp
            dv_acc[pl.ds(start, bq), cols] += dvp
            return (carry + jnp.sum(lk, axis=1, keepdims=True), gcarry + jnp.sum(gw, axis=1, keepdims=True), dq + dqp)

        def block(n, state, mask):
            j = i - n
            slot = n % 2
            for cp in fetch(j, slot):
                cp.wait()

            @pl.when(n < i)
            def _():
                for cp in fetch(j - 1, 1 - slot):
                    cp.start()

            start = pl.multiple_of(j * bq, bq)
            return tuple(_lockstep(head_block(start, slot, h, *state[h], mask) for h in range(hps)))

        zero = jnp.zeros((bq, 1), F32)
        state = block(0, ((zero, zero, jnp.zeros((bq, HEAD_DIM), F32)),) * hps, causal)
        state = lax.fori_loop(1, i + 1, lambda n, st: block(n, st, None), state)
        for h in range(hps):
            dq_ref[:, h * HEAD_DIM:(h + 1) * HEAD_DIM] = state[h][2]

        @pl.when(i == nq - 1)
        def _():
            cols = pl.ds(pl.multiple_of(g_id * W, W), W)
            copies = [pltpu.make_async_copy(dk_acc, dk_hbm.at[:, cols], out_sems.at[0]),
                      pltpu.make_async_copy(dv_acc, dv_hbm.at[:, cols], out_sems.at[1])]
            for cp in copies:
                cp.start()
            for cp in copies:
                cp.wait()

    qblk = pl.BlockSpec((bq, W), lambda g, i: (i, g))
    out = jax.ShapeDtypeStruct((T, width), F32)
    return pl.pallas_call(
        body, name=name, grid=(groups, nq),
        in_specs=[qblk, pl.BlockSpec(memory_space=pl.ANY), qblk, qblk],
        out_specs=[qblk, pl.BlockSpec(memory_space=pl.ANY), pl.BlockSpec(memory_space=pl.ANY)],
        out_shape=[out, out, out],
        scratch_shapes=[pltpu.VMEM((T, W), F32), pltpu.VMEM((T, W), F32), pltpu.VMEM((2, bq, W), BF16),
                        pltpu.VMEM((2, bq, W), BF16), pltpu.SemaphoreType.DMA((2, 2)), pltpu.SemaphoreType.DMA((2,))],
        compiler_params=_params(("arbitrary", "arbitrary"), 2 * T * W * 4),
    )(qkv, qkv, y, dy)


def _hg_constants():
    C = HG_CHUNK
    t = np.arange(C)[:, None]
    j = np.arange(C)[None, :]
    blocks = [(j <= t), (j > t)]
    masks = []
    for n in HG_LEVELS:
        right = (t % (2 * n)) >= n
        mid = (t // (2 * n)) * (2 * n) + n - 1
        blocks.append(right & (j > mid) & (j <= t))
        blocks.append((~right) & (j > t) & (j <= mid))
        tt, ss = np.arange(C)[:, None], np.arange(C)[None, :]
        same = (tt // (2 * n)) == (ss // (2 * n))
        masks.append(same & ((tt % (2 * n)) >= n) & ((ss % (2 * n)) < n))
    sums = np.concatenate(blocks, axis=0).astype(np.float32)
    return jnp.asarray(sums, BF16), jnp.asarray(np.stack(masks).astype(np.float32))


def _split3(x):
    hi = x.astype(BF16)
    r1 = x - hi.astype(F32)
    mid = r1.astype(BF16)
    lo = (r1 - mid.astype(F32)).astype(BF16)
    return hi, mid, lo


def _hg_gates(hq, hf, lb):
    sq = _sigmoid(hq)
    sf = _sigmoid(hf)
    f = lb + (1.0 - lb) * sf
    return hq * sq, sq, f, sf, 1.0 - f, jnp.log(f)


def _hg_exponents(sums, g):
    hi, mid, lo = _split3(g)
    return (jnp.dot(sums, hi, preferred_element_type=F32) + jnp.dot(sums, mid, preferred_element_type=F32)
            + jnp.dot(sums, lo, preferred_element_type=F32))


def _hg_level_scores(q, k, e_all):
    C = HG_CHUNK
    parts, prods = [], []
    for li in range(len(HG_LEVELS)):
        eq = jnp.exp(e_all[(2 + 2 * li) * C:(3 + 2 * li) * C])
        ek = jnp.exp(e_all[(3 + 2 * li) * C:(4 + 2 * li) * C])
        qt = q * eq
        kt = k * ek
        prods.append(lax.dot_general(qt.astype(BF16), kt.astype(BF16), _DIMS["nt"], preferred_element_type=F32))
        parts.append((eq, ek, qt, kt))
    return parts, prods


def _hg_intra(prods, masks_ref):
    a = masks_ref[0] * prods[0]
    for li in range(1, len(HG_LEVELS)):
        a = a + masks_ref[li] * prods[li]
    return a


def _lockstep(gens):
    gens = list(gens)
    results = [None] * len(gens)
    alive = list(range(len(gens)))
    while alive:
        for idx in list(alive):
            try:
                next(gens[idx])
            except StopIteration as done:
                results[idx] = done.value
                alive.remove(idx)
    return results


def _lower_bound(lbl_ref, cols):
    return _sigmoid(lbl_ref[0:1, cols] - lbl_ref[1:2, cols])


HG_HEADS_PER_STEP = 4


def _hg_fwd(proj, lb_logits, g_out, col0, width, name):
    T = proj.shape[0]
    heads = width // HEAD_DIM
    C = HG_CHUNK
    tb = _tile(T, 512, C)
    cpb = tb // C
    nb = T // tb
    sums, masks = _hg_constants()
    hps = min(HG_HEADS_PER_STEP, heads)
    groups, W = heads // hps, hps * HEAD_DIM
    assert col0 % W == 0 and width % W == 0
    cb = col0 // W

    def body(hq_ref, hf_ref, hi_ref, og_ref, lbl_ref, go_ref, sums_ref, masks_ref, y_ref, o_ref, st_ref, s_ref):
        i = pl.program_id(1)

        @pl.when(i == 0)
        def _():
            s_ref[...] = jnp.zeros_like(s_ref)

        go = go_ref[...]

        def chunk_head(c, rows, h):
            cols = slice(h * HEAD_DIM, (h + 1) * HEAD_DIM)
            lb = _lower_bound(lbl_ref, cols)
            q, _, _, _, k, g = _hg_gates(hq_ref[rows, cols], hf_ref[rows, cols], lb)
            v = hi_ref[rows, cols]
            vb = v.astype(BF16)
            st = s_ref[h]
            st_ref[h, c] = st
            e_all = _hg_exponents(sums_ref[...], g)
            yield
            b = e_all[0:C]
            ebl = jnp.exp(b[C - 1:C, :])
            qe = q * jnp.exp(b)
            o_inter = lax.dot_general(qe.astype(BF16), st.astype(BF16), _DIMS["nt"], preferred_element_type=F32)
            kd = k * jnp.exp(e_all[C:2 * C])
            s_new = lax.dot_general(vb, kd.astype(BF16), _DIMS["tn"], preferred_element_type=F32)
            _, prods = _hg_level_scores(q, k, e_all)
            yield
            a = _hg_intra(prods, masks_ref)
            o_intra = jnp.dot(a.astype(BF16), vb, preferred_element_type=F32)
            s_ref[h] = st * ebl + s_new
            yield
            o = (o_inter + o_intra) + jnp.sum(q * k, axis=1, keepdims=True) * v
            o_ref[rows, cols] = o
            r = lax.rsqrt(jnp.mean(o * o, axis=-1, keepdims=True) + EPS)
            og = og_ref[rows, cols]
            y_ref[rows, cols] = (((o * r) * go) * (og * _sigmoid(og))).astype(BF16)

        def chunk(c, _):
            rows = pl.ds(pl.multiple_of(c * C, C), C)
            _lockstep(chunk_head(c, rows, h) for h in range(hps))
            return 0

        lax.fori_loop(0, cpb, chunk, 0)

    def col(k):
        return pl.BlockSpec((tb, W), lambda g, i: (i, cb + k * groups + g))

    nsum = sums.shape[0]
    out_blk = pl.BlockSpec((tb, W), lambda g, i: (i, g))
    return pl.pallas_call(
        body, name=name, grid=(groups, nb),
        in_specs=[col(0), col(1), col(2), col(3),
                  pl.BlockSpec((2, W), lambda g, i: (0, g)),
                  pl.BlockSpec((1, HEAD_DIM), lambda g, i: (0, 0)),
                  pl.BlockSpec((nsum, C), lambda g, i: (0, 0)),
                  pl.BlockSpec((len(HG_LEVELS), C, C), lambda g, i: (0, 0, 0))],
        out_specs=[out_blk, out_blk,
                   pl.BlockSpec((hps, cpb, HEAD_DIM, HEAD_DIM), lambda g, i: (g, i, 0, 0))],
        out_shape=[jax.ShapeDtypeStruct((T, width), BF16), jax.ShapeDtypeStruct((T, width), F32),
                   jax.ShapeDtypeStruct((heads, T // C, HEAD_DIM, HEAD_DIM), F32)],
        scratch_shapes=[pltpu.VMEM((hps, HEAD_DIM, HEAD_DIM), F32)],
        compiler_params=_params(("parallel", "arbitrary"), tb * W * 4 * 7 + hps * cpb * HEAD_DIM * HEAD_DIM * 4),
    )(proj, proj, proj, proj, lb_logits, g_out, sums, masks)


def _hg_bwd(proj, lb_logits, g_out, o_saved, states, dy, dproj, col0, width, name):
    T = proj.shape[0]
    heads = width // HEAD_DIM
    C = HG_CHUNK
    tb = _tile(T, 512, C)
    cpb = tb // C
    nb = T // tb
    sums, masks = _hg_constants()
    nsum = sums.shape[0]
    nlev = len(HG_LEVELS)
    hps = min(HG_HEADS_PER_STEP, heads)
    groups, W = heads // hps, hps * HEAD_DIM
    cb = col0 // W

    def body(hq_ref, hf_ref, hi_ref, og_ref, lbl_ref, go_ref, sums_ref, masks_ref, o_ref, st_ref, dy_ref, dp_in,
             dp_ref, dlb_ref, dgo_ref, ds_ref, de_ref, dlb_acc, dgo_acc, dout_ref, out_sems):
        del dp_in
        grp = pl.program_id(0)
        i = pl.program_id(1)

        @pl.when(i == 0)
        def _():
            ds_ref[...] = jnp.zeros_like(ds_ref)
            dlb_acc[...] = jnp.zeros_like(dlb_acc)

        @pl.when(jnp.logical_and(grp == 0, i == 0))
        def _():
            dgo_acc[...] = jnp.zeros_like(dgo_acc)

        go = go_ref[...]
        last_row = lax.broadcasted_iota(jnp.int32, (C, HEAD_DIM), 0) == C - 1

        def chunk(n, _):
            c = cpb - 1 - n
            rows = pl.ds(pl.multiple_of(c * C, C), C)
            _lockstep(chunk_head(c, rows, h) for h in range(hps))
            return 0

        def chunk_head(c, rows, h):
            cols = slice(h * HEAD_DIM, (h + 1) * HEAD_DIM)
            de_h = de_ref.at[h]
            lb = _lower_bound(lbl_ref, cols)
            hq = hq_ref[rows, cols]
            og = og_ref[rows, cols]
            q, sq, f, sf, k, g = _hg_gates(hq, hf_ref[rows, cols], lb)
            v = hi_ref[rows, cols]
            vb = v.astype(BF16)
            st = st_ref[h, c]
            stb = st.astype(BF16)
            dst = ds_ref[h]
            dstb = dst.astype(BF16)
            o = o_ref[rows, cols]
            dyc = dy_ref[rows, cols].astype(F32)
            sg = _sigmoid(og)
            r = lax.rsqrt(jnp.mean(o * o, axis=-1, keepdims=True) + EPS)
            on = (o * r) * go
            don = dyc * (og * sg)
            dout_ref[3, rows, cols] = (dyc * on * (sg * (1.0 + og * (1.0 - sg)))).astype(BF16)
            dgo_acc[...] += jnp.sum((don * (o * r)).reshape(C // SUBLANE, SUBLANE, HEAD_DIM), axis=0)
            wn = don * go
            do = r * wn - o * (r * r * r * jnp.mean(o * wn, axis=-1, keepdims=True))
            dob = do.astype(BF16)
            e_all = _hg_exponents(sums_ref[...], g)
            da = lax.dot_general(dob, vb, _DIMS["nt"], preferred_element_type=F32)
            dqe = jnp.dot(dob, stb, preferred_element_type=F32)
            dkd = jnp.dot(vb, dstb, preferred_element_type=F32)
            yield
            eb = jnp.exp(e_all[0:C])
            esuf = jnp.exp(e_all[C:2 * C])
            ebl = eb[C - 1:C, :]
            qe = q * eb
            kd = k * esuf
            parts, prods = _hg_level_scores(q, k, e_all)
            dv_state = lax.dot_general(kd.astype(BF16), dstb, _DIMS["nt"], preferred_element_type=F32)
            ds_new = lax.dot_general(dob, qe.astype(BF16), _DIMS["tn"], preferred_element_type=F32)
            yield
            a = _hg_intra(prods, masks_ref)
            dv = lax.dot_general(a.astype(BF16), dob, _DIMS["tn"], preferred_element_type=F32)
            dlev = []
            for li in range(nlev):
                _, _, qt, kt = parts[li]
                dan = (masks_ref[li] * da).astype(BF16)
                dlev.append((jnp.dot(dan, kt.astype(BF16), preferred_element_type=F32),
                             lax.dot_general(dan, qt.astype(BF16), _DIMS["tn"], preferred_element_type=F32)))
            yield
            qk = jnp.sum(q * k, axis=1, keepdims=True)
            dv = dv + qk * do + dv_state
            dqk = jnp.sum(do * v, axis=1, keepdims=True)
            dq = dqk * k
            dk = dqk * q
            for li in range(nlev):
                eq, ek, qt, kt = parts[li]
                dqt, dkt = dlev[li]
                dq = dq + dqt * eq
                dk = dk + dkt * ek
                de_h[(2 + 2 * li) * C:(3 + 2 * li) * C, :] = dqt * qt
                de_h[(3 + 2 * li) * C:(4 + 2 * li) * C, :] = dkt * kt
            dq = dq + dqe * eb
            dk = dk + dkd * esuf
            debl = jnp.sum(dst * st, axis=0, keepdims=True)
            de_h[0:C, :] = dqe * qe + jnp.where(last_row, debl * ebl, 0.0)
            de_h[C:2 * C, :] = dkd * kd
            ds_ref[h] = dst * ebl + ds_new
            dehi, delo = _split2(de_h[...])
            dg = (lax.dot_general(sums_ref[...], dehi, _DIMS["tn"], preferred_element_type=F32)
                  + lax.dot_general(sums_ref[...], delo, _DIMS["tn"], preferred_element_type=F32))
            yield
            df = dg / f - dk
            dout_ref[0, rows, cols] = (dq * (sq * (1.0 + hq * (1.0 - sq)))).astype(BF16)
            dout_ref[1, rows, cols] = (df * (1.0 - lb) * (sf * (1.0 - sf))).astype(BF16)
            dout_ref[2, rows, cols] = dv.astype(BF16)
            dlb_acc[:, cols] += jnp.sum((df * (1.0 - sf)).reshape(C // SUBLANE, SUBLANE, HEAD_DIM), axis=0)

        lax.fori_loop(0, cpb, chunk, 0)

        rows = pl.ds(pl.multiple_of((nb - 1 - i) * tb, tb), tb)
        copies = [pltpu.make_async_copy(
            dout_ref.at[k], dp_ref.at[rows, pl.ds(pl.multiple_of((cb + k * groups + grp) * W, W), W)], out_sems.at[k])
            for k in range(4)]
        for cp in copies:
            cp.start()
        for cp in copies:
            cp.wait()

        @pl.when(i == nb - 1)
        def _():
            lb = _lower_bound(lbl_ref, slice(None))
            dl0 = jnp.sum(dlb_acc[...], axis=0, keepdims=True) * (lb * (1.0 - lb))
            dlb_ref[0:1, :] = dl0
            dlb_ref[1:2, :] = -dl0

        @pl.when(jnp.logical_and(grp == groups - 1, i == nb - 1))
        def _():
            dgo_ref[...] = jnp.sum(dgo_acc[...], axis=0, keepdims=True)

    def col(k):
        return pl.BlockSpec((tb, W), lambda g, i: (nb - 1 - i, cb + k * groups + g))

    rev = pl.BlockSpec((tb, W), lambda g, i: (nb - 1 - i, g))
    return pl.pallas_call(
        body, name=name, grid=(groups, nb),
        in_specs=[col(0), col(1), col(2), col(3),
                  pl.BlockSpec((2, W), lambda g, i: (0, g)),
                  pl.BlockSpec((1, HEAD_DIM), lambda g, i: (0, 0)),
                  pl.BlockSpec((nsum, C), lambda g, i: (0, 0)),
                  pl.BlockSpec((nlev, C, C), lambda g, i: (0, 0, 0)),
                  rev,
                  pl.BlockSpec((hps, cpb, HEAD_DIM, HEAD_DIM), lambda g, i: (g, nb - 1 - i, 0, 0)),
                  rev,
                  pl.BlockSpec(memory_space=pl.ANY)],
        out_specs=[pl.BlockSpec(memory_space=pl.ANY),
                   pl.BlockSpec((2, W), lambda g, i: (0, g)),
                   pl.BlockSpec((1, HEAD_DIM), lambda g, i: (0, 0))],
        out_shape=[jax.ShapeDtypeStruct(dproj.shape, BF16), jax.ShapeDtypeStruct((2, width), F32),
                   jax.ShapeDtypeStruct((1, HEAD_DIM), F32)],
        input_output_aliases={11: 0},
        scratch_shapes=[pltpu.VMEM((hps, HEAD_DIM, HEAD_DIM), F32), pltpu.VMEM((hps, nsum, HEAD_DIM), F32),
                        pltpu.VMEM((SUBLANE, W), F32), pltpu.VMEM((SUBLANE, HEAD_DIM), F32),
                        pltpu.VMEM((4, tb, W), BF16), pltpu.SemaphoreType.DMA((4,))],
        compiler_params=_params(("arbitrary", "arbitrary"), tb * W * 4 * 12 + hps * cpb * HEAD_DIM * HEAD_DIM * 4),
    )(proj, proj, proj, proj, lb_logits, g_out, sums, masks, o_saved, states, dy, dproj)


def _merge_fwd(proj, ma, mb, gate_col0, name):
    T, D = ma.shape
    tr = _tile(T, 256, SUBLANE)
    cw = _tile(D, 1024, LANE)
    nj = D // cw
    assert gate_col0 % cw == 0
    g0 = gate_col0 // cw

    def body(ga_ref, gb_ref, ma_ref, mb_ref, o_ref):
        o_ref[...] = (_sigmoid(ga_ref[...]) * ma_ref[...] + _sigmoid(gb_ref[...]) * mb_ref[...]).astype(BF16)

    blk = pl.BlockSpec((tr, cw), lambda i, j: (i, j))
    return pl.pallas_call(
        body, name=name, grid=(T // tr, nj),
        in_specs=[pl.BlockSpec((tr, cw), lambda i, j: (i, g0 + j)),
                  pl.BlockSpec((tr, cw), lambda i, j: (i, g0 + nj + j)), blk, blk],
        out_specs=blk, out_shape=jax.ShapeDtypeStruct((T, D), BF16),
        compiler_params=_params(("parallel", "parallel"), tr * cw * 18 * 2),
    )(proj, proj, ma, mb)


def _merge_bwd(proj, ma, mb, dm, gate_col0, name):
    T, D = ma.shape
    tr = _tile(T, 256, SUBLANE)
    cw = _tile(D, 1024, LANE)
    nj = D // cw
    g0 = gate_col0 // cw

    def body(g_ref, ma_ref, mb_ref, dm_ref, dp_ref, dmm_ref):
        branch = pl.program_id(2)
        s = _sigmoid(g_ref[...])
        dmv = dm_ref[...]
        mm = jnp.where(branch == 0, ma_ref[...], mb_ref[...])
        dp_ref[...] = (dmv * mm * (s * (1.0 - s))).astype(BF16)
        dmm_ref[...] = (dmv * s).astype(BF16)

    blk = pl.BlockSpec((tr, cw), lambda i, j, b: (i, j))
    gate = pl.BlockSpec((tr, cw), lambda i, j, b: (i, g0 + b * nj + j))
    return pl.pallas_call(
        body, name=name, grid=(T // tr, nj, 2),
        in_specs=[gate, blk, blk, blk],
        out_specs=[gate, pl.BlockSpec((None, tr, cw), lambda i, j, b: (b, i, j))],
        out_shape=[jax.ShapeDtypeStruct(proj.shape, BF16), jax.ShapeDtypeStruct((2, T, D), BF16)],
        compiler_params=_params(("parallel", "parallel", "arbitrary"), tr * cw * 20 * 2),
    )(proj, ma, mb, dm)


def _conv_fwd(up, convw, convb, name):
    T, F2 = up.shape
    tc = convw.shape[2]
    half = (F2 // 2) // tc
    assert half * tc * 2 == F2
    tr = _tile(T, 256, SUBLANE)
    hb = tr // SUBLANE

    def body(g_ref, gp_ref, v_ref, vp_ref, wg_ref, wv_ref, bg_ref, bv_ref, o_ref, gx_ref, vx_ref):
        first = pl.program_id(1) == 0

        def lanes(c, _):
            cols = pl.ds(pl.multiple_of(c * LANE, LANE), LANE)

            def conv(cur_ref, prev_ref, w_ref, b_ref, x_ref):
                x_ref[0:SUBLANE, :] = jnp.where(first, 0.0, prev_ref[:, cols])
                x_ref[SUBLANE:, :] = cur_ref[:, cols]
                w = w_ref[:, cols]
                out = b_ref[:, cols] + w[0:1, :] * x_ref[pl.ds(SUBLANE - 2, tr), :]
                out = out + w[1:2, :] * x_ref[pl.ds(SUBLANE - 1, tr), :]
                return out + w[2:3, :] * x_ref[pl.ds(SUBLANE, tr), :]

            gate = conv(g_ref, gp_ref, wg_ref, bg_ref, gx_ref)
            val = conv(v_ref, vp_ref, wv_ref, bv_ref, vx_ref)
            o_ref[:, cols] = ((gate * _sigmoid(gate)) * val).astype(BF16)
            return 0

        lax.fori_loop(0, tc // LANE, lanes, 0)

    def main(off):
        return pl.BlockSpec((tr, tc), lambda j, i: (i, off + j))

    def prev(off):
        return pl.BlockSpec((SUBLANE, tc), lambda j, i: (jnp.maximum(i * hb - 1, 0), off + j))

    def wspec(off):
        return pl.BlockSpec((None, 3, tc), lambda j, i: (off + j, 0, 0))

    def bspec(off):
        return pl.BlockSpec((1, tc), lambda j, i: (0, off + j))

    return pl.pallas_call(
        body, name=name, grid=(half, T // tr),
        in_specs=[main(0), prev(0), main(half), prev(half), wspec(0), wspec(half), bspec(0), bspec(half)],
        out_specs=pl.BlockSpec((tr, tc), lambda j, i: (i, j)),
        out_shape=jax.ShapeDtypeStruct((T, F2 // 2), BF16),
        scratch_shapes=[pltpu.VMEM((SUBLANE + tr, LANE), F32), pltpu.VMEM((SUBLANE + tr, LANE), F32)],
        compiler_params=_params(("parallel", "parallel"), tr * tc * 4 * 12),
    )(up, up, up, up, convw, convw, convb, convb)


def _conv_bwd(da, up, convw, convb, name):
    T, F2 = up.shape
    tc = convw.shape[2]
    half = (F2 // 2) // tc
    tr = _tile(T, 128, SUBLANE)
    hb = tr // SUBLANE
    nrow = T // tr
    n = tr + 2 * SUBLANE

    def body(g_ref, gp_ref, gn_ref, v_ref, vp_ref, vn_ref, da_ref, dan_ref, wg_ref, wv_ref, bg_ref, bv_ref,
             dup_ref, gw_ref, acc_ref, gx_ref, vx_ref, dg_ref, dv_ref):
        i = pl.program_id(1)
        first = i == 0
        last = i == nrow - 1

        @pl.when(first)
        def _():
            acc_ref[...] = jnp.zeros_like(acc_ref)

        m = tr + SUBLANE

        def fold(t):
            return jnp.sum(t.reshape(tr // SUBLANE, SUBLANE, LANE), axis=0)

        def lanes(c, _):
            cols = pl.ds(pl.multiple_of(c * LANE, LANE), LANE)

            def conv(cur_ref, prev_ref, next_ref, w_ref, b_ref, x_ref):
                x_ref[0:SUBLANE, :] = jnp.where(first, 0.0, prev_ref[:, cols])
                x_ref[SUBLANE:SUBLANE + tr, :] = cur_ref[:, cols]
                x_ref[SUBLANE + tr:, :] = next_ref[:, cols]
                w = w_ref[:, cols]
                taps = tuple(x_ref[pl.ds(SUBLANE - 2 + k, m), :] for k in range(3))
                out = b_ref[:, cols] + w[0:1, :] * taps[0]
                out = out + w[1:2, :] * taps[1]
                return out + w[2:3, :] * taps[2], taps, w

            gate, g_taps, wg = conv(g_ref, gp_ref, gn_ref, wg_ref, bg_ref, gx_ref)
            val, v_taps, wv = conv(v_ref, vp_ref, vn_ref, wv_ref, bv_ref, vx_ref)
            da_m = jnp.concatenate([da_ref[:, cols], jnp.where(last, 0.0, dan_ref[:, cols])], axis=0)
            sg = _sigmoid(gate)
            dg_ref[...] = da_m * val * (sg * (1.0 + gate * (1.0 - sg)))
            dv_ref[...] = da_m * (gate * sg)
            for hf, (d_ref, w, taps) in enumerate(((dg_ref, wg, g_taps), (dv_ref, wv, v_taps))):
                dc = d_ref[pl.ds(0, tr), :]
                dup = w[2:3, :] * dc + w[1:2, :] * d_ref[pl.ds(1, tr), :] + w[0:1, :] * d_ref[pl.ds(2, tr), :]
                dup_ref[hf, :, cols] = dup.astype(BF16)
                acc_ref[hf, 0, :, cols] += fold(dc)
                for k in range(3):
                    acc_ref[hf, 1 + k, :, cols] += fold(dc * taps[k][0:tr])
            return 0

        lax.fori_loop(0, tc // LANE, lanes, 0)

        @pl.when(last)
        def _():
            gw_ref[...] = jnp.sum(acc_ref[...], axis=2)

    def main(off):
        return pl.BlockSpec((tr, tc), lambda j, i: (i, off + j))

    def prev(off):
        return pl.BlockSpec((SUBLANE, tc), lambda j, i: (jnp.maximum(i * hb - 1, 0), off + j))

    def nxt(off):
        return pl.BlockSpec((SUBLANE, tc), lambda j, i: (jnp.minimum((i + 1) * hb, T // SUBLANE - 1), off + j))

    def wspec(off):
        return pl.BlockSpec((None, 3, tc), lambda j, i: (off + j, 0, 0))

    def bspec(off):
        return pl.BlockSpec((1, tc), lambda j, i: (0, off + j))

    return pl.pallas_call(
        body, name=name, grid=(half, nrow),
        in_specs=[main(0), prev(0), nxt(0), main(half), prev(half), nxt(half), main(0), nxt(0),
                  wspec(0), wspec(half), bspec(0), bspec(half)],
        out_specs=[pl.BlockSpec((2, tr, tc), lambda j, i: (0, i, j)), pl.BlockSpec((2, 4, tc), lambda j, i: (0, 0, j))],
        out_shape=[jax.ShapeDtypeStruct((2, T, F2 // 2), BF16), jax.ShapeDtypeStruct((2, 4, F2 // 2), F32)],
        scratch_shapes=[pltpu.VMEM((2, 4, SUBLANE, tc), F32), pltpu.VMEM((n, LANE), F32), pltpu.VMEM((n, LANE), F32),
                        pltpu.VMEM((tr + SUBLANE, LANE), F32), pltpu.VMEM((tr + SUBLANE, LANE), F32)],
        compiler_params=_params(("parallel", "arbitrary"), n * tc * 4 * 24),
    )(up, up, up, up, up, up, da, da, convw, convw, convb, convb)


def _loss_head(y, target, name):
    T, D = y.shape
    tr = _tile(T, 256, SUBLANE)
    nrow = T // tr

    def body(y_ref, t_ref, d_ref, db_ref, l_ref, acc_ref):
        i = pl.program_id(0)
        diff = y_ref[...] - t_ref[...]
        dy = diff / D
        d_ref[...] = dy
        db_ref[...] = dy.astype(BF16)
        part = jnp.sum((diff * diff).reshape(tr // SUBLANE, SUBLANE, D), axis=0)

        @pl.when(i == 0)
        def _():
            acc_ref[...] = part

        @pl.when(i > 0)
        def _():
            acc_ref[...] += part

        @pl.when(i == nrow - 1)
        def _():
            col = jnp.sum(acc_ref[...], axis=0, keepdims=True)
            l_ref[...] = jnp.broadcast_to(0.5 * (jnp.sum(col, axis=1, keepdims=True) / D), (1, LANE))

    row = pl.BlockSpec((tr, D), lambda i: (i, 0))
    return pl.pallas_call(
        body, name=name, grid=(nrow,), in_specs=[row, row],
        out_specs=[row, row, pl.BlockSpec((1, LANE), lambda i: (0, 0))],
        out_shape=[jax.ShapeDtypeStruct((T, D), F32), jax.ShapeDtypeStruct((T, D), BF16),
                   jax.ShapeDtypeStruct((1, LANE), F32)],
        scratch_shapes=[pltpu.VMEM((SUBLANE, D), F32)],
        compiler_params=_params(("arbitrary",), tr * D * 14 * 2),
    )(y, target)


def _adamw(w, parts, m, v, name):
    R, C = w.shape
    P = parts.shape[0]
    tr = _tile(R, 64, SUBLANE)
    tc = _tile(C, 2048, LANE)

    def body(w_ref, p_ref, m_ref, v_ref, g_ref, d_ref, nm_ref, nv_ref):
        g = p_ref[0].astype(F32)
        for s in range(1, P):
            g = g + p_ref[s].astype(F32)
        wv = w_ref[...]
        nm = ADAM_B1 * m_ref[...] + (1.0 - ADAM_B1) * g
        nv = ADAM_B2 * v_ref[...] + (1.0 - ADAM_B2) * (g * g)
        m_hat = nm / (1.0 - ADAM_B1 ** ADAM_STEP)
        v_hat = nv / (1.0 - ADAM_B2 ** ADAM_STEP)
        g_ref[...] = g
        d_ref[...] = -ADAM_LR * (m_hat / (jnp.sqrt(v_hat) + ADAM_EPS) + ADAM_WD * wv)
        nm_ref[...] = nm
        nv_ref[...] = nv

    blk = pl.BlockSpec((tr, tc), lambda i, j: (i, j))
    shp = jax.ShapeDtypeStruct((R, C), F32)
    return pl.pallas_call(
        body, name=name, grid=(R // tr, C // tc),
        in_specs=[blk, pl.BlockSpec((P, tr, tc), lambda i, j: (0, i, j)), blk, blk],
        out_specs=[blk] * 4, out_shape=[shp] * 4,
        compiler_params=_params(("parallel", "parallel"), tr * tc * 4 * (P + 8) * 2),
    )(w, parts, m, v)


def _place():
    x, y, c = (lax.axis_index(a) for a in MESH_AXES)
    return x, y, c, 4 * x + 2 * y + c


def _peers(x, y, c):
    out = []
    for d in range(1, N_DEV):
        px = x + (d >> 2 & 1) - 2 * x * (d >> 2 & 1)
        py = y + (d >> 1 & 1) - 2 * y * (d >> 1 & 1)
        pc = c + (d & 1) - 2 * c * (d & 1)
        out.append(((px, py, pc), 4 * px + 2 * py + pc))
    return out


class _Side:
    def __init__(self, arrays, scatter, side_by_side=()):
        self.arrays, self.scatter, self.n = list(arrays), scatter, len(arrays)
        self.side_by_side = set(side_by_side)
        assert not (scatter and self.side_by_side)

    def in_specs(self):
        return [pl.BlockSpec(memory_space=pltpu.HBM)] * self.n

    out_specs = in_specs

    def out_shape(self):
        def shape(t, a):
            if t in self.side_by_side:
                return (a.shape[0], N_DEV * a.shape[1])
            return a.shape if self.scatter else (N_DEV,) + a.shape
        return [jax.ShapeDtypeStruct(shape(t, a), a.dtype) for t, a in enumerate(self.arrays)]

    def _slot(self, t, out, idx):
        if t in self.side_by_side:
            w = self.arrays[t].shape[1]
            return out.at[:, pl.ds(pl.multiple_of(idx * w, w), w)]
        return out.at[idx]

    def scratch(self):
        return [pltpu.SemaphoreType.DMA((self.n, N_DEV - 1)), pltpu.SemaphoreType.DMA((self.n, N_DEV - 1)),
                pltpu.SemaphoreType.DMA((self.n,))]

    def _copies(self, ins, outs, sems):
        send_sems, recv_sems, local_sems = sems
        x, y, c, me = _place()
        peers = _peers(x, y, c)
        local, sends, recvs = [], [], []
        for t in range(self.n):
            src_me = ins[t].at[me] if self.scatter else ins[t]
            local.append(pltpu.make_async_copy(src_me, self._slot(t, outs[t], me), local_sems.at[t]))
            for d, (peer, pidx) in enumerate(peers):
                src = ins[t].at[pidx] if self.scatter else ins[t]
                sends.append(pltpu.make_async_remote_copy(
                    src_ref=src, dst_ref=self._slot(t, outs[t], me), send_sem=send_sems.at[t, d],
                    recv_sem=recv_sems.at[t, d], device_id=peer, device_id_type=MESH_ID))
                recvs.append(pltpu.make_async_remote_copy(
                    src_ref=src, dst_ref=self._slot(t, outs[t], pidx), send_sem=send_sems.at[t, d],
                    recv_sem=recv_sems.at[t, d], device_id=peer, device_id_type=MESH_ID))
        return local, sends, recvs

    def start(self, ins, outs, sems):
        local, sends, _ = self._copies(ins, outs, sems)
        for cp in local + sends:
            cp.start()

    def wait(self, ins, outs, sems):
        local, sends, recvs = self._copies(ins, outs, sems)
        for cp in recvs:
            cp.wait_recv()
        for cp in sends:
            cp.wait_send()
        for cp in local:
            cp.wait()


def _grid_edges(sizes):
    ids = [pl.program_id(a) for a in range(len(sizes))]
    first = functools.reduce(jnp.logical_and, [i == 0 for i in ids])
    last = functools.reduce(jnp.logical_and, [i == s - 1 for i, s in zip(ids, sizes)])
    return first, last


def _gather_two_level(arrays, name):
    n = len(arrays)

    def body(*refs):
        ins, outs = refs[:n], refs[n:2 * n]
        send_sems, recv_sems, local_sems = refs[2 * n:]
        x, y, c, me = _place()
        sibling = (x, y, 1 - c)
        chips = [(1 - x, y), (x, 1 - y), (1 - x, 1 - y)]

        def slot(px, py, pc):
            return 4 * px + 2 * py + pc

        def copy(t, k, block, to, src=None):
            return pltpu.make_async_remote_copy(
                src_ref=outs[t].at[slot(*block)] if src is None else src, dst_ref=outs[t].at[slot(*block)],
                send_sem=send_sems.at[t, k], recv_sem=recv_sems.at[t, k], device_id=to, device_id_type=MESH_ID)

        mine = [pltpu.make_async_copy(ins[t], outs[t].at[me], local_sems.at[t]) for t in range(n)]
        first = [copy(t, 0, (x, y, c), sibling, src=ins[t]) for t in range(n)]
        first += [copy(t, 1 + j, (x, y, c), (*chip, c), src=ins[t]) for t in range(n) for j, chip in enumerate(chips)]
        for cp in mine + first:
            cp.start()
        passed = []
        for t in range(n):
            for j, chip in enumerate(chips):
                copy(t, 1 + j, (*chip, c), (x, y, c)).wait_recv()
                cp = copy(t, 4 + j, (*chip, c), sibling)
                cp.start()
                passed.append(cp)
        for t in range(n):
            copy(t, 0, sibling, (x, y, c)).wait_recv()
            for j, chip in enumerate(chips):
                copy(t, 4 + j, (*chip, 1 - c), (x, y, c)).wait_recv()
        for cp in first + passed:
            cp.wait_send()
        for cp in mine:
            cp.wait()

    hbm = pl.BlockSpec(memory_space=pltpu.HBM)
    return pl.pallas_call(
        body, name=name, in_specs=[hbm] * n, out_specs=[hbm] * n,
        out_shape=[jax.ShapeDtypeStruct((N_DEV,) + a.shape, a.dtype) for a in arrays],
        scratch_shapes=[pltpu.SemaphoreType.DMA((n, N_DEV - 1)), pltpu.SemaphoreType.DMA((n, N_DEV - 1)),
                        pltpu.SemaphoreType.DMA((n,))],
    )(*arrays)


def _all_reduce_small(vec, name):
    R = vec.shape[0]

    def body(v_ref, o_ref, gath_ref, send_sems, recv_sems):
        x, y, c, me = _place()
        sends, recvs = [], []
        for d, (peer, pidx) in enumerate(_peers(x, y, c)):
            cp = pltpu.make_async_remote_copy(
                src_ref=v_ref, dst_ref=gath_ref.at[me], send_sem=send_sems.at[d], recv_sem=recv_sems.at[d],
                device_id=peer, device_id_type=MESH_ID)
            cp.start()
            sends.append(cp)
            recvs.append(pltpu.make_async_remote_copy(
                src_ref=v_ref, dst_ref=gath_ref.at[pidx], send_sem=send_sems.at[d], recv_sem=recv_sems.at[d],
                device_id=peer, device_id_type=MESH_ID))
        gath_ref[me] = v_ref[...]
        for cp in recvs:
            cp.wait_recv()
        for cp in sends:
            cp.wait_send()
        total = gath_ref[0]
        for s in range(1, N_DEV):
            total = total + gath_ref[s]
        o_ref[...] = total

    vm = pl.BlockSpec(memory_space=pltpu.VMEM)
    return pl.pallas_call(
        body, name=name, in_specs=[vm], out_specs=vm, out_shape=jax.ShapeDtypeStruct(vec.shape, F32),
        scratch_shapes=[pltpu.VMEM((N_DEV, R, LANE), F32), pltpu.SemaphoreType.DMA((N_DEV - 1,)),
                        pltpu.SemaphoreType.DMA((N_DEV - 1,))],
        compiler_params=pltpu.CompilerParams(vmem_limit_bytes=_vmem(R * LANE * 4 * 12)),
    )(vec)


def _pack(parts):
    flat = jnp.concatenate([p.reshape(-1).astype(F32) for p in parts])
    rows = -(-flat.shape[0] // (LANE * SUBLANE)) * SUBLANE
    return jnp.pad(flat, (0, rows * LANE - flat.shape[0])).reshape(rows, LANE)


def _unpack(packed, shapes):
    flat = packed.reshape(-1)
    out, at = [], 0
    for s in shapes:
        size = int(np.prod(s))
        out.append(flat[at:at + size].reshape(s))
        at += size
    return out


def kernel(x, g_mix, w_in, g_q, g_k, lb_logits, g_hg_out, p_a, p_b, w_o, g_ffn, w_up, conv_w, conv_b, w_down, loss_target, m_g_mix, m_w_in, m_g_q, m_g_k, m_lb_logits, m_g_hg_out, m_p_a, m_p_b, m_w_o, m_g_ffn, m_w_up, m_conv_w, m_conv_b, m_w_down, v_g_mix, v_w_in, v_g_q, v_g_k, v_lb_logits, v_g_hg_out, v_p_a, v_p_b, v_w_o, v_g_ffn, v_w_up, v_conv_w, v_conv_b, v_w_down):
    assert x.shape[0] == 1 and lb_logits.shape[0] == 2
    xs, target = x[0], loss_target[0]
    T, D = xs.shape
    A = p_a.shape[1]
    HW = p_b.shape[1]
    hg_col0 = 3 * A
    gate_col0 = 3 * A + 4 * HW
    F = w_down.shape[1] * N_DEV
    assert w_in.shape[2] * N_DEV == gate_col0 + 2 * D and w_up.shape[2] * N_DEV == 2 * F

    (win_g,) = _gather_two_level([w_in[0].astype(BF16)], "gather_w_in")
    later_weights = _Side([w_up[0].astype(BF16), p_a[0].astype(BF16), p_b[0].astype(BF16), w_o[0].astype(BF16),
                           w_down[0].astype(BF16), conv_w[0]], False, side_by_side=(1, 2))

    u = _rmsnorm_fwd(xs, g_mix, "norm_mix")
    proj = _matmul(u, win_g, mode="nn", b_shards=True, name="proj_in")
    gains = jnp.stack([g_q[0], g_k[0], jnp.ones_like(g_q[0])])[:, None, :]
    qkv = _qkv_prep(proj, gains, A, "qkv_prep")
    ya, ya32, (wup_g, pa_g, pb_g, wo_g, wdown_g, convw_g) = _sb_fwd(qkv, A, "sb_fwd", side=later_weights)
    wo_full = wo_g.reshape(D, D)
    wdown_full = wdown_g.reshape(F, D)
    yb, ob, states = _hg_fwd(proj, lb_logits, g_hg_out, hg_col0, HW, "hg_fwd")
    ma = _matmul(ya, pa_g, mode="nn", name="proj_a")
    mb = _matmul(yb, pb_g, mode="nn", name="proj_b")
    m = _merge_fwd(proj, ma, mb, gate_col0, "merge_fwd")
    h1 = _matmul(m, wo_full, mode="nn", add=xs, name="proj_o")
    u2 = _rmsnorm_fwd(h1, g_ffn, "norm_ffn")
    up = _matmul(u2, wup_g, mode="nn", b_shards=True, name="ffn_up")
    act = _conv_fwd(up, convw_g, conv_b, "conv_fwd")
    y = _matmul(act, wdown_full, mode="nn", add=h1, tn_pref=2048, name="ffn_down")
    dy, dyb, loss_part = _loss_head(y, target, "loss_head")
    loss = lax.psum(loss_part[0, 0], MESH_AXES)

    dact = _matmul(dyb, wdown_full, mode="nt", name="d_act")
    g_wdown = _matmul(act, dyb, mode="tn", out_dtype=BF16, name="g_w_down")
    dup, g_conv = _conv_bwd(dact, up, convw_g, conv_b, "conv_bwd")
    g_wup, (r_wdown,) = _matmul(u2, dup, mode="tn", b_halves=True, out_shards=True, out_dtype=BF16, tm_pref=1024,
                                name="g_w_up",
                                side=_Side([g_wdown.reshape(N_DEV, F // N_DEV, D)], True))
    du2, (r_wup,) = _matmul(dup, wup_g, mode="nt", a_halves=True, b_shards=True, tn_pref=2048, name="d_u2",
                            side=_Side([g_wup], True))
    dh1, dh1b, g_gffn = _rmsnorm_bwd(h1, g_ffn, du2, dy, "norm_ffn_bwd")
    dm = _matmul(dh1b, wo_full, mode="nt", name="d_m")
    g_wo = _matmul(m, dh1b, mode="tn", out_dtype=BF16, tm_pref=1024, name="g_w_o")
    dproj, dmab = _merge_bwd(proj, ma, mb, dm, gate_col0, "merge_bwd")
    dya = _matmul(dmab, pa_g, mode="nt", lead_a=0, out_dtype=BF16, name="d_ya")
    g_pa = _matmul(ya, dmab, mode="tn", out_shards=True, out_dtype=BF16, lead_b=0, name="g_p_a")
    dyb_ = _matmul(dmab, pb_g, mode="nt", lead_a=1, out_dtype=BF16, name="d_yb")
    g_pb = _matmul(yb, dmab, mode="tn", out_shards=True, out_dtype=BF16, lead_b=1, name="g_p_b")
    dproj, g_lb, g_ghg = _hg_bwd(proj, lb_logits, g_hg_out, ob, states, dyb_, dproj, hg_col0, HW, "hg_bwd")
    dq, dk, dv = _sb_bwd(qkv, ya32, dya, A, "sb_bwd")
    dproj, g_gains = _qkv_bwd(proj, gains, dq, dk, dv, dproj, "qkv_bwd")
    g_win, (r_wo, r_pa, r_pb) = _matmul(
        u, dproj, mode="tn", out_shards=True, out_dtype=BF16, tm_pref=1024, name="g_w_in",
        side=_Side([g_wo.reshape(N_DEV, D // N_DEV, D), g_pa, g_pb], True))
    du, (r_win,) = _matmul(dproj, win_g, mode="nt", b_shards=True, tn_pref=2048, name="d_u",
                           side=_Side([g_win], True))
    gx, _, g_gmix = _rmsnorm_bwd(xs, g_mix, du, dh1, "norm_mix_bwd")

    g_convb = g_conv[:, 0].reshape(1, 2 * F)
    g_convw = g_conv[:, 1:4].transpose(1, 0, 2).reshape(3, 2 * F)
    small = [g_gmix, g_gains[0], g_gains[1], g_lb, g_ghg, g_gffn, g_convb, g_convw]
    small_shapes = [p.shape for p in small]
    red = _unpack(_all_reduce_small(_pack(small), "reduce_small"), small_shapes)

    _, _, _, me = _place()
    cs = conv_w.shape[2]
    big = {
        "w_in": _adamw(w_in[0], r_win, m_w_in[0], v_w_in[0], "adamw_w_in"),
        "w_up": _adamw(w_up[0], r_wup, m_w_up[0], v_w_up[0], "adamw_w_up"),
        "p_a": _adamw(p_a[0], r_pa, m_p_a[0], v_p_a[0], "adamw_p_a"),
        "p_b": _adamw(p_b[0], r_pb, m_p_b[0], v_p_b[0], "adamw_p_b"),
        "w_o": _adamw(w_o[0], r_wo, m_w_o[0], v_w_o[0], "adamw_w_o"),
        "w_down": _adamw(w_down[0], r_wdown, m_w_down[0], v_w_down[0], "adamw_w_down"),
        "conv_w": _adamw(conv_w[0], lax.dynamic_slice_in_dim(red[7], me * cs, cs, axis=1)[None],
                         m_conv_w[0], v_conv_w[0], "adamw_conv_w"),
    }
    rep_w = [g_mix, g_q, g_k, lb_logits, g_hg_out, g_ffn, conv_b]
    rep_m = [m_g_mix, m_g_q, m_g_k, m_lb_logits, m_g_hg_out, m_g_ffn, m_conv_b]
    rep_v = [v_g_mix, v_g_q, v_g_k, v_lb_logits, v_g_hg_out, v_g_ffn, v_conv_b]
    rep_shapes = [p.shape for p in rep_w]
    rep_out = _adamw(_pack(rep_w), _pack(red[:7])[None], _pack(rep_m), _pack(rep_v), "adamw_small")
    rep = [_unpack(o, rep_shapes) for o in rep_out]
    rep_names = ["g_mix", "g_q", "g_k", "lb_logits", "g_hg_out", "g_ffn", "conv_b"]

    order = ["g_mix", "w_in", "g_q", "g_k", "lb_logits", "g_hg_out", "p_a", "p_b", "w_o", "g_ffn", "w_up",
             "conv_w", "conv_b", "w_down"]

    def leaf(kind, pname):
        if pname in big:
            return big[pname][kind][None]
        return rep[kind][rep_names.index(pname)]

    return (loss, gx[None], *[leaf(kind, p) for kind in range(4) for p in order])
```

```python
import functools

import numpy as np
import jax
import jax.numpy as jnp
from jax import lax
from jax.experimental import pallas as pl
from jax.experimental.pallas import tpu as pltpu

F32 = jnp.float32
BF16 = jnp.bfloat16

N_DEV = 8
HEAD_DIM = 128
HG_CHUNK = 64
HG_LEVELS = (1, 2, 4, 8, 16, 32)
EPS = 1e-6
ADAM_LR = 0.001
ADAM_B1 = 0.9
ADAM_B2 = 0.999
ADAM_EPS = 1e-08
ADAM_WD = 0.01
ADAM_STEP = 10
LANE = 128
SUBLANE = 8
VMEM_CAP = 56 << 20
MESH_AXES = ("x", "y", "c")
MESH_ID = pl.DeviceIdType.MESH


def _tile(dim, pref, align):
    if dim <= pref:
        return dim
    t = (pref // align) * align
    while t >= align:
        if dim % t == 0:
            return t
        t -= align
    return dim


def _vmem(est_bytes):
    return int(min(max(2 * est_bytes + (8 << 20), 32 << 20), VMEM_CAP))


def _params(sem, est_bytes):
    return pltpu.CompilerParams(dimension_semantics=sem, vmem_limit_bytes=_vmem(est_bytes))


def _sigmoid(x):
    return 1.0 / (1.0 + jnp.exp(-x))


_DIMS = {"nn": (((1,), (0,)), ((), ())), "nt": (((1,), (1,)), ((), ())), "tn": (((0,), (0,)), ((), ()))}


def _matmul(a, b, *, mode, name, out_dtype=F32, add=None, b_shards=False, out_shards=False, lead_a=None,
            lead_b=None, a_halves=False, b_halves=False, side=None, tm_pref=512, tn_pref=1408, tk_pref=2048):
    if mode == "tn":
        K, M = a.shape[-2:]
    else:
        M, K = a.shape[-2:]
    if a_halves:
        assert mode == "nt" and b_shards and lead_a is None
        K = 2 * K
    if lead_b is not None or b_halves:
        assert not b_shards and mode == "tn"
    if b_shards:
        S = b.shape[0]
        if mode == "nn":
            assert b.shape[1] == K
            N, tn, tk = S * b.shape[2], b.shape[2], _tile(K, tk_pref, LANE)
        else:
            assert mode == "nt" and S * b.shape[2] == K
            N, tk, tn = b.shape[1], b.shape[2], _tile(b.shape[1], tn_pref, LANE)
    else:
        if mode == "nt":
            N = b.shape[0]
            assert b.shape[1] == K
        else:
            N = b.shape[-1] * (2 if b_halves else 1)
            assert b.shape[-2] == K
        tn = _tile(N, tn_pref, LANE)
        tk = _tile(K, tk_pref, LANE)
    if out_shards:
        assert N % N_DEV == 0
        tn = N // N_DEV
    per_half_n, per_half_k = (N // 2) // tn, (K // 2) // tk
    assert not b_halves or per_half_n * tn * 2 == N
    assert not a_halves or per_half_k * tk * 2 == K
    tm = _tile(M, tm_pref, LANE)
    nm, nn, nk = M // tm, N // tn, K // tk
    assert nm * tm == M and nn * tn == N and nk * tk == K

    if mode == "tn":
        a_spec = pl.BlockSpec((tk, tm), lambda j, i, k: (k, i))
    elif lead_a is not None:
        a_spec = pl.BlockSpec((None, tm, tk), lambda j, i, k: (lead_a, i, k))
    elif a_halves:
        a_spec = pl.BlockSpec((None, tm, tk), lambda j, i, k: (k // per_half_k, i, k % per_half_k))
    else:
        a_spec = pl.BlockSpec((tm, tk), lambda j, i, k: (i, k))
    if b_halves:
        b_spec = pl.BlockSpec((None, tk, tn), lambda j, i, k: (j // per_half_n, k, j % per_half_n))
    elif lead_b is not None:
        b_spec = pl.BlockSpec((None, tk, tn), lambda j, i, k: (lead_b, k, j))
    elif b_shards and mode == "nn":
        b_spec = pl.BlockSpec((None, tk, tn), lambda j, i, k: (j, k, 0))
    elif b_shards:
        b_spec = pl.BlockSpec((None, tn, tk), lambda j, i, k: (k, j, 0))
    elif mode == "nt":
        b_spec = pl.BlockSpec((tn, tk), lambda j, i, k: (j, k))
    else:
        b_spec = pl.BlockSpec((tk, tn), lambda j, i, k: (k, j))
    in_specs = [a_spec, b_spec]
    operands = [a, b]
    if add is not None:
        assert not out_shards and add.shape == (M, N)
        in_specs.append(pl.BlockSpec((tm, tn), lambda j, i, k: (i, j)))
        operands.append(add)
    if out_shards:
        out_shape = jax.ShapeDtypeStruct((N_DEV, M, tn), out_dtype)
        out_spec = pl.BlockSpec((None, tm, tn), lambda j, i, k: (j, i, 0))
    else:
        out_shape = jax.ShapeDtypeStruct((M, N), out_dtype)
        out_spec = pl.BlockSpec((tm, tn), lambda j, i, k: (i, j))
    dims = _DIMS[mode]
    has_add = add is not None
    n_in = 3 if has_add else 2
    n_side = side.n if side is not None else 0

    def body(*refs):
        a_ref, b_ref = refs[0], refs[1]
        add_ref = refs[2] if has_add else None
        o_ref = refs[n_in + n_side]
        acc_ref = refs[n_in + 2 * n_side + 1]
        k = pl.program_id(2)
        if side is not None:
            side_refs = (refs[n_in:n_in + n_side], refs[n_in + n_side + 1:n_in + 2 * n_side + 1],
                         refs[n_in + 2 * n_side + 2:])
            first, last = _grid_edges((nn, nm, nk))

            @pl.when(first)
            def _():
                side.start(*side_refs)

        part = lax.dot_general(a_ref[...], b_ref[...], dims, preferred_element_type=F32)

        def finish(total):
            if has_add:
                total = add_ref[...] + total
            o_ref[...] = total.astype(out_dtype)

        if nk == 1:
            finish(part)
        else:
            @pl.when(k == 0)
            def _():
                acc_ref[...] = part

            @pl.when(jnp.logical_and(k > 0, k < nk - 1))
            def _():
                acc_ref[...] += part

            @pl.when(k == nk - 1)
            def _():
                finish(acc_ref[...] + part)

        if side is not None:
            @pl.when(last)
            def _():
                side.wait(*side_refs)

    est = 2 * (tm * tk * 2 + tk * tn * 2 + tm * tn * 4 * (2 if has_add else 1)) + tm * tn * 4 * 2
    acc = pltpu.VMEM((tm, tn) if nk > 1 else (SUBLANE, LANE), F32)
    if side is None:
        return pl.pallas_call(
            body, name=name, grid=(nn, nm, nk), in_specs=in_specs, out_specs=out_spec, out_shape=out_shape,
            scratch_shapes=[acc], compiler_params=_params(("parallel", "parallel", "arbitrary"), est),
        )(*operands)
    outs = pl.pallas_call(
        body, name=name, grid=(nn, nm, nk), in_specs=in_specs + side.in_specs(),
        out_specs=[out_spec] + side.out_specs(), out_shape=[out_shape] + side.out_shape(),
        scratch_shapes=[acc] + side.scratch(), compiler_params=_params(("arbitrary",) * 3, est),
    )(*operands, *side.arrays)
    return outs[0], outs[1:]


def _rmsnorm_fwd(x, g, name):
    T, D = x.shape
    tr = _tile(T, 256, SUBLANE)

    def body(x_ref, g_ref, o_ref):
        xf = x_ref[...]
        r = lax.rsqrt(jnp.mean(xf * xf, axis=-1, keepdims=True) + EPS)
        o_ref[...] = ((xf * r) * g_ref[...]).astype(BF16)

    return pl.pallas_call(
        body, name=name, grid=(T // tr,),
        in_specs=[pl.BlockSpec((tr, D), lambda i: (i, 0)), pl.BlockSpec((1, D), lambda i: (0, 0))],
        out_specs=pl.BlockSpec((tr, D), lambda i: (i, 0)), out_shape=jax.ShapeDtypeStruct((T, D), BF16),
        compiler_params=_params(("parallel",), tr * D * 6 * 2),
    )(x, g)


def _rmsnorm_bwd(x, g, dy, res, name):
    T, D = x.shape
    tr = _tile(T, 256, SUBLANE)
    nsteps = T // tr

    def body(x_ref, g_ref, dy_ref, res_ref, dx_ref, dxb_ref, dg_ref, acc_ref):
        i = pl.program_id(0)
        xf = x_ref[...]
        dyf = dy_ref[...].astype(F32)
        r = lax.rsqrt(jnp.mean(xf * xf, axis=-1, keepdims=True) + EPS)
        w = dyf * g_ref[...]
        s = jnp.mean(xf * w, axis=-1, keepdims=True)
        dx = res_ref[...] + (r * w - xf * (r * r * r * s))
        dx_ref[...] = dx
        dxb_ref[...] = dx.astype(BF16)
        part = jnp.sum((dyf * (xf * r)).reshape(tr // SUBLANE, SUBLANE, D), axis=0)

        @pl.when(i == 0)
        def _():
            acc_ref[...] = part

        @pl.when(i > 0)
        def _():
            acc_ref[...] += part

        @pl.when(i == nsteps - 1)
        def _():
            dg_ref[...] = jnp.sum(acc_ref[...], axis=0, keepdims=True)

    row = pl.BlockSpec((tr, D), lambda i: (i, 0))
    vec = pl.BlockSpec((1, D), lambda i: (0, 0))
    return pl.pallas_call(
        body, name=name, grid=(nsteps,), in_specs=[row, vec, row, row], out_specs=[row, row, vec],
        out_shape=[jax.ShapeDtypeStruct((T, D), F32), jax.ShapeDtypeStruct((T, D), BF16),
                   jax.ShapeDtypeStruct((1, D), F32)],
        scratch_shapes=[pltpu.VMEM((SUBLANE, D), F32)],
        compiler_params=_params(("arbitrary",), tr * D * 18 * 2),
    )(x, g, dy, res)


def _qkv_prep(proj, gains, width, name):
    T = proj.shape[0]
    tr = _tile(T, 256, SUBLANE)
    heads = width // HEAD_DIM

    def body(p_ref, g_ref, o_ref):
        j = pl.program_id(1)
        g = g_ref[...]
        for h in range(heads):
            xh = p_ref[:, h * HEAD_DIM:(h + 1) * HEAD_DIM]
            r = lax.rsqrt(jnp.mean(xh * xh, axis=-1, keepdims=True) + EPS)
            r = jnp.where(j < 2, r, 1.0)
            o_ref[:, h * HEAD_DIM:(h + 1) * HEAD_DIM] = ((xh * r) * g).astype(BF16)

    return pl.pallas_call(
        body, name=name, grid=(T // tr, 3),
        in_specs=[pl.BlockSpec((tr, width), lambda i, j: (i, j)),
                  pl.BlockSpec((None, 1, HEAD_DIM), lambda i, j: (j, 0, 0))],
        out_specs=pl.BlockSpec((tr, width), lambda i, j: (i, j)),
        out_shape=jax.ShapeDtypeStruct((T, 3 * width), BF16),
        compiler_params=_params(("parallel", "parallel"), tr * width * 6 * 2),
    )(proj, gains)


def _qkv_bwd(proj, gains, dq, dk, dv, dproj, name):
    T = proj.shape[0]
    width = dq.shape[1]
    tr = _tile(T, 256, SUBLANE)
    heads = width // HEAD_DIM
    nrow = T // tr

    def body(p_ref, g_ref, dq_ref, dk_ref, dv_ref, dp_in, o_ref, dg_ref, acc_ref):
        del dp_in
        i = pl.program_id(0)
        j = pl.program_id(1)
        g = g_ref[...]

        @pl.when(jnp.logical_and(i == 0, j == 0))
        def _():
            acc_ref[...] = jnp.zeros_like(acc_ref)

        part = jnp.zeros((SUBLANE, HEAD_DIM), F32)
        for h in range(heads):
            sl = slice(h * HEAD_DIM, (h + 1) * HEAD_DIM)
            xh = p_ref[:, sl]
            dyh = jnp.where(j == 0, dq_ref[:, sl], jnp.where(j == 1, dk_ref[:, sl], dv_ref[:, sl]))
            r = lax.rsqrt(jnp.mean(xh * xh, axis=-1, keepdims=True) + EPS)
            w = dyh * g
            s = jnp.mean(xh * w, axis=-1, keepdims=True)
            dx = r * w - xh * (r * r * r * s)
            o_ref[:, sl] = jnp.where(j < 2, dx, dyh).astype(BF16)
            part = part + jnp.sum((dyh * (xh * r)).reshape(tr // SUBLANE, SUBLANE, HEAD_DIM), axis=0)
        acc_ref[j] += part

        @pl.when(jnp.logical_and(i == nrow - 1, j == 2))
        def _():
            dg_ref[...] = jnp.sum(acc_ref[...], axis=1, keepdims=True)

    blk = pl.BlockSpec((tr, width), lambda i, j: (i, j))
    dblk = pl.BlockSpec((tr, width), lambda i, j: (i, 0))
    return pl.pallas_call(
        body, name=name, grid=(nrow, 3),
        in_specs=[blk, pl.BlockSpec((None, 1, HEAD_DIM), lambda i, j: (j, 0, 0)), dblk, dblk, dblk,
                  pl.BlockSpec(memory_space=pl.ANY)],
        out_specs=[blk, pl.BlockSpec((3, 1, HEAD_DIM), lambda i, j: (0, 0, 0))],
        out_shape=[jax.ShapeDtypeStruct(dproj.shape, BF16), jax.ShapeDtypeStruct((3, 1, HEAD_DIM), F32)],
        input_output_aliases={5: 0},
        scratch_shapes=[pltpu.VMEM((3, SUBLANE, HEAD_DIM), F32)],
        compiler_params=_params(("arbitrary", "arbitrary"), tr * width * 18 * 2),
    )(proj, gains, dq, dk, dv, dproj)


def _split2(x):
    hi = x.astype(BF16)
    lo = (x - hi.astype(F32)).astype(BF16)
    return hi, lo


LOG2E = 1.4426950408889634


def _sb_logits(q, kb):
    return lax.dot_general(q, kb, _DIMS["nt"], preferred_element_type=F32) * (HEAD_DIM ** -0.5 * LOG2E)


def _sb_scores(z2, mask):
    lk = -(jnp.maximum(z2, 0.0) + jnp.log2(1.0 + jnp.exp2(-jnp.abs(z2))))
    if mask is not None:
        lk = jnp.where(mask, lk, 0.0)
    return lk, lk + z2


def _sb_block_size(T):
    return _tile(T, 256, LANE)


SB_HEADS_PER_STEP = 2


def _sb_later(lk, upper, carry):
    hi, lo = _split2(lk)
    return (jnp.dot(hi, upper, preferred_element_type=F32) + jnp.dot(lo, upper, preferred_element_type=F32)) + carry


def _sb_fwd(qkv, width, name, side=None):
    T = qkv.shape[0]
    heads = width // HEAD_DIM
    bq = _sb_block_size(T)
    hps = min(2 * SB_HEADS_PER_STEP, heads)
    assert heads % hps == 0
    groups, W = heads // hps, hps * HEAD_DIM

    n_side = side.n if side is not None else 0

    def body(*refs):
        q_ref, k_ref, v_ref = refs[:3]
        o_ref, o32_ref = refs[3 + n_side:5 + n_side]
        if side is not None:
            side_refs = (refs[3:3 + n_side], refs[5 + n_side:5 + 2 * n_side], refs[5 + 2 * n_side:])
            first, last = _grid_edges((groups, T // bq))

            @pl.when(first)
            def _():
                side.start(*side_refs)

        i = pl.program_id(1)
        row = lax.broadcasted_iota(jnp.int32, (bq, bq), 0)
        col = lax.broadcasted_iota(jnp.int32, (bq, bq), 1)
        upper = (row > col).astype(BF16)
        causal = col < row

        def head_block(start, h, carry, acc, mask):
            cols = slice(h * HEAD_DIM, (h + 1) * HEAD_DIM)
            z2 = _sb_logits(q_ref[:, cols], k_ref[pl.ds(start, bq), cols])
            yield
            lk, lb = _sb_scores(z2, mask)
            later = _sb_later(lk, upper, carry)
            yield
            w = jnp.exp2(lb + later)
            if mask is not None:
                w = jnp.where(mask, w, 0.0)
            pv = jnp.dot(w.astype(BF16), v_ref[pl.ds(start, bq), cols], preferred_element_type=F32)
            yield
            return carry + jnp.sum(lk, axis=1, keepdims=True), acc + pv

        def block(j, state, mask):
            start = pl.multiple_of(j * bq, bq)
            return tuple(_lockstep(head_block(start, h, state[h][0], state[h][1], mask) for h in range(hps)))

        zero = (jnp.zeros((bq, 1), F32), jnp.zeros((bq, HEAD_DIM), F32))
        state = block(i, (zero,) * hps, causal)
        state = lax.fori_loop(0, i, lambda n, st: block(i - 1 - n, st, None), state)
        for h in range(hps):
            cols = slice(h * HEAD_DIM, (h + 1) * HEAD_DIM)
            o_ref[:, cols] = state[h][1].astype(BF16)
            o32_ref[:, cols] = state[h][1]

        if side is not None:
            @pl.when(last)
            def _():
                side.wait(*side_refs)

    oblk = pl.BlockSpec((bq, W), lambda g, i: (i, g))
    in_specs = [oblk, pl.BlockSpec((T, W), lambda g, i: (0, groups + g)),
                pl.BlockSpec((T, W), lambda g, i: (0, 2 * groups + g))]
    out_shape = [jax.ShapeDtypeStruct((T, width), BF16), jax.ShapeDtypeStruct((T, width), F32)]
    est = 2 * T * W * 2 * 2 + hps * 16 * bq * bq * 4
    if side is None:
        return pl.pallas_call(
            body, name=name, grid=(groups, T // bq), in_specs=in_specs, out_specs=[oblk, oblk], out_shape=out_shape,
            compiler_params=_params(("parallel", "arbitrary"), est),
        )(qkv, qkv, qkv)
    outs = pl.pallas_call(
        body, name=name, grid=(groups, T // bq), in_specs=in_specs + side.in_specs(),
        out_specs=[oblk, oblk] + side.out_specs(), out_shape=out_shape + side.out_shape(),
        scratch_shapes=side.scratch(), compiler_params=_params(("arbitrary", "arbitrary"), est),
    )(qkv, qkv, qkv, *side.arrays)
    return outs[0], outs[1], outs[2:]


def _sb_bwd(qkv, y, dy, width, name):
    T = qkv.shape[0]
    heads = width // HEAD_DIM
    bq = _sb_block_size(T)
    nq = T // bq
    scale = HEAD_DIM ** -0.5
    hps = min(2 * SB_HEADS_PER_STEP, heads)
    groups, W = heads // hps, hps * HEAD_DIM

    def body(q_ref, kv_hbm, y_ref, dy_ref, dq_ref, dk_hbm, dv_hbm, dk_acc, dv_acc, kbuf, vbuf, kv_sems, out_sems):
        g_id = pl.program_id(0)
        i = pl.program_id(1)
        kcol = pl.multiple_of((groups + g_id) * W, W)
        vcol = pl.multiple_of((2 * groups + g_id) * W, W)

        def fetch(j, slot):
            rows = pl.ds(pl.multiple_of(j * bq, bq), bq)
            return (pltpu.make_async_copy(kv_hbm.at[rows, pl.ds(kcol, W)], kbuf.at[slot], kv_sems.at[0, slot]),
                    pltpu.make_async_copy(kv_hbm.at[rows, pl.ds(vcol, W)], vbuf.at[slot], kv_sems.at[1, slot]))

        for cp in fetch(i, 0):
            cp.start()
        row = lax.broadcasted_iota(jnp.int32, (bq, bq), 0)
        col = lax.broadcasted_iota(jnp.int32, (bq, bq), 1)
        upper = (row > col).astype(BF16)
        upper_incl = (row >= col).astype(BF16)
        causal = col < row
        deltas = []
        for h in range(hps):
            cols = slice(h * HEAD_DIM, (h + 1) * HEAD_DIM)
            deltas.append(jnp.sum(dy_ref[:, cols].astype(F32) * y_ref[:, cols], axis=1, keepdims=True))

        @pl.when(i == 0)
        def _():
            dk_acc[...] = jnp.zeros_like(dk_acc)
            dv_acc[...] = jnp.zeros_like(dv_acc)

        def head_block(start, slot, h, carry, gcarry, dq, mask):
            cols = slice(h * HEAD_DIM, (h + 1) * HEAD_DIM)
            q = q_ref[:, cols]
            do = dy_ref[:, cols]
            kb = kbuf[slot, :, cols]
            vb = vbuf[slot, :, cols]
            z2 = _sb_logits(q, kb)
            dw = lax.dot_general(do, vb, _DIMS["nt"], preferred_element_type=F32)
            yield
            lk, lb = _sb_scores(z2, mask)
            later = _sb_later(lk, upper, carry)
            yield
            w = jnp.exp2(lb + later)
            if mask is not None:
                w = jnp.where(mask, w, 0.0)
            wb = w.astype(BF16)
            gw = dw * wb.astype(F32)
            gsuf = _sb_later(gw, upper_incl, gcarry)
            dvp = lax.dot_general(wb, do, _DIMS["tn"], preferred_element_type=F32)
            yield
            dz = gw - jnp.exp2(lb) * (gw + (deltas[h] - gsuf))
            if mask is not None:
                dz = jnp.where(mask, dz, 0.0)
            dzs = (dz * scale).astype(BF16)
            dqp = jnp.dot(dzs, kb, preferred_element_type=F32)
            dkp = lax.dot_general(dzs, q, _DIMS["tn"], preferred_element_type=F32)
            yield
            dk_acc[pl.ds(start, bq), cols] += dkp
            dv_acc[pl.ds(start, bq), cols] += dvp
            return (carry + jnp.sum(lk, axis=1, keepdims=True), gcarry + jnp.sum(gw, axis=1, keepdims=True), dq + dqp)

        def block(n, state, mask):
            j = i - n
            slot = n % 2
            for cp in fetch(j, slot):
                cp.wait()

            @pl.when(n < i)
            def _():
                for cp in fetch(j - 1, 1 - slot):
                    cp.start()

            start = pl.multiple_of(j * bq, bq)
            return tuple(_lockstep(head_block(start, slot, h, *state[h], mask) for h in range(hps)))

        zero = jnp.zeros((bq, 1), F32)
        state = block(0, ((zero, zero, jnp.zeros((bq, HEAD_DIM), F32)),) * hps, causal)
        state = lax.fori_loop(1, i + 1, lambda n, st: block(n, st, None), state)
        for h in range(hps):
            dq_ref[:, h * HEAD_DIM:(h + 1) * HEAD_DIM] = state[h][2]

        @pl.when(i == nq - 1)
        def _():
            cols = pl.ds(pl.multiple_of(g_id * W, W), W)
            copies = [pltpu.make_async_copy(dk_acc, dk_hbm.at[:, cols], out_sems.at[0]),
                      pltpu.make_async_copy(dv_acc, dv_hbm.at[:, cols], out_sems.at[1])]
            for cp in copies:
                cp.start()
            for cp in copies:
                cp.wait()

    qblk = pl.BlockSpec((bq, W), lambda g, i: (i, g))
    out = jax.ShapeDtypeStruct((T, width), F32)
    return pl.pallas_call(
        body, name=name, grid=(groups, nq),
        in_specs=[qblk, pl.BlockSpec(memory_space=pl.ANY), qblk, qblk],
        out_specs=[qblk, pl.BlockSpec(memory_space=pl.ANY), pl.BlockSpec(memory_space=pl.ANY)],
        out_shape=[out, out, out],
        scratch_shapes=[pltpu.VMEM((T, W), F32), pltpu.VMEM((T, W), F32), pltpu.VMEM((2, bq, W), BF16),
                        pltpu.VMEM((2, bq, W), BF16), pltpu.SemaphoreType.DMA((2, 2)), pltpu.SemaphoreType.DMA((2,))],
        compiler_params=_params(("arbitrary", "arbitrary"), 2 * T * W * 4),
    )(qkv, qkv, y, dy)


def _hg_constants():
    C = HG_CHUNK
    t = np.arange(C)[:, None]
    j = np.arange(C)[None, :]
    blocks = [(j <= t), (j > t)]
    masks = []
    for n in HG_LEVELS:
        right = (t % (2 * n)) >= n
        mid = (t // (2 * n)) * (2 * n) + n - 1
        blocks.append(right & (j > mid) & (j <= t))
        blocks.append((~right) & (j > t) & (j <= mid))
        tt, ss = np.arange(C)[:, None], np.arange(C)[None, :]
        same = (tt // (2 * n)) == (ss // (2 * n))
        masks.append(same & ((tt % (2 * n)) >= n) & ((ss % (2 * n)) < n))
    sums = np.concatenate(blocks, axis=0).astype(np.float32)
    return jnp.asarray(sums, BF16), jnp.asarray(np.stack(masks).astype(np.float32))


def _split3(x):
    hi = x.astype(BF16)
    r1 = x - hi.astype(F32)
    mid = r1.astype(BF16)
    lo = (r1 - mid.astype(F32)).astype(BF16)
    return hi, mid, lo


def _hg_gates(hq, hf, lb):
    sq = _sigmoid(hq)
    sf = _sigmoid(hf)
    f = lb + (1.0 - lb) * sf
    return hq * sq, sq, f, sf, 1.0 - f, jnp.log(f)


def _hg_exponents(sums, g):
    hi, mid, lo = _split3(g)
    return (jnp.dot(sums, hi, preferred_element_type=F32) + jnp.dot(sums, mid, preferred_element_type=F32)
            + jnp.dot(sums, lo, preferred_element_type=F32))


def _hg_level_scores(q, k, e_all):
    C = HG_CHUNK
    parts, prods = [], []
    for li in range(len(HG_LEVELS)):
        eq = jnp.exp(e_all[(2 + 2 * li) * C:(3 + 2 * li) * C])
        ek = jnp.exp(e_all[(3 + 2 * li) * C:(4 + 2 * li) * C])
        qt = q * eq
        kt = k * ek
        prods.append(lax.dot_general(qt.astype(BF16), kt.astype(BF16), _DIMS["nt"], preferred_element_type=F32))
        parts.append((eq, ek, qt, kt))
    return parts, prods


def _hg_intra(prods, masks_ref):
    a = masks_ref[0] * prods[0]
    for li in range(1, len(HG_LEVELS)):
        a = a + masks_ref[li] * prods[li]
    return a


def _lockstep(gens):
    gens = list(gens)
    results = [None] * len(gens)
    alive = list(range(len(gens)))
    while alive:
        for idx in list(alive):
            try:
                next(gens[idx])
            except StopIteration as done:
                results[idx] = done.value
                alive.remove(idx)
    return results


def _lower_bound(lbl_ref, cols):
    return _sigmoid(lbl_ref[0:1, cols] - lbl_ref[1:2, cols])


HG_HEADS_PER_STEP = 4


def _hg_fwd(proj, lb_logits, g_out, col0, width, name):
    T = proj.shape[0]
    heads = width // HEAD_DIM
    C = HG_CHUNK
    tb = _tile(T, 512, C)
    cpb = tb // C
    nb = T // tb
    sums, masks = _hg_constants()
    hps = min(HG_HEADS_PER_STEP, heads)
    groups, W = heads // hps, hps * HEAD_DIM
    assert col0 % W == 0 and width % W == 0
    cb = col0 // W

    def body(hq_ref, hf_ref, hi_ref, og_ref, lbl_ref, go_ref, sums_ref, masks_ref, y_ref, o_ref, st_ref, s_ref):
        i = pl.program_id(1)

        @pl.when(i == 0)
        def _():
            s_ref[...] = jnp.zeros_like(s_ref)

        go = go_ref[...]

        def chunk_head(c, rows, h):
            cols = slice(h * HEAD_DIM, (h + 1) * HEAD_DIM)
            lb = _lower_bound(lbl_ref, cols)
            q, _, _, _, k, g = _hg_gates(hq_ref[rows, cols], hf_ref[rows, cols], lb)
            v = hi_ref[rows, cols]
            vb = v.astype(BF16)
            st = s_ref[h]
            st_ref[h, c] = st
            e_all = _hg_exponents(sums_ref[...], g)
            yield
            b = e_all[0:C]
            ebl = jnp.exp(b[C - 1:C, :])
            qe = q * jnp.exp(b)
            o_inter = lax.dot_general(qe.astype(BF16), st.astype(BF16), _DIMS["nt"], preferred_element_type=F32)
            kd = k * jnp.exp(e_all[C:2 * C])
            s_new = lax.dot_general(vb, kd.astype(BF16), _DIMS["tn"], preferred_element_type=F32)
            _, prods = _hg_level_scores(q, k, e_all)
            yield
            a = _hg_intra(prods, masks_ref)
            o_intra = jnp.dot(a.astype(BF16), vb, preferred_element_type=F32)
            s_ref[h] = st * ebl + s_new
            yield
            o = (o_inter + o_intra) + jnp.sum(q * k, axis=1, keepdims=True) * v
            o_ref[rows, cols] = o
            r = lax.rsqrt(jnp.mean(o * o, axis=-1, keepdims=True) + EPS)
            og = og_ref[rows, cols]
            y_ref[rows, cols] = (((o * r) * go) * (og * _sigmoid(og))).astype(BF16)

        def chunk(c, _):
            rows = pl.ds(pl.multiple_of(c * C, C), C)
            _lockstep(chunk_head(c, rows, h) for h in range(hps))
            return 0

        lax.fori_loop(0, cpb, chunk, 0)

    def col(k):
        return pl.BlockSpec((tb, W), lambda g, i: (i, cb + k * groups + g))

    nsum = sums.shape[0]
    out_blk = pl.BlockSpec((tb, W), lambda g, i: (i, g))
    return pl.pallas_call(
        body, name=name, grid=(groups, nb),
        in_specs=[col(0), col(1), col(2), col(3),
                  pl.BlockSpec((2, W), lambda g, i: (0, g)),
                  pl.BlockSpec((1, HEAD_DIM), lambda g, i: (0, 0)),
                  pl.BlockSpec((nsum, C), lambda g, i: (0, 0)),
                  pl.BlockSpec((len(HG_LEVELS), C, C), lambda g, i: (0, 0, 0))],
        out_specs=[out_blk, out_blk,
                   pl.BlockSpec((hps, cpb, HEAD_DIM, HEAD_DIM), lambda g, i: (g, i, 0, 0))],
        out_shape=[jax.ShapeDtypeStruct((T, width), BF16), jax.ShapeDtypeStruct((T, width), F32),
                   jax.ShapeDtypeStruct((heads, T // C, HEAD_DIM, HEAD_DIM), F32)],
        scratch_shapes=[pltpu.VMEM((hps, HEAD_DIM, HEAD_DIM), F32)],
        compiler_params=_params(("parallel", "arbitrary"), tb * W * 4 * 7 + hps * cpb * HEAD_DIM * HEAD_DIM * 4),
    )(proj, proj, proj, proj, lb_logits, g_out, sums, masks)


def _hg_bwd(proj, lb_logits, g_out, o_saved, states, dy, dproj, col0, width, name):
    T = proj.shape[0]
    heads = width // HEAD_DIM
    C = HG_CHUNK
    tb = _tile(T, 512, C)
    cpb = tb // C
    nb = T // tb
    sums, masks = _hg_constants()
    nsum = sums.shape[0]
    nlev = len(HG_LEVELS)
    hps = min(HG_HEADS_PER_STEP, heads)
    groups, W = heads // hps, hps * HEAD_DIM
    cb = col0 // W

    def body(hq_ref, hf_ref, hi_ref, og_ref, lbl_ref, go_ref, sums_ref, masks_ref, o_ref, st_ref, dy_ref, dp_in,
             dp_ref, dlb_ref, dgo_ref, ds_ref, de_ref, dlb_acc, dgo_acc, dout_ref, out_sems):
        del dp_in
        grp = pl.program_id(0)
        i = pl.program_id(1)

        @pl.when(i == 0)
        def _():
            ds_ref[...] = jnp.zeros_like(ds_ref)
            dlb_acc[...] = jnp.zeros_like(dlb_acc)

        @pl.when(jnp.logical_and(grp == 0, i == 0))
        def _():
            dgo_acc[...] = jnp.zeros_like(dgo_acc)

        go = go_ref[...]
        last_row = lax.broadcasted_iota(jnp.int32, (C, HEAD_DIM), 0) == C - 1

        def chunk(n, _):
            c = cpb - 1 - n
            rows = pl.ds(pl.multiple_of(c * C, C), C)
            _lockstep(chunk_head(c, rows, h) for h in range(hps))
            return 0

        def chunk_head(c, rows, h):
            cols = slice(h * HEAD_DIM, (h + 1) * HEAD_DIM)
            de_h = de_ref.at[h]
            lb = _lower_bound(lbl_ref, cols)
            hq = hq_ref[rows, cols]
            og = og_ref[rows, cols]
            q, sq, f, sf, k, g = _hg_gates(hq, hf_ref[rows, cols], lb)
            v = hi_ref[rows, cols]
            vb = v.astype(BF16)
            st = st_ref[h, c]
            stb = st.astype(BF16)
            dst = ds_ref[h]
            dstb = dst.astype(BF16)
            o = o_ref[rows, cols]
            dyc = dy_ref[rows, cols].astype(F32)
            sg = _sigmoid(og)
            r = lax.rsqrt(jnp.mean(o * o, axis=-1, keepdims=True) + EPS)
            on = (o * r) * go
            don = dyc * (og * sg)
            dout_ref[3, rows, cols] = (dyc * on * (sg * (1.0 + og * (1.0 - sg)))).astype(BF16)
            dgo_acc[...] += jnp.sum((don * (o * r)).reshape(C // SUBLANE, SUBLANE, HEAD_DIM), axis=0)
            wn = don * go
            do = r * wn - o * (r * r * r * jnp.mean(o * wn, axis=-1, keepdims=True))
            dob = do.astype(BF16)
            e_all = _hg_exponents(sums_ref[...], g)
            da = lax.dot_general(dob, vb, _DIMS["nt"], preferred_element_type=F32)
            dqe = jnp.dot(dob, stb, preferred_element_type=F32)
            dkd = jnp.dot(vb, dstb, preferred_element_type=F32)
            yield
            eb = jnp.exp(e_all[0:C])
            esuf = jnp.exp(e_all[C:2 * C])
            ebl = eb[C - 1:C, :]
            qe = q * eb
            kd = k * esuf
            parts, prods = _hg_level_scores(q, k, e_all)
            dv_state = lax.dot_general(kd.astype(BF16), dstb, _DIMS["nt"], preferred_element_type=F32)
            ds_new = lax.dot_general(dob, qe.astype(BF16), _DIMS["tn"], preferred_element_type=F32)
            yield
            a = _hg_intra(prods, masks_ref)
            dv = lax.dot_general(a.astype(BF16), dob, _DIMS["tn"], preferred_element_type=F32)
            dlev = []
            for li in range(nlev):
                _, _, qt, kt = parts[li]
                dan = (masks_ref[li] * da).astype(BF16)
                dlev.append((jnp.dot(dan, kt.astype(BF16), preferred_element_type=F32),
                             lax.dot_general(dan, qt.astype(BF16), _DIMS["tn"], preferred_element_type=F32)))
            yield
            qk = jnp.sum(q * k, axis=1, keepdims=True)
            dv = dv + qk * do + dv_state
            dqk = jnp.sum(do * v, axis=1, keepdims=True)
            dq = dqk * k
            dk = dqk * q
            for li in range(nlev):
                eq, ek, qt, kt = parts[li]
                dqt, dkt = dlev[li]
                dq = dq + dqt * eq
                dk = dk + dkt * ek
                de_h[(2 + 2 * li) * C:(3 + 2 * li) * C, :] = dqt * qt
                de_h[(3 + 2 * li) * C:(4 + 2 * li) * C, :] = dkt * kt
            dq = dq + dqe * eb
            dk = dk + dkd * esuf
            debl = jnp.sum(dst * st, axis=0, keepdims=True)
            de_h[0:C, :] = dqe * qe + jnp.where(last_row, debl * ebl, 0.0)
            de_h[C:2 * C, :] = dkd * kd
            ds_ref[h] = dst * ebl + ds_new
            dehi, delo = _split2(de_h[...])
            dg = (lax.dot_general(sums_ref[...], dehi, _DIMS["tn"], preferred_element_type=F32)
                  + lax.dot_general(sums_ref[...], delo, _DIMS["tn"], preferred_element_type=F32))
            yield
            df = dg / f - dk
            dout_ref[0, rows, cols] = (dq * (sq * (1.0 + hq * (1.0 - sq)))).astype(BF16)
            dout_ref[1, rows, cols] = (df * (1.0 - lb) * (sf * (1.0 - sf))).astype(BF16)
            dout_ref[2, rows, cols] = dv.astype(BF16)
            dlb_acc[:, cols] += jnp.sum((df * (1.0 - sf)).reshape(C // SUBLANE, SUBLANE, HEAD_DIM), axis=0)

        lax.fori_loop(0, cpb, chunk, 0)

        rows = pl.ds(pl.multiple_of((nb - 1 - i) * tb, tb), tb)
        copies = [pltpu.make_async_copy(
            dout_ref.at[k], dp_ref.at[rows, pl.ds(pl.multiple_of((cb + k * groups + grp) * W, W), W)], out_sems.at[k])
            for k in range(4)]
        for cp in copies:
            cp.start()
        for cp in copies:
            cp.wait()

        @pl.when(i == nb - 1)
        def _():
            lb = _lower_bound(lbl_ref, slice(None))
            dl0 = jnp.sum(dlb_acc[...], axis=0, keepdims=True) * (lb * (1.0 - lb))
            dlb_ref[0:1, :] = dl0
            dlb_ref[1:2, :] = -dl0

        @pl.when(jnp.logical_and(grp == groups - 1, i == nb - 1))
        def _():
            dgo_ref[...] = jnp.sum(dgo_acc[...], axis=0, keepdims=True)

    def col(k):
        return pl.BlockSpec((tb, W), lambda g, i: (nb - 1 - i, cb + k * groups + g))

    rev = pl.BlockSpec((tb, W), lambda g, i: (nb - 1 - i, g))
    return pl.pallas_call(
        body, name=name, grid=(groups, nb),
        in_specs=[col(0), col(1), col(2), col(3),
                  pl.BlockSpec((2, W), lambda g, i: (0, g)),
                  pl.BlockSpec((1, HEAD_DIM), lambda g, i: (0, 0)),
                  pl.BlockSpec((nsum, C), lambda g, i: (0, 0)),
                  pl.BlockSpec((nlev, C, C), lambda g, i: (0, 0, 0)),
                  rev,
                  pl.BlockSpec((hps, cpb, HEAD_DIM, HEAD_DIM), lambda g, i: (g, nb - 1 - i, 0, 0)),
                  rev,
                  pl.BlockSpec(memory_space=pl.ANY)],
        out_specs=[pl.BlockSpec(memory_space=pl.ANY),
                   pl.BlockSpec((2, W), lambda g, i: (0, g)),
                   pl.BlockSpec((1, HEAD_DIM), lambda g, i: (0, 0))],
        out_shape=[jax.ShapeDtypeStruct(dproj.shape, BF16), jax.ShapeDtypeStruct((2, width), F32),
                   jax.ShapeDtypeStruct((1, HEAD_DIM), F32)],
        input_output_aliases={11: 0},
        scratch_shapes=[pltpu.VMEM((hps, HEAD_DIM, HEAD_DIM), F32), pltpu.VMEM((hps, nsum, HEAD_DIM), F32),
                        pltpu.VMEM((SUBLANE, W), F32), pltpu.VMEM((SUBLANE, HEAD_DIM), F32),
                        pltpu.VMEM((4, tb, W), BF16), pltpu.SemaphoreType.DMA((4,))],
        compiler_params=_params(("arbitrary", "arbitrary"), tb * W * 4 * 12 + hps * cpb * HEAD_DIM * HEAD_DIM * 4),
    )(proj, proj, proj, proj, lb_logits, g_out, sums, masks, o_saved, states, dy, dproj)


def _merge_fwd(proj, ma, mb, gate_col0, name):
    T, D = ma.shape
    tr = _tile(T, 256, SUBLANE)
    cw = _tile(D, 1024, LANE)
    nj = D // cw
    assert gate_col0 % cw == 0
    g0 = gate_col0 // cw

    def body(ga_ref, gb_ref, ma_ref, mb_ref, o_ref):
        o_ref[...] = (_sigmoid(ga_ref[...]) * ma_ref[...] + _sigmoid(gb_ref[...]) * mb_ref[...]).astype(BF16)

    blk = pl.BlockSpec((tr, cw), lambda i, j: (i, j))
    return pl.pallas_call(
        body, name=name, grid=(T // tr, nj),
        in_specs=[pl.BlockSpec((tr, cw), lambda i, j: (i, g0 + j)),
                  pl.BlockSpec((tr, cw), lambda i, j: (i, g0 + nj + j)), blk, blk],
        out_specs=blk, out_shape=jax.ShapeDtypeStruct((T, D), BF16),
        compiler_params=_params(("parallel", "parallel"), tr * cw * 18 * 2),
    )(proj, proj, ma, mb)


def _merge_bwd(proj, ma, mb, dm, gate_col0, name):
    T, D = ma.shape
    tr = _tile(T, 256, SUBLANE)
    cw = _tile(D, 1024, LANE)
    nj = D // cw
    g0 = gate_col0 // cw

    def body(g_ref, ma_ref, mb_ref, dm_ref, dp_ref, dmm_ref):
        branch = pl.program_id(2)
        s = _sigmoid(g_ref[...])
        dmv = dm_ref[...]
        mm = jnp.where(branch == 0, ma_ref[...], mb_ref[...])
        dp_ref[...] = (dmv * mm * (s * (1.0 - s))).astype(BF16)
        dmm_ref[...] = (dmv * s).astype(BF16)

    blk = pl.BlockSpec((tr, cw), lambda i, j, b: (i, j))
    gate = pl.BlockSpec((tr, cw), lambda i, j, b: (i, g0 + b * nj + j))
    return pl.pallas_call(
        body, name=name, grid=(T // tr, nj, 2),
        in_specs=[gate, blk, blk, blk],
        out_specs=[gate, pl.BlockSpec((None, tr, cw), lambda i, j, b: (b, i, j))],
        out_shape=[jax.ShapeDtypeStruct(proj.shape, BF16), jax.ShapeDtypeStruct((2, T, D), BF16)],
        compiler_params=_params(("parallel", "parallel", "arbitrary"), tr * cw * 20 * 2),
    )(proj, ma, mb, dm)


def _conv_fwd(up, convw, convb, name):
    T, F2 = up.shape
    tc = convw.shape[2]
    half = (F2 // 2) // tc
    assert half * tc * 2 == F2
    tr = _tile(T, 256, SUBLANE)
    hb = tr // SUBLANE

    def body(g_ref, gp_ref, v_ref, vp_ref, wg_ref, wv_ref, bg_ref, bv_ref, o_ref, gx_ref, vx_ref):
        first = pl.program_id(1) == 0

        def lanes(c, _):
            cols = pl.ds(pl.multiple_of(c * LANE, LANE), LANE)

            def conv(cur_ref, prev_ref, w_ref, b_ref, x_ref):
                x_ref[0:SUBLANE, :] = jnp.where(first, 0.0, prev_ref[:, cols])
                x_ref[SUBLANE:, :] = cur_ref[:, cols]
                w = w_ref[:, cols]
                out = b_ref[:, cols] + w[0:1, :] * x_ref[pl.ds(SUBLANE - 2, tr), :]
                out = out + w[1:2, :] * x_ref[pl.ds(SUBLANE - 1, tr), :]
                return out + w[2:3, :] * x_ref[pl.ds(SUBLANE, tr), :]

            gate = conv(g_ref, gp_ref, wg_ref, bg_ref, gx_ref)
            val = conv(v_ref, vp_ref, wv_ref, bv_ref, vx_ref)
            o_ref[:, cols] = ((gate * _sigmoid(gate)) * val).astype(BF16)
            return 0

        lax.fori_loop(0, tc // LANE, lanes, 0)

    def main(off):
        return pl.BlockSpec((tr, tc), lambda j, i: (i, off + j))

    def prev(off):
        return pl.BlockSpec((SUBLANE, tc), lambda j, i: (jnp.maximum(i * hb - 1, 0), off + j))

    def wspec(off):
        return pl.BlockSpec((None, 3, tc), lambda j, i: (off + j, 0, 0))

    def bspec(off):
        return pl.BlockSpec((1, tc), lambda j, i: (0, off + j))

    return pl.pallas_call(
        body, name=name, grid=(half, T // tr),
        in_specs=[main(0), prev(0), main(half), prev(half), wspec(0), wspec(half), bspec(0), bspec(half)],
        out_specs=pl.BlockSpec((tr, tc), lambda j, i: (i, j)),
        out_shape=jax.ShapeDtypeStruct((T, F2 // 2), BF16),
        scratch_shapes=[pltpu.VMEM((SUBLANE + tr, LANE), F32), pltpu.VMEM((SUBLANE + tr, LANE), F32)],
        compiler_params=_params(("parallel", "parallel"), tr * tc * 4 * 12),
    )(up, up, up, up, convw, convw, convb, convb)


def _conv_bwd(da, up, convw, convb, name):
    T, F2 = up.shape
    tc = convw.shape[2]
    half = (F2 // 2) // tc
    tr = _tile(T, 128, SUBLANE)
    hb = tr // SUBLANE
    nrow = T // tr
    n = tr + 2 * SUBLANE

    def body(g_ref, gp_ref, gn_ref, v_ref, vp_ref, vn_ref, da_ref, dan_ref, wg_ref, wv_ref, bg_ref, bv_ref,
             dup_ref, gw_ref, acc_ref, gx_ref, vx_ref, dg_ref, dv_ref):
        i = pl.program_id(1)
        first = i == 0
        last = i == nrow - 1

        @pl.when(first)
        def _():
            acc_ref[...] = jnp.zeros_like(acc_ref)

        m = tr + SUBLANE

        def fold(t):
            return jnp.sum(t.reshape(tr // SUBLANE, SUBLANE, LANE), axis=0)

        def lanes(c, _):
            cols = pl.ds(pl.multiple_of(c * LANE, LANE), LANE)

            def conv(cur_ref, prev_ref, next_ref, w_ref, b_ref, x_ref):
                x_ref[0:SUBLANE, :] = jnp.where(first, 0.0, prev_ref[:, cols])
                x_ref[SUBLANE:SUBLANE + tr, :] = cur_ref[:, cols]
                x_ref[SUBLANE + tr:, :] = next_ref[:, cols]
                w = w_ref[:, cols]
                taps = tuple(x_ref[pl.ds(SUBLANE - 2 + k, m), :] for k in range(3))
                out = b_ref[:, cols] + w[0:1, :] * taps[0]
                out = out + w[1:2, :] * taps[1]
                return out + w[2:3, :] * taps[2], taps, w

            gate, g_taps, wg = conv(g_ref, gp_ref, gn_ref, wg_ref, bg_ref, gx_ref)
            val, v_taps, wv = conv(v_ref, vp_ref, vn_ref, wv_ref, bv_ref, vx_ref)
            da_m = jnp.concatenate([da_ref[:, cols], jnp.where(last, 0.0, dan_ref[:, cols])], axis=0)
            sg = _sigmoid(gate)
            dg_ref[...] = da_m * val * (sg * (1.0 + gate * (1.0 - sg)))
            dv_ref[...] = da_m * (gate * sg)
            for hf, (d_ref, w, taps) in enumerate(((dg_ref, wg, g_taps), (dv_ref, wv, v_taps))):
                dc = d_ref[pl.ds(0, tr), :]
                dup = w[2:3, :] * dc + w[1:2, :] * d_ref[pl.ds(1, tr), :] + w[0:1, :] * d_ref[pl.ds(2, tr), :]
                dup_ref[hf, :, cols] = dup.astype(BF16)
                acc_ref[hf, 0, :, cols] += fold(dc)
                for k in range(3):
                    acc_ref[hf, 1 + k, :, cols] += fold(dc * taps[k][0:tr])
            return 0

        lax.fori_loop(0, tc // LANE, lanes, 0)

        @pl.when(last)
        def _():
            gw_ref[...] = jnp.sum(acc_ref[...], axis=2)

    def main(off):
        return pl.BlockSpec((tr, tc), lambda j, i: (i, off + j))

    def prev(off):
        return pl.BlockSpec((SUBLANE, tc), lambda j, i: (jnp.maximum(i * hb - 1, 0), off + j))

    def nxt(off):
        return pl.BlockSpec((SUBLANE, tc), lambda j, i: (jnp.minimum((i + 1) * hb, T // SUBLANE - 1), off + j))

    def wspec(off):
        return pl.BlockSpec((None, 3, tc), lambda j, i: (off + j, 0, 0))

    def bspec(off):
        return pl.BlockSpec((1, tc), lambda j, i: (0, off + j))

    return pl.pallas_call(
        body, name=name, grid=(half, nrow),
        in_specs=[main(0), prev(0), nxt(0), main(half), prev(half), nxt(half), main(0), nxt(0),
                  wspec(0), wspec(half), bspec(0), bspec(half)],
        out_specs=[pl.BlockSpec((2, tr, tc), lambda j, i: (0, i, j)), pl.BlockSpec((2, 4, tc), lambda j, i: (0, 0, j))],
        out_shape=[jax.ShapeDtypeStruct((2, T, F2 // 2), BF16), jax.ShapeDtypeStruct((2, 4, F2 // 2), F32)],
        scratch_shapes=[pltpu.VMEM((2, 4, SUBLANE, tc), F32), pltpu.VMEM((n, LANE), F32), pltpu.VMEM((n, LANE), F32),
                        pltpu.VMEM((tr + SUBLANE, LANE), F32), pltpu.VMEM((tr + SUBLANE, LANE), F32)],
        compiler_params=_params(("parallel", "arbitrary"), n * tc * 4 * 24),
    )(up, up, up, up, up, up, da, da, convw, convw, convb, convb)


def _loss_head(y, target, name):
    T, D = y.shape
    tr = _tile(T, 256, SUBLANE)
    nrow = T // tr

    def body(y_ref, t_ref, d_ref, db_ref, l_ref, acc_ref):
        i = pl.program_id(0)
        diff = y_ref[...] - t_ref[...]
        dy = diff / D
        d_ref[...] = dy
        db_ref[...] = dy.astype(BF16)
        part = jnp.sum((diff * diff).reshape(tr // SUBLANE, SUBLANE, D), axis=0)

        @pl.when(i == 0)
        def _():
            acc_ref[...] = part

        @pl.when(i > 0)
        def _():
            acc_ref[...] += part

        @pl.when(i == nrow - 1)
        def _():
            col = jnp.sum(acc_ref[...], axis=0, keepdims=True)
            l_ref[...] = jnp.broadcast_to(0.5 * (jnp.sum(col, axis=1, keepdims=True) / D), (1, LANE))

    row = pl.BlockSpec((tr, D), lambda i: (i, 0))
    return pl.pallas_call(
        body, name=name, grid=(nrow,), in_specs=[row, row],
        out_specs=[row, row, pl.BlockSpec((1, LANE), lambda i: (0, 0))],
        out_shape=[jax.ShapeDtypeStruct((T, D), F32), jax.ShapeDtypeStruct((T, D), BF16),
                   jax.ShapeDtypeStruct((1, LANE), F32)],
        scratch_shapes=[pltpu.VMEM((SUBLANE, D), F32)],
        compiler_params=_params(("arbitrary",), tr * D * 14 * 2),
    )(y, target)


def _adamw(w, parts, m, v, name):
    R, C = w.shape
    P = parts.shape[0]
    tr = _tile(R, 64, SUBLANE)
    tc = _tile(C, 2048, LANE)

    def body(w_ref, p_ref, m_ref, v_ref, g_ref, d_ref, nm_ref, nv_ref):
        g = p_ref[0].astype(F32)
        for s in range(1, P):
            g = g + p_ref[s].astype(F32)
        wv = w_ref[...]
        nm = ADAM_B1 * m_ref[...] + (1.0 - ADAM_B1) * g
        nv = ADAM_B2 * v_ref[...] + (1.0 - ADAM_B2) * (g * g)
        m_hat = nm / (1.0 - ADAM_B1 ** ADAM_STEP)
        v_hat = nv / (1.0 - ADAM_B2 ** ADAM_STEP)
        g_ref[...] = g
        d_ref[...] = -ADAM_LR * (m_hat / (jnp.sqrt(v_hat) + ADAM_EPS) + ADAM_WD * wv)
        nm_ref[...] = nm
        nv_ref[...] = nv

    blk = pl.BlockSpec((tr, tc), lambda i, j: (i, j))
    shp = jax.ShapeDtypeStruct((R, C), F32)
    return pl.pallas_call(
        body, name=name, grid=(R // tr, C // tc),
        in_specs=[blk, pl.BlockSpec((P, tr, tc), lambda i, j: (0, i, j)), blk, blk],
        out_specs=[blk] * 4, out_shape=[shp] * 4,
        compiler_params=_params(("parallel", "parallel"), tr * tc * 4 * (P + 8) * 2),
    )(w, parts, m, v)


def _place():
    x, y, c = (lax.axis_index(a) for a in MESH_AXES)
    return x, y, c, 4 * x + 2 * y + c


def _peers(x, y, c):
    out = []
    for d in range(1, N_DEV):
        px = x + (d >> 2 & 1) - 2 * x * (d >> 2 & 1)
        py = y + (d >> 1 & 1) - 2 * y * (d >> 1 & 1)
        pc = c + (d & 1) - 2 * c * (d & 1)
        out.append(((px, py, pc), 4 * px + 2 * py + pc))
    return out


class _Side:
    def __init__(self, arrays, scatter, side_by_side=()):
        self.arrays, self.scatter, self.n = list(arrays), scatter, len(arrays)
        self.side_by_side = set(side_by_side)
        assert not (scatter and self.side_by_side)

    def in_specs(self):
        return [pl.BlockSpec(memory_space=pltpu.HBM)] * self.n

    out_specs = in_specs

    def out_shape(self):
        def shape(t, a):
            if t in self.side_by_side:
                return (a.shape[0], N_DEV * a.shape[1])
            return a.shape if self.scatter else (N_DEV,) + a.shape
        return [jax.ShapeDtypeStruct(shape(t, a), a.dtype) for t, a in enumerate(self.arrays)]

    def _slot(self, t, out, idx):
        if t in self.side_by_side:
            w = self.arrays[t].shape[1]
            return out.at[:, pl.ds(pl.multiple_of(idx * w, w), w)]
        return out.at[idx]

    def scratch(self):
        return [pltpu.SemaphoreType.DMA((self.n, N_DEV - 1)), pltpu.SemaphoreType.DMA((self.n, N_DEV - 1)),
                pltpu.SemaphoreType.DMA((self.n,))]

    def _copies(self, ins, outs, sems):
        send_sems, recv_sems, local_sems = sems
        x, y, c, me = _place()
        peers = _peers(x, y, c)
        local, sends, recvs = [], [], []
        for t in range(self.n):
            src_me = ins[t].at[me] if self.scatter else ins[t]
            local.append(pltpu.make_async_copy(src_me, self._slot(t, outs[t], me), local_sems.at[t]))
            for d, (peer, pidx) in enumerate(peers):
                src = ins[t].at[pidx] if self.scatter else ins[t]
                sends.append(pltpu.make_async_remote_copy(
                    src_ref=src, dst_ref=self._slot(t, outs[t], me), send_sem=send_sems.at[t, d],
                    recv_sem=recv_sems.at[t, d], device_id=peer, device_id_type=MESH_ID))
                recvs.append(pltpu.make_async_remote_copy(
                    src_ref=src, dst_ref=self._slot(t, outs[t], pidx), send_sem=send_sems.at[t, d],
                    recv_sem=recv_sems.at[t, d], device_id=peer, device_id_type=MESH_ID))
        return local, sends, recvs

    def start(self, ins, outs, sems):
        local, sends, _ = self._copies(ins, outs, sems)
        for cp in local + sends:
            cp.start()

    def wait(self, ins, outs, sems):
        local, sends, recvs = self._copies(ins, outs, sems)
        for cp in recvs:
            cp.wait_recv()
        for cp in sends:
            cp.wait_send()
        for cp in local:
            cp.wait()


def _grid_edges(sizes):
    ids = [pl.program_id(a) for a in range(len(sizes))]
    first = functools.reduce(jnp.logical_and, [i == 0 for i in ids])
    last = functools.reduce(jnp.logical_and, [i == s - 1 for i, s in zip(ids, sizes)])
    return first, last


def _gather_two_level(arrays, name):
    n = len(arrays)

    def body(*refs):
        ins, outs = refs[:n], refs[n:2 * n]
        send_sems, recv_sems, local_sems = refs[2 * n:]
        x, y, c, me = _place()
        sibling = (x, y, 1 - c)
        chips = [(1 - x, y), (x, 1 - y), (1 - x, 1 - y)]

        def slot(px, py, pc):
            return 4 * px + 2 * py + pc

        def copy(t, k, block, to, src=None):
            return pltpu.make_async_remote_copy(
                src_ref=outs[t].at[slot(*block)] if src is None else src, dst_ref=outs[t].at[slot(*block)],
                send_sem=send_sems.at[t, k], recv_sem=recv_sems.at[t, k], device_id=to, device_id_type=MESH_ID)

        mine = [pltpu.make_async_copy(ins[t], outs[t].at[me], local_sems.at[t]) for t in range(n)]
        first = [copy(t, 0, (x, y, c), sibling, src=ins[t]) for t in range(n)]
        first += [copy(t, 1 + j, (x, y, c), (*chip, c), src=ins[t]) for t in range(n) for j, chip in enumerate(chips)]
        for cp in mine + first:
            cp.start()
        passed = []
        for t in range(n):
            for j, chip in enumerate(chips):
                copy(t, 1 + j, (*chip, c), (x, y, c)).wait_recv()
                cp = copy(t, 4 + j, (*chip, c), sibling)
                cp.start()
                passed.append(cp)
        for t in range(n):
            copy(t, 0, sibling, (x, y, c)).wait_recv()
            for j, chip in enumerate(chips):
                copy(t, 4 + j, (*chip, 1 - c), (x, y, c)).wait_recv()
        for cp in first + passed:
            cp.wait_send()
        for cp in mine:
            cp.wait()

    hbm = pl.BlockSpec(memory_space=pltpu.HBM)
    return pl.pallas_call(
        body, name=name, in_specs=[hbm] * n, out_specs=[hbm] * n,
        out_shape=[jax.ShapeDtypeStruct((N_DEV,) + a.shape, a.dtype) for a in arrays],
        scratch_shapes=[pltpu.SemaphoreType.DMA((n, N_DEV - 1)), pltpu.SemaphoreType.DMA((n, N_DEV - 1)),
                        pltpu.SemaphoreType.DMA((n,))],
    )(*arrays)


def _all_reduce_small(vec, name):
    R = vec.shape[0]

    def body(v_ref, o_ref, gath_ref, send_sems, recv_sems):
        x, y, c, me = _place()
        sends, recvs = [], []
        for d, (peer, pidx) in enumerate(_peers(x, y, c)):
            cp = pltpu.make_async_remote_copy(
                src_ref=v_ref, dst_ref=gath_ref.at[me], send_sem=send_sems.at[d], recv_sem=recv_sems.at[d],
                device_id=peer, device_id_type=MESH_ID)
            cp.start()
            sends.append(cp)
            recvs.append(pltpu.make_async_remote_copy(
                src_ref=v_ref, dst_ref=gath_ref.at[pidx], send_sem=send_sems.at[d], recv_sem=recv_sems.at[d],
                device_id=peer, device_id_type=MESH_ID))
        gath_ref[me] = v_ref[...]
        for cp in recvs:
            cp.wait_recv()
        for cp in sends:
            cp.wait_send()
        total = gath_ref[0]
        for s in range(1, N_DEV):
            total = total + gath_ref[s]
        o_ref[...] = total

    vm = pl.BlockSpec(memory_space=pltpu.VMEM)
    return pl.pallas_call(
        body, name=name, in_specs=[vm], out_specs=vm, out_shape=jax.ShapeDtypeStruct(vec.shape, F32),
        scratch_shapes=[pltpu.VMEM((N_DEV, R, LANE), F32), pltpu.SemaphoreType.DMA((N_DEV - 1,)),
                        pltpu.SemaphoreType.DMA((N_DEV - 1,))],
        compiler_params=pltpu.CompilerParams(vmem_limit_bytes=_vmem(R * LANE * 4 * 12)),
    )(vec)


def _pack(parts):
    flat = jnp.concatenate([p.reshape(-1).astype(F32) for p in parts])
    rows = -(-flat.shape[0] // (LANE * SUBLANE)) * SUBLANE
    return jnp.pad(flat, (0, rows * LANE - flat.shape[0])).reshape(rows, LANE)


def _unpack(packed, shapes):
    flat = packed.reshape(-1)
    out, at = [], 0
    for s in shapes:
        size = int(np.prod(s))
        out.append(flat[at:at + size].reshape(s))
        at += size
    return out


def kernel(x, g_mix, w_in, g_q, g_k, lb_logits, g_hg_out, p_a, p_b, w_o, g_ffn, w_up, conv_w, conv_b, w_down, loss_target, m_g_mix, m_w_in, m_g_q, m_g_k, m_lb_logits, m_g_hg_out, m_p_a, m_p_b, m_w_o, m_g_ffn, m_w_up, m_conv_w, m_conv_b, m_w_down, v_g_mix, v_w_in, v_g_q, v_g_k, v_lb_logits, v_g_hg_out, v_p_a, v_p_b, v_w_o, v_g_ffn, v_w_up, v_conv_w, v_conv_b, v_w_down):
    assert x.shape[0] == 1 and lb_logits.shape[0] == 2
    xs, target = x[0], loss_target[0]
    T, D = xs.shape
    A = p_a.shape[1]
    HW = p_b.shape[1]
    hg_col0 = 3 * A
    gate_col0 = 3 * A + 4 * HW
    F = w_down.shape[1] * N_DEV
    assert w_in.shape[2] * N_DEV == gate_col0 + 2 * D and w_up.shape[2] * N_DEV == 2 * F

    (win_g,) = _gather_two_level([w_in[0].astype(BF16)], "gather_w_in")
    later_weights = _Side([w_up[0].astype(BF16), p_a[0].astype(BF16), p_b[0].astype(BF16), w_o[0].astype(BF16),
                           w_down[0].astype(BF16), conv_w[0]], False, side_by_side=(1, 2))

    u = _rmsnorm_fwd(xs, g_mix, "norm_mix")
    proj = _matmul(u, win_g, mode="nn", b_shards=True, tm_pref=1024, name="proj_in")
    gains = jnp.stack([g_q[0], g_k[0], jnp.ones_like(g_q[0])])[:, None, :]
    qkv = _qkv_prep(proj, gains, A, "qkv_prep")
    ya, ya32, (wup_g, pa_g, pb_g, wo_g, wdown_g, convw_g) = _sb_fwd(qkv, A, "sb_fwd", side=later_weights)
    wo_full = wo_g.reshape(D, D)
    wdown_full = wdown_g.reshape(F, D)
    yb, ob, states = _hg_fwd(proj, lb_logits, g_hg_out, hg_col0, HW, "hg_fwd")
    ma = _matmul(ya, pa_g, mode="nn", name="proj_a")
    mb = _matmul(yb, pb_g, mode="nn", name="proj_b")
    m = _merge_fwd(proj, ma, mb, gate_col0, "merge_fwd")
    h1 = _matmul(m, wo_full, mode="nn", add=xs, name="proj_o")
    u2 = _rmsnorm_fwd(h1, g_ffn, "norm_ffn")
    up = _matmul(u2, wup_g, mode="nn", b_shards=True, tm_pref=1024, name="ffn_up")
    act = _conv_fwd(up, convw_g, conv_b, "conv_fwd")
    y = _matmul(act, wdown_full, mode="nn", add=h1, tn_pref=2048, name="ffn_down")
    dy, dyb, loss_part = _loss_head(y, target, "loss_head")
    loss = lax.psum(loss_part[0, 0], MESH_AXES)

    dact = _matmul(dyb, wdown_full, mode="nt", tm_pref=1024, name="d_act")
    g_wdown = _matmul(act, dyb, mode="tn", out_dtype=BF16, name="g_w_down")
    dup, g_conv = _conv_bwd(dact, up, convw_g, conv_b, "conv_bwd")
    g_wup, (r_wdown,) = _matmul(u2, dup, mode="tn", b_halves=True, out_shards=True, out_dtype=BF16, tm_pref=1024,
                                name="g_w_up",
                                side=_Side([g_wdown.reshape(N_DEV, F // N_DEV, D)], True))
    du2, (r_wup,) = _matmul(dup, wup_g, mode="nt", a_halves=True, b_shards=True, tn_pref=2048, name="d_u2",
                            side=_Side([g_wup], True))
    dh1, dh1b, g_gffn = _rmsnorm_bwd(h1, g_ffn, du2, dy, "norm_ffn_bwd")
    dm = _matmul(dh1b, wo_full, mode="nt", name="d_m")
    g_wo = _matmul(m, dh1b, mode="tn", out_dtype=BF16, tm_pref=1024, name="g_w_o")
    dproj, dmab = _merge_bwd(proj, ma, mb, dm, gate_col0, "merge_bwd")
    dya = _matmul(dmab, pa_g, mode="nt", lead_a=0, out_dtype=BF16, name="d_ya")
    g_pa = _matmul(ya, dmab, mode="tn", out_shards=True, out_dtype=BF16, lead_b=0, name="g_p_a")
    dyb_ = _matmul(dmab, pb_g, mode="nt", lead_a=1, out_dtype=BF16, name="d_yb")
    g_pb = _matmul(yb, dmab, mode="tn", out_shards=True, out_dtype=BF16, lead_b=1, name="g_p_b")
    dproj, g_lb, g_ghg = _hg_bwd(proj, lb_logits, g_hg_out, ob, states, dyb_, dproj, hg_col0, HW, "hg_bwd")
    dq, dk, dv = _sb_bwd(qkv, ya32, dya, A, "sb_bwd")
    dproj, g_gains = _qkv_bwd(proj, gains, dq, dk, dv, dproj, "qkv_bwd")
    g_win, (r_wo, r_pa, r_pb) = _matmul(
        u, dproj, mode="tn", out_shards=True, out_dtype=BF16, tm_pref=1024, name="g_w_in",
        side=_Side([g_wo.reshape(N_DEV, D // N_DEV, D), g_pa, g_pb], True))
    du, (r_win,) = _matmul(dproj, win_g, mode="nt", b_shards=True, tn_pref=2048, name="d_u",
                           side=_Side([g_win], True))
    gx, _, g_gmix = _rmsnorm_bwd(xs, g_mix, du, dh1, "norm_mix_bwd")

    g_convb = g_conv[:, 0].reshape(1, 2 * F)
    g_convw = g_conv[:, 1:4].transpose(1, 0, 2).reshape(3, 2 * F)
    small = [g_gmix, g_gains[0], g_gains[1], g_lb, g_ghg, g_gffn, g_convb, g_convw]
    small_shapes = [p.shape for p in small]
    red = _unpack(_all_reduce_small(_pack(small), "reduce_small"), small_shapes)

    _, _, _, me = _place()
    cs = conv_w.shape[2]
    big = {
        "w_in": _adamw(w_in[0], r_win, m_w_in[0], v_w_in[0], "adamw_w_in"),
        "w_up": _adamw(w_up[0], r_wup, m_w_up[0], v_w_up[0], "adamw_w_up"),
        "p_a": _adamw(p_a[0], r_pa, m_p_a[0], v_p_a[0], "adamw_p_a"),
        "p_b": _adamw(p_b[0], r_pb, m_p_b[0], v_p_b[0], "adamw_p_b"),
        "w_o": _adamw(w_o[0], r_wo, m_w_o[0], v_w_o[0], "adamw_w_o"),
        "w_down": _adamw(w_down[0], r_wdown, m_w_down[0], v_w_down[0], "adamw_w_down"),
        "conv_w": _adamw(conv_w[0], lax.dynamic_slice_in_dim(red[7], me * cs, cs, axis=1)[None],
                         m_conv_w[0], v_conv_w[0], "adamw_conv_w"),
    }
    rep_w = [g_mix, g_q, g_k, lb_logits, g_hg_out, g_ffn, conv_b]
    rep_m = [m_g_mix, m_g_q, m_g_k, m_lb_logits, m_g_hg_out, m_g_ffn, m_conv_b]
    rep_v = [v_g_mix, v_g_q, v_g_k, v_lb_logits, v_g_hg_out, v_g_ffn, v_conv_b]
    rep_shapes = [p.shape for p in rep_w]
    rep_out = _adamw(_pack(rep_w), _pack(red[:7])[None], _pack(rep_m), _pack(rep_v), "adamw_small")
    rep = [_unpack(o, rep_shapes) for o in rep_out]
    rep_names = ["g_mix", "g_q", "g_k", "lb_logits", "g_hg_out", "g_ffn", "conv_b"]

    order = ["g_mix", "w_in", "g_q", "g_k", "lb_logits", "g_hg_out", "p_a", "p_b", "w_o", "g_ffn", "w_up",
             "conv_w", "conv_b", "w_down"]

    def leaf(kind, pname):
        if pname in big:
            return big[pname][kind][None]
        return rep[kind][rep_names.index(pname)]

    return (loss, gx[None], *[leaf(kind, p) for kind in range(4) for p in order])
```

```python
import functools

import numpy as np
import jax
import jax.numpy as jnp
from jax import lax
from jax.experimental import pallas as pl
from jax.experimental.pallas import tpu as pltpu

F32 = jnp.float32
BF16 = jnp.bfloat16

N_DEV = 8
HEAD_DIM = 128
HG_CHUNK = 64
HG_LEVELS = (1, 2, 4, 8, 16, 32)
EPS = 1e-6
ADAM_LR = 0.001
ADAM_B1 = 0.9
ADAM_B2 = 0.999
ADAM_EPS = 1e-08
ADAM_WD = 0.01
ADAM_STEP = 10
LANE = 128
SUBLANE = 8
VMEM_CAP = 56 << 20
MESH_AXES = ("x", "y", "c")
MESH_ID = pl.DeviceIdType.MESH


def _tile(dim, pref, align):
    if dim <= pref:
        return dim
    t = (pref // align) * align
    while t >= align:
        if dim % t == 0:
            return t
        t -= align
    return dim


def _vmem(est_bytes):
    return int(min(max(2 * est_bytes + (8 << 20), 32 << 20), VMEM_CAP))


def _params(sem, est_bytes):
    return pltpu.CompilerParams(dimension_semantics=sem, vmem_limit_bytes=_vmem(est_bytes))


def _sigmoid(x):
    return 1.0 / (1.0 + jnp.exp(-x))


_DIMS = {"nn": (((1,), (0,)), ((), ())), "nt": (((1,), (1,)), ((), ())), "tn": (((0,), (0,)), ((), ()))}


def _matmul(a, b, *, mode, name, out_dtype=F32, add=None, b_shards=False, out_shards=False, lead_a=None,
            lead_b=None, a_halves=False, b_halves=False, side=None, tm_pref=512, tn_pref=1408, tk_pref=2048):
    if mode == "tn":
        K, M = a.shape[-2:]
    else:
        M, K = a.shape[-2:]
    if a_halves:
        assert mode == "nt" and b_shards and lead_a is None
        K = 2 * K
    if lead_b is not None or b_halves:
        assert not b_shards and mode == "tn"
    if b_shards:
        S = b.shape[0]
        if mode == "nn":
            assert b.shape[1] == K
            N, tn, tk = S * b.shape[2], b.shape[2], _tile(K, tk_pref, LANE)
        else:
            assert mode == "nt" and S * b.shape[2] == K
            N, tk, tn = b.shape[1], b.shape[2], _tile(b.shape[1], tn_pref, LANE)
    else:
        if mode == "nt":
            N = b.shape[0]
            assert b.shape[1] == K
        else:
            N = b.shape[-1] * (2 if b_halves else 1)
            assert b.shape[-2] == K
        tn = _tile(N, tn_pref, LANE)
        tk = _tile(K, tk_pref, LANE)
    if out_shards:
        assert N % N_DEV == 0
        tn = N // N_DEV
    per_half_n, per_half_k = (N // 2) // tn, (K // 2) // tk
    assert not b_halves or per_half_n * tn * 2 == N
    assert not a_halves or per_half_k * tk * 2 == K
    tm = _tile(M, tm_pref, LANE)
    nm, nn, nk = M // tm, N // tn, K // tk
    assert nm * tm == M and nn * tn == N and nk * tk == K

    if mode == "tn":
        a_spec = pl.BlockSpec((tk, tm), lambda j, i, k: (k, i))
    elif lead_a is not None:
        a_spec = pl.BlockSpec((None, tm, tk), lambda j, i, k: (lead_a, i, k))
    elif a_halves:
        a_spec = pl.BlockSpec((None, tm, tk), lambda j, i, k: (k // per_half_k, i, k % per_half_k))
    else:
        a_spec = pl.BlockSpec((tm, tk), lambda j, i, k: (i, k))
    if b_halves:
        b_spec = pl.BlockSpec((None, tk, tn), lambda j, i, k: (j // per_half_n, k, j % per_half_n))
    elif lead_b is not None:
        b_spec = pl.BlockSpec((None, tk, tn), lambda j, i, k: (lead_b, k, j))
    elif b_shards and mode == "nn":
        b_spec = pl.BlockSpec((None, tk, tn), lambda j, i, k: (j, k, 0))
    elif b_shards:
        b_spec = pl.BlockSpec((None, tn, tk), lambda j, i, k: (k, j, 0))
    elif mode == "nt":
        b_spec = pl.BlockSpec((tn, tk), lambda j, i, k: (j, k))
    else:
        b_spec = pl.BlockSpec((tk, tn), lambda j, i, k: (k, j))
    in_specs = [a_spec, b_spec]
    operands = [a, b]
    if add is not None:
        assert not out_shards and add.shape == (M, N)
        in_specs.append(pl.BlockSpec((tm, tn), lambda j, i, k: (i, j)))
        operands.append(add)
    if out_shards:
        out_shape = jax.ShapeDtypeStruct((N_DEV, M, tn), out_dtype)
        out_spec = pl.BlockSpec((None, tm, tn), lambda j, i, k: (j, i, 0))
    else:
        out_shape = jax.ShapeDtypeStruct((M, N), out_dtype)
        out_spec = pl.BlockSpec((tm, tn), lambda j, i, k: (i, j))
    dims = _DIMS[mode]
    has_add = add is not None
    n_in = 3 if has_add else 2
    n_side = side.n if side is not None else 0

    def body(*refs):
        a_ref, b_ref = refs[0], refs[1]
        add_ref = refs[2] if has_add else None
        o_ref = refs[n_in + n_side]
        acc_ref = refs[n_in + 2 * n_side + 1]
        k = pl.program_id(2)
        if side is not None:
            side_refs = (refs[n_in:n_in + n_side], refs[n_in + n_side + 1:n_in + 2 * n_side + 1],
                         refs[n_in + 2 * n_side + 2:])
            first, last = _grid_edges((nn, nm, nk))

            @pl.when(first)
            def _():
                side.start(*side_refs)

        part = lax.dot_general(a_ref[...], b_ref[...], dims, preferred_element_type=F32)

        def finish(total):
            if has_add:
                total = add_ref[...] + total
            o_ref[...] = total.astype(out_dtype)

        if nk == 1:
            finish(part)
        else:
            @pl.when(k == 0)
            def _():
                acc_ref[...] = part

            @pl.when(jnp.logical_and(k > 0, k < nk - 1))
            def _():
                acc_ref[...] += part

            @pl.when(k == nk - 1)
            def _():
                finish(acc_ref[...] + part)

        if side is not None:
            @pl.when(last)
            def _():
                side.wait(*side_refs)

    est = 2 * (tm * tk * 2 + tk * tn * 2 + tm * tn * 4 * (2 if has_add else 1)) + tm * tn * 4 * 2
    acc = pltpu.VMEM((tm, tn) if nk > 1 else (SUBLANE, LANE), F32)
    if side is None:
        return pl.pallas_call(
            body, name=name, grid=(nn, nm, nk), in_specs=in_specs, out_specs=out_spec, out_shape=out_shape,
            scratch_shapes=[acc], compiler_params=_params(("parallel", "parallel", "arbitrary"), est),
        )(*operands)
    outs = pl.pallas_call(
        body, name=name, grid=(nn, nm, nk), in_specs=in_specs + side.in_specs(),
        out_specs=[out_spec] + side.out_specs(), out_shape=[out_shape] + side.out_shape(),
        scratch_shapes=[acc] + side.scratch(), compiler_params=_params(("arbitrary",) * 3, est),
    )(*operands, *side.arrays)
    return outs[0], outs[1:]


def _rmsnorm_fwd(x, g, name):
    T, D = x.shape
    tr = _tile(T, 256, SUBLANE)

    def body(x_ref, g_ref, o_ref):
        xf = x_ref[...]
        r = lax.rsqrt(jnp.mean(xf * xf, axis=-1, keepdims=True) + EPS)
        o_ref[...] = ((xf * r) * g_ref[...]).astype(BF16)

    return pl.pallas_call(
        body, name=name, grid=(T // tr,),
        in_specs=[pl.BlockSpec((tr, D), lambda i: (i, 0)), pl.BlockSpec((1, D), lambda i: (0, 0))],
        out_specs=pl.BlockSpec((tr, D), lambda i: (i, 0)), out_shape=jax.ShapeDtypeStruct((T, D), BF16),
        compiler_params=_params(("parallel",), tr * D * 6 * 2),
    )(x, g)


def _rmsnorm_bwd(x, g, dy, res, name):
    T, D = x.shape
    tr = _tile(T, 256, SUBLANE)
    nsteps = T // tr

    def body(x_ref, g_ref, dy_ref, res_ref, dx_ref, dxb_ref, dg_ref, acc_ref):
        i = pl.program_id(0)
        xf = x_ref[...]
        dyf = dy_ref[...].astype(F32)
        r = lax.rsqrt(jnp.mean(xf * xf, axis=-1, keepdims=True) + EPS)
        w = dyf * g_ref[...]
        s = jnp.mean(xf * w, axis=-1, keepdims=True)
        dx = res_ref[...] + (r * w - xf * (r * r * r * s))
        dx_ref[...] = dx
        dxb_ref[...] = dx.astype(BF16)
        part = jnp.sum((dyf * (xf * r)).reshape(tr // SUBLANE, SUBLANE, D), axis=0)

        @pl.when(i == 0)
        def _():
            acc_ref[...] = part

        @pl.when(i > 0)
        def _():
            acc_ref[...] += part

        @pl.when(i == nsteps - 1)
        def _():
            dg_ref[...] = jnp.sum(acc_ref[...], axis=0, keepdims=True)

    row = pl.BlockSpec((tr, D), lambda i: (i, 0))
    vec = pl.BlockSpec((1, D), lambda i: (0, 0))
    return pl.pallas_call(
        body, name=name, grid=(nsteps,), in_specs=[row, vec, row, row], out_specs=[row, row, vec],
        out_shape=[jax.ShapeDtypeStruct((T, D), F32), jax.ShapeDtypeStruct((T, D), BF16),
                   jax.ShapeDtypeStruct((1, D), F32)],
        scratch_shapes=[pltpu.VMEM((SUBLANE, D), F32)],
        compiler_params=_params(("arbitrary",), tr * D * 18 * 2),
    )(x, g, dy, res)


def _qkv_prep(proj, gains, width, name):
    T = proj.shape[0]
    tr = _tile(T, 256, SUBLANE)
    heads = width // HEAD_DIM

    def body(p_ref, g_ref, o_ref):
        j = pl.program_id(1)
        g = g_ref[...]
        for h in range(heads):
            xh = p_ref[:, h * HEAD_DIM:(h + 1) * HEAD_DIM]
            r = lax.rsqrt(jnp.mean(xh * xh, axis=-1, keepdims=True) + EPS)
            r = jnp.where(j < 2, r, 1.0)
            o_ref[:, h * HEAD_DIM:(h + 1) * HEAD_DIM] = ((xh * r) * g).astype(BF16)

    return pl.pallas_call(
        body, name=name, grid=(T // tr, 3),
        in_specs=[pl.BlockSpec((tr, width), lambda i, j: (i, j)),
                  pl.BlockSpec((None, 1, HEAD_DIM), lambda i, j: (j, 0, 0))],
        out_specs=pl.BlockSpec((tr, width), lambda i, j: (i, j)),
        out_shape=jax.ShapeDtypeStruct((T, 3 * width), BF16),
        compiler_params=_params(("parallel", "parallel"), tr * width * 6 * 2),
    )(proj, gains)


def _qkv_bwd(proj, gains, dq, dk, dv, dproj, name):
    T = proj.shape[0]
    width = dq.shape[1]
    tr = _tile(T, 256, SUBLANE)
    heads = width // HEAD_DIM
    nrow = T // tr

    def body(p_ref, g_ref, dq_ref, dk_ref, dv_ref, dp_in, o_ref, dg_ref, acc_ref):
        del dp_in
        i = pl.program_id(0)
        j = pl.program_id(1)
        g = g_ref[...]

        @pl.when(jnp.logical_and(i == 0, j == 0))
        def _():
            acc_ref[...] = jnp.zeros_like(acc_ref)

        part = jnp.zeros((SUBLANE, HEAD_DIM), F32)
        for h in range(heads):
            sl = slice(h * HEAD_DIM, (h + 1) * HEAD_DIM)
            xh = p_ref[:, sl]
            dyh = jnp.where(j == 0, dq_ref[:, sl], jnp.where(j == 1, dk_ref[:, sl], dv_ref[:, sl]))
            r = lax.rsqrt(jnp.mean(xh * xh, axis=-1, keepdims=True) + EPS)
            w = dyh * g
            s = jnp.mean(xh * w, axis=-1, keepdims=True)
            dx = r * w - xh * (r * r * r * s)
            o_ref[:, sl] = jnp.where(j < 2, dx, dyh).astype(BF16)
            part = part + jnp.sum((dyh * (xh * r)).reshape(tr // SUBLANE, SUBLANE, HEAD_DIM), axis=0)
        acc_ref[j] += part

        @pl.when(jnp.logical_and(i == nrow - 1, j == 2))
        def _():
            dg_ref[...] = jnp.sum(acc_ref[...], axis=1, keepdims=True)

    blk = pl.BlockSpec((tr, width), lambda i, j: (i, j))
    dblk = pl.BlockSpec((tr, width), lambda i, j: (i, 0))
    return pl.pallas_call(
        body, name=name, grid=(nrow, 3),
        in_specs=[blk, pl.BlockSpec((None, 1, HEAD_DIM), lambda i, j: (j, 0, 0)), dblk, dblk, dblk,
                  pl.BlockSpec(memory_space=pl.ANY)],
        out_specs=[blk, pl.BlockSpec((3, 1, HEAD_DIM), lambda i, j: (0, 0, 0))],
        out_shape=[jax.ShapeDtypeStruct(dproj.shape, BF16), jax.ShapeDtypeStruct((3, 1, HEAD_DIM), F32)],
        input_output_aliases={5: 0},
        scratch_shapes=[pltpu.VMEM((3, SUBLANE, HEAD_DIM), F32)],
        compiler_params=_params(("arbitrary", "arbitrary"), tr * width * 18 * 2),
    )(proj, gains, dq, dk, dv, dproj)


def _split2(x):
    hi = x.astype(BF16)
    lo = (x - hi.astype(F32)).astype(BF16)
    return hi, lo


LOG2E = 1.4426950408889634


def _sb_logits(q, kb):
    return lax.dot_general(q, kb, _DIMS["nt"], preferred_element_type=F32) * (HEAD_DIM ** -0.5 * LOG2E)


def _sb_scores(z2, mask):
    lk = -(jnp.maximum(z2, 0.0) + jnp.log2(1.0 + jnp.exp2(-jnp.abs(z2))))
    if mask is not None:
        lk = jnp.where(mask, lk, 0.0)
    return lk, lk + z2


def _sb_block_size(T):
    return _tile(T, 256, LANE)


SB_HEADS_PER_STEP = 2


def _sb_later(lk, upper, carry):
    hi, lo = _split2(lk)
    return (jnp.dot(hi, upper, preferred_element_type=F32) + jnp.dot(lo, upper, preferred_element_type=F32)) + carry


def _sb_fwd(qkv, width, name, side=None):
    T = qkv.shape[0]
    heads = width // HEAD_DIM
    bq = _sb_block_size(T)
    hps = min(2 * SB_HEADS_PER_STEP, heads)
    assert heads % hps == 0
    groups, W = heads // hps, hps * HEAD_DIM

    n_side = side.n if side is not None else 0

    def body(*refs):
        q_ref, k_ref, v_ref = refs[:3]
        o_ref, o32_ref = refs[3 + n_side:5 + n_side]
        if side is not None:
            side_refs = (refs[3:3 + n_side], refs[5 + n_side:5 + 2 * n_side], refs[5 + 2 * n_side:])
            first, last = _grid_edges((groups, T // bq))

            @pl.when(first)
            def _():
                side.start(*side_refs)

        i = pl.program_id(1)
        row = lax.broadcasted_iota(jnp.int32, (bq, bq), 0)
        col = lax.broadcasted_iota(jnp.int32, (bq, bq), 1)
        upper = (row > col).astype(BF16)
        causal = col < row

        def head_block(start, h, carry, acc, mask):
            cols = slice(h * HEAD_DIM, (h + 1) * HEAD_DIM)
            z2 = _sb_logits(q_ref[:, cols], k_ref[pl.ds(start, bq), cols])
            yield
            lk, lb = _sb_scores(z2, mask)
            later = _sb_later(lk, upper, carry)
            yield
            w = jnp.exp2(lb + later)
            if mask is not None:
                w = jnp.where(mask, w, 0.0)
            pv = jnp.dot(w.astype(BF16), v_ref[pl.ds(start, bq), cols], preferred_element_type=F32)
            yield
            return carry + jnp.sum(lk, axis=1, keepdims=True), acc + pv

        def block(j, state, mask):
            start = pl.multiple_of(j * bq, bq)
            return tuple(_lockstep(head_block(start, h, state[h][0], state[h][1], mask) for h in range(hps)))

        zero = (jnp.zeros((bq, 1), F32), jnp.zeros((bq, HEAD_DIM), F32))
        state = block(i, (zero,) * hps, causal)
        state = lax.fori_loop(0, i, lambda n, st: block(i - 1 - n, st, None), state)
        for h in range(hps):
            cols = slice(h * HEAD_DIM, (h + 1) * HEAD_DIM)
            o_ref[:, cols] = state[h][1].astype(BF16)
            o32_ref[:, cols] = state[h][1]

        if side is not None:
            @pl.when(last)
            def _():
                side.wait(*side_refs)

    oblk = pl.BlockSpec((bq, W), lambda g, i: (i, g))
    in_specs = [oblk, pl.BlockSpec((T, W), lambda g, i: (0, groups + g)),
                pl.BlockSpec((T, W), lambda g, i: (0, 2 * groups + g))]
    out_shape = [jax.ShapeDtypeStruct((T, width), BF16), jax.ShapeDtypeStruct((T, width), F32)]
    est = 2 * T * W * 2 * 2 + hps * 16 * bq * bq * 4
    if side is None:
        return pl.pallas_call(
            body, name=name, grid=(groups, T // bq), in_specs=in_specs, out_specs=[oblk, oblk], out_shape=out_shape,
            compiler_params=_params(("parallel", "arbitrary"), est),
        )(qkv, qkv, qkv)
    outs = pl.pallas_call(
        body, name=name, grid=(groups, T // bq), in_specs=in_specs + side.in_specs(),
        out_specs=[oblk, oblk] + side.out_specs(), out_shape=out_shape + side.out_shape(),
        scratch_shapes=side.scratch(), compiler_params=_params(("arbitrary", "arbitrary"), est),
    )(qkv, qkv, qkv, *side.arrays)
    return outs[0], outs[1], outs[2:]


def _sb_bwd(qkv, y, dy, width, name):
    T = qkv.shape[0]
    heads = width // HEAD_DIM
    bq = _sb_block_size(T)
    nq = T // bq
    scale = HEAD_DIM ** -0.5
    hps = min(2 * SB_HEADS_PER_STEP, heads)
    groups, W = heads // hps, hps * HEAD_DIM

    def body(q_ref, kv_hbm, y_ref, dy_ref, dq_ref, dk_hbm, dv_hbm, dk_acc, dv_acc, kbuf, vbuf, kv_sems, out_sems):
        g_id = pl.program_id(0)
        i = pl.program_id(1)
        kcol = pl.multiple_of((groups + g_id) * W, W)
        vcol = pl.multiple_of((2 * groups + g_id) * W, W)

        def fetch(j, slot):
            rows = pl.ds(pl.multiple_of(j * bq, bq), bq)
            return (pltpu.make_async_copy(kv_hbm.at[rows, pl.ds(kcol, W)], kbuf.at[slot], kv_sems.at[0, slot]),
                    pltpu.make_async_copy(kv_hbm.at[rows, pl.ds(vcol, W)], vbuf.at[slot], kv_sems.at[1, slot]))

        for cp in fetch(i, 0):
            cp.start()
        row = lax.broadcasted_iota(jnp.int32, (bq, bq), 0)
        col = lax.broadcasted_iota(jnp.int32, (bq, bq), 1)
        upper = (row > col).astype(BF16)
        upper_incl = (row >= col).astype(BF16)
        causal = col < row
        deltas = []
        for h in range(hps):
            cols = slice(h * HEAD_DIM, (h + 1) * HEAD_DIM)
            deltas.append(jnp.sum(dy_ref[:, cols].astype(F32) * y_ref[:, cols], axis=1, keepdims=True))

        @pl.when(i == 0)
        def _():
            dk_acc[...] = jnp.zeros_like(dk_acc)
            dv_acc[...] = jnp.zeros_like(dv_acc)

        def head_block(start, slot, h, carry, gcarry, dq, mask):
            cols = slice(h * HEAD_DIM, (h + 1) * HEAD_DIM)
            q = q_ref[:, cols]
            do = dy_ref[:, cols]
            kb = kbuf[slot, :, cols]
            vb = vbuf[slot, :, cols]
            z2 = _sb_logits(q, kb)
            dw = lax.dot_general(do, vb, _DIMS["nt"], preferred_element_type=F32)
            yield
            lk, lb = _sb_scores(z2, mask)
            later = _sb_later(lk, upper, carry)
            yield
            w = jnp.exp2(lb + later)
            if mask is not None:
                w = jnp.where(mask, w, 0.0)
            wb = w.astype(BF16)
            gw = dw * wb.astype(F32)
            gsuf = _sb_later(gw, upper_incl, gcarry)
            dvp = lax.dot_general(wb, do, _DIMS["tn"], preferred_element_type=F32)
            yield
            dz = gw - jnp.exp2(lb) * (gw + (deltas[h] - gsuf))
            if mask is not None:
                dz = jnp.where(mask, dz, 0.0)
            dzs = (dz * scale).astype(BF16)
            dqp = jnp.dot(dzs, kb, preferred_element_type=F32)
            dkp = lax.dot_general(dzs, q, _DIMS["tn"], preferred_element_type=F32)
            yield
            dk_acc[pl.ds(start, bq), cols] += dkp
            dv_acc[pl.ds(start, bq), cols] += dvp
            return (carry + jnp.sum(lk, axis=1, keepdims=True), gcarry + jnp.sum(gw, axis=1, keepdims=True), dq + dqp)

        def block(n, state, mask):
            j = i - n
            slot = n % 2
            for cp in fetch(j, slot):
                cp.wait()

            @pl.when(n < i)
            def _():
                for cp in fetch(j - 1, 1 - slot):
                    cp.start()

            start = pl.multiple_of(j * bq, bq)
            return tuple(_lockstep(head_block(start, slot, h, *state[h], mask) for h in range(hps)))

        zero = jnp.zeros((bq, 1), F32)
        state = block(0, ((zero, zero, jnp.zeros((bq, HEAD_DIM), F32)),) * hps, causal)
        state = lax.fori_loop(1, i + 1, lambda n, st: block(n, st, None), state)
        for h in range(hps):
            dq_ref[:, h * HEAD_DIM:(h + 1) * HEAD_DIM] = state[h][2]

        @pl.when(i == nq - 1)
        def _():
            cols = pl.ds(pl.multiple_of(g_id * W, W), W)
            copies = [pltpu.make_async_copy(dk_acc, dk_hbm.at[:, cols], out_sems.at[0]),
                      pltpu.make_async_copy(dv_acc, dv_hbm.at[:, cols], out_sems.at[1])]
            for cp in copies:
                cp.start()
            for cp in copies:
                cp.wait()

    qblk = pl.BlockSpec((bq, W), lambda g, i: (i, g))
    out = jax.ShapeDtypeStruct((T, width), F32)
    return pl.pallas_call(
        body, name=name, grid=(groups, nq),
        in_specs=[qblk, pl.BlockSpec(memory_space=pl.ANY), qblk, qblk],
        out_specs=[qblk, pl.BlockSpec(memory_space=pl.ANY), pl.BlockSpec(memory_space=pl.ANY)],
        out_shape=[out, out, out],
        scratch_shapes=[pltpu.VMEM((T, W), F32), pltpu.VMEM((T, W), F32), pltpu.VMEM((2, bq, W), BF16),
                        pltpu.VMEM((2, bq, W), BF16), pltpu.SemaphoreType.DMA((2, 2)), pltpu.SemaphoreType.DMA((2,))],
        compiler_params=_params(("arbitrary", "arbitrary"), 2 * T * W * 4),
    )(qkv, qkv, y, dy)


def _hg_constants():
    C = HG_CHUNK
    t = np.arange(C)[:, None]
    j = np.arange(C)[None, :]
    blocks = [(j <= t), (j > t)]
    masks = []
    for n in HG_LEVELS:
        right = (t % (2 * n)) >= n
        mid = (t // (2 * n)) * (2 * n) + n - 1
        blocks.append(right & (j > mid) & (j <= t))
        blocks.append((~right) & (j > t) & (j <= mid))
        tt, ss = np.arange(C)[:, None], np.arange(C)[None, :]
        same = (tt // (2 * n)) == (ss // (2 * n))
        masks.append(same & ((tt % (2 * n)) >= n) & ((ss % (2 * n)) < n))
    sums = np.concatenate(blocks, axis=0).astype(np.float32)
    return jnp.asarray(sums, BF16), jnp.asarray(np.stack(masks).astype(np.float32))


def _split3(x):
    hi = x.astype(BF16)
    r1 = x - hi.astype(F32)
    mid = r1.astype(BF16)
    lo = (r1 - mid.astype(F32)).astype(BF16)
    return hi, mid, lo


def _hg_gates(hq, hf, lb):
    sq = _sigmoid(hq)
    sf = _sigmoid(hf)
    f = lb + (1.0 - lb) * sf
    return hq * sq, sq, f, sf, 1.0 - f, jnp.log(f)


def _hg_exponents(sums, g):
    hi, mid, lo = _split3(g)
    return (jnp.dot(sums, hi, preferred_element_type=F32) + jnp.dot(sums, mid, preferred_element_type=F32)
            + jnp.dot(sums, lo, preferred_element_type=F32))


def _hg_level_scores(q, k, e_all):
    C = HG_CHUNK
    parts, prods = [], []
    for li in range(len(HG_LEVELS)):
        eq = jnp.exp(e_all[(2 + 2 * li) * C:(3 + 2 * li) * C])
        ek = jnp.exp(e_all[(3 + 2 * li) * C:(4 + 2 * li) * C])
        qt = q * eq
        kt = k * ek
        prods.append(lax.dot_general(qt.astype(BF16), kt.astype(BF16), _DIMS["nt"], preferred_element_type=F32))
        parts.append((eq, ek, qt, kt))
    return parts, prods


def _hg_intra(prods, masks_ref):
    a = masks_ref[0] * prods[0]
    for li in range(1, len(HG_LEVELS)):
        a = a + masks_ref[li] * prods[li]
    return a


def _lockstep(gens):
    gens = list(gens)
    results = [None] * len(gens)
    alive = list(range(len(gens)))
    while alive:
        for idx in list(alive):
            try:
                next(gens[idx])
            except StopIteration as done:
                results[idx] = done.value
                alive.remove(idx)
    return results


def _lower_bound(lbl_ref, cols):
    return _sigmoid(lbl_ref[0:1, cols] - lbl_ref[1:2, cols])


HG_HEADS_PER_STEP = 4


def _hg_fwd(proj, lb_logits, g_out, col0, width, name):
    T = proj.shape[0]
    heads = width // HEAD_DIM
    C = HG_CHUNK
    tb = _tile(T, 512, C)
    cpb = tb // C
    nb = T // tb
    sums, masks = _hg_constants()
    hps = min(HG_HEADS_PER_STEP, heads)
    groups, W = heads // hps, hps * HEAD_DIM
    assert col0 % W == 0 and width % W == 0
    cb = col0 // W

    def body(hq_ref, hf_ref, hi_ref, og_ref, lbl_ref, go_ref, sums_ref, masks_ref, y_ref, o_ref, st_ref, s_ref):
        i = pl.program_id(1)

        @pl.when(i == 0)
        def _():
            s_ref[...] = jnp.zeros_like(s_ref)

        go = go_ref[...]

        def chunk_head(c, rows, h):
            cols = slice(h * HEAD_DIM, (h + 1) * HEAD_DIM)
            lb = _lower_bound(lbl_ref, cols)
            q, _, _, _, k, g = _hg_gates(hq_ref[rows, cols], hf_ref[rows, cols], lb)
            v = hi_ref[rows, cols]
            vb = v.astype(BF16)
            st = s_ref[h]
            st_ref[h, c] = st
            e_all = _hg_exponents(sums_ref[...], g)
            yield
            b = e_all[0:C]
            ebl = jnp.exp(b[C - 1:C, :])
            qe = q * jnp.exp(b)
            o_inter = lax.dot_general(qe.astype(BF16), st.astype(BF16), _DIMS["nt"], preferred_element_type=F32)
            kd = k * jnp.exp(e_all[C:2 * C])
            s_new = lax.dot_general(vb, kd.astype(BF16), _DIMS["tn"], preferred_element_type=F32)
            _, prods = _hg_level_scores(q, k, e_all)
            yield
            a = _hg_intra(prods, masks_ref)
            o_intra = jnp.dot(a.astype(BF16), vb, preferred_element_type=F32)
            s_ref[h] = st * ebl + s_new
            yield
            o = (o_inter + o_intra) + jnp.sum(q * k, axis=1, keepdims=True) * v
            o_ref[rows, cols] = o
            r = lax.rsqrt(jnp.mean(o * o, axis=-1, keepdims=True) + EPS)
            og = og_ref[rows, cols]
            y_ref[rows, cols] = (((o * r) * go) * (og * _sigmoid(og))).astype(BF16)

        def chunk(c, _):
            rows = pl.ds(pl.multiple_of(c * C, C), C)
            _lockstep(chunk_head(c, rows, h) for h in range(hps))
            return 0

        lax.fori_loop(0, cpb, chunk, 0)

    def col(k):
        return pl.BlockSpec((tb, W), lambda g, i: (i, cb + k * groups + g))

    nsum = sums.shape[0]
    out_blk = pl.BlockSpec((tb, W), lambda g, i: (i, g))
    return pl.pallas_call(
        body, name=name, grid=(groups, nb),
        in_specs=[col(0), col(1), col(2), col(3),
                  pl.BlockSpec((2, W), lambda g, i: (0, g)),
                  pl.BlockSpec((1, HEAD_DIM), lambda g, i: (0, 0)),
                  pl.BlockSpec((nsum, C), lambda g, i: (0, 0)),
                  pl.BlockSpec((len(HG_LEVELS), C, C), lambda g, i: (0, 0, 0))],
        out_specs=[out_blk, out_blk,
                   pl.BlockSpec((hps, cpb, HEAD_DIM, HEAD_DIM), lambda g, i: (g, i, 0, 0))],
        out_shape=[jax.ShapeDtypeStruct((T, width), BF16), jax.ShapeDtypeStruct((T, width), F32),
                   jax.ShapeDtypeStruct((heads, T // C, HEAD_DIM, HEAD_DIM), F32)],
        scratch_shapes=[pltpu.VMEM((hps, HEAD_DIM, HEAD_DIM), F32)],
        compiler_params=_params(("parallel", "arbitrary"), tb * W * 4 * 7 + hps * cpb * HEAD_DIM * HEAD_DIM * 4),
    )(proj, proj, proj, proj, lb_logits, g_out, sums, masks)


def _hg_bwd(proj, lb_logits, g_out, o_saved, states, dy, dproj, col0, width, name):
    T = proj.shape[0]
    heads = width // HEAD_DIM
    C = HG_CHUNK
    tb = _tile(T, 512, C)
    cpb = tb // C
    nb = T // tb
    sums, masks = _hg_constants()
    nsum = sums.shape[0]
    nlev = len(HG_LEVELS)
    hps = min(HG_HEADS_PER_STEP, heads)
    groups, W = heads // hps, hps * HEAD_DIM
    cb = col0 // W

    def body(hq_ref, hf_ref, hi_ref, og_ref, lbl_ref, go_ref, sums_ref, masks_ref, o_ref, st_ref, dy_ref, dp_in,
             dp_ref, dlb_ref, dgo_ref, ds_ref, de_ref, dlb_acc, dgo_acc, dout_ref, out_sems):
        del dp_in
        grp = pl.program_id(0)
        i = pl.program_id(1)

        @pl.when(i == 0)
        def _():
            ds_ref[...] = jnp.zeros_like(ds_ref)
            dlb_acc[...] = jnp.zeros_like(dlb_acc)

        @pl.when(jnp.logical_and(grp == 0, i == 0))
        def _():
            dgo_acc[...] = jnp.zeros_like(dgo_acc)

        go = go_ref[...]
        last_row = lax.broadcasted_iota(jnp.int32, (C, HEAD_DIM), 0) == C - 1

        def chunk(n, _):
            c = cpb - 1 - n
            rows = pl.ds(pl.multiple_of(c * C, C), C)
            _lockstep(chunk_head(c, rows, h) for h in range(hps))
            return 0

        def chunk_head(c, rows, h):
            cols = slice(h * HEAD_DIM, (h + 1) * HEAD_DIM)
            de_h = de_ref.at[h]
            lb = _lower_bound(lbl_ref, cols)
            hq = hq_ref[rows, cols]
            og = og_ref[rows, cols]
            q, sq, f, sf, k, g = _hg_gates(hq, hf_ref[rows, cols], lb)
            v = hi_ref[rows, cols]
            vb = v.astype(BF16)
            st = st_ref[h, c]
            stb = st.astype(BF16)
            dst = ds_ref[h]
            dstb = dst.astype(BF16)
            o = o_ref[rows, cols]
            dyc = dy_ref[rows, cols].astype(F32)
            sg = _sigmoid(og)
            r = lax.rsqrt(jnp.mean(o * o, axis=-1, keepdims=True) + EPS)
            on = (o * r) * go
            don = dyc * (og * sg)
            dout_ref[3, rows, cols] = (dyc * on * (sg * (1.0 + og * (1.0 - sg)))).astype(BF16)
            dgo_acc[...] += jnp.sum((don * (o * r)).reshape(C // SUBLANE, SUBLANE, HEAD_DIM), axis=0)
            wn = don * go
            do = r * wn - o * (r * r * r * jnp.mean(o * wn, axis=-1, keepdims=True))
            dob = do.astype(BF16)
            e_all = _hg_exponents(sums_ref[...], g)
            da = lax.dot_general(dob, vb, _DIMS["nt"], preferred_element_type=F32)
            dqe = jnp.dot(dob, stb, preferred_element_type=F32)
            dkd = jnp.dot(vb, dstb, preferred_element_type=F32)
            yield
            eb = jnp.exp(e_all[0:C])
            esuf = jnp.exp(e_all[C:2 * C])
            ebl = eb[C - 1:C, :]
            qe = q * eb
            kd = k * esuf
            parts, prods = _hg_level_scores(q, k, e_all)
            dv_state = lax.dot_general(kd.astype(BF16), dstb, _DIMS["nt"], preferred_element_type=F32)
            ds_new = lax.dot_general(dob, qe.astype(BF16), _DIMS["tn"], preferred_element_type=F32)
            yield
            a = _hg_intra(prods, masks_ref)
            dv = lax.dot_general(a.astype(BF16), dob, _DIMS["tn"], preferred_element_type=F32)
            dlev = []
            for li in range(nlev):
                _, _, qt, kt = parts[li]
                dan = (masks_ref[li] * da).astype(BF16)
                dlev.append((jnp.dot(dan, kt.astype(BF16), preferred_element_type=F32),
                             lax.dot_general(dan, qt.astype(BF16), _DIMS["tn"], preferred_element_type=F32)))
            yield
            qk = jnp.sum(q * k, axis=1, keepdims=True)
            dv = dv + qk * do + dv_state
            dqk = jnp.sum(do * v, axis=1, keepdims=True)
            dq = dqk * k
            dk = dqk * q
            for li in range(nlev):
                eq, ek, qt, kt = parts[li]
                dqt, dkt = dlev[li]
                dq = dq + dqt * eq
                dk = dk + dkt * ek
                de_h[(2 + 2 * li) * C:(3 + 2 * li) * C, :] = dqt * qt
                de_h[(3 + 2 * li) * C:(4 + 2 * li) * C, :] = dkt * kt
            dq = dq + dqe * eb
            dk = dk + dkd * esuf
            debl = jnp.sum(dst * st, axis=0, keepdims=True)
            de_h[0:C, :] = dqe * qe + jnp.where(last_row, debl * ebl, 0.0)
            de_h[C:2 * C, :] = dkd * kd
            ds_ref[h] = dst * ebl + ds_new
            dehi, delo = _split2(de_h[...])
            dg = (lax.dot_general(sums_ref[...], dehi, _DIMS["tn"], preferred_element_type=F32)
                  + lax.dot_general(sums_ref[...], delo, _DIMS["tn"], preferred_element_type=F32))
            yield
            df = dg / f - dk
            dout_ref[0, rows, cols] = (dq * (sq * (1.0 + hq * (1.0 - sq)))).astype(BF16)
            dout_ref[1, rows, cols] = (df * (1.0 - lb) * (sf * (1.0 - sf))).astype(BF16)
            dout_ref[2, rows, cols] = dv.astype(BF16)
            dlb_acc[:, cols] += jnp.sum((df * (1.0 - sf)).reshape(C // SUBLANE, SUBLANE, HEAD_DIM), axis=0)

        lax.fori_loop(0, cpb, chunk, 0)

        rows = pl.ds(pl.multiple_of((nb - 1 - i) * tb, tb), tb)
        copies = [pltpu.make_async_copy(
            dout_ref.at[k], dp_ref.at[rows, pl.ds(pl.multiple_of((cb + k * groups + grp) * W, W), W)], out_sems.at[k])
            for k in range(4)]
        for cp in copies:
            cp.start()
        for cp in copies:
            cp.wait()

        @pl.when(i == nb - 1)
        def _():
            lb = _lower_bound(lbl_ref, slice(None))
            dl0 = jnp.sum(dlb_acc[...], axis=0, keepdims=True) * (lb * (1.0 - lb))
            dlb_ref[0:1, :] = dl0
            dlb_ref[1:2, :] = -dl0

        @pl.when(jnp.logical_and(grp == groups - 1, i == nb - 1))
        def _():
            dgo_ref[...] = jnp.sum(dgo_acc[...], axis=0, keepdims=True)

    def col(k):
        return pl.BlockSpec((tb, W), lambda g, i: (nb - 1 - i, cb + k * groups + g))

    rev = pl.BlockSpec((tb, W), lambda g, i: (nb - 1 - i, g))
    return pl.pallas_call(
        body, name=name, grid=(groups, nb),
        in_specs=[col(0), col(1), col(2), col(3),
                  pl.BlockSpec((2, W), lambda g, i: (0, g)),
                  pl.BlockSpec((1, HEAD_DIM), lambda g, i: (0, 0)),
                  pl.BlockSpec((nsum, C), lambda g, i: (0, 0)),
                  pl.BlockSpec((nlev, C, C), lambda g, i: (0, 0, 0)),
                  rev,
                  pl.BlockSpec((hps, cpb, HEAD_DIM, HEAD_DIM), lambda g, i: (g, nb - 1 - i, 0, 0)),
                  rev,
                  pl.BlockSpec(memory_space=pl.ANY)],
        out_specs=[pl.BlockSpec(memory_space=pl.ANY),
                   pl.BlockSpec((2, W), lambda g, i: (0, g)),
                   pl.BlockSpec((1, HEAD_DIM), lambda g, i: (0, 0))],
        out_shape=[jax.ShapeDtypeStruct(dproj.shape, BF16), jax.ShapeDtypeStruct((2, width), F32),
                   jax.ShapeDtypeStruct((1, HEAD_DIM), F32)],
        input_output_aliases={11: 0},
        scratch_shapes=[pltpu.VMEM((hps, HEAD_DIM, HEAD_DIM), F32), pltpu.VMEM((hps, nsum, HEAD_DIM), F32),
                        pltpu.VMEM((SUBLANE, W), F32), pltpu.VMEM((SUBLANE, HEAD_DIM), F32),
                        pltpu.VMEM((4, tb, W), BF16), pltpu.SemaphoreType.DMA((4,))],
        compiler_params=_params(("arbitrary", "arbitrary"), tb * W * 4 * 12 + hps * cpb * HEAD_DIM * HEAD_DIM * 4),
    )(proj, proj, proj, proj, lb_logits, g_out, sums, masks, o_saved, states, dy, dproj)


def _merge_fwd(proj, ma, mb, gate_col0, name):
    T, D = ma.shape
    tr = _tile(T, 256, SUBLANE)
    cw = _tile(D, 1024, LANE)
    nj = D // cw
    assert gate_col0 % cw == 0
    g0 = gate_col0 // cw

    def body(ga_ref, gb_ref, ma_ref, mb_ref, o_ref):
        o_ref[...] = (_sigmoid(ga_ref[...]) * ma_ref[...] + _sigmoid(gb_ref[...]) * mb_ref[...]).astype(BF16)

    blk = pl.BlockSpec((tr, cw), lambda i, j: (i, j))
    return pl.pallas_call(
        body, name=name, grid=(T // tr, nj),
        in_specs=[pl.BlockSpec((tr, cw), lambda i, j: (i, g0 + j)),
                  pl.BlockSpec((tr, cw), lambda i, j: (i, g0 + nj + j)), blk, blk],
        out_specs=blk, out_shape=jax.ShapeDtypeStruct((T, D), BF16),
        compiler_params=_params(("parallel", "parallel"), tr * cw * 18 * 2),
    )(proj, proj, ma, mb)


def _merge_bwd(proj, ma, mb, dm, gate_col0, name):
    T, D = ma.shape
    tr = _tile(T, 256, SUBLANE)
    cw = _tile(D, 1024, LANE)
    nj = D // cw
    g0 = gate_col0 // cw

    def body(g_ref, ma_ref, mb_ref, dm_ref, dp_ref, dmm_ref):
        branch = pl.program_id(2)
        s = _sigmoid(g_ref[...])
        dmv = dm_ref[...]
        mm = jnp.where(branch == 0, ma_ref[...], mb_ref[...])
        dp_ref[...] = (dmv * mm * (s * (1.0 - s))).astype(BF16)
        dmm_ref[...] = (dmv * s).astype(BF16)

    blk = pl.BlockSpec((tr, cw), lambda i, j, b: (i, j))
    gate = pl.BlockSpec((tr, cw), lambda i, j, b: (i, g0 + b * nj + j))
    return pl.pallas_call(
        body, name=name, grid=(T // tr, nj, 2),
        in_specs=[gate, blk, blk, blk],
        out_specs=[gate, pl.BlockSpec((None, tr, cw), lambda i, j, b: (b, i, j))],
        out_shape=[jax.ShapeDtypeStruct(proj.shape, BF16), jax.ShapeDtypeStruct((2, T, D), BF16)],
        compiler_params=_params(("parallel", "parallel", "arbitrary"), tr * cw * 20 * 2),
    )(proj, ma, mb, dm)


def _conv_fwd(up, convw, convb, name):
    T, F2 = up.shape
    tc = convw.shape[2]
    half = (F2 // 2) // tc
    assert half * tc * 2 == F2
    tr = _tile(T, 256, SUBLANE)
    hb = tr // SUBLANE

    def body(g_ref, gp_ref, v_ref, vp_ref, wg_ref, wv_ref, bg_ref, bv_ref, o_ref, gx_ref, vx_ref):
        first = pl.program_id(1) == 0

        def lanes(c, _):
            cols = pl.ds(pl.multiple_of(c * LANE, LANE), LANE)

            def conv(cur_ref, prev_ref, w_ref, b_ref, x_ref):
                x_ref[0:SUBLANE, :] = jnp.where(first, 0.0, prev_ref[:, cols])
                x_ref[SUBLANE:, :] = cur_ref[:, cols]
                w = w_ref[:, cols]
                out = b_ref[:, cols] + w[0:1, :] * x_ref[pl.ds(SUBLANE - 2, tr), :]
                out = out + w[1:2, :] * x_ref[pl.ds(SUBLANE - 1, tr), :]
                return out + w[2:3, :] * x_ref[pl.ds(SUBLANE, tr), :]

            gate = conv(g_ref, gp_ref, wg_ref, bg_ref, gx_ref)
            val = conv(v_ref, vp_ref, wv_ref, bv_ref, vx_ref)
            o_ref[:, cols] = ((gate * _sigmoid(gate)) * val).astype(BF16)
            return 0

        lax.fori_loop(0, tc // LANE, lanes, 0)

    def main(off):
        return pl.BlockSpec((tr, tc), lambda j, i: (i, off + j))

    def prev(off):
        return pl.BlockSpec((SUBLANE, tc), lambda j, i: (jnp.maximum(i * hb - 1, 0), off + j))

    def wspec(off):
        return pl.BlockSpec((None, 3, tc), lambda j, i: (off + j, 0, 0))

    def bspec(off):
        return pl.BlockSpec((1, tc), lambda j, i: (0, off + j))

    return pl.pallas_call(
        body, name=name, grid=(half, T // tr),
        in_specs=[main(0), prev(0), main(half), prev(half), wspec(0), wspec(half), bspec(0), bspec(half)],
        out_specs=pl.BlockSpec((tr, tc), lambda j, i: (i, j)),
        out_shape=jax.ShapeDtypeStruct((T, F2 // 2), BF16),
        scratch_shapes=[pltpu.VMEM((SUBLANE + tr, LANE), F32), pltpu.VMEM((SUBLANE + tr, LANE), F32)],
        compiler_params=_params(("parallel", "parallel"), tr * tc * 4 * 12),
    )(up, up, up, up, convw, convw, convb, convb)


def _conv_bwd(da, up, convw, convb, name):
    T, F2 = up.shape
    tc = convw.shape[2]
    half = (F2 // 2) // tc
    tr = _tile(T, 128, SUBLANE)
    hb = tr // SUBLANE
    nrow = T // tr
    n = tr + 2 * SUBLANE

    def body(g_ref, gp_ref, gn_ref, v_ref, vp_ref, vn_ref, da_ref, dan_ref, wg_ref, wv_ref, bg_ref, bv_ref,
             dup_ref, gw_ref, acc_ref, gx_ref, vx_ref, dg_ref, dv_ref):
        i = pl.program_id(1)
        first = i == 0
        last = i == nrow - 1

        @pl.when(first)
        def _():
            acc_ref[...] = jnp.zeros_like(acc_ref)

        m = tr + SUBLANE

        def fold(t):
            return jnp.sum(t.reshape(tr // SUBLANE, SUBLANE, LANE), axis=0)

        def lanes(c, _):
            cols = pl.ds(pl.multiple_of(c * LANE, LANE), LANE)

            def conv(cur_ref, prev_ref, next_ref, w_ref, b_ref, x_ref):
                x_ref[0:SUBLANE, :] = jnp.where(first, 0.0, prev_ref[:, cols])
                x_ref[SUBLANE:SUBLANE + tr, :] = cur_ref[:, cols]
                x_ref[SUBLANE + tr:, :] = next_ref[:, cols]
                w = w_ref[:, cols]
                taps = tuple(x_ref[pl.ds(SUBLANE - 2 + k, m), :] for k in range(3))
                out = b_ref[:, cols] + w[0:1, :] * taps[0]
                out = out + w[1:2, :] * taps[1]
                return out + w[2:3, :] * taps[2], taps, w

            gate, g_taps, wg = conv(g_ref, gp_ref, gn_ref, wg_ref, bg_ref, gx_ref)
            val, v_taps, wv = conv(v_ref, vp_ref, vn_ref, wv_ref, bv_ref, vx_ref)
            da_m = jnp.concatenate([da_ref[:, cols], jnp.where(last, 0.0, dan_ref[:, cols])], axis=0)
            sg = _sigmoid(gate)
            dg_ref[...] = da_m * val * (sg * (1.0 + gate * (1.0 - sg)))
            dv_ref[...] = da_m * (gate * sg)
            for hf, (d_ref, w, taps) in enumerate(((dg_ref, wg, g_taps), (dv_ref, wv, v_taps))):
                dc = d_ref[pl.ds(0, tr), :]
                dup = w[2:3, :] * dc + w[1:2, :] * d_ref[pl.ds(1, tr), :] + w[0:1, :] * d_ref[pl.ds(2, tr), :]
                dup_ref[hf, :, cols] = dup.astype(BF16)
                acc_ref[hf, 0, :, cols] += fold(dc)
                for k in range(3):
                    acc_ref[hf, 1 + k, :, cols] += fold(dc * taps[k][0:tr])
            return 0

        lax.fori_loop(0, tc // LANE, lanes, 0)

        @pl.when(last)
        def _():
            gw_ref[...] = jnp.sum(acc_ref[...], axis=2)

    def main(off):
        return pl.BlockSpec((tr, tc), lambda j, i: (i, off + j))

    def prev(off):
        return pl.BlockSpec((SUBLANE, tc), lambda j, i: (jnp.maximum(i * hb - 1, 0), off + j))

    def nxt(off):
        return pl.BlockSpec((SUBLANE, tc), lambda j, i: (jnp.minimum((i + 1) * hb, T // SUBLANE - 1), off + j))

    def wspec(off):
        return pl.BlockSpec((None, 3, tc), lambda j, i: (off + j, 0, 0))

    def bspec(off):
        return pl.BlockSpec((1, tc), lambda j, i: (0, off + j))

    return pl.pallas_call(
        body, name=name, grid=(half, nrow),
        in_specs=[main(0), prev(0), nxt(0), main(half), prev(half), nxt(half), main(0), nxt(0),
                  wspec(0), wspec(half), bspec(0), bspec(half)],
        out_specs=[pl.BlockSpec((2, tr, tc), lambda j, i: (0, i, j)), pl.BlockSpec((2, 4, tc), lambda j, i: (0, 0, j))],
        out_shape=[jax.ShapeDtypeStruct((2, T, F2 // 2), BF16), jax.ShapeDtypeStruct((2, 4, F2 // 2), F32)],
        scratch_shapes=[pltpu.VMEM((2, 4, SUBLANE, tc), F32), pltpu.VMEM((n, LANE), F32), pltpu.VMEM((n, LANE), F32),
                        pltpu.VMEM((tr + SUBLANE, LANE), F32), pltpu.VMEM((tr + SUBLANE, LANE), F32)],
        compiler_params=_params(("parallel", "arbitrary"), n * tc * 4 * 24),
    )(up, up, up, up, up, up, da, da, convw, convw, convb, convb)


def _loss_head(y, target, name):
    T, D = y.shape
    tr = _tile(T, 256, SUBLANE)
    nrow = T // tr

    def body(y_ref, t_ref, d_ref, db_ref, l_ref, acc_ref):
        i = pl.program_id(0)
        diff = y_ref[...] - t_ref[...]
        dy = diff / D
        d_ref[...] = dy
        db_ref[...] = dy.astype(BF16)
        part = jnp.sum((diff * diff).reshape(tr // SUBLANE, SUBLANE, D), axis=0)

        @pl.when(i == 0)
        def _():
            acc_ref[...] = part

        @pl.when(i > 0)
        def _():
            acc_ref[...] += part

        @pl.when(i == nrow - 1)
        def _():
            col = jnp.sum(acc_ref[...], axis=0, keepdims=True)
            l_ref[...] = jnp.broadcast_to(0.5 * (jnp.sum(col, axis=1, keepdims=True) / D), (1, LANE))

    row = pl.BlockSpec((tr, D), lambda i: (i, 0))
    return pl.pallas_call(
        body, name=name, grid=(nrow,), in_specs=[row, row],
        out_specs=[row, row, pl.BlockSpec((1, LANE), lambda i: (0, 0))],
        out_shape=[jax.ShapeDtypeStruct((T, D), F32), jax.ShapeDtypeStruct((T, D), BF16),
                   jax.ShapeDtypeStruct((1, LANE), F32)],
        scratch_shapes=[pltpu.VMEM((SUBLANE, D), F32)],
        compiler_params=_params(("arbitrary",), tr * D * 14 * 2),
    )(y, target)


def _adamw(w, parts, m, v, name):
    R, C = w.shape
    P = parts.shape[0]
    tr = _tile(R, 64, SUBLANE)
    tc = _tile(C, 2048, LANE)

    def body(w_ref, p_ref, m_ref, v_ref, g_ref, d_ref, nm_ref, nv_ref):
        g = p_ref[0].astype(F32)
        for s in range(1, P):
            g = g + p_ref[s].astype(F32)
        wv = w_ref[...]
        nm = ADAM_B1 * m_ref[...] + (1.0 - ADAM_B1) * g
        nv = ADAM_B2 * v_ref[...] + (1.0 - ADAM_B2) * (g * g)
        m_hat = nm / (1.0 - ADAM_B1 ** ADAM_STEP)
        v_hat = nv / (1.0 - ADAM_B2 ** ADAM_STEP)
        g_ref[...] = g
        d_ref[...] = -ADAM_LR * (m_hat / (jnp.sqrt(v_hat) + ADAM_EPS) + ADAM_WD * wv)
        nm_ref[...] = nm
        nv_ref[...] = nv

    blk = pl.BlockSpec((tr, tc), lambda i, j: (i, j))
    shp = jax.ShapeDtypeStruct((R, C), F32)
    return pl.pallas_call(
        body, name=name, grid=(R // tr, C // tc),
        in_specs=[blk, pl.BlockSpec((P, tr, tc), lambda i, j: (0, i, j)), blk, blk],
        out_specs=[blk] * 4, out_shape=[shp] * 4,
        compiler_params=_params(("parallel", "parallel"), tr * tc * 4 * (P + 8) * 2),
    )(w, parts, m, v)


def _place():
    x, y, c = (lax.axis_index(a) for a in MESH_AXES)
    return x, y, c, 4 * x + 2 * y + c


def _peers(x, y, c):
    out = []
    for d in range(1, N_DEV):
        px = x + (d >> 2 & 1) - 2 * x * (d >> 2 & 1)
        py = y + (d >> 1 & 1) - 2 * y * (d >> 1 & 1)
        pc = c + (d & 1) - 2 * c * (d & 1)
        out.append(((px, py, pc), 4 * px + 2 * py + pc))
    return out


class _Side:
    def __init__(self, arrays, scatter, side_by_side=()):
        self.arrays, self.scatter, self.n = list(arrays), scatter, len(arrays)
        self.side_by_side = set(side_by_side)
        assert not (scatter and self.side_by_side)

    def in_specs(self):
        return [pl.BlockSpec(memory_space=pltpu.HBM)] * self.n

    out_specs = in_specs

    def out_shape(self):
        def shape(t, a):
            if t in self.side_by_side:
                return (a.shape[0], N_DEV * a.shape[1])
            return a.shape if self.scatter else (N_DEV,) + a.shape
        return [jax.ShapeDtypeStruct(shape(t, a), a.dtype) for t, a in enumerate(self.arrays)]

    def _slot(self, t, out, idx):
        if t in self.side_by_side:
            w = self.arrays[t].shape[1]
            return out.at[:, pl.ds(pl.multiple_of(idx * w, w), w)]
        return out.at[idx]

    def scratch(self):
        return [pltpu.SemaphoreType.DMA((self.n, N_DEV - 1)), pltpu.SemaphoreType.DMA((self.n, N_DEV - 1)),
                pltpu.SemaphoreType.DMA((self.n,))]

    def _copies(self, ins, outs, sems):
        send_sems, recv_sems, local_sems = sems
        x, y, c, me = _place()
        peers = _peers(x, y, c)
        local, sends, recvs = [], [], []
        for t in range(self.n):
            src_me = ins[t].at[me] if self.scatter else ins[t]
            local.append(pltpu.make_async_copy(src_me, self._slot(t, outs[t], me), local_sems.at[t]))
            for d, (peer, pidx) in enumerate(peers):
                src = ins[t].at[pidx] if self.scatter else ins[t]
                sends.append(pltpu.make_async_remote_copy(
                    src_ref=src, dst_ref=self._slot(t, outs[t], me), send_sem=send_sems.at[t, d],
                    recv_sem=recv_sems.at[t, d], device_id=peer, device_id_type=MESH_ID))
                recvs.append(pltpu.make_async_remote_copy(
                    src_ref=src, dst_ref=self._slot(t, outs[t], pidx), send_sem=send_sems.at[t, d],
                    recv_sem=recv_sems.at[t, d], device_id=peer, device_id_type=MESH_ID))
        return local, sends, recvs

    def start(self, ins, outs, sems):
        local, sends, _ = self._copies(ins, outs, sems)
        for cp in local + sends:
            cp.start()

    def wait(self, ins, outs, sems):
        local, sends, recvs = self._copies(ins, outs, sems)
        for cp in recvs:
            cp.wait_recv()
        for cp in sends:
            cp.wait_send()
        for cp in local:
            cp.wait()


def _grid_edges(sizes):
    ids = [pl.program_id(a) for a in range(len(sizes))]
    first = functools.reduce(jnp.logical_and, [i == 0 for i in ids])
    last = functools.reduce(jnp.logical_and, [i == s - 1 for i, s in zip(ids, sizes)])
    return first, last


def _gather_two_level(arrays, name):
    n = len(arrays)

    def body(*refs):
        ins, outs = refs[:n], refs[n:2 * n]
        send_sems, recv_sems, local_sems = refs[2 * n:]
        x, y, c, me = _place()
        sibling = (x, y, 1 - c)
        chips = [(1 - x, y), (x, 1 - y), (1 - x, 1 - y)]

        def slot(px, py, pc):
            return 4 * px + 2 * py + pc

        def copy(t, k, block, to, src=None):
            return pltpu.make_async_remote_copy(
                src_ref=outs[t].at[slot(*block)] if src is None else src, dst_ref=outs[t].at[slot(*block)],
                send_sem=send_sems.at[t, k], recv_sem=recv_sems.at[t, k], device_id=to, device_id_type=MESH_ID)

        mine = [pltpu.make_async_copy(ins[t], outs[t].at[me], local_sems.at[t]) for t in range(n)]
        first = [copy(t, 0, (x, y, c), sibling, src=ins[t]) for t in range(n)]
        first += [copy(t, 1 + j, (x, y, c), (*chip, c), src=ins[t]) for t in range(n) for j, chip in enumerate(chips)]
        for cp in mine + first:
            cp.start()
        passed = []
        for t in range(n):
            for j, chip in enumerate(chips):
                copy(t, 1 + j, (*chip, c), (x, y, c)).wait_recv()
                cp = copy(t, 4 + j, (*chip, c), sibling)
                cp.start()
                passed.append(cp)
        for t in range(n):
            copy(t, 0, sibling, (x, y, c)).wait_recv()
            for j, chip in enumerate(chips):
                copy(t, 4 + j, (*chip, 1 - c), (x, y, c)).wait_recv()
        for cp in first + passed:
            cp.wait_send()
        for cp in mine:
            cp.wait()

    hbm = pl.BlockSpec(memory_space=pltpu.HBM)
    return pl.pallas_call(
        body, name=name, in_specs=[hbm] * n, out_specs=[hbm] * n,
        out_shape=[jax.ShapeDtypeStruct((N_DEV,) + a.shape, a.dtype) for a in arrays],
        scratch_shapes=[pltpu.SemaphoreType.DMA((n, N_DEV - 1)), pltpu.SemaphoreType.DMA((n, N_DEV - 1)),
                        pltpu.SemaphoreType.DMA((n,))],
    )(*arrays)


def _all_reduce_small(vec, name):
    R = vec.shape[0]

    def body(v_ref, o_ref, gath_ref, send_sems, recv_sems):
        x, y, c, me = _place()
        sends, recvs = [], []
        for d, (peer, pidx) in enumerate(_peers(x, y, c)):
            cp = pltpu.make_async_remote_copy(
                src_ref=v_ref, dst_ref=gath_ref.at[me], send_sem=send_sems.at[d], recv_sem=recv_sems.at[d],
                device_id=peer, device_id_type=MESH_ID)
            cp.start()
            sends.append(cp)
            recvs.append(pltpu.make_async_remote_copy(
                src_ref=v_ref, dst_ref=gath_ref.at[pidx], send_sem=send_sems.at[d], recv_sem=recv_sems.at[d],
                device_id=peer, device_id_type=MESH_ID))
        gath_ref[me] = v_ref[...]
        for cp in recvs:
            cp.wait_recv()
        for cp in sends:
            cp.wait_send()
        total = gath_ref[0]
        for s in range(1, N_DEV):
            total = total + gath_ref[s]
        o_ref[...] = total

    vm = pl.BlockSpec(memory_space=pltpu.VMEM)
    return pl.pallas_call(
        body, name=name, in_specs=[vm], out_specs=vm, out_shape=jax.ShapeDtypeStruct(vec.shape, F32),
        scratch_shapes=[pltpu.VMEM((N_DEV, R, LANE), F32), pltpu.SemaphoreType.DMA((N_DEV - 1,)),
                        pltpu.SemaphoreType.DMA((N_DEV - 1,))],
        compiler_params=pltpu.CompilerParams(vmem_limit_bytes=_vmem(R * LANE * 4 * 12)),
    )(vec)


def _pack(parts):
    flat = jnp.concatenate([p.reshape(-1).astype(F32) for p in parts])
    rows = -(-flat.shape[0] // (LANE * SUBLANE)) * SUBLANE
    return jnp.pad(flat, (0, rows * LANE - flat.shape[0])).reshape(rows, LANE)


def _unpack(packed, shapes):
    flat = packed.reshape(-1)
    out, at = [], 0
    for s in shapes:
        size = int(np.prod(s))
        out.append(flat[at:at + size].reshape(s))
        at += size
    return out


def kernel(x, g_mix, w_in, g_q, g_k, lb_logits, g_hg_out, p_a, p_b, w_o, g_ffn, w_up, conv_w, conv_b, w_down, loss_target, m_g_mix, m_w_in, m_g_q, m_g_k, m_lb_logits, m_g_hg_out, m_p_a, m_p_b, m_w_o, m_g_ffn, m_w_up, m_conv_w, m_conv_b, m_w_down, v_g_mix, v_w_in, v_g_q, v_g_k, v_lb_logits, v_g_hg_out, v_p_a, v_p_b, v_w_o, v_g_ffn, v_w_up, v_conv_w, v_conv_b, v_w_down):
    assert x.shape[0] == 1 and lb_logits.shape[0] == 2
    xs, target = x[0], loss_target[0]
    T, D = xs.shape
    A = p_a.shape[1]
    HW = p_b.shape[1]
    hg_col0 = 3 * A
    gate_col0 = 3 * A + 4 * HW
    F = w_down.shape[1] * N_DEV
    assert w_in.shape[2] * N_DEV == gate_col0 + 2 * D and w_up.shape[2] * N_DEV == 2 * F

    (win_g,) = _gather_two_level([w_in[0].astype(BF16)], "gather_w_in")
    later_weights = _Side([w_up[0].astype(BF16), p_a[0].astype(BF16), p_b[0].astype(BF16), w_o[0].astype(BF16),
                           w_down[0].astype(BF16), conv_w[0]], False, side_by_side=(1, 2))

    u = _rmsnorm_fwd(xs, g_mix, "norm_mix")
    proj = _matmul(u, win_g, mode="nn", b_shards=True, tm_pref=1024, name="proj_in")
    gains = jnp.stack([g_q[0], g_k[0], jnp.ones_like(g_q[0])])[:, None, :]
    qkv = _qkv_prep(proj, gains, A, "qkv_prep")
    ya, ya32, (wup_g, pa_g, pb_g, wo_g, wdown_g, convw_g) = _sb_fwd(qkv, A, "sb_fwd", side=later_weights)
    wo_full = wo_g.reshape(D, D)
    wdown_full = wdown_g.reshape(F, D)
    yb, ob, states = _hg_fwd(proj, lb_logits, g_hg_out, hg_col0, HW, "hg_fwd")
    ma = _matmul(ya, pa_g, mode="nn", name="proj_a")
    mb = _matmul(yb, pb_g, mode="nn", name="proj_b")
    m = _merge_fwd(proj, ma, mb, gate_col0, "merge_fwd")
    h1 = _matmul(m, wo_full, mode="nn", add=xs, tm_pref=1024, name="proj_o")
    u2 = _rmsnorm_fwd(h1, g_ffn, "norm_ffn")
    up = _matmul(u2, wup_g, mode="nn", b_shards=True, tm_pref=1024, name="ffn_up")
    act = _conv_fwd(up, convw_g, conv_b, "conv_fwd")
    y = _matmul(act, wdown_full, mode="nn", add=h1, tn_pref=2048, name="ffn_down")
    dy, dyb, loss_part = _loss_head(y, target, "loss_head")
    loss = lax.psum(loss_part[0, 0], MESH_AXES)

    dact = _matmul(dyb, wdown_full, mode="nt", tm_pref=1024, name="d_act")
    g_wdown = _matmul(act, dyb, mode="tn", out_dtype=BF16, name="g_w_down")
    dup, g_conv = _conv_bwd(dact, up, convw_g, conv_b, "conv_bwd")
    g_wup, (r_wdown,) = _matmul(u2, dup, mode="tn", b_halves=True, out_shards=True, out_dtype=BF16, tm_pref=1024,
                                name="g_w_up",
                                side=_Side([g_wdown.reshape(N_DEV, F // N_DEV, D)], True))
    du2, (r_wup,) = _matmul(dup, wup_g, mode="nt", a_halves=True, b_shards=True, tn_pref=2048, name="d_u2",
                            side=_Side([g_wup], True))
    dh1, dh1b, g_gffn = _rmsnorm_bwd(h1, g_ffn, du2, dy, "norm_ffn_bwd")
    dm = _matmul(dh1b, wo_full, mode="nt", tm_pref=1024, name="d_m")
    g_wo = _matmul(m, dh1b, mode="tn", out_dtype=BF16, tm_pref=1024, name="g_w_o")
    dproj, dmab = _merge_bwd(proj, ma, mb, dm, gate_col0, "merge_bwd")
    dya = _matmul(dmab, pa_g, mode="nt", lead_a=0, out_dtype=BF16, name="d_ya")
    g_pa = _matmul(ya, dmab, mode="tn", out_shards=True, out_dtype=BF16, lead_b=0, name="g_p_a")
    dyb_ = _matmul(dmab, pb_g, mode="nt", lead_a=1, out_dtype=BF16, name="d_yb")
    g_pb = _matmul(yb, dmab, mode="tn", out_shards=True, out_dtype=BF16, lead_b=1, name="g_p_b")
    dproj, g_lb, g_ghg = _hg_bwd(proj, lb_logits, g_hg_out, ob, states, dyb_, dproj, hg_col0, HW, "hg_bwd")
    dq, dk, dv = _sb_bwd(qkv, ya32, dya, A, "sb_bwd")
    dproj, g_gains = _qkv_bwd(proj, gains, dq, dk, dv, dproj, "qkv_bwd")
    g_win, (r_wo, r_pa, r_pb) = _matmul(
        u, dproj, mode="tn", out_shards=True, out_dtype=BF16, tm_pref=1024, name="g_w_in",
        side=_Side([g_wo.reshape(N_DEV, D // N_DEV, D), g_pa, g_pb], True))
    du, (r_win,) = _matmul(dproj, win_g, mode="nt", b_shards=True, tn_pref=2048, name="d_u",
                           side=_Side([g_win], True))
    gx, _, g_gmix = _rmsnorm_bwd(xs, g_mix, du, dh1, "norm_mix_bwd")

    g_convb = g_conv[:, 0].reshape(1, 2 * F)
    g_convw = g_conv[:, 1:4].transpose(1, 0, 2).reshape(3, 2 * F)
    small = [g_gmix, g_gains[0], g_gains[1], g_lb, g_ghg, g_gffn, g_convb, g_convw]
    small_shapes = [p.shape for p in small]
    red = _unpack(_all_reduce_small(_pack(small), "reduce_small"), small_shapes)

    _, _, _, me = _place()
    cs = conv_w.shape[2]
    big = {
        "w_in": _adamw(w_in[0], r_win, m_w_in[0], v_w_in[0], "adamw_w_in"),
        "w_up": _adamw(w_up[0], r_wup, m_w_up[0], v_w_up[0], "adamw_w_up"),
        "p_a": _adamw(p_a[0], r_pa, m_p_a[0], v_p_a[0], "adamw_p_a"),
        "p_b": _adamw(p_b[0], r_pb, m_p_b[0], v_p_b[0], "adamw_p_b"),
        "w_o": _adamw(w_o[0], r_wo, m_w_o[0], v_w_o[0], "adamw_w_o"),
        "w_down": _adamw(w_down[0], r_wdown, m_w_down[0], v_w_down[0], "adamw_w_down"),
        "conv_w": _adamw(conv_w[0], lax.dynamic_slice_in_dim(red[7], me * cs, cs, axis=1)[None],
                         m_conv_w[0], v_conv_w[0], "adamw_conv_w"),
    }
    rep_w = [g_mix, g_q, g_k, lb_logits, g_hg_out, g_ffn, conv_b]
    rep_m = [m_g_mix, m_g_q, m_g_k, m_lb_logits, m_g_hg_out, m_g_ffn, m_conv_b]
    rep_v = [v_g_mix, v_g_q, v_g_k, v_lb_logits, v_g_hg_out, v_g_ffn, v_conv_b]
    rep_shapes = [p.shape for p in rep_w]
    rep_out = _adamw(_pack(rep_w), _pack(red[:7])[None], _pack(rep_m), _pack(rep_v), "adamw_small")
    rep = [_unpack(o, rep_shapes) for o in rep_out]
    rep_names = ["g_mix", "g_q", "g_k", "lb_logits", "g_hg_out", "g_ffn", "conv_b"]

    order = ["g_mix", "w_in", "g_q", "g_k", "lb_logits", "g_hg_out", "p_a", "p_b", "w_o", "g_ffn", "w_up",
             "conv_w", "conv_b", "w_down"]

    def leaf(kind, pname):
        if pname in big:
            return big[pname][kind][None]
        return rep[kind][rep_names.index(pname)]

    return (loss, gx[None], *[leaf(kind, p) for kind in range(4) for p in order])
```
